```python
import jax, jax.numpy as jnp
from jax import lax
import numpy as np

D_MODEL = 2048
BATCH = 2
SEQ = 8192
DEPTH = 1

GRID_W = 64
CTX_LEN = 256
CHUNK = 64
EPS = 1e-6
A_HEADS = 16
A_DK = 128
A_DV = 128
A_KW = A_HEADS * A_DK
A_VW = A_HEADS * A_DV
B_HEADS = 4
B_DK = 256
B_DV = 512
B_KW = B_HEADS * B_DK
B_VW = B_HEADS * B_DV
B_RANK = 16
GATE_NORMALIZER = 16.0
IN_COLS = (A_KW, A_VW, A_KW, A_KW, A_VW, B_KW, B_KW, B_VW, B_RANK, B_RANK, B_VW, D_MODEL, D_MODEL)
N_IN = sum(IN_COLS)

kernel_name = 'hgrn2_gla_parallel_prefix_dit_block'


def rms_norm(x, g):
    xf = x.astype(jnp.float32)
    y = xf * lax.rsqrt(jnp.mean(xf * xf, axis=-1, keepdims=True) + EPS)
    return (y * g.astype(jnp.float32)).astype(x.dtype)


def heads(t, n_heads):
    b, l, w = t.shape
    return t.reshape(b, l, n_heads, w // n_heads).transpose(0, 2, 1, 3)


def unheads(t):
    b, h, l, d = t.shape
    return t.transpose(0, 2, 1, 3).reshape(b, l, h * d)


def to_colmajor(t):
    b, l, w = t.shape
    rows = l // GRID_W
    return t.reshape(b, rows, GRID_W, w).transpose(0, 2, 1, 3).reshape(b, l, w)


def from_colmajor(t):
    b, l, w = t.shape
    rows = l // GRID_W
    return t.reshape(b, GRID_W, rows, w).transpose(0, 2, 1, 3).reshape(b, l, w)


def identity(t):
    return t


def chunked_gated_scan(q, k, v, log_f, s0):
    out_dtype = v.dtype
    b, h, l, dk = q.shape
    dv = v.shape[-1]
    n = l // CHUNK

    def chunks(t):
        return jnp.moveaxis(t.astype(jnp.float32).reshape(b, h, n, CHUNK, t.shape[-1]), 2, 0)

    incl = jnp.tril(jnp.ones((CHUNK, CHUNK), dtype=bool))[:, :, None]

    def step(state, inp):
        qc, kc, vc, lfc = inp
        cum = jnp.cumsum(lfc, axis=-2)
        o_inter = jnp.einsum('bhcd,bhde->bhce', qc * jnp.exp(cum), state)
        rel = cum[:, :, :, None, :] - cum[:, :, None, :, :]
        decay = jnp.exp(jnp.where(incl, rel, -jnp.inf))
        scores = jnp.einsum('bhid,bhjd,bhijd->bhij', qc, kc, decay)
        o_intra = jnp.einsum('bhij,bhje->bhie', scores, vc)
        last = cum[:, :, -1:, :]
        k_to_end = kc * jnp.exp(last - cum)
        new_state = jnp.exp(last[:, :, 0, :])[..., None] * state + jnp.einsum('bhcd,bhce->bhde', k_to_end, vc)
        return new_state, o_inter + o_intra

    s_final, o = lax.scan(step, s0.astype(jnp.float32), (chunks(q), chunks(k), chunks(v), chunks(log_f)))
    o = jnp.moveaxis(o, 0, 2).reshape(b, h, l, dv)
    return o.astype(out_dtype), s_final


def bidirectional_scan(q, k_fwd, k_bwd, v, lf_fwd, lf_bwd, s0_fwd, s0_bwd):
    o_f, s_f = chunked_gated_scan(q, k_fwd, v, lf_fwd, s0_fwd)
    o_b, s_b = chunked_gated_scan(jnp.flip(q, 2), jnp.flip(k_bwd, 2), jnp.flip(v, 2), jnp.flip(lf_bwd, 2), s0_bwd)
    return o_f + jnp.flip(o_b, 2), s_f, s_b


def group_rms_gate(o, n_heads, g, gate):
    b, l, w = o.shape
    of = o.astype(jnp.float32).reshape(b, l, n_heads, w // n_heads)
    of = of * lax.rsqrt(jnp.mean(of * of, axis=-1, keepdims=True) + EPS)
    return (of.reshape(b, l, w) * g.astype(jnp.float32)).astype(gate.dtype) * jax.nn.silu(gate)


def hybrid_mixer(h, w_in, lb, gk_w, gk_b, a_norm_g, b_norm_g, w_pa, w_pb, w_out, states, latent, with_output):
    split_at = [int(s) for s in np.cumsum(IN_COLS)[:-1]]
    (a_q, a_i, a_ff, a_fb, a_g, b_q, b_k, b_v, b_rf, b_rb, b_g, m_a, m_b) = jnp.split(
        jnp.einsum('bld,dn->bln', h, w_in), split_at, axis=-1)
    s_af0, s_ab0, s_bf0, s_bb0 = states

    lb = lb.astype(jnp.float32)
    f_f = lb[:A_KW] + (1.0 - lb[:A_KW]) * jax.nn.sigmoid(a_ff.astype(jnp.float32))
    f_b = lb[A_KW:] + (1.0 - lb[A_KW:]) * jax.nn.sigmoid(a_fb.astype(jnp.float32))
    o_a, s_af, s_ab = bidirectional_scan(
        heads(a_q, A_HEADS), heads(1.0 - f_f, A_HEADS), heads(1.0 - f_b, A_HEADS), heads(a_i, A_HEADS),
        heads(jnp.log(f_f), A_HEADS), heads(jnp.log(f_b), A_HEADS), s_af0, s_ab0)

    order, unorder = (to_colmajor, from_colmajor) if latent else (identity, identity)
    lf_bf = jax.nn.log_sigmoid(jnp.einsum('blr,rk->blk', b_rf, gk_w[0]).astype(jnp.float32)
                               + gk_b[0].astype(jnp.float32)) / GATE_NORMALIZER
    lf_bb = jax.nn.log_sigmoid(jnp.einsum('blr,rk->blk', b_rb, gk_w[1]).astype(jnp.float32)
                               + gk_b[1].astype(jnp.float32)) / GATE_NORMALIZER
    kb = heads(order(b_k), B_HEADS)
    o_b, s_bf, s_bb = bidirectional_scan(
        heads(order(b_q), B_HEADS) * (B_DK ** -0.5), kb, kb, heads(order(b_v), B_HEADS),
        heads(order(lf_bf), B_HEADS), heads(order(lf_bb), B_HEADS), s_bf0, s_bb0)
    new_states = (s_af, s_ab, s_bf, s_bb)
    if not with_output:
        return None, new_states

    o_a = group_rms_gate(unheads(o_a), A_HEADS, a_norm_g, a_g)
    o_b = group_rms_gate(unorder(unheads(o_b)), B_HEADS, b_norm_g, b_g)
    y = (jax.nn.sigmoid(m_a) * jnp.einsum('blv,vd->bld', o_a, w_pa)
         + jax.nn.sigmoid(m_b) * jnp.einsum('blv,vd->bld', o_b, w_pb))
    return jnp.einsum('bld,de->ble', y, w_out), new_states


def setup_inputs(seed: int = 0) -> dict:
    key = jax.random.key(seed)
    ks = jax.random.split(key, 20)
    f32 = jnp.float32
    nrm = lambda k, shape, s: jax.random.normal(k, shape, f32) * s
    return {
        'x': nrm(ks[0], (BATCH, SEQ, D_MODEL), 1.0),
        'c': nrm(ks[1], (BATCH, D_MODEL), 1.0),
        'ctx': nrm(ks[2], (BATCH, CTX_LEN, D_MODEL), 1.0),
        'c_ctx': nrm(ks[3], (D_MODEL,), 1.0),
        'w_ada': nrm(ks[4], (DEPTH, D_MODEL, 3 * D_MODEL), 0.5 * D_MODEL ** -0.5),
        'b_ada': nrm(ks[5], (DEPTH, 3 * D_MODEL), 0.01),
        'norm_g': 1.0 + nrm(ks[6], (DEPTH, D_MODEL), 0.1),
        'w_in': nrm(ks[7], (DEPTH, D_MODEL, N_IN), D_MODEL ** -0.5),
        'hgrn_lb_logits': nrm(ks[8], (DEPTH + 1, 2 * A_KW), 0.5),
        'gla_w_gk': nrm(ks[9], (DEPTH, 2, B_RANK, B_KW), B_RANK ** -0.5),
        'gla_b_gk': nrm(ks[10], (DEPTH, 2, B_KW), 0.1),
        'hgrn_onorm_g': 1.0 + nrm(ks[11], (DEPTH, A_VW), 0.1),
        'gla_onorm_g': 1.0 + nrm(ks[12], (DEPTH, B_VW), 0.1),
        'w_pa': nrm(ks[13], (DEPTH, A_VW, D_MODEL), A_VW ** -0.5),
        'w_pb': nrm(ks[14], (DEPTH, B_VW, D_MODEL), B_VW ** -0.5),
        'w_out': nrm(ks[15], (DEPTH, D_MODEL, D_MODEL), D_MODEL ** -0.5),
        'final_norm_g': 1.0 + nrm(ks[16], (D_MODEL,), 0.1),
    }


def reference(x, c, ctx, c_ctx, w_ada, b_ada, norm_g, w_in, hgrn_lb_logits, gla_w_gk, gla_b_gk,
              hgrn_onorm_g, gla_onorm_g, w_pa, w_pb, w_out, final_norm_g):
    b = x.shape[0]
    lower_bounds = jnp.cumsum(jax.nn.softmax(hgrn_lb_logits.astype(jnp.float32), axis=0), axis=0)
    for l in range(DEPTH):
        shift, scale, gate = jnp.split(jnp.einsum('bd,de->be', jax.nn.silu(c), w_ada[l]) + b_ada[l], 3, axis=-1)
        shift_c, scale_c, gate_c = jnp.split(jnp.einsum('d,de->e', jax.nn.silu(c_ctx), w_ada[l]) + b_ada[l], 3, axis=-1)
        params = (w_in[l], lower_bounds[l], gla_w_gk[l], gla_b_gk[l], hgrn_onorm_g[l], gla_onorm_g[l],
                  w_pa[l], w_pb[l], w_out[l])
        zero_states = (jnp.zeros((b, A_HEADS, A_DK, A_DV), jnp.float32), jnp.zeros((b, A_HEADS, A_DK, A_DV), jnp.float32),
                       jnp.zeros((b, B_HEADS, B_DK, B_DV), jnp.float32), jnp.zeros((b, B_HEADS, B_DK, B_DV), jnp.float32))
        last = l == DEPTH - 1
        h_ctx = rms_norm(ctx, norm_g[l]) * (1.0 + scale_c) + shift_c
        y_ctx, ctx_states = hybrid_mixer(h_ctx, *params, zero_states, latent=False, with_output=not last)
        h_x = rms_norm(x, norm_g[l]) * (1.0 + scale[:, None, :]) + shift[:, None, :]
        y_x, _ = hybrid_mixer(h_x, *params, ctx_states, latent=True, with_output=True)
        x = x + gate[:, None, :] * y_x
        if not last:
            ctx = ctx + gate_c * y_ctx
    return rms_norm(x, final_norm_g)
```

```python
import functools

import jax
import jax.numpy as jnp
from jax import lax
from jax.experimental import pallas as pl
from jax.experimental.pallas import tpu as pltpu

F32 = jnp.float32
BF16 = jnp.bfloat16

CHUNK = 64
SUB = 16
GRID_W = 64
EPS = 1e-6
A_HEADS, A_DK, A_DV = 16, 128, 128
B_HEADS, B_DK, B_DV = 4, 256, 512
B_RANK = 16
GATE_NORMALIZER = 16.0
LOG2E = 1.4426950408889634

VMEM_LIMIT_BYTES = 56 * 1024 * 1024
LANES = 128
SUBLANES = 8


def _sigmoid(z):
    return 1.0 / (1.0 + jnp.exp(-z))


def _adaln_kernel(c_ref, w_ref, b_ref, o_ref):
    c = c_ref[...]
    s = c * _sigmoid(c)
    o_ref[...] = jnp.dot(s, w_ref[...], preferred_element_type=F32,
                         precision=lax.Precision.HIGHEST) + b_ref[...]


def _adaln(cvec, w, b):
    rows, d = cvec.shape
    n = w.shape[1]
    tn = 768
    return pl.pallas_call(
        _adaln_kernel,
        grid=(n // tn,),
        in_specs=[pl.BlockSpec((rows, d), lambda j: (0, 0)),
                  pl.BlockSpec((d, tn), lambda j: (0, j)),
                  pl.BlockSpec((1, tn), lambda j: (0, j))],
        out_specs=pl.BlockSpec((rows, tn), lambda j: (0, j)),
        out_shape=jax.ShapeDtypeStruct((rows, n), F32),
        compiler_params=pltpu.CompilerParams(
            dimension_semantics=("arbitrary",), vmem_limit_bytes=32 * 1024 * 1024),
        name="adaln",
    )(cvec, w, b)


def _lower_bound_kernel(l_ref, o_ref):
    x = l_ref[...]
    n_rows = x.shape[0]
    m = jnp.max(x, axis=0, keepdims=True)
    e = jnp.exp(x - m)
    tot = jnp.sum(e, axis=0, keepdims=True)
    run = jnp.zeros_like(tot)
    for r in range(n_rows - 1):
        run = run + e[r:r + 1]
        o_ref[r:r + 1, :] = run / tot


def _lower_bounds(logits):
    n_rows, w = logits.shape
    return pl.pallas_call(
        _lower_bound_kernel,
        out_shape=jax.ShapeDtypeStruct((n_rows - 1, w), F32),
        name="hgrn_lower_bounds",
    )(logits)


TN = 512
NORM_ROWS = 128
T_AQI = 4096 // TN
T_ALF = 4096 // TN
T_BQKV = 4096 // TN
T_BLF = 2048 // TN
T_GATES = 8192 // TN
J_ALF = T_AQI
J_BQKV = J_ALF + T_ALF
J_BLF = J_BQKV + T_BQKV
J_GATES = J_BLF + T_BLF
J_END = J_GATES + T_GATES


def _inproj_kernel(x_ref, mod_ref, ng_ref, w_ref, wr_ref, lb_ref, gkw_ref, gkb_ref, *rest, with_gates):
    if with_gates:
        aqi_ref, alf_ref, bqkv_ref, blf_ref, gates_ref, h_ref, r_ref = rest
    else:
        aqi_ref, alf_ref, bqkv_ref, blf_ref, h_ref, r_ref = rest
        gates_ref = None
    j = pl.program_id(1)

    @pl.when(j == 0)
    def _():
        def norm_rows(t, carry):
            rows = pl.ds(pl.multiple_of(t * NORM_ROWS, NORM_ROWS), NORM_ROWS)
            x = x_ref[rows, :]
            ms = jnp.mean(x * x, axis=-1, keepdims=True)
            y = x * lax.rsqrt(ms + EPS) * ng_ref[...]
            h = (y * (1.0 + mod_ref[0, 1:2, :]) + mod_ref[0, 0:1, :]).astype(BF16)
            h_ref[rows, :] = h
            r_ref[rows, :] = jnp.dot(h, wr_ref[...], preferred_element_type=F32).astype(BF16)
            return carry
        lax.fori_loop(0, x_ref.shape[0] // NORM_ROWS, norm_rows, 0)

    def mm():
        return jnp.dot(h_ref[...], w_ref[...], preferred_element_type=F32)

    @pl.when(j < J_ALF)
    def _():
        aqi_ref[...] = mm().astype(BF16)

    @pl.when((j >= J_ALF) & (j < J_BQKV))
    def _():
        lb = lb_ref[...]
        f = lb + (1.0 - lb) * _sigmoid(mm())
        alf_ref[...] = jnp.log2(f)

    @pl.when((j >= J_BQKV) & (j < J_BLF))
    def _():
        scale = jnp.where(j < J_BQKV + (B_HEADS * B_DK) // TN, B_DK ** -0.5, 1.0)
        bqkv_ref[...] = (mm() * scale).astype(BF16)

    @pl.when((j >= J_BLF) & (j < J_GATES))
    def _():
        z = jnp.dot(r_ref[...], gkw_ref[...], preferred_element_type=F32) + gkb_ref[...]
        ls = jnp.minimum(z, 0.0) - jnp.log(1.0 + jnp.exp(-jnp.abs(z)))
        blf_ref[...] = ls * (LOG2E / GATE_NORMALIZER)

    if with_gates:
        @pl.when(j >= J_GATES)
        def _():
            z = mm()
            sg = _sigmoid(z)
            gates_ref[...] = (sg * jnp.where(j < J_GATES + T_GATES // 2, z, 1.0)).astype(BF16)


def _inproj(x2d, mod3, mod_row_of_tile, norm_g, w_main, w_r, lb, gk_pad, gkb, *, tm, with_gates):
    n_tok, d = x2d.shape
    n_j = J_END if with_gates else J_GATES

    def cl(j, lo, n):
        return jnp.clip(j - lo, 0, n - 1)

    in_specs = [
        pl.BlockSpec((tm, d), lambda i, j: (i, 0)),
        pl.BlockSpec((1, 3, d), lambda i, j: (mod_row_of_tile(i), 0, 0)),
        pl.BlockSpec((1, d), lambda i, j: (0, 0)),
        pl.BlockSpec((d, TN), lambda i, j: (0, j - jnp.clip(j - (J_BLF - 1), 0, T_BLF))),
        pl.BlockSpec((d, LANES), lambda i, j: (0, 0)),
        pl.BlockSpec((1, TN), lambda i, j: (0, cl(j, J_ALF, T_ALF))),
        pl.BlockSpec((LANES, TN), lambda i, j: (0, cl(j, J_BLF, T_BLF))),
        pl.BlockSpec((1, TN), lambda i, j: (0, cl(j, J_BLF, T_BLF))),
    ]
    out_specs = [
        pl.BlockSpec((tm, TN), lambda i, j: (i, cl(j, 0, T_AQI))),
        pl.BlockSpec((tm, TN), lambda i, j: (i, cl(j, J_ALF, T_ALF))),
        pl.BlockSpec((tm, TN), lambda i, j: (i, cl(j, J_BQKV, T_BQKV))),
        pl.BlockSpec((tm, TN), lambda i, j: (i, cl(j, J_BLF, T_BLF))),
    ]
    out_shape = [
        jax.ShapeDtypeStruct((n_tok, T_AQI * TN), BF16),
        jax.ShapeDtypeStruct((n_tok, T_ALF * TN), F32),
        jax.ShapeDtypeStruct((n_tok, T_BQKV * TN), BF16),
        jax.ShapeDtypeStruct((n_tok, T_BLF * TN), F32),
    ]
    if with_gates:
        out_specs.append(pl.BlockSpec((tm, TN), lambda i, j: (i, cl(j, J_GATES, T_GATES))))
        out_shape.append(jax.ShapeDtypeStruct((n_tok, T_GATES * TN), BF16))
    return pl.pallas_call(
        functools.partial(_inproj_kernel, with_gates=with_gates),
        grid=(n_tok // tm, n_j),
        in_specs=in_specs,
        out_specs=out_specs,
        out_shape=out_shape,
        scratch_shapes=[pltpu.VMEM((tm, d), BF16), pltpu.VMEM((tm, LANES), BF16)],
        compiler_params=pltpu.CompilerParams(
            dimension_semantics=("arbitrary", "arbitrary"), vmem_limit_bytes=VMEM_LIMIT_BYTES),
        name="inproj_latent" if with_gates else "inproj_ctx",
    )(x2d, mod3, norm_g, w_main, w_r, lb, gk_pad, gkb)


def _scan_consts():
    row = lax.broadcasted_iota(jnp.int32, (CHUNK, CHUNK), 0)
    col = lax.broadcasted_iota(jnp.int32, (CHUNK, CHUNK), 1)
    return dict(
        mask_f=col <= row,
        mask_b=col >= row,
        tri_f=jnp.where(col <= row, 1.0, 0.0).astype(BF16),
        tri_b=jnp.where(col >= row, 1.0, 0.0).astype(BF16),
        lane=lax.broadcasted_iota(jnp.int32, (SUBLANES, CHUNK), 1),
    )


def _chunk_unit(q, k, v, lf, st_ref, fwd, cst, with_out):
    dk = q.shape[1]
    nt = (((1,), (1,)), ((), ()))
    tn = (((0,), (0,)), ((), ()))
    tri = cst["tri_f"] if fwd else cst["tri_b"]
    hi = lf.astype(BF16)
    lo = (lf - hi.astype(F32)).astype(BF16)
    cc = jnp.dot(tri, jnp.concatenate([hi, lo], axis=1), preferred_element_type=F32)
    c = cc[:, :dk] + cc[:, dk:]
    tot = c[CHUNK - 1:CHUNK] if fwd else c[0:1]

    if k is None:
        a = c - jnp.log2(1.0 - jnp.exp2(lf))
        kf = None

        def kscaled(r0, r1, ref):
            return jnp.exp2(ref - a[r0:r1])
    else:
        a = c
        kf = k.astype(F32)

        def kscaled(r0, r1, ref):
            return kf[r0:r1] * jnp.exp2(ref - c[r0:r1])

    kte = kscaled(0, CHUNK, tot).astype(BF16)
    st = st_ref[...]
    st_ref[...] = st * jnp.exp2(tot) + lax.dot_general(v, kte, tn, preferred_element_type=F32)
    if not with_out:
        return None

    qf = q.astype(F32)
    qd = (qf * jnp.exp2(c)).astype(BF16)
    n_sub = CHUNK // SUB
    half = SUB // SUBLANES
    rows = []
    for r in range(n_sub):
        r0 = SUB * r
        cb = c[r0:r0 + SUB]
        qb = qf[r0:r0 + SUB]
        soff = None
        if fwd and r > 0:
            ref = c[r0:r0 + 1]
            kr = jnp.concatenate([kscaled(0, r0, ref).astype(BF16),
                                  jnp.zeros((CHUNK - r0, dk), BF16)], axis=0)
        elif (not fwd) and r < n_sub - 1:
            ref = c[r0 + SUB - 1:r0 + SUB]
            kr = jnp.concatenate([jnp.zeros((r0 + SUB, dk), BF16),
                                  kscaled(r0 + SUB, CHUNK, ref).astype(BF16)], axis=0)
        else:
            kr = None
        if kr is not None:
            qr = (qb * jnp.exp2(cb - ref)).astype(BF16)
            soff = lax.dot_general(qr, kr, nt, preferred_element_type=F32)
        acc = [jnp.zeros((SUBLANES, CHUNK), F32) for _ in range(half)]
        for jj in range(SUB):
            aj = a[r0 + jj:r0 + jj + 1]
            jg = jj // SUBLANES
            groups = range(jg, half) if fwd else range(0, jg + 1)
            for g in groups:
                e = cb[g * SUBLANES:(g + 1) * SUBLANES] - aj
                if g == jg:
                    e = jnp.minimum(e, 0.0)
                t = qb[g * SUBLANES:(g + 1) * SUBLANES] * jnp.exp2(e)
                if kf is not None:
                    t = t * kf[r0 + jj:r0 + jj + 1]
                sj = jnp.sum(t, axis=1, keepdims=True)
                acc[g] = jnp.where(cst["lane"] == r0 + jj, sj, acc[g])
        d = jnp.concatenate(acc, axis=0)
        rows.append(d if soff is None else d + soff)
    p = jnp.concatenate(rows, axis=0)
    p = jnp.where(cst["mask_f"] if fwd else cst["mask_b"], p, 0.0).astype(BF16)
    o = jnp.dot(p, v, preferred_element_type=F32)
    o = o + lax.dot_general(qd, st.astype(BF16), nt, preferred_element_type=F32)
    return o


def _scan_kernel(*refs, n_heads, dk, dv, has_k, n_ctx_chunks, n_step_chunks):
    n_in = 4 if has_k else 3
    idx = 0
    groups = []
    for _ in range(4):
        groups.append(refs[idx:idx + n_in])
        idx += n_in
    of_ref, ob_ref, st_ref = refs[idx:idx + 3]
    ctx_f, ctx_b, lat_f, lat_b = groups
    s = pl.program_id(2)
    cst = _scan_consts()

    def load(group, r, h):
        rows = pl.ds(r, CHUNK)
        q = group[0][rows, h * dk:(h + 1) * dk]
        if has_k:
            k = group[1][rows, h * dk:(h + 1) * dk]
            v = group[2][rows, h * dv:(h + 1) * dv]
            lf = group[3][rows, h * dk:(h + 1) * dk]
        else:
            k = None
            v = group[1][rows, h * dv:(h + 1) * dv]
            lf = group[2][rows, h * dk:(h + 1) * dk]
        return q, k, v, lf

    def run(gf, gb, n_chunks, with_out):
        def body(ci, carry):
            rf = pl.multiple_of(ci * CHUNK, CHUNK)
            rb = pl.multiple_of((n_chunks - 1 - ci) * CHUNK, CHUNK)
            for h in range(n_heads):
                q, k, v, lf = load(gf, rf, h)
                o = _chunk_unit(q, k, v, lf, st_ref.at[0, h], True, cst, with_out)
                if with_out:
                    of_ref[pl.ds(rf, CHUNK), h * dv:(h + 1) * dv] = o.astype(of_ref.dtype)
                q, k, v, lf = load(gb, rb, h)
                o = _chunk_unit(q, k, v, lf, st_ref.at[1, h], False, cst, with_out)
                if with_out:
                    ob_ref[pl.ds(rb, CHUNK), h * dv:(h + 1) * dv] = o.astype(ob_ref.dtype)
            return carry
        lax.fori_loop(0, n_chunks, body, 0)

    @pl.when(s == 0)
    def _():
        st_ref[...] = jnp.zeros(st_ref.shape, F32)
        run(ctx_f, ctx_b, n_ctx_chunks, False)

    @pl.when(s > 0)
    def _():
        run(lat_f, lat_b, n_step_chunks, True)


def _scan_call(inputs, in_specs, out_struct, out_specs, grid, *, n_heads, dk, dv, has_k,
               n_ctx_chunks, n_step_chunks, name):
    return pl.pallas_call(
        functools.partial(_scan_kernel, n_heads=n_heads, dk=dk, dv=dv, has_k=has_k,
                          n_ctx_chunks=n_ctx_chunks, n_step_chunks=n_step_chunks),
        grid=grid,
        in_specs=in_specs,
        out_specs=out_specs,
        out_shape=out_struct,
        scratch_shapes=[pltpu.VMEM((2, n_heads, dv, dk), F32)],
        compiler_params=pltpu.CompilerParams(
            dimension_semantics=("arbitrary", "arbitrary", "arbitrary"),
            vmem_limit_bytes=VMEM_LIMIT_BYTES),
        name=name,
    )(*inputs)


A_SCAN_HEADS = 2
A_SCAN_CHUNKS = 4


def _scan_a(c_qi, c_lf, qi, lf):
    bsz, n_ctx, _ = c_qi.shape
    seq = qi.shape[1]
    g = A_SCAN_HEADS
    gw = g * A_DK
    n_hg = A_HEADS // g
    ts = A_SCAN_CHUNKS * CHUNK
    n_steps = seq // ts

    def fstep(s):
        return jnp.maximum(s - 1, 0)

    def bstep(s):
        return n_steps - 1 - jnp.maximum(s - 1, 0)

    cspec = lambda off: pl.BlockSpec((None, n_ctx, gw), lambda b, h, s: (b, 0, off + h))
    fspec = lambda off: pl.BlockSpec((None, ts, gw), lambda b, h, s: (b, fstep(s), off + h))
    bspec = lambda off: pl.BlockSpec((None, ts, gw), lambda b, h, s: (b, bstep(s), off + h))
    in_specs = [cspec(0), cspec(n_hg), cspec(0),
                cspec(0), cspec(n_hg), cspec(n_hg),
                fspec(0), fspec(n_hg), fspec(0),
                bspec(0), bspec(n_hg), bspec(n_hg)]
    inputs = [c_qi, c_qi, c_lf, c_qi, c_qi, c_lf, qi, qi, lf, qi, qi, lf]
    out_struct = [jax.ShapeDtypeStruct((bsz, seq, A_HEADS * A_DV), BF16)] * 2
    out_specs = [pl.BlockSpec((None, ts, gw), lambda b, h, s: (b, fstep(s), h)),
                 pl.BlockSpec((None, ts, gw), lambda b, h, s: (b, bstep(s), h))]
    return _scan_call(inputs, in_specs, out_struct, out_specs, (bsz, n_hg, n_steps + 1),
                      n_heads=g, dk=A_DK, dv=A_DV, has_k=False,
                      n_ctx_chunks=n_ctx // CHUNK, n_step_chunks=A_SCAN_CHUNKS, name="scan_hgrn2")


def _scan_b(c_qkv, c_lf, qkv, lf):
    bsz, n_ctx, wq = c_qkv.shape
    seq = qkv.shape[1]
    wl = lf.shape[2]
    n_rows = seq // GRID_W
    qkv_c = qkv.reshape(bsz, n_rows, GRID_W * wq)
    lf_c = lf.reshape(bsz, n_rows, GRID_W * wl)
    kw = B_HEADS * B_DK
    q_per, v_per, l_per = wq // B_DK, wq // B_DV, wl // B_DK

    def fcol(s):
        return jnp.maximum(s - 1, 0)

    def bcol(s):
        return GRID_W - 1 - jnp.maximum(s - 1, 0)

    def cspec(width, off):
        return pl.BlockSpec((None, n_ctx, width), lambda b, h, s: (b, 0, off + h))

    def lspec(width, per, off, colf):
        return pl.BlockSpec((None, n_rows, width), lambda b, h, s: (b, 0, colf(s) * per + off + h))

    k_off, v_off = kw // B_DK, (2 * kw) // B_DV
    in_specs = [cspec(B_DK, 0), cspec(B_DK, k_off), cspec(B_DV, v_off), cspec(B_DK, 0),
                cspec(B_DK, 0), cspec(B_DK, k_off), cspec(B_DV, v_off), cspec(B_DK, B_HEADS),
                lspec(B_DK, q_per, 0, fcol), lspec(B_DK, q_per, k_off, fcol),
                lspec(B_DV, v_per, v_off, fcol), lspec(B_DK, l_per, 0, fcol),
                lspec(B_DK, q_per, 0, bcol), lspec(B_DK, q_per, k_off, bcol),
                lspec(B_DV, v_per, v_off, bcol), lspec(B_DK, l_per, B_HEADS, bcol)]
    inputs = [c_qkv, c_qkv, c_qkv, c_lf, c_qkv, c_qkv, c_qkv, c_lf,
              qkv_c, qkv_c, qkv_c, lf_c, qkv_c, qkv_c, qkv_c, lf_c]
    vw = B_HEADS * B_DV
    out_struct = [jax.ShapeDtypeStruct((bsz, n_rows, GRID_W * vw), BF16)] * 2
    out_specs = [pl.BlockSpec((None, n_rows, B_DV), lambda b, h, s: (b, 0, fcol(s) * B_HEADS + h)),
                 pl.BlockSpec((None, n_rows, B_DV), lambda b, h, s: (b, 0, bcol(s) * B_HEADS + h))]
    of, ob = _scan_call(inputs, in_specs, out_struct, out_specs, (bsz, B_HEADS, GRID_W + 1),
                        n_heads=1, dk=B_DK, dv=B_DV, has_k=True,
                        n_ctx_chunks=n_ctx // CHUNK, n_step_chunks=n_rows // CHUNK, name="scan_gla")
    return of.reshape(bsz, seq, vw), ob.reshape(bsz, seq, vw)


def _merge_kernel(oaf, oab, obf, obb, sga, sgb, sma, smb, ga, gb, wpa, wpb, y_ref):
    def branch(of, ob, sg, g, n_heads):
        o = of[...].astype(F32) + ob[...].astype(F32)
        hw = o.shape[1] // n_heads
        parts = []
        for h in range(n_heads):
            oh = o[:, h * hw:(h + 1) * hw]
            ms = jnp.mean(oh * oh, axis=-1, keepdims=True)
            parts.append(oh * lax.rsqrt(ms + EPS))
        on = jnp.concatenate(parts, axis=1) * g[...]
        return (on * sg[...].astype(F32)).astype(BF16)

    ya = jnp.dot(branch(oaf, oab, sga, ga, A_HEADS), wpa[...], preferred_element_type=F32)
    yb = jnp.dot(branch(obf, obb, sgb, gb, B_HEADS), wpb[...], preferred_element_type=F32)
    y_ref[...] = (sma[...].astype(F32) * ya + smb[...].astype(F32) * yb).astype(BF16)


def _merge(oaf, oab, obf, obb, gates, ga, gb, wpa, wpb, *, tm):
    n_tok, d = oaf.shape
    tok = lambda off: pl.BlockSpec((tm, d), lambda i: (i, off))
    const = lambda shape: pl.BlockSpec(shape, lambda i: (0, 0), pipeline_mode=pl.Buffered(1))
    return pl.pallas_call(
        _merge_kernel,
        grid=(n_tok // tm,),
        in_specs=[tok(0), tok(0), tok(0), tok(0), tok(0), tok(1), tok(2), tok(3),
                  const((1, d)), const((1, d)), const(wpa.shape), const(wpb.shape)],
        out_specs=pl.BlockSpec((tm, wpa.shape[1]), lambda i: (i, 0)),
        out_shape=jax.ShapeDtypeStruct((n_tok, wpa.shape[1]), BF16),
        compiler_params=pltpu.CompilerParams(
            dimension_semantics=("arbitrary",), vmem_limit_bytes=VMEM_LIMIT_BYTES),
        name="merge_proj",
    )(oaf, oab, obf, obb, gates, gates, gates, gates, ga, gb, wpa, wpb)


def _final_kernel(y_ref, x_ref, mod_ref, w_ref, g_ref, o_ref):
    yo = jnp.dot(y_ref[...], w_ref[...], preferred_element_type=F32)
    z = x_ref[...] + mod_ref[0, 2:3, :] * yo
    ms = jnp.mean(z * z, axis=-1, keepdims=True)
    o_ref[...] = z * lax.rsqrt(ms + EPS) * g_ref[...]


def _final(y, x2d, mod3, w_out, fg, *, tm, tiles_per_batch):
    n_tok, d = x2d.shape
    const = lambda shape: pl.BlockSpec(shape, lambda i: (0, 0), pipeline_mode=pl.Buffered(1))
    return pl.pallas_call(
        _final_kernel,
        grid=(n_tok // tm,),
        in_specs=[pl.BlockSpec((tm, d), lambda i: (i, 0)),
                  pl.BlockSpec((tm, d), lambda i: (i, 0)),
                  pl.BlockSpec((1, 3, d), lambda i: (i // tiles_per_batch, 0, 0)),
                  const(w_out.shape), const((1, d))],
        out_specs=pl.BlockSpec((tm, d), lambda i: (i, 0)),
        out_shape=jax.ShapeDtypeStruct((n_tok, d), F32),
        compiler_params=pltpu.CompilerParams(
            dimension_semantics=("arbitrary",), vmem_limit_bytes=VMEM_LIMIT_BYTES),
        name="out_proj_final",
    )(y, x2d, mod3, w_out, fg)


def kernel(x, c, ctx, c_ctx, w_ada, b_ada, norm_g, w_in, hgrn_lb_logits, gla_w_gk, gla_b_gk,
           hgrn_onorm_g, gla_onorm_g, w_pa, w_pb, w_out, final_norm_g):
    bsz, seq, d = x.shape
    n_ctx = ctx.shape[1]
    depth = w_in.shape[0]
    assert depth == 1, "single-layer trunk"
    a_kw, a_vw = A_HEADS * A_DK, A_HEADS * A_DV
    b_kw, b_vw = B_HEADS * B_DK, B_HEADS * B_DV

    n_rows = -(-(bsz + 1) // SUBLANES) * SUBLANES
    cvec = jnp.zeros((n_rows, d), F32).at[:bsz].set(c).at[bsz].set(c_ctx)
    mod = _adaln(cvec, w_ada[0], b_ada[0].reshape(1, -1))
    mod3 = mod.reshape(n_rows, 3, d)

    lb = _lower_bounds(hgrn_lb_logits)[0:1]

    w = w_in[0]
    o_ag = 2 * a_kw + 2 * a_vw
    o_bq = o_ag + a_vw
    o_br = o_bq + 2 * b_kw + b_vw
    o_bg = o_br + 2 * B_RANK
    w_main = jnp.concatenate(
        [w[:, 0:a_kw + a_vw], w[:, a_kw + a_vw:o_ag], w[:, o_bq:o_br], w[:, o_ag:o_bq], w[:, o_bg:]],
        axis=1).astype(BF16)
    w_r = jnp.pad(w[:, o_br:o_bg], ((0, 0), (0, LANES - 2 * B_RANK))).astype(BF16)
    gk_pad = jnp.zeros((LANES, 2 * b_kw), F32)
    gk_pad = gk_pad.at[0:B_RANK, 0:b_kw].set(gla_w_gk[0, 0]).at[B_RANK:2 * B_RANK, b_kw:].set(gla_w_gk[0, 1])
    gk_pad = gk_pad.astype(BF16)
    gkb = gla_b_gk[0].reshape(1, 2 * b_kw)
    ng = norm_g[0].reshape(1, d)

    x2d = x.reshape(bsz * seq, d)
    ctx2d = ctx.reshape(bsz * n_ctx, d)
    tm = 1024
    tiles_per_batch = seq // tm

    c_aqi, c_alf, c_bqkv, c_blf = _inproj(
        ctx2d, mod3, lambda i: bsz, ng, w_main, w_r, lb, gk_pad, gkb,
        tm=bsz * n_ctx, with_gates=False)
    aqi, alf, bqkv, blf, gates = _inproj(
        x2d, mod3, lambda i: i // tiles_per_batch, ng, w_main, w_r, lb, gk_pad, gkb,
        tm=tm, with_gates=True)

    r3 = lambda t, n: t.reshape(bsz, n, t.shape[-1])
    oaf, oab = _scan_a(r3(c_aqi, n_ctx), r3(c_alf, n_ctx), r3(aqi, seq), r3(alf, seq))
    obf, obb = _scan_b(r3(c_bqkv, n_ctx), r3(c_blf, n_ctx), r3(bqkv, seq), r3(blf, seq))

    r2 = lambda t: t.reshape(bsz * seq, t.shape[-1])
    y = _merge(r2(oaf), r2(oab), r2(obf), r2(obb), gates,
               hgrn_onorm_g[0].reshape(1, -1), gla_onorm_g[0].reshape(1, -1),
               w_pa[0].astype(BF16), w_pb[0].astype(BF16), tm=256)
    out = _final(y, x2d, mod3, w_out[0].astype(BF16), final_norm_g.reshape(1, d),
                 tm=512, tiles_per_batch=seq // 512)
    return out.reshape(bsz, seq, d)
```

```python
import functools

import jax
import jax.numpy as jnp
from jax import lax
from jax.experimental import pallas as pl
from jax.experimental.pallas import tpu as pltpu

F32 = jnp.float32
BF16 = jnp.bfloat16

CHUNK = 64
SUB = 16
GRID_W = 64
EPS = 1e-6
A_HEADS, A_DK, A_DV = 16, 128, 128
B_HEADS, B_DK, B_DV = 4, 256, 512
B_RANK = 16
GATE_NORMALIZER = 16.0
LOG2E = 1.4426950408889634

VMEM_LIMIT_BYTES = 56 * 1024 * 1024
LANES = 128
SUBLANES = 8


def _sigmoid(z):
    return 1.0 / (1.0 + jnp.exp(-z))


def _adaln_kernel(c_ref, w_ref, b_ref, o_ref):
    c = c_ref[...]
    s = c * _sigmoid(c)
    o_ref[...] = jnp.dot(s, w_ref[...], preferred_element_type=F32,
                         precision=lax.Precision.HIGHEST) + b_ref[...]


def _adaln(cvec, w, b):
    rows, d = cvec.shape
    n = w.shape[1]
    tn = 768
    return pl.pallas_call(
        _adaln_kernel,
        grid=(n // tn,),
        in_specs=[pl.BlockSpec((rows, d), lambda j: (0, 0)),
                  pl.BlockSpec((d, tn), lambda j: (0, j)),
                  pl.BlockSpec((1, tn), lambda j: (0, j))],
        out_specs=pl.BlockSpec((rows, tn), lambda j: (0, j)),
        out_shape=jax.ShapeDtypeStruct((rows, n), F32),
        compiler_params=pltpu.CompilerParams(
            dimension_semantics=("arbitrary",), vmem_limit_bytes=32 * 1024 * 1024),
        name="adaln",
    )(cvec, w, b)


def _lower_bound_kernel(l_ref, o_ref):
    x = l_ref[...]
    n_rows = x.shape[0]
    m = jnp.max(x, axis=0, keepdims=True)
    e = jnp.exp(x - m)
    tot = jnp.sum(e, axis=0, keepdims=True)
    run = jnp.zeros_like(tot)
    for r in range(n_rows - 1):
        run = run + e[r:r + 1]
        o_ref[r:r + 1, :] = run / tot


def _lower_bounds(logits):
    n_rows, w = logits.shape
    return pl.pallas_call(
        _lower_bound_kernel,
        out_shape=jax.ShapeDtypeStruct((n_rows - 1, w), F32),
        name="hgrn_lower_bounds",
    )(logits)


TN = 512
NORM_ROWS = 128
T_AQI = 4096 // TN
T_ALF = 4096 // TN
T_BQKV = 4096 // TN
T_BLF = 2048 // TN
T_GATES = 8192 // TN
J_ALF = T_AQI
J_BQKV = J_ALF + T_ALF
J_BLF = J_BQKV + T_BQKV
J_GATES = J_BLF + T_BLF
J_END = J_GATES + T_GATES


def _inproj_kernel(x_ref, mod_ref, ng_ref, w_ref, wr_ref, lb_ref, gkw_ref, gkb_ref, *rest, latent):
    if latent:
        aqi_ref, alf_ref, bqkv_ref, blf_ref, gates_ref, h_ref, r_ref, z_ref = rest
    else:
        aqi_ref, alf_ref, bqkv_ref, blf_ref, h_ref, r_ref = rest
        gates_ref = z_ref = None
    j = pl.program_id(1)

    def store_gla(out_ref, z):
        if not latent:
            out_ref[...] = z.astype(out_ref.dtype)
            return
        n_planes, n_tok, _ = z_ref.shape
        for l in range(n_planes):
            z_ref[l] = z[:, l * LANES:(l + 1) * LANES]
        n_rows = n_tok // GRID_W
        for col in range(GRID_W):
            rows = pl.ds(col, n_rows, stride=GRID_W)
            out_ref[col] = jnp.concatenate(
                [z_ref[l, rows, :] for l in range(n_planes)], axis=1).astype(out_ref.dtype)

    @pl.when(j == 0)
    def _():
        def norm_rows(t, carry):
            rows = pl.ds(pl.multiple_of(t * NORM_ROWS, NORM_ROWS), NORM_ROWS)
            x = x_ref[rows, :]
            ms = jnp.mean(x * x, axis=-1, keepdims=True)
            y = x * lax.rsqrt(ms + EPS) * ng_ref[...]
            h = (y * (1.0 + mod_ref[0, 1:2, :]) + mod_ref[0, 0:1, :]).astype(BF16)
            h_ref[rows, :] = h
            r_ref[rows, :] = jnp.dot(h, wr_ref[...], preferred_element_type=F32).astype(BF16)
            return carry
        lax.fori_loop(0, x_ref.shape[0] // NORM_ROWS, norm_rows, 0)

    def mm():
        return jnp.dot(h_ref[...], w_ref[...], preferred_element_type=F32)

    @pl.when(j < J_ALF)
    def _():
        aqi_ref[...] = mm().astype(BF16)

    @pl.when((j >= J_ALF) & (j < J_BQKV))
    def _():
        lb = lb_ref[...]
        f = lb + (1.0 - lb) * _sigmoid(mm())
        alf_ref[...] = jnp.log2(f)

    @pl.when((j >= J_BQKV) & (j < J_BLF))
    def _():
        scale = jnp.where(j < J_BQKV + (B_HEADS * B_DK) // TN, B_DK ** -0.5, 1.0)
        store_gla(bqkv_ref, mm() * scale)

    @pl.when((j >= J_BLF) & (j < J_GATES))
    def _():
        z = jnp.dot(r_ref[...], gkw_ref[...], preferred_element_type=F32) + gkb_ref[...]
        ls = jnp.minimum(z, 0.0) - jnp.log(1.0 + jnp.exp(-jnp.abs(z)))
        store_gla(blf_ref, ls * (LOG2E / GATE_NORMALIZER))

    if latent:
        @pl.when(j >= J_GATES)
        def _():
            z = mm()
            sg = _sigmoid(z)
            gates_ref[...] = (sg * jnp.where(j < J_GATES + T_GATES // 2, z, 1.0)).astype(BF16)


def _inproj(x2d, mod3, mod_row_of_tile, norm_g, w_main, w_r, lb, gk_pad, gkb, *, tm, latent, n_batch=1):
    n_tok, d = x2d.shape
    n_j = J_END if latent else J_GATES
    tile_rows = tm // GRID_W
    seq_rows = n_tok // n_batch // GRID_W
    tiles_per_batch = n_tok // n_batch // tm

    def cl(j, lo, n):
        return jnp.clip(j - lo, 0, n - 1)

    def gla_spec(j_lo, n):
        if not latent:
            return pl.BlockSpec((tm, TN), lambda i, j: (i, cl(j, j_lo, n)))
        return pl.BlockSpec((None, GRID_W, tile_rows, TN),
                            lambda i, j: (i // tiles_per_batch, 0, i % tiles_per_batch, cl(j, j_lo, n)))

    def gla_shape(n, dtype):
        if not latent:
            return jax.ShapeDtypeStruct((n_tok, n * TN), dtype)
        return jax.ShapeDtypeStruct((n_batch, GRID_W, seq_rows, n * TN), dtype)

    in_specs = [
        pl.BlockSpec((tm, d), lambda i, j: (i, 0)),
        pl.BlockSpec((1, 3, d), lambda i, j: (mod_row_of_tile(i), 0, 0)),
        pl.BlockSpec((1, d), lambda i, j: (0, 0)),
        pl.BlockSpec((d, TN), lambda i, j: (0, j - jnp.clip(j - (J_BLF - 1), 0, T_BLF))),
        pl.BlockSpec((d, LANES), lambda i, j: (0, 0)),
        pl.BlockSpec((1, TN), lambda i, j: (0, cl(j, J_ALF, T_ALF))),
        pl.BlockSpec((LANES, TN), lambda i, j: (0, cl(j, J_BLF, T_BLF))),
        pl.BlockSpec((1, TN), lambda i, j: (0, cl(j, J_BLF, T_BLF))),
    ]
    out_specs = [
        pl.BlockSpec((tm, TN), lambda i, j: (i, cl(j, 0, T_AQI))),
        pl.BlockSpec((tm, TN), lambda i, j: (i, cl(j, J_ALF, T_ALF))),
        gla_spec(J_BQKV, T_BQKV),
        gla_spec(J_BLF, T_BLF),
    ]
    out_shape = [
        jax.ShapeDtypeStruct((n_tok, T_AQI * TN), BF16),
        jax.ShapeDtypeStruct((n_tok, T_ALF * TN), F32),
        gla_shape(T_BQKV, BF16),
        gla_shape(T_BLF, F32),
    ]
    scratch = [pltpu.VMEM((tm, d), BF16), pltpu.VMEM((tm, LANES), BF16)]
    if latent:
        out_specs.append(pl.BlockSpec((tm, TN), lambda i, j: (i, cl(j, J_GATES, T_GATES))))
        out_shape.append(jax.ShapeDtypeStruct((n_tok, T_GATES * TN), BF16))
        scratch.append(pltpu.VMEM((TN // LANES, tm, LANES), F32))
    return pl.pallas_call(
        functools.partial(_inproj_kernel, latent=latent),
        grid=(n_tok // tm, n_j),
        in_specs=in_specs,
        out_specs=out_specs,
        out_shape=out_shape,
        scratch_shapes=scratch,
        compiler_params=pltpu.CompilerParams(
            dimension_semantics=("arbitrary", "arbitrary"), vmem_limit_bytes=VMEM_LIMIT_BYTES),
        name="inproj_latent" if latent else "inproj_ctx",
    )(x2d, mod3, norm_g, w_main, w_r, lb, gk_pad, gkb)


def _scan_consts():
    row = lax.broadcasted_iota(jnp.int32, (CHUNK, CHUNK), 0)
    col = lax.broadcasted_iota(jnp.int32, (CHUNK, CHUNK), 1)
    return dict(
        mask_f=col <= row,
        mask_b=col >= row,
        tri_f=jnp.where(col <= row, 1.0, 0.0).astype(BF16),
        tri_b=jnp.where(col >= row, 1.0, 0.0).astype(BF16),
        lane=lax.broadcasted_iota(jnp.int32, (SUBLANES, CHUNK), 1),
    )


def _chunk_unit(q, k, v, lf, st_ref, fwd, cst, with_out):
    dk = q.shape[1]
    nt = (((1,), (1,)), ((), ()))
    tn = (((0,), (0,)), ((), ()))
    tri = cst["tri_f"] if fwd else cst["tri_b"]
    hi = lf.astype(BF16)
    lo = (lf - hi.astype(F32)).astype(BF16)
    cc = jnp.dot(tri, jnp.concatenate([hi, lo], axis=1), preferred_element_type=F32)
    c = cc[:, :dk] + cc[:, dk:]
    tot = c[CHUNK - 1:CHUNK] if fwd else c[0:1]

    if k is None:
        a = c - jnp.log2(1.0 - jnp.exp2(lf))
        kf = None

        def kscaled(r0, r1, ref):
            return jnp.exp2(ref - a[r0:r1])
    else:
        a = c
        kf = k.astype(F32)

        def kscaled(r0, r1, ref):
            return kf[r0:r1] * jnp.exp2(ref - c[r0:r1])

    kte = kscaled(0, CHUNK, tot).astype(BF16)
    st = st_ref[...]
    st_ref[...] = st * jnp.exp2(tot) + lax.dot_general(v, kte, tn, preferred_element_type=F32)
    if not with_out:
        return None

    qf = q.astype(F32)
    qd = (qf * jnp.exp2(c)).astype(BF16)
    n_sub = CHUNK // SUB
    half = SUB // SUBLANES
    rows = []
    for r in range(n_sub):
        r0 = SUB * r
        cb = c[r0:r0 + SUB]
        qb = qf[r0:r0 + SUB]
        soff = None
        if fwd and r > 0:
            ref = c[r0:r0 + 1]
            kr = jnp.concatenate([kscaled(0, r0, ref).astype(BF16),
                                  jnp.zeros((CHUNK - r0, dk), BF16)], axis=0)
        elif (not fwd) and r < n_sub - 1:
            ref = c[r0 + SUB - 1:r0 + SUB]
            kr = jnp.concatenate([jnp.zeros((r0 + SUB, dk), BF16),
                                  kscaled(r0 + SUB, CHUNK, ref).astype(BF16)], axis=0)
        else:
            kr = None
        if kr is not None:
            qr = (qb * jnp.exp2(cb - ref)).astype(BF16)
            soff = lax.dot_general(qr, kr, nt, preferred_element_type=F32)
        acc = [jnp.zeros((SUBLANES, CHUNK), F32) for _ in range(half)]
        for jj in range(SUB):
            aj = a[r0 + jj:r0 + jj + 1]
            jg = jj // SUBLANES
            groups = range(jg, half) if fwd else range(0, jg + 1)
            for g in groups:
                e = cb[g * SUBLANES:(g + 1) * SUBLANES] - aj
                if g == jg:
                    e = jnp.minimum(e, 0.0)
                t = qb[g * SUBLANES:(g + 1) * SUBLANES] * jnp.exp2(e)
                if kf is not None:
                    t = t * kf[r0 + jj:r0 + jj + 1]
                sj = jnp.sum(t, axis=1, keepdims=True)
                acc[g] = jnp.where(cst["lane"] == r0 + jj, sj, acc[g])
        d = jnp.concatenate(acc, axis=0)
        rows.append(d if soff is None else d + soff)
    p = jnp.concatenate(rows, axis=0)
    p = jnp.where(cst["mask_f"] if fwd else cst["mask_b"], p, 0.0).astype(BF16)
    o = jnp.dot(p, v, preferred_element_type=F32)
    o = o + lax.dot_general(qd, st.astype(BF16), nt, preferred_element_type=F32)
    return o


def _scan_kernel(*refs, n_heads, dk, dv, has_k, n_ctx_chunks, n_step_chunks):
    n_in = 4 if has_k else 3
    idx = 0
    groups = []
    for _ in range(4):
        groups.append(refs[idx:idx + n_in])
        idx += n_in
    of_ref, ob_ref, st_ref = refs[idx:idx + 3]
    ctx_f, ctx_b, lat_f, lat_b = groups
    s = pl.program_id(2)
    cst = _scan_consts()

    def load(group, r, h):
        rows = pl.ds(r, CHUNK)
        q = group[0][rows, h * dk:(h + 1) * dk]
        if has_k:
            k = group[1][rows, h * dk:(h + 1) * dk]
            v = group[2][rows, h * dv:(h + 1) * dv]
            lf = group[3][rows, h * dk:(h + 1) * dk]
        else:
            k = None
            v = group[1][rows, h * dv:(h + 1) * dv]
            lf = group[2][rows, h * dk:(h + 1) * dk]
        return q, k, v, lf

    def run(gf, gb, n_chunks, with_out):
        def body(ci, carry):
            rf = pl.multiple_of(ci * CHUNK, CHUNK)
            rb = pl.multiple_of((n_chunks - 1 - ci) * CHUNK, CHUNK)
            for h in range(n_heads):
                q, k, v, lf = load(gf, rf, h)
                o = _chunk_unit(q, k, v, lf, st_ref.at[0, h], True, cst, with_out)
                if with_out:
                    of_ref[pl.ds(rf, CHUNK), h * dv:(h + 1) * dv] = o.astype(of_ref.dtype)
                q, k, v, lf = load(gb, rb, h)
                o = _chunk_unit(q, k, v, lf, st_ref.at[1, h], False, cst, with_out)
                if with_out:
                    ob_ref[pl.ds(rb, CHUNK), h * dv:(h + 1) * dv] = o.astype(ob_ref.dtype)
            return carry
        lax.fori_loop(0, n_chunks, body, 0)

    @pl.when(s == 0)
    def _():
        st_ref[...] = jnp.zeros(st_ref.shape, F32)
        run(ctx_f, ctx_b, n_ctx_chunks, False)

    @pl.when(s > 0)
    def _():
        run(lat_f, lat_b, n_step_chunks, True)


def _scan_call(inputs, in_specs, out_struct, out_specs, grid, *, n_heads, dk, dv, has_k,
               n_ctx_chunks, n_step_chunks, name):
    return pl.pallas_call(
        functools.partial(_scan_kernel, n_heads=n_heads, dk=dk, dv=dv, has_k=has_k,
                          n_ctx_chunks=n_ctx_chunks, n_step_chunks=n_step_chunks),
        grid=grid,
        in_specs=in_specs,
        out_specs=out_specs,
        out_shape=out_struct,
        scratch_shapes=[pltpu.VMEM((2, n_heads, dv, dk), F32)],
        compiler_params=pltpu.CompilerParams(
            dimension_semantics=("arbitrary", "arbitrary", "arbitrary"),
            vmem_limit_bytes=VMEM_LIMIT_BYTES),
        name=name,
    )(*inputs)


A_SCAN_HEADS = 2
A_SCAN_CHUNKS = 4


def _scan_a(c_qi, c_lf, qi, lf):
    bsz, n_ctx, _ = c_qi.shape
    seq = qi.shape[1]
    g = A_SCAN_HEADS
    gw = g * A_DK
    n_hg = A_HEADS // g
    ts = A_SCAN_CHUNKS * CHUNK
    n_steps = seq // ts

    def fstep(s):
        return jnp.maximum(s - 1, 0)

    def bstep(s):
        return n_steps - 1 - jnp.maximum(s - 1, 0)

    cspec = lambda off: pl.BlockSpec((None, n_ctx, gw), lambda b, h, s: (b, 0, off + h))
    fspec = lambda off: pl.BlockSpec((None, ts, gw), lambda b, h, s: (b, fstep(s), off + h))
    bspec = lambda off: pl.BlockSpec((None, ts, gw), lambda b, h, s: (b, bstep(s), off + h))
    in_specs = [cspec(0), cspec(n_hg), cspec(0),
                cspec(0), cspec(n_hg), cspec(n_hg),
                fspec(0), fspec(n_hg), fspec(0),
                bspec(0), bspec(n_hg), bspec(n_hg)]
    inputs = [c_qi, c_qi, c_lf, c_qi, c_qi, c_lf, qi, qi, lf, qi, qi, lf]
    out_struct = [jax.ShapeDtypeStruct((bsz, seq, A_HEADS * A_DV), BF16)] * 2
    out_specs = [pl.BlockSpec((None, ts, gw), lambda b, h, s: (b, fstep(s), h)),
                 pl.BlockSpec((None, ts, gw), lambda b, h, s: (b, bstep(s), h))]
    return _scan_call(inputs, in_specs, out_struct, out_specs, (bsz, n_hg, n_steps + 1),
                      n_heads=g, dk=A_DK, dv=A_DV, has_k=False,
                      n_ctx_chunks=n_ctx // CHUNK, n_step_chunks=A_SCAN_CHUNKS, name="scan_hgrn2")


def _scan_b(c_qkv, c_lf, qkv, lf):
    bsz, n_ctx, _ = c_qkv.shape
    n_rows = qkv.shape[2]
    kw = B_HEADS * B_DK

    def fcol(s):
        return jnp.maximum(s - 1, 0)

    def bcol(s):
        return GRID_W - 1 - jnp.maximum(s - 1, 0)

    def cspec(width, off):
        return pl.BlockSpec((None, n_ctx, width), lambda b, h, s: (b, 0, off + h))

    def lspec(width, off, colf):
        return pl.BlockSpec((None, None, n_rows, width), lambda b, h, s: (b, colf(s), 0, off + h))

    k_off, v_off = kw // B_DK, (2 * kw) // B_DV
    in_specs = [cspec(B_DK, 0), cspec(B_DK, k_off), cspec(B_DV, v_off), cspec(B_DK, 0),
                cspec(B_DK, 0), cspec(B_DK, k_off), cspec(B_DV, v_off), cspec(B_DK, B_HEADS),
                lspec(B_DK, 0, fcol), lspec(B_DK, k_off, fcol), lspec(B_DV, v_off, fcol), lspec(B_DK, 0, fcol),
                lspec(B_DK, 0, bcol), lspec(B_DK, k_off, bcol), lspec(B_DV, v_off, bcol),
                lspec(B_DK, B_HEADS, bcol)]
    inputs = [c_qkv, c_qkv, c_qkv, c_lf, c_qkv, c_qkv, c_qkv, c_lf,
              qkv, qkv, qkv, lf, qkv, qkv, qkv, lf]
    vw = B_HEADS * B_DV
    out_struct = [jax.ShapeDtypeStruct((bsz, GRID_W, n_rows, vw), BF16)] * 2
    out_specs = [pl.BlockSpec((None, None, n_rows, B_DV), lambda b, h, s: (b, fcol(s), 0, h)),
                 pl.BlockSpec((None, None, n_rows, B_DV), lambda b, h, s: (b, bcol(s), 0, h))]
    return _scan_call(inputs, in_specs, out_struct, out_specs, (bsz, B_HEADS, GRID_W + 1),
                      n_heads=1, dk=B_DK, dv=B_DV, has_k=True,
                      n_ctx_chunks=n_ctx // CHUNK, n_step_chunks=n_rows // CHUNK, name="scan_gla")


GLA_FIN_ROWS = 16


def _gla_finalize_kernel(of_ref, ob_ref, sg_ref, g_ref, o_ref, t_ref):
    n_rows = of_ref.shape[1]
    n_planes = t_ref.shape[0]
    g = g_ref[...]
    for col in range(GRID_W):
        o = of_ref[col].astype(F32) + ob_ref[col].astype(F32)
        ms = jnp.mean(o * o, axis=-1, keepdims=True)
        on = o * lax.rsqrt(ms + EPS) * g
        for l in range(n_planes):
            t_ref[l, pl.ds(col, n_rows, stride=GRID_W), :] = on[:, l * LANES:(l + 1) * LANES]
    for l in range(n_planes):
        lanes = slice(l * LANES, (l + 1) * LANES)
        o_ref[:, lanes] = (t_ref[l] * sg_ref[:, lanes].astype(F32)).astype(o_ref.dtype)


def _gla_finalize(of, ob, gates, gain, *, sg_col_off):
    bsz, _, n_rows, vw = of.shape
    tm = GLA_FIN_ROWS * GRID_W
    tiles = n_rows // GLA_FIN_ROWS
    cm = pl.BlockSpec((None, GRID_W, GLA_FIN_ROWS, B_DV), lambda b, r, h: (b, 0, r, h))
    return pl.pallas_call(
        _gla_finalize_kernel,
        grid=(bsz, tiles, B_HEADS),
        in_specs=[cm, cm,
                  pl.BlockSpec((tm, B_DV), lambda b, r, h: (b * tiles + r, sg_col_off + h)),
                  pl.BlockSpec((1, B_DV), lambda b, r, h: (0, h))],
        out_specs=pl.BlockSpec((tm, B_DV), lambda b, r, h: (b * tiles + r, h)),
        out_shape=jax.ShapeDtypeStruct((bsz * n_rows * GRID_W, vw), BF16),
        scratch_shapes=[pltpu.VMEM((B_DV // LANES, tm, LANES), F32)],
        compiler_params=pltpu.CompilerParams(
            dimension_semantics=("arbitrary", "arbitrary", "arbitrary"),
            vmem_limit_bytes=32 * 1024 * 1024),
        name="gla_finalize",
    )(of, ob, gates, gain)


def _merge_kernel(oaf, oab, obp, sga, sma, smb, ga, wpa, wpb, y_ref):
    o = oaf[...].astype(F32) + oab[...].astype(F32)
    parts = []
    for h in range(A_HEADS):
        oh = o[:, h * A_DV:(h + 1) * A_DV]
        ms = jnp.mean(oh * oh, axis=-1, keepdims=True)
        parts.append(oh * lax.rsqrt(ms + EPS))
    oa = (jnp.concatenate(parts, axis=1) * ga[...] * sga[...].astype(F32)).astype(BF16)
    ya = jnp.dot(oa, wpa[...], preferred_element_type=F32)
    yb = jnp.dot(obp[...], wpb[...], preferred_element_type=F32)
    y_ref[...] = (sma[...].astype(F32) * ya + smb[...].astype(F32) * yb).astype(BF16)


def _merge(oaf, oab, obp, gates, ga, wpa, wpb, *, tm):
    n_tok, d = oaf.shape
    tok = lambda off: pl.BlockSpec((tm, d), lambda i: (i, off))
    const = lambda shape: pl.BlockSpec(shape, lambda i: (0, 0), pipeline_mode=pl.Buffered(1))
    return pl.pallas_call(
        _merge_kernel,
        grid=(n_tok // tm,),
        in_specs=[tok(0), tok(0), tok(0), tok(0), tok(2), tok(3),
                  const((1, d)), const(wpa.shape), const(wpb.shape)],
        out_specs=pl.BlockSpec((tm, wpa.shape[1]), lambda i: (i, 0)),
        out_shape=jax.ShapeDtypeStruct((n_tok, wpa.shape[1]), BF16),
        compiler_params=pltpu.CompilerParams(
            dimension_semantics=("arbitrary",), vmem_limit_bytes=VMEM_LIMIT_BYTES),
        name="merge_proj",
    )(oaf, oab, obp, gates, gates, gates, ga, wpa, wpb)


def _final_kernel(y_ref, x_ref, mod_ref, w_ref, g_ref, o_ref):
    yo = jnp.dot(y_ref[...], w_ref[...], preferred_element_type=F32)
    z = x_ref[...] + mod_ref[0, 2:3, :] * yo
    ms = jnp.mean(z * z, axis=-1, keepdims=True)
    o_ref[...] = z * lax.rsqrt(ms + EPS) * g_ref[...]


def _final(y, x2d, mod3, w_out, fg, *, tm, tiles_per_batch):
    n_tok, d = x2d.shape
    const = lambda shape: pl.BlockSpec(shape, lambda i: (0, 0), pipeline_mode=pl.Buffered(1))
    return pl.pallas_call(
        _final_kernel,
        grid=(n_tok // tm,),
        in_specs=[pl.BlockSpec((tm, d), lambda i: (i, 0)),
                  pl.BlockSpec((tm, d), lambda i: (i, 0)),
                  pl.BlockSpec((1, 3, d), lambda i: (i // tiles_per_batch, 0, 0)),
                  const(w_out.shape), const((1, d))],
        out_specs=pl.BlockSpec((tm, d), lambda i: (i, 0)),
        out_shape=jax.ShapeDtypeStruct((n_tok, d), F32),
        compiler_params=pltpu.CompilerParams(
            dimension_semantics=("arbitrary",), vmem_limit_bytes=VMEM_LIMIT_BYTES),
        name="out_proj_final",
    )(y, x2d, mod3, w_out, fg)


def kernel(x, c, ctx, c_ctx, w_ada, b_ada, norm_g, w_in, hgrn_lb_logits, gla_w_gk, gla_b_gk,
           hgrn_onorm_g, gla_onorm_g, w_pa, w_pb, w_out, final_norm_g):
    bsz, seq, d = x.shape
    n_ctx = ctx.shape[1]
    depth = w_in.shape[0]
    assert depth == 1, "single-layer trunk"
    a_kw, a_vw = A_HEADS * A_DK, A_HEADS * A_DV
    b_kw, b_vw = B_HEADS * B_DK, B_HEADS * B_DV

    n_rows = -(-(bsz + 1) // SUBLANES) * SUBLANES
    cvec = jnp.zeros((n_rows, d), F32).at[:bsz].set(c).at[bsz].set(c_ctx)
    mod = _adaln(cvec, w_ada[0], b_ada[0].reshape(1, -1))
    mod3 = mod.reshape(n_rows, 3, d)

    lb = _lower_bounds(hgrn_lb_logits)[0:1]

    w = w_in[0]
    o_ag = 2 * a_kw + 2 * a_vw
    o_bq = o_ag + a_vw
    o_br = o_bq + 2 * b_kw + b_vw
    o_bg = o_br + 2 * B_RANK
    w_main = jnp.concatenate(
        [w[:, 0:a_kw + a_vw], w[:, a_kw + a_vw:o_ag], w[:, o_bq:o_br], w[:, o_ag:o_bq], w[:, o_bg:]],
        axis=1).astype(BF16)
    w_r = jnp.pad(w[:, o_br:o_bg], ((0, 0), (0, LANES - 2 * B_RANK))).astype(BF16)
    gk_pad = jnp.zeros((LANES, 2 * b_kw), F32)
    gk_pad = gk_pad.at[0:B_RANK, 0:b_kw].set(gla_w_gk[0, 0]).at[B_RANK:2 * B_RANK, b_kw:].set(gla_w_gk[0, 1])
    gk_pad = gk_pad.astype(BF16)
    gkb = gla_b_gk[0].reshape(1, 2 * b_kw)
    ng = norm_g[0].reshape(1, d)

    x2d = x.reshape(bsz * seq, d)
    ctx2d = ctx.reshape(bsz * n_ctx, d)
    tm = 1024
    tiles_per_batch = seq // tm

    c_aqi, c_alf, c_bqkv, c_blf = _inproj(
        ctx2d, mod3, lambda i: bsz, ng, w_main, w_r, lb, gk_pad, gkb,
        tm=bsz * n_ctx, latent=False)
    aqi, alf, bqkv, blf, gates = _inproj(
        x2d, mod3, lambda i: i // tiles_per_batch, ng, w_main, w_r, lb, gk_pad, gkb,
        tm=tm, latent=True, n_batch=bsz)

    r3 = lambda t, n: t.reshape(bsz, n, t.shape[-1])
    oaf, oab = _scan_a(r3(c_aqi, n_ctx), r3(c_alf, n_ctx), r3(aqi, seq), r3(alf, seq))
    obf, obb = _scan_b(r3(c_bqkv, n_ctx), r3(c_blf, n_ctx), bqkv, blf)
    obp = _gla_finalize(obf, obb, gates, gla_onorm_g[0].reshape(1, -1), sg_col_off=a_vw // B_DV)

    r2 = lambda t: t.reshape(bsz * seq, t.shape[-1])
    y = _merge(r2(oaf), r2(oab), obp, gates, hgrn_onorm_g[0].reshape(1, -1),
               w_pa[0].astype(BF16), w_pb[0].astype(BF16), tm=256)
    out = _final(y, x2d, mod3, w_out[0].astype(BF16), final_norm_g.reshape(1, d),
                 tm=512, tiles_per_batch=seq // 512)
    return out.reshape(bsz, seq, d)
```

```python
import functools

import jax
import jax.numpy as jnp
from jax import lax
from jax.experimental import pallas as pl
from jax.experimental.pallas import tpu as pltpu

F32 = jnp.float32
BF16 = jnp.bfloat16

CHUNK = 64
SUB = 16
GRID_W = 64
EPS = 1e-6
A_HEADS, A_DK, A_DV = 16, 128, 128
B_HEADS, B_DK, B_DV = 4, 256, 512
B_RANK = 16
GATE_NORMALIZER = 16.0
LOG2E = 1.4426950408889634

VMEM_LIMIT_BYTES = 56 * 1024 * 1024
LANES = 128
SUBLANES = 8

NT_DIMS = (((1,), (1,)), ((), ()))
TN_DIMS = (((0,), (0,)), ((), ()))


def _sigmoid(z):
    return 1.0 / (1.0 + jnp.exp(-z))


def _adaln_kernel(c_ref, w_ref, b_ref, o_ref):
    c = c_ref[...]
    s = c * _sigmoid(c)
    o_ref[...] = jnp.dot(s, w_ref[...], preferred_element_type=F32,
                         precision=lax.Precision.HIGHEST) + b_ref[...]


def _adaln(cvec, w, b):
    rows, d = cvec.shape
    n = w.shape[1]
    tn = 768
    return pl.pallas_call(
        _adaln_kernel,
        grid=(n // tn,),
        in_specs=[pl.BlockSpec((rows, d), lambda j: (0, 0)),
                  pl.BlockSpec((d, tn), lambda j: (0, j)),
                  pl.BlockSpec((1, tn), lambda j: (0, j))],
        out_specs=pl.BlockSpec((rows, tn), lambda j: (0, j)),
        out_shape=jax.ShapeDtypeStruct((rows, n), F32),
        compiler_params=pltpu.CompilerParams(
            dimension_semantics=("arbitrary",), vmem_limit_bytes=32 * 1024 * 1024),
        name="adaln",
    )(cvec, w, b)


def _lower_bound_kernel(l_ref, o_ref):
    x = l_ref[...]
    n_rows = x.shape[0]
    m = jnp.max(x, axis=0, keepdims=True)
    e = jnp.exp(x - m)
    tot = jnp.sum(e, axis=0, keepdims=True)
    run = jnp.zeros_like(tot)
    for r in range(n_rows - 1):
        run = run + e[r:r + 1]
        o_ref[r:r + 1, :] = run / tot


def _lower_bounds(logits):
    n_rows, w = logits.shape
    return pl.pallas_call(
        _lower_bound_kernel,
        out_shape=jax.ShapeDtypeStruct((n_rows - 1, w), F32),
        name="hgrn_lower_bounds",
    )(logits)


TN = 512
NORM_ROWS = 128
T_AQI = 4096 // TN
T_ALF = 4096 // TN
T_BQKV = 4096 // TN
T_BLF = 2048 // TN
T_GATES = 8192 // TN
J_ALF = T_AQI
J_BQKV = J_ALF + T_ALF
J_BLF = J_BQKV + T_BQKV
J_GATES = J_BLF + T_BLF
J_END = J_GATES + T_GATES


def _inproj_kernel(x_ref, mod_ref, ng_ref, w_ref, wr_ref, lb_ref, gkw_ref, gkb_ref, *rest, latent):
    if latent:
        aqi_ref, alf_ref, bqkv_ref, blf_ref, gates_ref, h_ref, r_ref = rest
    else:
        aqi_ref, alf_ref, bqkv_ref, blf_ref, h_ref, r_ref = rest
        gates_ref = None
    j = pl.program_id(1)

    def store_gla(out_ref, z):
        if not latent:
            out_ref[...] = z.astype(out_ref.dtype)
            return
        n_rows = z.shape[0] // GRID_W
        zc = jnp.swapaxes(z.reshape(n_rows, GRID_W, z.shape[1]), 0, 1)
        out_ref[...] = zc.astype(out_ref.dtype)

    @pl.when(j == 0)
    def _():
        def norm_rows(t, carry):
            rows = pl.ds(pl.multiple_of(t * NORM_ROWS, NORM_ROWS), NORM_ROWS)
            x = x_ref[rows, :]
            ms = jnp.mean(x * x, axis=-1, keepdims=True)
            y = x * lax.rsqrt(ms + EPS) * ng_ref[...]
            h = (y * (1.0 + mod_ref[0, 1:2, :]) + mod_ref[0, 0:1, :]).astype(BF16)
            h_ref[rows, :] = h
            r_ref[rows, :] = jnp.dot(h, wr_ref[...], preferred_element_type=F32).astype(BF16)
            return carry
        lax.fori_loop(0, x_ref.shape[0] // NORM_ROWS, norm_rows, 0)

    def mm():
        return jnp.dot(h_ref[...], w_ref[...], preferred_element_type=F32)

    @pl.when(j < J_ALF)
    def _():
        aqi_ref[...] = mm().astype(BF16)

    @pl.when((j >= J_ALF) & (j < J_BQKV))
    def _():
        lb = lb_ref[...]
        f = lb + (1.0 - lb) * _sigmoid(mm())
        alf_ref[...] = jnp.log2(f)

    @pl.when((j >= J_BQKV) & (j < J_BLF))
    def _():
        scale = jnp.where(j < J_BQKV + (B_HEADS * B_DK) // TN, B_DK ** -0.5, 1.0)
        store_gla(bqkv_ref, mm() * scale)

    @pl.when((j >= J_BLF) & (j < J_GATES))
    def _():
        z = jnp.dot(r_ref[...], gkw_ref[...], preferred_element_type=F32) + gkb_ref[...]
        ls = jnp.minimum(z, 0.0) - jnp.log(1.0 + jnp.exp(-jnp.abs(z)))
        store_gla(blf_ref, ls * (LOG2E / GATE_NORMALIZER))

    if latent:
        @pl.when(j >= J_GATES)
        def _():
            z = mm()
            sg = _sigmoid(z)
            gates_ref[...] = (sg * jnp.where(j < J_GATES + T_GATES // 2, z, 1.0)).astype(BF16)


def _inproj(x2d, mod3, mod_row_of_tile, norm_g, w_main, w_r, lb, gk_pad, gkb, *, tm, latent, n_batch=1):
    n_tok, d = x2d.shape
    n_j = J_END if latent else J_GATES
    tile_rows = tm // GRID_W
    seq_rows = n_tok // n_batch // GRID_W
    tiles_per_batch = n_tok // n_batch // tm

    def cl(j, lo, n):
        return jnp.clip(j - lo, 0, n - 1)

    def gla_spec(j_lo, n):
        if not latent:
            return pl.BlockSpec((tm, TN), lambda i, j: (i, cl(j, j_lo, n)))
        return pl.BlockSpec((None, GRID_W, tile_rows, TN),
                            lambda i, j: (i // tiles_per_batch, 0, i % tiles_per_batch, cl(j, j_lo, n)))

    def gla_shape(n, dtype):
        if not latent:
            return jax.ShapeDtypeStruct((n_tok, n * TN), dtype)
        return jax.ShapeDtypeStruct((n_batch, GRID_W, seq_rows, n * TN), dtype)

    in_specs = [
        pl.BlockSpec((tm, d), lambda i, j: (i, 0)),
        pl.BlockSpec((1, 3, d), lambda i, j: (mod_row_of_tile(i), 0, 0)),
        pl.BlockSpec((1, d), lambda i, j: (0, 0)),
        pl.BlockSpec((d, TN), lambda i, j: (0, j - jnp.clip(j - (J_BLF - 1), 0, T_BLF))),
        pl.BlockSpec((d, LANES), lambda i, j: (0, 0)),
        pl.BlockSpec((1, TN), lambda i, j: (0, cl(j, J_ALF, T_ALF))),
        pl.BlockSpec((LANES, TN), lambda i, j: (0, cl(j, J_BLF, T_BLF))),
        pl.BlockSpec((1, TN), lambda i, j: (0, cl(j, J_BLF, T_BLF))),
    ]
    out_specs = [
        pl.BlockSpec((tm, TN), lambda i, j: (i, cl(j, 0, T_AQI))),
        pl.BlockSpec((tm, TN), lambda i, j: (i, cl(j, J_ALF, T_ALF))),
        gla_spec(J_BQKV, T_BQKV),
        gla_spec(J_BLF, T_BLF),
    ]
    out_shape = [
        jax.ShapeDtypeStruct((n_tok, T_AQI * TN), BF16),
        jax.ShapeDtypeStruct((n_tok, T_ALF * TN), F32),
        gla_shape(T_BQKV, BF16),
        gla_shape(T_BLF, F32),
    ]
    if latent:
        out_specs.append(pl.BlockSpec((tm, TN), lambda i, j: (i, cl(j, J_GATES, T_GATES))))
        out_shape.append(jax.ShapeDtypeStruct((n_tok, T_GATES * TN), BF16))
    return pl.pallas_call(
        functools.partial(_inproj_kernel, latent=latent),
        grid=(n_tok // tm, n_j),
        in_specs=in_specs,
        out_specs=out_specs,
        out_shape=out_shape,
        scratch_shapes=[pltpu.VMEM((tm, d), BF16), pltpu.VMEM((tm, LANES), BF16)],
        compiler_params=pltpu.CompilerParams(
            dimension_semantics=("arbitrary", "arbitrary"), vmem_limit_bytes=VMEM_LIMIT_BYTES),
        name="inproj_latent" if latent else "inproj_ctx",
    )(x2d, mod3, norm_g, w_main, w_r, lb, gk_pad, gkb)


def _scan_consts():
    row = lax.broadcasted_iota(jnp.int32, (CHUNK, CHUNK), 0)
    col = lax.broadcasted_iota(jnp.int32, (CHUNK, CHUNK), 1)
    return dict(
        mask_f=col <= row,
        mask_b=col >= row,
        tri_f=jnp.where(col <= row, 1.0, 0.0).astype(BF16),
        tri_b=jnp.where(col >= row, 1.0, 0.0).astype(BF16),
        lane=lax.broadcasted_iota(jnp.int32, (SUBLANES, CHUNK), 1),
    )


def _cumsum_stage(lf, fwd, cst):
    dk = lf.shape[1]
    tri = cst["tri_f"] if fwd else cst["tri_b"]
    hi = lf.astype(BF16)
    lo = (lf - hi.astype(F32)).astype(BF16)
    cc = jnp.dot(tri, jnp.concatenate([hi, lo], axis=1), preferred_element_type=F32)
    return cc[:, :dk] + cc[:, dk:]


def _intra_stage(q, k, v, lf, c, st_ref, fwd, cst, keep):
    dk = lf.shape[1]
    tot = c[CHUNK - 1:CHUNK] if fwd else c[0:1]

    if k is None:
        a = c - jnp.log2(1.0 - jnp.exp2(lf))
        kf = None

        def kscaled(r0, r1, ref):
            return jnp.exp2(ref - a[r0:r1])
    else:
        a = c
        kf = k.astype(F32)

        def kscaled(r0, r1, ref):
            return kf[r0:r1] * jnp.exp2(ref - c[r0:r1])

    kte = kscaled(0, CHUNK, tot).astype(BF16)
    st = st_ref[...]
    st_ref[...] = st * jnp.exp2(tot) + lax.dot_general(v, kte, TN_DIMS, preferred_element_type=F32)
    if keep is None:
        return
    yield

    qf = q.astype(F32)
    qd = (qf * jnp.exp2(c)).astype(BF16)
    n_sub = CHUNK // SUB
    half = SUB // SUBLANES
    soffs = []
    for r in range(n_sub):
        r0 = SUB * r
        if fwd and r > 0:
            ref = c[r0:r0 + 1]
            kr = jnp.concatenate([kscaled(0, r0, ref).astype(BF16),
                                  jnp.zeros((CHUNK - r0, dk), BF16)], axis=0)
        elif (not fwd) and r < n_sub - 1:
            ref = c[r0 + SUB - 1:r0 + SUB]
            kr = jnp.concatenate([jnp.zeros((r0 + SUB, dk), BF16),
                                  kscaled(r0 + SUB, CHUNK, ref).astype(BF16)], axis=0)
        else:
            soffs.append(None)
            continue
        qr = (qf[r0:r0 + SUB] * jnp.exp2(c[r0:r0 + SUB] - ref)).astype(BF16)
        soffs.append(lax.dot_general(qr, kr, NT_DIMS, preferred_element_type=F32))
    yield

    rows = []
    for r in range(n_sub):
        r0 = SUB * r
        cb = c[r0:r0 + SUB]
        qb = qf[r0:r0 + SUB]
        acc = [jnp.zeros((SUBLANES, CHUNK), F32) for _ in range(half)]
        for jj in range(SUB):
            aj = a[r0 + jj:r0 + jj + 1]
            jg = jj // SUBLANES
            groups = range(jg, half) if fwd else range(0, jg + 1)
            for g in groups:
                e = cb[g * SUBLANES:(g + 1) * SUBLANES] - aj
                if g == jg:
                    e = jnp.minimum(e, 0.0)
                t = qb[g * SUBLANES:(g + 1) * SUBLANES] * jnp.exp2(e)
                if kf is not None:
                    t = t * kf[r0 + jj:r0 + jj + 1]
                sj = jnp.sum(t, axis=1, keepdims=True)
                acc[g] = jnp.where(cst["lane"] == r0 + jj, sj, acc[g])
        d = jnp.concatenate(acc, axis=0)
        rows.append(d if soffs[r] is None else d + soffs[r])
    p = jnp.concatenate(rows, axis=0)
    p = jnp.where(cst["mask_f"] if fwd else cst["mask_b"], p, 0.0).astype(BF16)
    keep(p, qd, st.astype(BF16))


def _run_staged(units):
    live = list(units)
    while live:
        nxt = []
        for u in live:
            try:
                next(u)
                nxt.append(u)
            except StopIteration:
                pass
        live = nxt


def _scan_kernel(*refs, n_heads, dk, dv, has_k, n_ctx_chunks, n_step_chunks, lat_index):
    n_ctx_in = 4 if has_k else 3
    n_lat_in = 4 if has_k else 3
    ctx_refs = refs[:n_ctx_in]
    lat_refs = (refs[n_ctx_in:n_ctx_in + n_lat_in], refs[n_ctx_in + n_lat_in:n_ctx_in + 2 * n_lat_in])
    rest = refs[n_ctx_in + 2 * n_lat_in:]
    out_refs = rest[0:2]
    st_ref, c_ref, p_ref, qd_ref, stb_ref = rest[2:]
    s = pl.program_id(2)
    cst = _scan_consts()
    units = [(h, d) for h in range(n_heads) for d in range(2)]

    def kcols(h):
        return slice(h * dk, (h + 1) * dk)

    def vcols(h):
        return slice(h * dv, (h + 1) * dv)

    def ctx_step():
        st_ref[...] = jnp.zeros(st_ref.shape, F32)
        p_ref[...] = jnp.zeros(p_ref.shape, BF16)
        qd_ref[...] = jnp.zeros(qd_ref.shape, BF16)
        stb_ref[...] = jnp.zeros(stb_ref.shape, BF16)
        if has_k:
            k_ref, v_ref, lff_ref, lfb_ref = ctx_refs
        else:
            v_ref, lff_ref, lfb_ref = ctx_refs
            k_ref = None

        def unit(h, d, rows):
            lf = (lff_ref, lfb_ref)[d][rows, kcols(h)]
            c = _cumsum_stage(lf, d == 0, cst)
            yield
            k = None if k_ref is None else k_ref[rows, kcols(h)]
            yield from _intra_stage(None, k, v_ref[rows, vcols(h)], lf, c, st_ref.at[d, h], d == 0, cst, None)

        def body(i, carry):
            rows = (pl.ds(pl.multiple_of(i * CHUNK, CHUNK), CHUNK),
                    pl.ds(pl.multiple_of((n_ctx_chunks - 1 - i) * CHUNK, CHUNK), CHUNK))
            _run_staged([unit(h, d, rows[d]) for h, d in units])
            return carry
        lax.fori_loop(0, n_ctx_chunks, body, 0)

    def latent_step():
        n = n_step_chunks

        def index(d, pos):
            return lat_index(pos if d == 0 else n - 1 - pos)

        def read(d, pos, h, what):
            refs_d = lat_refs[d]
            ref = refs_d[{"q": 0, "k": 1, "v": 2 if has_k else 1, "lf": 3 if has_k else 2}[what]]
            cols = vcols(h) if what == "v" else kcols(h)
            return ref[index(d, pos) + (cols,)]

        def cumsum_all(pos, slot):
            for u, (h, d) in enumerate(units):
                c_ref[slot, u] = _cumsum_stage(read(d, pos, h, "lf"), d == 0, cst)

        def output_all(pos):
            for u, (h, d) in enumerate(units):
                o = jnp.dot(p_ref[u], read(d, pos, h, "v"), preferred_element_type=F32)
                o = o + lax.dot_general(qd_ref[u], stb_ref[u], NT_DIMS, preferred_element_type=F32)
                out_refs[d][index(d, pos) + (vcols(h),)] = o.astype(out_refs[d].dtype)

        def intra_all(pos, cs):
            def keeper(u):
                def keep(p, qd, stb):
                    p_ref[u] = p
                    qd_ref[u] = qd
                    stb_ref[u] = stb
                return keep
            gens = []
            for u, (h, d) in enumerate(units):
                k = read(d, pos, h, "k") if has_k else None
                gens.append(_intra_stage(read(d, pos, h, "q"), k, read(d, pos, h, "v"),
                                         read(d, pos, h, "lf"), cs[u], st_ref.at[d, h],
                                         d == 0, cst, keeper(u)))
            _run_staged(gens)

        def body(i, carry):
            cumsum_all(i, 0)
            cs = [c_ref[0, u] for u in range(len(units))]
            intra_all(i, cs)
            output_all(i)
            return carry
        lax.fori_loop(0, n, body, 0)

    @pl.when(s == 0)
    def _():
        ctx_step()

    @pl.when(s > 0)
    def _():
        latent_step()


def _scan_call(inputs, in_specs, out_struct, out_specs, grid, *, n_heads, dk, dv, has_k,
               n_ctx_chunks, n_step_chunks, lat_index, name):
    n_units = 2 * n_heads
    return pl.pallas_call(
        functools.partial(_scan_kernel, n_heads=n_heads, dk=dk, dv=dv, has_k=has_k,
                          n_ctx_chunks=n_ctx_chunks, n_step_chunks=n_step_chunks, lat_index=lat_index),
        grid=grid,
        in_specs=in_specs,
        out_specs=out_specs,
        out_shape=out_struct,
        scratch_shapes=[pltpu.VMEM((2, n_heads, dv, dk), F32),
                        pltpu.VMEM((2, n_units, CHUNK, dk), F32),
                        pltpu.VMEM((n_units, CHUNK, CHUNK), BF16),
                        pltpu.VMEM((n_units, CHUNK, dk), BF16),
                        pltpu.VMEM((n_units, dv, dk), BF16)],
        compiler_params=pltpu.CompilerParams(
            dimension_semantics=("arbitrary", "arbitrary", "arbitrary"),
            vmem_limit_bytes=VMEM_LIMIT_BYTES),
        name=name,
    )(*inputs)


A_SCAN_HEADS = 2
A_SCAN_CHUNKS = 8


def _scan_a(c_qi, c_lf, qi, lf):
    bsz, n_ctx, _ = c_qi.shape
    seq = qi.shape[1]
    g = A_SCAN_HEADS
    gw = g * A_DK
    n_hg = A_HEADS // g
    ts = A_SCAN_CHUNKS * CHUNK
    n_steps = seq // ts

    def fstep(s):
        return jnp.maximum(s - 1, 0)

    def bstep(s):
        return n_steps - 1 - jnp.maximum(s - 1, 0)

    cspec = lambda off: pl.BlockSpec((None, n_ctx, gw), lambda b, h, s: (b, 0, off + h))
    fspec = lambda off: pl.BlockSpec((None, ts, gw), lambda b, h, s: (b, fstep(s), off + h))
    bspec = lambda off: pl.BlockSpec((None, ts, gw), lambda b, h, s: (b, bstep(s), off + h))
    in_specs = [cspec(n_hg), cspec(0), cspec(n_hg),
                fspec(0), fspec(n_hg), fspec(0),
                bspec(0), bspec(n_hg), bspec(n_hg)]
    inputs = [c_qi, c_lf, c_lf, qi, qi, lf, qi, qi, lf]
    out_struct = [jax.ShapeDtypeStruct((bsz, seq, A_HEADS * A_DV), BF16)] * 2
    out_specs = [pl.BlockSpec((None, ts, gw), lambda b, h, s: (b, fstep(s), h)),
                 pl.BlockSpec((None, ts, gw), lambda b, h, s: (b, bstep(s), h))]

    def lat_index(chunk):
        row = chunk * CHUNK
        if not isinstance(row, int):
            row = pl.multiple_of(row, CHUNK)
        return (pl.ds(row, CHUNK),)

    return _scan_call(inputs, in_specs, out_struct, out_specs, (bsz, n_hg, n_steps + 1),
                      n_heads=g, dk=A_DK, dv=A_DV, has_k=False,
                      n_ctx_chunks=n_ctx // CHUNK, n_step_chunks=A_SCAN_CHUNKS,
                      lat_index=lat_index, name="scan_hgrn2")


B_SCAN_COLS = 4


def _scan_b(c_qkv, c_lf, qkv, lf):
    bsz, n_ctx, _ = c_qkv.shape
    n_rows = qkv.shape[2]
    kw = B_HEADS * B_DK
    nc = B_SCAN_COLS
    n_steps = GRID_W // nc
    chunks_per_col = n_rows // CHUNK

    def fblk(s):
        return jnp.maximum(s - 1, 0)

    def bblk(s):
        return n_steps - 1 - jnp.maximum(s - 1, 0)

    def cspec(width, off):
        return pl.BlockSpec((None, n_ctx, width), lambda b, h, s: (b, 0, off + h))

    def lspec(width, off, blk):
        return pl.BlockSpec((None, nc, n_rows, width), lambda b, h, s: (b, blk(s), 0, off + h))

    k_off, v_off = kw // B_DK, (2 * kw) // B_DV
    in_specs = [cspec(B_DK, k_off), cspec(B_DV, v_off), cspec(B_DK, 0), cspec(B_DK, B_HEADS),
                lspec(B_DK, 0, fblk), lspec(B_DK, k_off, fblk), lspec(B_DV, v_off, fblk), lspec(B_DK, 0, fblk),
                lspec(B_DK, 0, bblk), lspec(B_DK, k_off, bblk), lspec(B_DV, v_off, bblk),
                lspec(B_DK, B_HEADS, bblk)]
    inputs = [c_qkv, c_qkv, c_lf, c_lf, qkv, qkv, qkv, lf, qkv, qkv, qkv, lf]
    vw = B_HEADS * B_DV
    out_struct = [jax.ShapeDtypeStruct((bsz, GRID_W, n_rows, vw), BF16)] * 2
    out_specs = [pl.BlockSpec((None, nc, n_rows, B_DV), lambda b, h, s: (b, fblk(s), 0, h)),
                 pl.BlockSpec((None, nc, n_rows, B_DV), lambda b, h, s: (b, bblk(s), 0, h))]

    def lat_index(chunk):
        col = chunk // chunks_per_col
        row = (chunk - col * chunks_per_col) * CHUNK
        if not isinstance(row, int):
            row = pl.multiple_of(row, CHUNK)
        return (col, pl.ds(row, CHUNK))

    return _scan_call(inputs, in_specs, out_struct, out_specs, (bsz, B_HEADS, n_steps + 1),
                      n_heads=1, dk=B_DK, dv=B_DV, has_k=True,
                      n_ctx_chunks=n_ctx // CHUNK, n_step_chunks=nc * chunks_per_col,
                      lat_index=lat_index, name="scan_gla")


GLA_FIN_ROWS = 16


def _gla_finalize_kernel(of_ref, ob_ref, sg_ref, g_ref, o_ref):
    n_cols, n_rows, w = of_ref.shape
    o = of_ref[...].astype(F32) + ob_ref[...].astype(F32)
    ms = jnp.mean(o * o, axis=-1, keepdims=True)
    on = o * lax.rsqrt(ms + EPS) * g_ref[...]
    on = jnp.swapaxes(on, 0, 1).reshape(n_rows * n_cols, w)
    o_ref[...] = (on * sg_ref[...].astype(F32)).astype(o_ref.dtype)


def _gla_finalize(of, ob, gates, gain, *, sg_col_off):
    bsz, _, n_rows, vw = of.shape
    tm = GLA_FIN_ROWS * GRID_W
    tiles = n_rows // GLA_FIN_ROWS
    cm = pl.BlockSpec((None, GRID_W, GLA_FIN_ROWS, B_DV), lambda b, r, h: (b, 0, r, h))
    return pl.pallas_call(
        _gla_finalize_kernel,
        grid=(bsz, tiles, B_HEADS),
        in_specs=[cm, cm,
                  pl.BlockSpec((tm, B_DV), lambda b, r, h: (b * tiles + r, sg_col_off + h)),
                  pl.BlockSpec((1, B_DV), lambda b, r, h: (0, h))],
        out_specs=pl.BlockSpec((tm, B_DV), lambda b, r, h: (b * tiles + r, h)),
        out_shape=jax.ShapeDtypeStruct((bsz * n_rows * GRID_W, vw), BF16),
        compiler_params=pltpu.CompilerParams(
            dimension_semantics=("arbitrary", "arbitrary", "arbitrary"),
            vmem_limit_bytes=32 * 1024 * 1024),
        name="gla_finalize",
    )(of, ob, gates, gain)


def _merge_kernel(oaf, oab, obp, sga, sma, smb, ga, wpa, wpb, y_ref):
    o = oaf[...].astype(F32) + oab[...].astype(F32)
    parts = []
    for h in range(A_HEADS):
        oh = o[:, h * A_DV:(h + 1) * A_DV]
        ms = jnp.mean(oh * oh, axis=-1, keepdims=True)
        parts.append(oh * lax.rsqrt(ms + EPS))
    oa = (jnp.concatenate(parts, axis=1) * ga[...] * sga[...].astype(F32)).astype(BF16)
    ya = jnp.dot(oa, wpa[...], preferred_element_type=F32)
    yb = jnp.dot(obp[...], wpb[...], preferred_element_type=F32)
    y_ref[...] = (sma[...].astype(F32) * ya + smb[...].astype(F32) * yb).astype(BF16)


def _merge(oaf, oab, obp, gates, ga, wpa, wpb, *, tm):
    n_tok, d = oaf.shape
    tok = lambda off: pl.BlockSpec((tm, d), lambda i: (i, off))
    const = lambda shape: pl.BlockSpec(shape, lambda i: (0, 0), pipeline_mode=pl.Buffered(1))
    return pl.pallas_call(
        _merge_kernel,
        grid=(n_tok // tm,),
        in_specs=[tok(0), tok(0), tok(0), tok(0), tok(2), tok(3),
                  const((1, d)), const(wpa.shape), const(wpb.shape)],
        out_specs=pl.BlockSpec((tm, wpa.shape[1]), lambda i: (i, 0)),
        out_shape=jax.ShapeDtypeStruct((n_tok, wpa.shape[1]), BF16),
        compiler_params=pltpu.CompilerParams(
            dimension_semantics=("arbitrary",), vmem_limit_bytes=VMEM_LIMIT_BYTES),
        name="merge_proj",
    )(oaf, oab, obp, gates, gates, gates, ga, wpa, wpb)


def _final_kernel(y_ref, x_ref, mod_ref, w_ref, g_ref, o_ref):
    yo = jnp.dot(y_ref[...], w_ref[...], preferred_element_type=F32)
    z = x_ref[...] + mod_ref[0, 2:3, :] * yo
    ms = jnp.mean(z * z, axis=-1, keepdims=True)
    o_ref[...] = z * lax.rsqrt(ms + EPS) * g_ref[...]


def _final(y, x2d, mod3, w_out, fg, *, tm, tiles_per_batch):
    n_tok, d = x2d.shape
    const = lambda shape: pl.BlockSpec(shape, lambda i: (0, 0), pipeline_mode=pl.Buffered(1))
    return pl.pallas_call(
        _final_kernel,
        grid=(n_tok // tm,),
        in_specs=[pl.BlockSpec((tm, d), lambda i: (i, 0)),
                  pl.BlockSpec((tm, d), lambda i: (i, 0)),
                  pl.BlockSpec((1, 3, d), lambda i: (i // tiles_per_batch, 0, 0)),
                  const(w_out.shape), const((1, d))],
        out_specs=pl.BlockSpec((tm, d), lambda i: (i, 0)),
        out_shape=jax.ShapeDtypeStruct((n_tok, d), F32),
        compiler_params=pltpu.CompilerParams(
            dimension_semantics=("arbitrary",), vmem_limit_bytes=VMEM_LIMIT_BYTES),
        name="out_proj_final",
    )(y, x2d, mod3, w_out, fg)


def kernel(x, c, ctx, c_ctx, w_ada, b_ada, norm_g, w_in, hgrn_lb_logits, gla_w_gk, gla_b_gk,
           hgrn_onorm_g, gla_onorm_g, w_pa, w_pb, w_out, final_norm_g):
    bsz, seq, d = x.shape
    n_ctx = ctx.shape[1]
    depth = w_in.shape[0]
    assert depth == 1, "single-layer trunk"
    a_kw, a_vw = A_HEADS * A_DK, A_HEADS * A_DV
    b_kw, b_vw = B_HEADS * B_DK, B_HEADS * B_DV

    n_rows = -(-(bsz + 1) // SUBLANES) * SUBLANES
    cvec = jnp.zeros((n_rows, d), F32).at[:bsz].set(c).at[bsz].set(c_ctx)
    mod = _adaln(cvec, w_ada[0], b_ada[0].reshape(1, -1))
    mod3 = mod.reshape(n_rows, 3, d)

    lb = _lower_bounds(hgrn_lb_logits)[0:1]

    w = w_in[0]
    o_ag = 2 * a_kw + 2 * a_vw
    o_bq = o_ag + a_vw
    o_br = o_bq + 2 * b_kw + b_vw
    o_bg = o_br + 2 * B_RANK
    w_main = jnp.concatenate(
        [w[:, 0:a_kw + a_vw], w[:, a_kw + a_vw:o_ag], w[:, o_bq:o_br], w[:, o_ag:o_bq], w[:, o_bg:]],
        axis=1).astype(BF16)
    w_r = jnp.pad(w[:, o_br:o_bg], ((0, 0), (0, LANES - 2 * B_RANK))).astype(BF16)
    gk_pad = jnp.zeros((LANES, 2 * b_kw), F32)
    gk_pad = gk_pad.at[0:B_RANK, 0:b_kw].set(gla_w_gk[0, 0]).at[B_RANK:2 * B_RANK, b_kw:].set(gla_w_gk[0, 1])
    gk_pad = gk_pad.astype(BF16)
    gkb = gla_b_gk[0].reshape(1, 2 * b_kw)
    ng = norm_g[0].reshape(1, d)

    x2d = x.reshape(bsz * seq, d)
    ctx2d = ctx.reshape(bsz * n_ctx, d)
    tm = 1024
    tiles_per_batch = seq // tm

    c_aqi, c_alf, c_bqkv, c_blf = _inproj(
        ctx2d, mod3, lambda i: bsz, ng, w_main, w_r, lb, gk_pad, gkb,
        tm=bsz * n_ctx, latent=False)
    aqi, alf, bqkv, blf, gates = _inproj(
        x2d, mod3, lambda i: i // tiles_per_batch, ng, w_main, w_r, lb, gk_pad, gkb,
        tm=tm, latent=True, n_batch=bsz)

    r3 = lambda t, n: t.reshape(bsz, n, t.shape[-1])
    oaf, oab = _scan_a(r3(c_aqi, n_ctx), r3(c_alf, n_ctx), r3(aqi, seq), r3(alf, seq))
    obf, obb = _scan_b(r3(c_bqkv, n_ctx), r3(c_blf, n_ctx), bqkv, blf)
    obp = _gla_finalize(obf, obb, gates, gla_onorm_g[0].reshape(1, -1), sg_col_off=a_vw // B_DV)

    r2 = lambda t: t.reshape(bsz * seq, t.shape[-1])
    y = _merge(r2(oaf), r2(oab), obp, gates, hgrn_onorm_g[0].reshape(1, -1),
               w_pa[0].astype(BF16), w_pb[0].astype(BF16), tm=256)
    out = _final(y, x2d, mod3, w_out[0].astype(BF16), final_norm_g.reshape(1, d),
                 tm=512, tiles_per_batch=seq // 512)
    return out.reshape(bsz, seq, d)
```

```python
import functools

import jax
import jax.numpy as jnp
from jax import lax
from jax.experimental import pallas as pl
from jax.experimental.pallas import tpu as pltpu

F32 = jnp.float32
BF16 = jnp.bfloat16

CHUNK = 64
SUB = 16
GRID_W = 64
EPS = 1e-6
A_HEADS, A_DK, A_DV = 16, 128, 128
B_HEADS, B_DK, B_DV = 4, 256, 512
B_RANK = 16
GATE_NORMALIZER = 16.0
LOG2E = 1.4426950408889634

VMEM_LIMIT_BYTES = 56 * 1024 * 1024
LANES = 128
SUBLANES = 8

NT_DIMS = (((1,), (1,)), ((), ()))
TN_DIMS = (((0,), (0,)), ((), ()))


def _sigmoid(z):
    return 1.0 / (1.0 + jnp.exp(-z))


def _adaln_kernel(c_ref, w_ref, b_ref, o_ref):
    c = c_ref[...]
    s = c * _sigmoid(c)
    o_ref[...] = jnp.dot(s, w_ref[...], preferred_element_type=F32,
                         precision=lax.Precision.HIGHEST) + b_ref[...]


def _adaln(cvec, w, b):
    rows, d = cvec.shape
    n = w.shape[1]
    tn = 768
    return pl.pallas_call(
        _adaln_kernel,
        grid=(n // tn,),
        in_specs=[pl.BlockSpec((rows, d), lambda j: (0, 0)),
                  pl.BlockSpec((d, tn), lambda j: (0, j)),
                  pl.BlockSpec((1, tn), lambda j: (0, j))],
        out_specs=pl.BlockSpec((rows, tn), lambda j: (0, j)),
        out_shape=jax.ShapeDtypeStruct((rows, n), F32),
        compiler_params=pltpu.CompilerParams(
            dimension_semantics=("arbitrary",), vmem_limit_bytes=32 * 1024 * 1024),
        name="adaln",
    )(cvec, w, b)


def _lower_bound_kernel(l_ref, o_ref):
    x = l_ref[...]
    n_rows = x.shape[0]
    m = jnp.max(x, axis=0, keepdims=True)
    e = jnp.exp(x - m)
    tot = jnp.sum(e, axis=0, keepdims=True)
    run = jnp.zeros_like(tot)
    for r in range(n_rows - 1):
        run = run + e[r:r + 1]
        o_ref[r:r + 1, :] = run / tot


def _lower_bounds(logits):
    n_rows, w = logits.shape
    return pl.pallas_call(
        _lower_bound_kernel,
        out_shape=jax.ShapeDtypeStruct((n_rows - 1, w), F32),
        name="hgrn_lower_bounds",
    )(logits)


TN = 512
NORM_ROWS = 128
T_AQI = 4096 // TN
T_ALF = 4096 // TN
T_BQKV = 4096 // TN
T_BLF = 2048 // TN
T_GATES = 8192 // TN
J_ALF = T_AQI
J_BQKV = J_ALF + T_ALF
J_BLF = J_BQKV + T_BQKV
J_GATES = J_BLF + T_BLF
J_END = J_GATES + T_GATES


def _inproj_kernel(x_ref, mod_ref, ng_ref, w_ref, wr_ref, lb_ref, gkw_ref, gkb_ref, *rest, latent):
    if latent:
        aqi_ref, alf_ref, bqkv_ref, blf_ref, gates_ref, h_ref, r_ref = rest
    else:
        aqi_ref, alf_ref, bqkv_ref, blf_ref, h_ref, r_ref = rest
        gates_ref = None
    j = pl.program_id(1)

    def store_gla(out_ref, z):
        if not latent:
            out_ref[...] = z.astype(out_ref.dtype)
            return
        n_rows = z.shape[0] // GRID_W
        zc = jnp.swapaxes(z.reshape(n_rows, GRID_W, z.shape[1]), 0, 1)
        out_ref[...] = zc.astype(out_ref.dtype)

    @pl.when(j == 0)
    def _():
        def norm_rows(t, carry):
            rows = pl.ds(pl.multiple_of(t * NORM_ROWS, NORM_ROWS), NORM_ROWS)
            x = x_ref[rows, :]
            ms = jnp.mean(x * x, axis=-1, keepdims=True)
            y = x * lax.rsqrt(ms + EPS) * ng_ref[...]
            h = (y * (1.0 + mod_ref[0, 1:2, :]) + mod_ref[0, 0:1, :]).astype(BF16)
            h_ref[rows, :] = h
            r_ref[rows, :] = jnp.dot(h, wr_ref[...], preferred_element_type=F32).astype(BF16)
            return carry
        lax.fori_loop(0, x_ref.shape[0] // NORM_ROWS, norm_rows, 0)

    def mm():
        return jnp.dot(h_ref[...], w_ref[...], preferred_element_type=F32)

    @pl.when(j < J_ALF)
    def _():
        aqi_ref[...] = mm().astype(BF16)

    @pl.when((j >= J_ALF) & (j < J_BQKV))
    def _():
        lb = lb_ref[...]
        f = lb + (1.0 - lb) * _sigmoid(mm())
        alf_ref[...] = jnp.log2(f)

    @pl.when((j >= J_BQKV) & (j < J_BLF))
    def _():
        scale = jnp.where(j < J_BQKV + (B_HEADS * B_DK) // TN, B_DK ** -0.5, 1.0)
        store_gla(bqkv_ref, mm() * scale)

    @pl.when((j >= J_BLF) & (j < J_GATES))
    def _():
        z = jnp.dot(r_ref[...], gkw_ref[...], preferred_element_type=F32) + gkb_ref[...]
        ls = jnp.minimum(z, 0.0) - jnp.log(1.0 + jnp.exp(-jnp.abs(z)))
        store_gla(blf_ref, ls * (LOG2E / GATE_NORMALIZER))

    if latent:
        @pl.when(j >= J_GATES)
        def _():
            z = mm()
            sg = _sigmoid(z)
            gates_ref[...] = (sg * jnp.where(j < J_GATES + T_GATES // 2, z, 1.0)).astype(BF16)


def _inproj(x2d, mod3, mod_row_of_tile, norm_g, w_main, w_r, lb, gk_pad, gkb, *, tm, latent, n_batch=1):
    n_tok, d = x2d.shape
    n_j = J_END if latent else J_GATES
    tile_rows = tm // GRID_W
    seq_rows = n_tok // n_batch // GRID_W
    tiles_per_batch = n_tok // n_batch // tm

    def cl(j, lo, n):
        return jnp.clip(j - lo, 0, n - 1)

    def gla_spec(j_lo, n):
        if not latent:
            return pl.BlockSpec((tm, TN), lambda i, j: (i, cl(j, j_lo, n)))
        return pl.BlockSpec((None, GRID_W, tile_rows, TN),
                            lambda i, j: (i // tiles_per_batch, 0, i % tiles_per_batch, cl(j, j_lo, n)))

    def gla_shape(n, dtype):
        if not latent:
            return jax.ShapeDtypeStruct((n_tok, n * TN), dtype)
        return jax.ShapeDtypeStruct((n_batch, GRID_W, seq_rows, n * TN), dtype)

    in_specs = [
        pl.BlockSpec((tm, d), lambda i, j: (i, 0)),
        pl.BlockSpec((1, 3, d), lambda i, j: (mod_row_of_tile(i), 0, 0)),
        pl.BlockSpec((1, d), lambda i, j: (0, 0)),
        pl.BlockSpec((d, TN), lambda i, j: (0, j - jnp.clip(j - (J_BLF - 1), 0, T_BLF))),
        pl.BlockSpec((d, LANES), lambda i, j: (0, 0)),
        pl.BlockSpec((1, TN), lambda i, j: (0, cl(j, J_ALF, T_ALF))),
        pl.BlockSpec((LANES, TN), lambda i, j: (0, cl(j, J_BLF, T_BLF))),
        pl.BlockSpec((1, TN), lambda i, j: (0, cl(j, J_BLF, T_BLF))),
    ]
    out_specs = [
        pl.BlockSpec((tm, TN), lambda i, j: (i, cl(j, 0, T_AQI))),
        pl.BlockSpec((tm, TN), lambda i, j: (i, cl(j, J_ALF, T_ALF))),
        gla_spec(J_BQKV, T_BQKV),
        gla_spec(J_BLF, T_BLF),
    ]
    out_shape = [
        jax.ShapeDtypeStruct((n_tok, T_AQI * TN), BF16),
        jax.ShapeDtypeStruct((n_tok, T_ALF * TN), F32),
        gla_shape(T_BQKV, BF16),
        gla_shape(T_BLF, F32),
    ]
    if latent:
        out_specs.append(pl.BlockSpec((tm, TN), lambda i, j: (i, cl(j, J_GATES, T_GATES))))
        out_shape.append(jax.ShapeDtypeStruct((n_tok, T_GATES * TN), BF16))
    return pl.pallas_call(
        functools.partial(_inproj_kernel, latent=latent),
        grid=(n_tok // tm, n_j),
        in_specs=in_specs,
        out_specs=out_specs,
        out_shape=out_shape,
        scratch_shapes=[pltpu.VMEM((tm, d), BF16), pltpu.VMEM((tm, LANES), BF16)],
        compiler_params=pltpu.CompilerParams(
            dimension_semantics=("arbitrary", "arbitrary"), vmem_limit_bytes=VMEM_LIMIT_BYTES),
        name="inproj_latent" if latent else "inproj_ctx",
    )(x2d, mod3, norm_g, w_main, w_r, lb, gk_pad, gkb)


def _scan_consts():
    row = lax.broadcasted_iota(jnp.int32, (CHUNK, CHUNK), 0)
    col = lax.broadcasted_iota(jnp.int32, (CHUNK, CHUNK), 1)
    return dict(
        mask_f=col <= row,
        mask_b=col >= row,
        tri_f=jnp.where(col <= row, 1.0, 0.0).astype(BF16),
        tri_b=jnp.where(col >= row, 1.0, 0.0).astype(BF16),
        lane=lax.broadcasted_iota(jnp.int32, (SUBLANES, CHUNK), 1),
    )


def _cumsum_stage(lf, fwd, cst):
    dk = lf.shape[1]
    tri = cst["tri_f"] if fwd else cst["tri_b"]
    hi = lf.astype(BF16)
    lo = (lf - hi.astype(F32)).astype(BF16)
    cc = jnp.dot(tri, jnp.concatenate([hi, lo], axis=1), preferred_element_type=F32)
    return cc[:, :dk] + cc[:, dk:]


def _intra_stage(q, k, v, lf, c, st_ref, fwd, cst, keep):
    dk = lf.shape[1]
    tot = c[CHUNK - 1:CHUNK] if fwd else c[0:1]

    if k is None:
        a = c - jnp.log2(1.0 - jnp.exp2(lf))
        kf = None

        def kscaled(r0, r1, ref):
            return jnp.exp2(ref - a[r0:r1])
    else:
        a = c
        kf = k.astype(F32)

        def kscaled(r0, r1, ref):
            return kf[r0:r1] * jnp.exp2(ref - c[r0:r1])

    kte = kscaled(0, CHUNK, tot).astype(BF16)
    st = st_ref[...]
    if keep is not None:
        keep.prev_state(st.astype(BF16))
    st_ref[...] = st * jnp.exp2(tot) + lax.dot_general(v, kte, TN_DIMS, preferred_element_type=F32)
    if keep is None:
        return
    yield

    qf = q.astype(F32)
    keep.scaled_q((qf * jnp.exp2(c)).astype(BF16))
    n_sub = CHUNK // SUB
    half = SUB // SUBLANES
    soffs = []
    for r in range(n_sub):
        r0 = SUB * r
        if fwd and r > 0:
            ref = c[r0:r0 + 1]
            kr = jnp.concatenate([kscaled(0, r0, ref).astype(BF16),
                                  jnp.zeros((CHUNK - r0, dk), BF16)], axis=0)
        elif (not fwd) and r < n_sub - 1:
            ref = c[r0 + SUB - 1:r0 + SUB]
            kr = jnp.concatenate([jnp.zeros((r0 + SUB, dk), BF16),
                                  kscaled(r0 + SUB, CHUNK, ref).astype(BF16)], axis=0)
        else:
            soffs.append(None)
            continue
        qr = (qf[r0:r0 + SUB] * jnp.exp2(c[r0:r0 + SUB] - ref)).astype(BF16)
        soffs.append(lax.dot_general(qr, kr, NT_DIMS, preferred_element_type=F32))
    yield

    rows = []
    for r in range(n_sub):
        r0 = SUB * r
        cb = c[r0:r0 + SUB]
        qb = qf[r0:r0 + SUB]
        acc = [jnp.zeros((SUBLANES, CHUNK), F32) for _ in range(half)]
        for jj in range(SUB):
            aj = a[r0 + jj:r0 + jj + 1]
            jg = jj // SUBLANES
            groups = range(jg, half) if fwd else range(0, jg + 1)
            for g in groups:
                e = cb[g * SUBLANES:(g + 1) * SUBLANES] - aj
                if g == jg:
                    e = jnp.minimum(e, 0.0)
                t = qb[g * SUBLANES:(g + 1) * SUBLANES] * jnp.exp2(e)
                if kf is not None:
                    t = t * kf[r0 + jj:r0 + jj + 1]
                sj = jnp.sum(t, axis=1, keepdims=True)
                acc[g] = jnp.where(cst["lane"] == r0 + jj, sj, acc[g])
        d = jnp.concatenate(acc, axis=0)
        rows.append(d if soffs[r] is None else d + soffs[r])
    p = jnp.concatenate(rows, axis=0)
    keep.scores(jnp.where(cst["mask_f"] if fwd else cst["mask_b"], p, 0.0).astype(BF16))


def _run_staged(units):
    live = list(units)
    while live:
        nxt = []
        for u in live:
            try:
                next(u)
                nxt.append(u)
            except StopIteration:
                pass
        live = nxt


def _scan_kernel(*refs, n_heads, dk, dv, has_k, n_ctx_chunks, n_step_chunks, lat_index):
    n_ctx_in = 4 if has_k else 3
    n_lat_in = 4 if has_k else 3
    ctx_refs = refs[:n_ctx_in]
    lat_refs = (refs[n_ctx_in:n_ctx_in + n_lat_in], refs[n_ctx_in + n_lat_in:n_ctx_in + 2 * n_lat_in])
    rest = refs[n_ctx_in + 2 * n_lat_in:]
    out_refs = rest[0:2]
    st_ref, c_ref, stb_ref = rest[2:]
    s = pl.program_id(2)
    cst = _scan_consts()
    units = [(h, d) for h in range(n_heads) for d in range(2)]

    def kcols(h):
        return slice(h * dk, (h + 1) * dk)

    def vcols(h):
        return slice(h * dv, (h + 1) * dv)

    def ctx_step():
        st_ref[...] = jnp.zeros(st_ref.shape, F32)
        if has_k:
            k_ref, v_ref, lff_ref, lfb_ref = ctx_refs
        else:
            v_ref, lff_ref, lfb_ref = ctx_refs
            k_ref = None

        def unit(h, d, rows):
            lf = (lff_ref, lfb_ref)[d][rows, kcols(h)]
            c = _cumsum_stage(lf, d == 0, cst)
            yield
            k = None if k_ref is None else k_ref[rows, kcols(h)]
            yield from _intra_stage(None, k, v_ref[rows, vcols(h)], lf, c, st_ref.at[d, h], d == 0, cst, None)

        def body(i, carry):
            rows = (pl.ds(pl.multiple_of(i * CHUNK, CHUNK), CHUNK),
                    pl.ds(pl.multiple_of((n_ctx_chunks - 1 - i) * CHUNK, CHUNK), CHUNK))
            _run_staged([unit(h, d, rows[d]) for h, d in units])
            return carry
        lax.fori_loop(0, n_ctx_chunks, body, 0)

    def latent_step():
        n = n_step_chunks

        def index(d, pos):
            return lat_index(pos if d == 0 else n - 1 - pos)

        def read(d, pos, h, what):
            refs_d = lat_refs[d]
            ref = refs_d[{"q": 0, "k": 1, "v": 2 if has_k else 1, "lf": 3 if has_k else 2}[what]]
            cols = vcols(h) if what == "v" else kcols(h)
            return ref[index(d, pos) + (cols,)]

        def cumsum_all(pos, slot):
            for u, (h, d) in enumerate(units):
                c_ref[slot, u] = _cumsum_stage(read(d, pos, h, "lf"), d == 0, cst)

        def output_all(pos, carried):
            for u, (h, d) in enumerate(units):
                p, qd = carried[u]
                o = jnp.dot(p, read(d, pos, h, "v"), preferred_element_type=F32)
                o = o + lax.dot_general(qd, stb_ref[u], NT_DIMS, preferred_element_type=F32)
                out_refs[d][index(d, pos) + (vcols(h),)] = o.astype(out_refs[d].dtype)

        class Keeper:
            def __init__(self, u):
                self.u = u
                self.p = self.qd = None

            def prev_state(self, stb):
                stb_ref[self.u] = stb

            def scaled_q(self, qd):
                self.qd = qd

            def scores(self, p):
                self.p = p

        def intra_all(pos, cs):
            keepers = [Keeper(u) for u in range(len(units))]
            gens = []
            for u, (h, d) in enumerate(units):
                k = read(d, pos, h, "k") if has_k else None
                gens.append(_intra_stage(read(d, pos, h, "q"), k, read(d, pos, h, "v"),
                                         read(d, pos, h, "lf"), cs[u], st_ref.at[d, h],
                                         d == 0, cst, keepers[u]))
            _run_staged(gens)
            return tuple((kp.p, kp.qd) for kp in keepers)

        def load_cumsums(slot):
            return [c_ref[slot, u] for u in range(len(units))]

        cumsum_all(0, 0)
        cs0 = load_cumsums(0)
        cumsum_all(1, 1)
        ps0 = intra_all(0, cs0)

        def body(i, ps):
            slot = lax.rem(i, 2)
            cs = load_cumsums(slot)
            output_all(i - 1, ps)
            cumsum_all(jnp.minimum(i + 1, n - 1), 1 - slot)
            return intra_all(i, cs)
        output_all(n - 1, lax.fori_loop(1, n, body, ps0))

    @pl.when(s == 0)
    def _():
        ctx_step()

    @pl.when(s > 0)
    def _():
        latent_step()


def _scan_call(inputs, in_specs, out_struct, out_specs, grid, *, n_heads, dk, dv, has_k,
               n_ctx_chunks, n_step_chunks, lat_index, name):
    n_units = 2 * n_heads
    return pl.pallas_call(
        functools.partial(_scan_kernel, n_heads=n_heads, dk=dk, dv=dv, has_k=has_k,
                          n_ctx_chunks=n_ctx_chunks, n_step_chunks=n_step_chunks, lat_index=lat_index),
        grid=grid,
        in_specs=in_specs,
        out_specs=out_specs,
        out_shape=out_struct,
        scratch_shapes=[pltpu.VMEM((2, n_heads, dv, dk), F32),
                        pltpu.VMEM((2, n_units, CHUNK, dk), F32),
                        pltpu.VMEM((n_units, dv, dk), BF16)],
        compiler_params=pltpu.CompilerParams(
            dimension_semantics=("arbitrary", "arbitrary", "arbitrary"),
            vmem_limit_bytes=VMEM_LIMIT_BYTES),
        name=name,
    )(*inputs)


A_SCAN_HEADS = 2
A_SCAN_CHUNKS = 8


def _scan_a(c_qi, c_lf, qi, lf):
    bsz, n_ctx, _ = c_qi.shape
    seq = qi.shape[1]
    g = A_SCAN_HEADS
    gw = g * A_DK
    n_hg = A_HEADS // g
    ts = A_SCAN_CHUNKS * CHUNK
    n_steps = seq // ts

    def fstep(s):
        return jnp.maximum(s - 1, 0)

    def bstep(s):
        return n_steps - 1 - jnp.maximum(s - 1, 0)

    cspec = lambda off: pl.BlockSpec((None, n_ctx, gw), lambda b, h, s: (b, 0, off + h))
    fspec = lambda off: pl.BlockSpec((None, ts, gw), lambda b, h, s: (b, fstep(s), off + h))
    bspec = lambda off: pl.BlockSpec((None, ts, gw), lambda b, h, s: (b, bstep(s), off + h))
    in_specs = [cspec(n_hg), cspec(0), cspec(n_hg),
                fspec(0), fspec(n_hg), fspec(0),
                bspec(0), bspec(n_hg), bspec(n_hg)]
    inputs = [c_qi, c_lf, c_lf, qi, qi, lf, qi, qi, lf]
    out_struct = [jax.ShapeDtypeStruct((bsz, seq, A_HEADS * A_DV), BF16)] * 2
    out_specs = [pl.BlockSpec((None, ts, gw), lambda b, h, s: (b, fstep(s), h)),
                 pl.BlockSpec((None, ts, gw), lambda b, h, s: (b, bstep(s), h))]

    def lat_index(chunk):
        row = chunk * CHUNK
        if not isinstance(row, int):
            row = pl.multiple_of(row, CHUNK)
        return (pl.ds(row, CHUNK),)

    return _scan_call(inputs, in_specs, out_struct, out_specs, (bsz, n_hg, n_steps + 1),
                      n_heads=g, dk=A_DK, dv=A_DV, has_k=False,
                      n_ctx_chunks=n_ctx // CHUNK, n_step_chunks=A_SCAN_CHUNKS,
                      lat_index=lat_index, name="scan_hgrn2")


B_SCAN_COLS = 4


def _scan_b(c_qkv, c_lf, qkv, lf):
    bsz, n_ctx, _ = c_qkv.shape
    n_rows = qkv.shape[2]
    kw = B_HEADS * B_DK
    nc = B_SCAN_COLS
    n_steps = GRID_W // nc
    chunks_per_col = n_rows // CHUNK

    def fblk(s):
        return jnp.maximum(s - 1, 0)

    def bblk(s):
        return n_steps - 1 - jnp.maximum(s - 1, 0)

    def cspec(width, off):
        return pl.BlockSpec((None, n_ctx, width), lambda b, h, s: (b, 0, off + h))

    def lspec(width, off, blk):
        return pl.BlockSpec((None, nc, n_rows, width), lambda b, h, s: (b, blk(s), 0, off + h))

    k_off, v_off = kw // B_DK, (2 * kw) // B_DV
    in_specs = [cspec(B_DK, k_off), cspec(B_DV, v_off), cspec(B_DK, 0), cspec(B_DK, B_HEADS),
                lspec(B_DK, 0, fblk), lspec(B_DK, k_off, fblk), lspec(B_DV, v_off, fblk), lspec(B_DK, 0, fblk),
                lspec(B_DK, 0, bblk), lspec(B_DK, k_off, bblk), lspec(B_DV, v_off, bblk),
                lspec(B_DK, B_HEADS, bblk)]
    inputs = [c_qkv, c_qkv, c_lf, c_lf, qkv, qkv, qkv, lf, qkv, qkv, qkv, lf]
    vw = B_HEADS * B_DV
    out_struct = [jax.ShapeDtypeStruct((bsz, GRID_W, n_rows, vw), BF16)] * 2
    out_specs = [pl.BlockSpec((None, nc, n_rows, B_DV), lambda b, h, s: (b, fblk(s), 0, h)),
                 pl.BlockSpec((None, nc, n_rows, B_DV), lambda b, h, s: (b, bblk(s), 0, h))]

    def lat_index(chunk):
        col = chunk // chunks_per_col
        row = (chunk - col * chunks_per_col) * CHUNK
        if not isinstance(row, int):
            row = pl.multiple_of(row, CHUNK)
        return (col, pl.ds(row, CHUNK))

    return _scan_call(inputs, in_specs, out_struct, out_specs, (bsz, B_HEADS, n_steps + 1),
                      n_heads=1, dk=B_DK, dv=B_DV, has_k=True,
                      n_ctx_chunks=n_ctx // CHUNK, n_step_chunks=nc * chunks_per_col,
                      lat_index=lat_index, name="scan_gla")


GLA_FIN_ROWS = 16


def _gla_finalize_kernel(of_ref, ob_ref, sg_ref, g_ref, o_ref):
    n_cols, n_rows, w = of_ref.shape
    o = of_ref[...].astype(F32) + ob_ref[...].astype(F32)
    ms = jnp.mean(o * o, axis=-1, keepdims=True)
    on = o * lax.rsqrt(ms + EPS) * g_ref[...]
    on = jnp.swapaxes(on, 0, 1).reshape(n_rows * n_cols, w)
    o_ref[...] = (on * sg_ref[...].astype(F32)).astype(o_ref.dtype)


def _gla_finalize(of, ob, gates, gain, *, sg_col_off):
    bsz, _, n_rows, vw = of.shape
    tm = GLA_FIN_ROWS * GRID_W
    tiles = n_rows // GLA_FIN_ROWS
    cm = pl.BlockSpec((None, GRID_W, GLA_FIN_ROWS, B_DV), lambda b, r, h: (b, 0, r, h))
    return pl.pallas_call(
        _gla_finalize_kernel,
        grid=(bsz, tiles, B_HEADS),
        in_specs=[cm, cm,
                  pl.BlockSpec((tm, B_DV), lambda b, r, h: (b * tiles + r, sg_col_off + h)),
                  pl.BlockSpec((1, B_DV), lambda b, r, h: (0, h))],
        out_specs=pl.BlockSpec((tm, B_DV), lambda b, r, h: (b * tiles + r, h)),
        out_shape=jax.ShapeDtypeStruct((bsz * n_rows * GRID_W, vw), BF16),
        compiler_params=pltpu.CompilerParams(
            dimension_semantics=("arbitrary", "arbitrary", "arbitrary"),
            vmem_limit_bytes=32 * 1024 * 1024),
        name="gla_finalize",
    )(of, ob, gates, gain)


def _merge_kernel(oaf, oab, obp, sga, sma, smb, ga, wpa, wpb, y_ref):
    o = oaf[...].astype(F32) + oab[...].astype(F32)
    parts = []
    for h in range(A_HEADS):
        oh = o[:, h * A_DV:(h + 1) * A_DV]
        ms = jnp.mean(oh * oh, axis=-1, keepdims=True)
        parts.append(oh * lax.rsqrt(ms + EPS))
    oa = (jnp.concatenate(parts, axis=1) * ga[...] * sga[...].astype(F32)).astype(BF16)
    ya = jnp.dot(oa, wpa[...], preferred_element_type=F32)
    yb = jnp.dot(obp[...], wpb[...], preferred_element_type=F32)
    y_ref[...] = (sma[...].astype(F32) * ya + smb[...].astype(F32) * yb).astype(BF16)


def _merge(oaf, oab, obp, gates, ga, wpa, wpb, *, tm):
    n_tok, d = oaf.shape
    tok = lambda off: pl.BlockSpec((tm, d), lambda i: (i, off))
    const = lambda shape: pl.BlockSpec(shape, lambda i: (0, 0), pipeline_mode=pl.Buffered(1))
    return pl.pallas_call(
        _merge_kernel,
        grid=(n_tok // tm,),
        in_specs=[tok(0), tok(0), tok(0), tok(0), tok(2), tok(3),
                  const((1, d)), const(wpa.shape), const(wpb.shape)],
        out_specs=pl.BlockSpec((tm, wpa.shape[1]), lambda i: (i, 0)),
        out_shape=jax.ShapeDtypeStruct((n_tok, wpa.shape[1]), BF16),
        compiler_params=pltpu.CompilerParams(
            dimension_semantics=("arbitrary",), vmem_limit_bytes=VMEM_LIMIT_BYTES),
        name="merge_proj",
    )(oaf, oab, obp, gates, gates, gates, ga, wpa, wpb)


def _final_kernel(y_ref, x_ref, mod_ref, w_ref, g_ref, o_ref):
    yo = jnp.dot(y_ref[...], w_ref[...], preferred_element_type=F32)
    z = x_ref[...] + mod_ref[0, 2:3, :] * yo
    ms = jnp.mean(z * z, axis=-1, keepdims=True)
    o_ref[...] = z * lax.rsqrt(ms + EPS) * g_ref[...]


def _final(y, x2d, mod3, w_out, fg, *, tm, tiles_per_batch):
    n_tok, d = x2d.shape
    const = lambda shape: pl.BlockSpec(shape, lambda i: (0, 0), pipeline_mode=pl.Buffered(1))
    return pl.pallas_call(
        _final_kernel,
        grid=(n_tok // tm,),
        in_specs=[pl.BlockSpec((tm, d), lambda i: (i, 0)),
                  pl.BlockSpec((tm, d), lambda i: (i, 0)),
                  pl.BlockSpec((1, 3, d), lambda i: (i // tiles_per_batch, 0, 0)),
                  const(w_out.shape), const((1, d))],
        out_specs=pl.BlockSpec((tm, d), lambda i: (i, 0)),
        out_shape=jax.ShapeDtypeStruct((n_tok, d), F32),
        compiler_params=pltpu.CompilerParams(
            dimension_semantics=("arbitrary",), vmem_limit_bytes=VMEM_LIMIT_BYTES),
        name="out_proj_final",
    )(y, x2d, mod3, w_out, fg)


def kernel(x, c, ctx, c_ctx, w_ada, b_ada, norm_g, w_in, hgrn_lb_logits, gla_w_gk, gla_b_gk,
           hgrn_onorm_g, gla_onorm_g, w_pa, w_pb, w_out, final_norm_g):
    bsz, seq, d = x.shape
    n_ctx = ctx.shape[1]
    depth = w_in.shape[0]
    assert depth == 1, "single-layer trunk"
    a_kw, a_vw = A_HEADS * A_DK, A_HEADS * A_DV
    b_kw, b_vw = B_HEADS * B_DK, B_HEADS * B_DV

    n_rows = -(-(bsz + 1) // SUBLANES) * SUBLANES
    cvec = jnp.zeros((n_rows, d), F32).at[:bsz].set(c).at[bsz].set(c_ctx)
    mod = _adaln(cvec, w_ada[0], b_ada[0].reshape(1, -1))
    mod3 = mod.reshape(n_rows, 3, d)

    lb = _lower_bounds(hgrn_lb_logits)[0:1]

    w = w_in[0]
    o_ag = 2 * a_kw + 2 * a_vw
    o_bq = o_ag + a_vw
    o_br = o_bq + 2 * b_kw + b_vw
    o_bg = o_br + 2 * B_RANK
    w_main = jnp.concatenate(
        [w[:, 0:a_kw + a_vw], w[:, a_kw + a_vw:o_ag], w[:, o_bq:o_br], w[:, o_ag:o_bq], w[:, o_bg:]],
        axis=1).astype(BF16)
    w_r = jnp.pad(w[:, o_br:o_bg], ((0, 0), (0, LANES - 2 * B_RANK))).astype(BF16)
    gk_pad = jnp.zeros((LANES, 2 * b_kw), F32)
    gk_pad = gk_pad.at[0:B_RANK, 0:b_kw].set(gla_w_gk[0, 0]).at[B_RANK:2 * B_RANK, b_kw:].set(gla_w_gk[0, 1])
    gk_pad = gk_pad.astype(BF16)
    gkb = gla_b_gk[0].reshape(1, 2 * b_kw)
    ng = norm_g[0].reshape(1, d)

    x2d = x.reshape(bsz * seq, d)
    ctx2d = ctx.reshape(bsz * n_ctx, d)
    tm = 1024
    tiles_per_batch = seq // tm

    c_aqi, c_alf, c_bqkv, c_blf = _inproj(
        ctx2d, mod3, lambda i: bsz, ng, w_main, w_r, lb, gk_pad, gkb,
        tm=bsz * n_ctx, latent=False)
    aqi, alf, bqkv, blf, gates = _inproj(
        x2d, mod3, lambda i: i // tiles_per_batch, ng, w_main, w_r, lb, gk_pad, gkb,
        tm=tm, latent=True, n_batch=bsz)

    r3 = lambda t, n: t.reshape(bsz, n, t.shape[-1])
    oaf, oab = _scan_a(r3(c_aqi, n_ctx), r3(c_alf, n_ctx), r3(aqi, seq), r3(alf, seq))
    obf, obb = _scan_b(r3(c_bqkv, n_ctx), r3(c_blf, n_ctx), bqkv, blf)
    obp = _gla_finalize(obf, obb, gates, gla_onorm_g[0].reshape(1, -1), sg_col_off=a_vw // B_DV)

    r2 = lambda t: t.reshape(bsz * seq, t.shape[-1])
    y = _merge(r2(oaf), r2(oab), obp, gates, hgrn_onorm_g[0].reshape(1, -1),
               w_pa[0].astype(BF16), w_pb[0].astype(BF16), tm=256)
    out = _final(y, x2d, mod3, w_out[0].astype(BF16), final_norm_g.reshape(1, d),
                 tm=512, tiles_per_batch=seq // 512)
    return out.reshape(bsz, seq, d)
```

```python
import functools

import jax
import jax.numpy as jnp
from jax import lax
from jax.experimental import pallas as pl
from jax.experimental.pallas import tpu as pltpu

F32 = jnp.float32
BF16 = jnp.bfloat16

CHUNK = 64
SUB = 16
GRID_W = 64
EPS = 1e-6
A_HEADS, A_DK, A_DV = 16, 128, 128
B_HEADS, B_DK, B_DV = 4, 256, 512
B_RANK = 16
GATE_NORMALIZER = 16.0
LOG2E = 1.4426950408889634

VMEM_LIMIT_BYTES = 56 * 1024 * 1024
LANES = 128
SUBLANES = 8

NT_DIMS = (((1,), (1,)), ((), ()))
TN_DIMS = (((0,), (0,)), ((), ()))


def _sigmoid(z):
    return 1.0 / (1.0 + jnp.exp(-z))


def _adaln_kernel(c_ref, w_ref, b_ref, o_ref):
    c = c_ref[...]
    s = c * _sigmoid(c)
    o_ref[...] = jnp.dot(s, w_ref[...], preferred_element_type=F32,
                         precision=lax.Precision.HIGHEST) + b_ref[...]


def _adaln(cvec, w, b):
    rows, d = cvec.shape
    n = w.shape[1]
    tn = 768
    return pl.pallas_call(
        _adaln_kernel,
        grid=(n // tn,),
        in_specs=[pl.BlockSpec((rows, d), lambda j: (0, 0)),
                  pl.BlockSpec((d, tn), lambda j: (0, j)),
                  pl.BlockSpec((1, tn), lambda j: (0, j))],
        out_specs=pl.BlockSpec((rows, tn), lambda j: (0, j)),
        out_shape=jax.ShapeDtypeStruct((rows, n), F32),
        compiler_params=pltpu.CompilerParams(
            dimension_semantics=("arbitrary",), vmem_limit_bytes=32 * 1024 * 1024),
        name="adaln",
    )(cvec, w, b)


def _lower_bound_kernel(l_ref, o_ref):
    x = l_ref[...]
    n_rows = x.shape[0]
    m = jnp.max(x, axis=0, keepdims=True)
    e = jnp.exp(x - m)
    tot = jnp.sum(e, axis=0, keepdims=True)
    run = jnp.zeros_like(tot)
    for r in range(n_rows - 1):
        run = run + e[r:r + 1]
        o_ref[r:r + 1, :] = run / tot


def _lower_bounds(logits):
    n_rows, w = logits.shape
    return pl.pallas_call(
        _lower_bound_kernel,
        out_shape=jax.ShapeDtypeStruct((n_rows - 1, w), F32),
        name="hgrn_lower_bounds",
    )(logits)


TN = 512
NORM_ROWS = 128
T_AQI = 4096 // TN
T_ALF = 4096 // TN
T_BQKV = 4096 // TN
T_BLF = 2048 // TN
T_GATES = 8192 // TN
J_ALF = T_AQI
J_BQKV = J_ALF + T_ALF
J_BLF = J_BQKV + T_BQKV
J_GATES = J_BLF + T_BLF
J_END = J_GATES + T_GATES


def _inproj_kernel(x_ref, mod_ref, ng_ref, w_ref, wr_ref, lb_ref, gkw_ref, gkb_ref, *rest, latent):
    if latent:
        aqi_ref, alf_ref, bqkv_ref, blf_ref, gates_ref, h_ref, r_ref = rest
    else:
        aqi_ref, alf_ref, bqkv_ref, blf_ref, h_ref, r_ref = rest
        gates_ref = None
    j = pl.program_id(1)

    def store_gla(out_ref, z):
        if not latent:
            out_ref[...] = z.astype(out_ref.dtype)
            return
        n_rows = z.shape[0] // GRID_W
        zc = jnp.swapaxes(z.reshape(n_rows, GRID_W, z.shape[1]), 0, 1)
        out_ref[...] = zc.astype(out_ref.dtype)

    @pl.when(j == 0)
    def _():
        def norm_rows(t, carry):
            rows = pl.ds(pl.multiple_of(t * NORM_ROWS, NORM_ROWS), NORM_ROWS)
            x = x_ref[rows, :]
            ms = jnp.mean(x * x, axis=-1, keepdims=True)
            y = x * lax.rsqrt(ms + EPS) * ng_ref[...]
            h = (y * (1.0 + mod_ref[0, 1:2, :]) + mod_ref[0, 0:1, :]).astype(BF16)
            h_ref[rows, :] = h
            r_ref[rows, :] = jnp.dot(h, wr_ref[...], preferred_element_type=F32).astype(BF16)
            return carry
        lax.fori_loop(0, x_ref.shape[0] // NORM_ROWS, norm_rows, 0)

    def mm():
        return jnp.dot(h_ref[...], w_ref[...], preferred_element_type=F32)

    @pl.when(j < J_ALF)
    def _():
        aqi_ref[...] = mm().astype(BF16)

    @pl.when((j >= J_ALF) & (j < J_BQKV))
    def _():
        lb = lb_ref[...]
        f = lb + (1.0 - lb) * _sigmoid(mm())
        alf_ref[...] = jnp.log2(f)

    @pl.when((j >= J_BQKV) & (j < J_BLF))
    def _():
        scale = jnp.where(j < J_BQKV + (B_HEADS * B_DK) // TN, B_DK ** -0.5, 1.0)
        store_gla(bqkv_ref, mm() * scale)

    @pl.when((j >= J_BLF) & (j < J_GATES))
    def _():
        z = jnp.dot(r_ref[...], gkw_ref[...], preferred_element_type=F32) + gkb_ref[...]
        ls = jnp.minimum(z, 0.0) - jnp.log(1.0 + jnp.exp(-jnp.abs(z)))
        store_gla(blf_ref, ls * (LOG2E / GATE_NORMALIZER))

    if latent:
        @pl.when(j >= J_GATES)
        def _():
            z = mm()
            sg = _sigmoid(z)
            gates_ref[...] = (sg * jnp.where(j < J_GATES + T_GATES // 2, z, 1.0)).astype(BF16)


def _inproj(x2d, mod3, mod_row_of_tile, norm_g, w_main, w_r, lb, gk_pad, gkb, *, tm, latent, n_batch=1):
    n_tok, d = x2d.shape
    n_j = J_END if latent else J_GATES
    tile_rows = tm // GRID_W
    seq_rows = n_tok // n_batch // GRID_W
    tiles_per_batch = n_tok // n_batch // tm

    def cl(j, lo, n):
        return jnp.clip(j - lo, 0, n - 1)

    def gla_spec(j_lo, n):
        if not latent:
            return pl.BlockSpec((tm, TN), lambda i, j: (i, cl(j, j_lo, n)))
        return pl.BlockSpec((None, GRID_W, tile_rows, TN),
                            lambda i, j: (i // tiles_per_batch, 0, i % tiles_per_batch, cl(j, j_lo, n)))

    def gla_shape(n, dtype):
        if not latent:
            return jax.ShapeDtypeStruct((n_tok, n * TN), dtype)
        return jax.ShapeDtypeStruct((n_batch, GRID_W, seq_rows, n * TN), dtype)

    in_specs = [
        pl.BlockSpec((tm, d), lambda i, j: (i, 0)),
        pl.BlockSpec((1, 3, d), lambda i, j: (mod_row_of_tile(i), 0, 0)),
        pl.BlockSpec((1, d), lambda i, j: (0, 0)),
        pl.BlockSpec((d, TN), lambda i, j: (0, j - jnp.clip(j - (J_BLF - 1), 0, T_BLF))),
        pl.BlockSpec((d, LANES), lambda i, j: (0, 0)),
        pl.BlockSpec((1, TN), lambda i, j: (0, cl(j, J_ALF, T_ALF))),
        pl.BlockSpec((LANES, TN), lambda i, j: (0, cl(j, J_BLF, T_BLF))),
        pl.BlockSpec((1, TN), lambda i, j: (0, cl(j, J_BLF, T_BLF))),
    ]
    out_specs = [
        pl.BlockSpec((tm, TN), lambda i, j: (i, cl(j, 0, T_AQI))),
        pl.BlockSpec((tm, TN), lambda i, j: (i, cl(j, J_ALF, T_ALF))),
        gla_spec(J_BQKV, T_BQKV),
        gla_spec(J_BLF, T_BLF),
    ]
    out_shape = [
        jax.ShapeDtypeStruct((n_tok, T_AQI * TN), BF16),
        jax.ShapeDtypeStruct((n_tok, T_ALF * TN), F32),
        gla_shape(T_BQKV, BF16),
        gla_shape(T_BLF, F32),
    ]
    if latent:
        out_specs.append(pl.BlockSpec((tm, TN), lambda i, j: (i, cl(j, J_GATES, T_GATES))))
        out_shape.append(jax.ShapeDtypeStruct((n_tok, T_GATES * TN), BF16))
    return pl.pallas_call(
        functools.partial(_inproj_kernel, latent=latent),
        grid=(n_tok // tm, n_j),
        in_specs=in_specs,
        out_specs=out_specs,
        out_shape=out_shape,
        scratch_shapes=[pltpu.VMEM((tm, d), BF16), pltpu.VMEM((tm, LANES), BF16)],
        compiler_params=pltpu.CompilerParams(
            dimension_semantics=("arbitrary", "arbitrary"), vmem_limit_bytes=VMEM_LIMIT_BYTES),
        name="inproj_latent" if latent else "inproj_ctx",
    )(x2d, mod3, norm_g, w_main, w_r, lb, gk_pad, gkb)


def _scan_consts():
    row = lax.broadcasted_iota(jnp.int32, (CHUNK, CHUNK), 0)
    col = lax.broadcasted_iota(jnp.int32, (CHUNK, CHUNK), 1)
    return dict(
        mask_f=col <= row,
        mask_b=col >= row,
        tri_f=jnp.where(col <= row, 1.0, 0.0).astype(BF16),
        tri_b=jnp.where(col >= row, 1.0, 0.0).astype(BF16),
        lane=lax.broadcasted_iota(jnp.int32, (SUBLANES, CHUNK), 1),
    )


DIAG_FACTOR_MAX_LOG2 = 64.0


def _cumsum_stage(lf, has_k, fwd, cst):
    dk = lf.shape[1]
    tri = cst["tri_f"] if fwd else cst["tri_b"]
    hi = lf.astype(BF16)
    lo = (lf - hi.astype(F32)).astype(BF16)
    cc = jnp.dot(tri, jnp.concatenate([hi, lo], axis=1), preferred_element_type=F32)
    c = cc[:, :dk] + cc[:, dk:]
    a = None if has_k else c - jnp.log2(1.0 - jnp.exp2(lf))
    mag = jnp.abs(lf)
    excess = None
    for r0 in range(0, CHUNK, SUB):
        e = jnp.sum(mag[r0:r0 + SUB], axis=0, keepdims=True)
        excess = e if excess is None else jnp.maximum(excess, e)
    return c, a, excess


def _intra_stage(q, k, v, c, a, st_ref, fwd, cst, keep, exact_diag=True):
    dk = c.shape[1]
    tot = c[CHUNK - 1:CHUNK] if fwd else c[0:1]

    if k is None:
        kf = None

        def kscaled(r0, r1, ref):
            return jnp.exp2(ref - a[r0:r1])
    else:
        a = c
        kf = k.astype(F32)

        def kscaled(r0, r1, ref):
            return kf[r0:r1] * jnp.exp2(ref - c[r0:r1])

    kte = kscaled(0, CHUNK, tot).astype(BF16)
    st = st_ref[...]
    if keep is not None:
        keep.prev_state(st.astype(BF16))
    st_ref[...] = st * jnp.exp2(tot) + lax.dot_general(v, kte, TN_DIMS, preferred_element_type=F32)
    if keep is None:
        return
    yield

    qf = q.astype(F32)
    keep.scaled_q((qf * jnp.exp2(c)).astype(BF16))
    n_sub = CHUNK // SUB
    half = SUB // SUBLANES
    own = 0 if exact_diag else SUB
    soffs = []
    for r in range(n_sub):
        r0 = SUB * r
        if fwd:
            lo_row, hi_row = 0, r0 + own
            ref = c[r0:r0 + 1]
        else:
            lo_row, hi_row = r0 + SUB - own, CHUNK
            ref = c[r0 + SUB - 1:r0 + SUB]
        if hi_row == lo_row:
            soffs.append(None)
            continue
        pieces = [kscaled(lo_row, hi_row, ref).astype(BF16)]
        if lo_row > 0:
            pieces.insert(0, jnp.zeros((lo_row, dk), BF16))
        if hi_row < CHUNK:
            pieces.append(jnp.zeros((CHUNK - hi_row, dk), BF16))
        kr = jnp.concatenate(pieces, axis=0) if len(pieces) > 1 else pieces[0]
        qr = (qf[r0:r0 + SUB] * jnp.exp2(c[r0:r0 + SUB] - ref)).astype(BF16)
        soffs.append(lax.dot_general(qr, kr, NT_DIMS, preferred_element_type=F32))
    yield

    if not exact_diag:
        p = jnp.concatenate(soffs, axis=0)
        keep.scores(jnp.where(cst["mask_f"] if fwd else cst["mask_b"], p, 0.0).astype(BF16))
        return

    rows = []
    for r in range(n_sub):
        r0 = SUB * r
        cb = c[r0:r0 + SUB]
        qb = qf[r0:r0 + SUB]
        acc = [jnp.zeros((SUBLANES, CHUNK), F32) for _ in range(half)]
        for jj in range(SUB):
            aj = a[r0 + jj:r0 + jj + 1]
            jg = jj // SUBLANES
            groups = range(jg, half) if fwd else range(0, jg + 1)
            for g in groups:
                e = cb[g * SUBLANES:(g + 1) * SUBLANES] - aj
                if g == jg:
                    e = jnp.minimum(e, 0.0)
                t = qb[g * SUBLANES:(g + 1) * SUBLANES] * jnp.exp2(e)
                if kf is not None:
                    t = t * kf[r0 + jj:r0 + jj + 1]
                sj = jnp.sum(t, axis=1, keepdims=True)
                acc[g] = jnp.where(cst["lane"] == r0 + jj, sj, acc[g])
        d = jnp.concatenate(acc, axis=0)
        rows.append(d if soffs[r] is None else d + soffs[r])
    p = jnp.concatenate(rows, axis=0)
    keep.scores(jnp.where(cst["mask_f"] if fwd else cst["mask_b"], p, 0.0).astype(BF16))


def _run_staged(units):
    live = list(units)
    while live:
        nxt = []
        for u in live:
            try:
                next(u)
                nxt.append(u)
            except StopIteration:
                pass
        live = nxt


def _scan_kernel(*refs, n_heads, dk, dv, has_k, n_ctx_chunks, n_step_chunks, lat_index):
    n_ctx_in = 4 if has_k else 3
    n_lat_in = 4 if has_k else 3
    ctx_refs = refs[:n_ctx_in]
    lat_refs = (refs[n_ctx_in:n_ctx_in + n_lat_in], refs[n_ctx_in + n_lat_in:n_ctx_in + 2 * n_lat_in])
    rest = refs[n_ctx_in + 2 * n_lat_in:]
    out_refs = rest[0:2]
    if has_k:
        st_ref, c_ref, stb_ref = rest[2:]
        a_ref = None
    else:
        st_ref, c_ref, stb_ref, a_ref = rest[2:]
    s = pl.program_id(2)
    cst = _scan_consts()
    units = [(h, d) for h in range(n_heads) for d in range(2)]

    def kcols(h):
        return slice(h * dk, (h + 1) * dk)

    def vcols(h):
        return slice(h * dv, (h + 1) * dv)

    def ctx_step():
        st_ref[...] = jnp.zeros(st_ref.shape, F32)
        if has_k:
            k_ref, v_ref, lff_ref, lfb_ref = ctx_refs
        else:
            v_ref, lff_ref, lfb_ref = ctx_refs
            k_ref = None

        def unit(h, d, rows):
            lf = (lff_ref, lfb_ref)[d][rows, kcols(h)]
            c, a, _ = _cumsum_stage(lf, has_k, d == 0, cst)
            yield
            k = None if k_ref is None else k_ref[rows, kcols(h)]
            yield from _intra_stage(None, k, v_ref[rows, vcols(h)], c, a, st_ref.at[d, h], d == 0, cst, None)

        def body(i, carry):
            rows = (pl.ds(pl.multiple_of(i * CHUNK, CHUNK), CHUNK),
                    pl.ds(pl.multiple_of((n_ctx_chunks - 1 - i) * CHUNK, CHUNK), CHUNK))
            _run_staged([unit(h, d, rows[d]) for h, d in units])
            return carry
        lax.fori_loop(0, n_ctx_chunks, body, 0)

    def latent_step():
        n = n_step_chunks

        def index(d, pos):
            return lat_index(pos if d == 0 else n - 1 - pos)

        def read(d, pos, h, what):
            refs_d = lat_refs[d]
            ref = refs_d[{"q": 0, "k": 1, "v": 2 if has_k else 1, "lf": 3 if has_k else 2}[what]]
            cols = vcols(h) if what == "v" else kcols(h)
            return ref[index(d, pos) + (cols,)]

        def cumsum_all(pos, slot):
            worst = None
            for u, (h, d) in enumerate(units):
                c, a, excess = _cumsum_stage(read(d, pos, h, "lf"), has_k, d == 0, cst)
                c_ref[slot, u] = c
                if a is not None:
                    a_ref[slot, u] = a
                worst = excess if worst is None else jnp.maximum(worst, excess)
            return jnp.max(worst)

        def output_all(pos, carried):
            for u, (h, d) in enumerate(units):
                p, qd = carried[u]
                o = jnp.dot(p, read(d, pos, h, "v"), preferred_element_type=F32)
                o = o + lax.dot_general(qd, stb_ref[u], NT_DIMS, preferred_element_type=F32)
                out_refs[d][index(d, pos) + (vcols(h),)] = o.astype(out_refs[d].dtype)

        class Keeper:
            def __init__(self, u):
                self.u = u
                self.p = self.qd = None

            def prev_state(self, stb):
                stb_ref[self.u] = stb

            def scaled_q(self, qd):
                self.qd = qd

            def scores(self, p):
                self.p = p

        def intra_all(pos, cs, exact_diag):
            keepers = [Keeper(u) for u in range(len(units))]
            gens = []
            for u, (h, d) in enumerate(units):
                k = read(d, pos, h, "k") if has_k else None
                c, a = cs[u]
                gens.append(_intra_stage(read(d, pos, h, "q"), k, read(d, pos, h, "v"), c, a,
                                         st_ref.at[d, h], d == 0, cst, keepers[u], exact_diag))
            _run_staged(gens)
            return tuple((kp.p, kp.qd) for kp in keepers)

        def load_cumsums(slot):
            return [(c_ref[slot, u], None if has_k else a_ref[slot, u]) for u in range(len(units))]

        def either_path(worst, build):
            return lax.cond(worst <= DIAG_FACTOR_MAX_LOG2,
                            functools.partial(build, False), functools.partial(build, True))

        worst0 = cumsum_all(0, 0)
        cs0 = load_cumsums(0)
        worst1 = cumsum_all(1, 1)
        carried0 = either_path(worst0, lambda exact_diag: intra_all(0, cs0, exact_diag))

        def body(i, state):
            worst, carried = state
            slot = lax.rem(i, 2)

            def iteration(exact_diag):
                cs = load_cumsums(slot)
                output_all(i - 1, carried)
                worst_next = cumsum_all(jnp.minimum(i + 1, n - 1), 1 - slot)
                return worst_next, intra_all(i, cs, exact_diag)
            return either_path(worst, iteration)
        _, carried_last = lax.fori_loop(1, n, body, (worst1, carried0))
        output_all(n - 1, carried_last)

    @pl.when(s == 0)
    def _():
        ctx_step()

    @pl.when(s > 0)
    def _():
        latent_step()


def _scan_call(inputs, in_specs, out_struct, out_specs, grid, *, n_heads, dk, dv, has_k,
               n_ctx_chunks, n_step_chunks, lat_index, name):
    n_units = 2 * n_heads
    scratch = [pltpu.VMEM((2, n_heads, dv, dk), F32),
               pltpu.VMEM((2, n_units, CHUNK, dk), F32),
               pltpu.VMEM((n_units, dv, dk), BF16)]
    if not has_k:
        scratch.append(pltpu.VMEM((2, n_units, CHUNK, dk), F32))
    return pl.pallas_call(
        functools.partial(_scan_kernel, n_heads=n_heads, dk=dk, dv=dv, has_k=has_k,
                          n_ctx_chunks=n_ctx_chunks, n_step_chunks=n_step_chunks, lat_index=lat_index),
        grid=grid,
        in_specs=in_specs,
        out_specs=out_specs,
        out_shape=out_struct,
        scratch_shapes=scratch,
        compiler_params=pltpu.CompilerParams(
            dimension_semantics=("arbitrary", "arbitrary", "arbitrary"),
            vmem_limit_bytes=VMEM_LIMIT_BYTES),
        name=name,
    )(*inputs)


A_SCAN_HEADS = 2
A_SCAN_CHUNKS = 8


def _scan_a(c_qi, c_lf, qi, lf):
    bsz, n_ctx, _ = c_qi.shape
    seq = qi.shape[1]
    g = A_SCAN_HEADS
    gw = g * A_DK
    n_hg = A_HEADS // g
    ts = A_SCAN_CHUNKS * CHUNK
    n_steps = seq // ts

    def fstep(s):
        return jnp.maximum(s - 1, 0)

    def bstep(s):
        return n_steps - 1 - jnp.maximum(s - 1, 0)

    cspec = lambda off: pl.BlockSpec((None, n_ctx, gw), lambda b, h, s: (b, 0, off + h))
    fspec = lambda off: pl.BlockSpec((None, ts, gw), lambda b, h, s: (b, fstep(s), off + h))
    bspec = lambda off: pl.BlockSpec((None, ts, gw), lambda b, h, s: (b, bstep(s), off + h))
    in_specs = [cspec(n_hg), cspec(0), cspec(n_hg),
                fspec(0), fspec(n_hg), fspec(0),
                bspec(0), bspec(n_hg), bspec(n_hg)]
    inputs = [c_qi, c_lf, c_lf, qi, qi, lf, qi, qi, lf]
    out_struct = [jax.ShapeDtypeStruct((bsz, seq, A_HEADS * A_DV), BF16)] * 2
    out_specs = [pl.BlockSpec((None, ts, gw), lambda b, h, s: (b, fstep(s), h)),
                 pl.BlockSpec((None, ts, gw), lambda b, h, s: (b, bstep(s), h))]

    def lat_index(chunk):
        row = chunk * CHUNK
        if not isinstance(row, int):
            row = pl.multiple_of(row, CHUNK)
        return (pl.ds(row, CHUNK),)

    return _scan_call(inputs, in_specs, out_struct, out_specs, (bsz, n_hg, n_steps + 1),
                      n_heads=g, dk=A_DK, dv=A_DV, has_k=False,
                      n_ctx_chunks=n_ctx // CHUNK, n_step_chunks=A_SCAN_CHUNKS,
                      lat_index=lat_index, name="scan_hgrn2")


B_SCAN_COLS = 4


def _scan_b(c_qkv, c_lf, qkv, lf):
    bsz, n_ctx, _ = c_qkv.shape
    n_rows = qkv.shape[2]
    kw = B_HEADS * B_DK
    nc = B_SCAN_COLS
    n_steps = GRID_W // nc
    chunks_per_col = n_rows // CHUNK

    def fblk(s):
        return jnp.maximum(s - 1, 0)

    def bblk(s):
        return n_steps - 1 - jnp.maximum(s - 1, 0)

    def cspec(width, off):
        return pl.BlockSpec((None, n_ctx, width), lambda b, h, s: (b, 0, off + h))

    def lspec(width, off, blk):
        return pl.BlockSpec((None, nc, n_rows, width), lambda b, h, s: (b, blk(s), 0, off + h))

    k_off, v_off = kw // B_DK, (2 * kw) // B_DV
    in_specs = [cspec(B_DK, k_off), cspec(B_DV, v_off), cspec(B_DK, 0), cspec(B_DK, B_HEADS),
                lspec(B_DK, 0, fblk), lspec(B_DK, k_off, fblk), lspec(B_DV, v_off, fblk), lspec(B_DK, 0, fblk),
                lspec(B_DK, 0, bblk), lspec(B_DK, k_off, bblk), lspec(B_DV, v_off, bblk),
                lspec(B_DK, B_HEADS, bblk)]
    inputs = [c_qkv, c_qkv, c_lf, c_lf, qkv, qkv, qkv, lf, qkv, qkv, qkv, lf]
    vw = B_HEADS * B_DV
    out_struct = [jax.ShapeDtypeStruct((bsz, GRID_W, n_rows, vw), BF16)] * 2
    out_specs = [pl.BlockSpec((None, nc, n_rows, B_DV), lambda b, h, s: (b, fblk(s), 0, h)),
                 pl.BlockSpec((None, nc, n_rows, B_DV), lambda b, h, s: (b, bblk(s), 0, h))]

    def lat_index(chunk):
        col = chunk // chunks_per_col
        row = (chunk - col * chunks_per_col) * CHUNK
        if not isinstance(row, int):
            row = pl.multiple_of(row, CHUNK)
        return (col, pl.ds(row, CHUNK))

    return _scan_call(inputs, in_specs, out_struct, out_specs, (bsz, B_HEADS, n_steps + 1),
                      n_heads=1, dk=B_DK, dv=B_DV, has_k=True,
                      n_ctx_chunks=n_ctx // CHUNK, n_step_chunks=nc * chunks_per_col,
                      lat_index=lat_index, name="scan_gla")


GLA_FIN_ROWS = 16


def _gla_finalize_kernel(of_ref, ob_ref, sg_ref, g_ref, o_ref):
    n_cols, n_rows, w = of_ref.shape
    o = of_ref[...].astype(F32) + ob_ref[...].astype(F32)
    ms = jnp.mean(o * o, axis=-1, keepdims=True)
    on = o * lax.rsqrt(ms + EPS) * g_ref[...]
    on = jnp.swapaxes(on, 0, 1).reshape(n_rows * n_cols, w)
    o_ref[...] = (on * sg_ref[...].astype(F32)).astype(o_ref.dtype)


def _gla_finalize(of, ob, gates, gain, *, sg_col_off):
    bsz, _, n_rows, vw = of.shape
    tm = GLA_FIN_ROWS * GRID_W
    tiles = n_rows // GLA_FIN_ROWS
    cm = pl.BlockSpec((None, GRID_W, GLA_FIN_ROWS, B_DV), lambda b, r, h: (b, 0, r, h))
    return pl.pallas_call(
        _gla_finalize_kernel,
        grid=(bsz, tiles, B_HEADS),
        in_specs=[cm, cm,
                  pl.BlockSpec((tm, B_DV), lambda b, r, h: (b * tiles + r, sg_col_off + h)),
                  pl.BlockSpec((1, B_DV), lambda b, r, h: (0, h))],
        out_specs=pl.BlockSpec((tm, B_DV), lambda b, r, h: (b * tiles + r, h)),
        out_shape=jax.ShapeDtypeStruct((bsz * n_rows * GRID_W, vw), BF16),
        compiler_params=pltpu.CompilerParams(
            dimension_semantics=("arbitrary", "arbitrary", "arbitrary"),
            vmem_limit_bytes=32 * 1024 * 1024),
        name="gla_finalize",
    )(of, ob, gates, gain)


def _merge_kernel(oaf, oab, obp, sga, sma, smb, ga, wpa, wpb, y_ref):
    o = oaf[...].astype(F32) + oab[...].astype(F32)
    parts = []
    for h in range(A_HEADS):
        oh = o[:, h * A_DV:(h + 1) * A_DV]
        ms = jnp.mean(oh * oh, axis=-1, keepdims=True)
        parts.append(oh * lax.rsqrt(ms + EPS))
    oa = (jnp.concatenate(parts, axis=1) * ga[...] * sga[...].astype(F32)).astype(BF16)
    ya = jnp.dot(oa, wpa[...], preferred_element_type=F32)
    yb = jnp.dot(obp[...], wpb[...], preferred_element_type=F32)
    y_ref[...] = (sma[...].astype(F32) * ya + smb[...].astype(F32) * yb).astype(BF16)


def _merge(oaf, oab, obp, gates, ga, wpa, wpb, *, tm):
    n_tok, d = oaf.shape
    tok = lambda off: pl.BlockSpec((tm, d), lambda i: (i, off))
    const = lambda shape: pl.BlockSpec(shape, lambda i: (0, 0), pipeline_mode=pl.Buffered(1))
    return pl.pallas_call(
        _merge_kernel,
        grid=(n_tok // tm,),
        in_specs=[tok(0), tok(0), tok(0), tok(0), tok(2), tok(3),
                  const((1, d)), const(wpa.shape), const(wpb.shape)],
        out_specs=pl.BlockSpec((tm, wpa.shape[1]), lambda i: (i, 0)),
        out_shape=jax.ShapeDtypeStruct((n_tok, wpa.shape[1]), BF16),
        compiler_params=pltpu.CompilerParams(
            dimension_semantics=("arbitrary",), vmem_limit_bytes=VMEM_LIMIT_BYTES),
        name="merge_proj",
    )(oaf, oab, obp, gates, gates, gates, ga, wpa, wpb)


def _final_kernel(y_ref, x_ref, mod_ref, w_ref, g_ref, o_ref):
    yo = jnp.dot(y_ref[...], w_ref[...], preferred_element_type=F32)
    z = x_ref[...] + mod_ref[0, 2:3, :] * yo
    ms = jnp.mean(z * z, axis=-1, keepdims=True)
    o_ref[...] = z * lax.rsqrt(ms + EPS) * g_ref[...]


def _final(y, x2d, mod3, w_out, fg, *, tm, tiles_per_batch):
    n_tok, d = x2d.shape
    const = lambda shape: pl.BlockSpec(shape, lambda i: (0, 0), pipeline_mode=pl.Buffered(1))
    return pl.pallas_call(
        _final_kernel,
        grid=(n_tok // tm,),
        in_specs=[pl.BlockSpec((tm, d), lambda i: (i, 0)),
                  pl.BlockSpec((tm, d), lambda i: (i, 0)),
                  pl.BlockSpec((1, 3, d), lambda i: (i // tiles_per_batch, 0, 0)),
                  const(w_out.shape), const((1, d))],
        out_specs=pl.BlockSpec((tm, d), lambda i: (i, 0)),
        out_shape=jax.ShapeDtypeStruct((n_tok, d), F32),
        compiler_params=pltpu.CompilerParams(
            dimension_semantics=("arbitrary",), vmem_limit_bytes=VMEM_LIMIT_BYTES),
        name="out_proj_final",
    )(y, x2d, mod3, w_out, fg)


def kernel(x, c, ctx, c_ctx, w_ada, b_ada, norm_g, w_in, hgrn_lb_logits, gla_w_gk, gla_b_gk,
           hgrn_onorm_g, gla_onorm_g, w_pa, w_pb, w_out, final_norm_g):
    bsz, seq, d = x.shape
    n_ctx = ctx.shape[1]
    depth = w_in.shape[0]
    assert depth == 1, "single-layer trunk"
    a_kw, a_vw = A_HEADS * A_DK, A_HEADS * A_DV
    b_kw, b_vw = B_HEADS * B_DK, B_HEADS * B_DV

    n_rows = -(-(bsz + 1) // SUBLANES) * SUBLANES
    cvec = jnp.zeros((n_rows, d), F32).at[:bsz].set(c).at[bsz].set(c_ctx)
    mod = _adaln(cvec, w_ada[0], b_ada[0].reshape(1, -1))
    mod3 = mod.reshape(n_rows, 3, d)

    lb = _lower_bounds(hgrn_lb_logits)[0:1]

    w = w_in[0]
    o_ag = 2 * a_kw + 2 * a_vw
    o_bq = o_ag + a_vw
    o_br = o_bq + 2 * b_kw + b_vw
    o_bg = o_br + 2 * B_RANK
    w_main = jnp.concatenate(
        [w[:, 0:a_kw + a_vw], w[:, a_kw + a_vw:o_ag], w[:, o_bq:o_br], w[:, o_ag:o_bq], w[:, o_bg:]],
        axis=1).astype(BF16)
    w_r = jnp.pad(w[:, o_br:o_bg], ((0, 0), (0, LANES - 2 * B_RANK))).astype(BF16)
    gk_pad = jnp.zeros((LANES, 2 * b_kw), F32)
    gk_pad = gk_pad.at[0:B_RANK, 0:b_kw].set(gla_w_gk[0, 0]).at[B_RANK:2 * B_RANK, b_kw:].set(gla_w_gk[0, 1])
    gk_pad = gk_pad.astype(BF16)
    gkb = gla_b_gk[0].reshape(1, 2 * b_kw)
    ng = norm_g[0].reshape(1, d)

    x2d = x.reshape(bsz * seq, d)
    ctx2d = ctx.reshape(bsz * n_ctx, d)
    tm = 1024
    tiles_per_batch = seq // tm

    c_aqi, c_alf, c_bqkv, c_blf = _inproj(
        ctx2d, mod3, lambda i: bsz, ng, w_main, w_r, lb, gk_pad, gkb,
        tm=bsz * n_ctx, latent=False)
    aqi, alf, bqkv, blf, gates = _inproj(
        x2d, mod3, lambda i: i // tiles_per_batch, ng, w_main, w_r, lb, gk_pad, gkb,
        tm=tm, latent=True, n_batch=bsz)

    r3 = lambda t, n: t.reshape(bsz, n, t.shape[-1])
    oaf, oab = _scan_a(r3(c_aqi, n_ctx), r3(c_alf, n_ctx), r3(aqi, seq), r3(alf, seq))
    obf, obb = _scan_b(r3(c_bqkv, n_ctx), r3(c_blf, n_ctx), bqkv, blf)
    obp = _gla_finalize(obf, obb, gates, gla_onorm_g[0].reshape(1, -1), sg_col_off=a_vw // B_DV)

    r2 = lambda t: t.reshape(bsz * seq, t.shape[-1])
    y = _merge(r2(oaf), r2(oab), obp, gates, hgrn_onorm_g[0].reshape(1, -1),
               w_pa[0].astype(BF16), w_pb[0].astype(BF16), tm=256)
    out = _final(y, x2d, mod3, w_out[0].astype(BF16), final_norm_g.reshape(1, d),
                 tm=512, tiles_per_batch=seq // 512)
    return out.reshape(bsz, seq, d)
```

```python
import functools

import jax
import jax.numpy as jnp
from jax import lax
from jax.experimental import pallas as pl
from jax.experimental.pallas import tpu as pltpu

F32 = jnp.float32
BF16 = jnp.bfloat16

CHUNK = 64
SUB = 16
GRID_W = 64
EPS = 1e-6
A_HEADS, A_DK, A_DV = 16, 128, 128
B_HEADS, B_DK, B_DV = 4, 256, 512
B_RANK = 16
GATE_NORMALIZER = 16.0
LOG2E = 1.4426950408889634

VMEM_LIMIT_BYTES = 56 * 1024 * 1024
LANES = 128
SUBLANES = 8

NT_DIMS = (((1,), (1,)), ((), ()))
TN_DIMS = (((0,), (0,)), ((), ()))


def _sigmoid(z):
    return 1.0 / (1.0 + jnp.exp(-z))


def _adaln_kernel(c_ref, w_ref, b_ref, o_ref):
    c = c_ref[...]
    s = c * _sigmoid(c)
    o_ref[...] = jnp.dot(s, w_ref[...], preferred_element_type=F32,
                         precision=lax.Precision.HIGHEST) + b_ref[...]


def _adaln(cvec, w, b):
    rows, d = cvec.shape
    n = w.shape[1]
    tn = 768
    return pl.pallas_call(
        _adaln_kernel,
        grid=(n // tn,),
        in_specs=[pl.BlockSpec((rows, d), lambda j: (0, 0)),
                  pl.BlockSpec((d, tn), lambda j: (0, j)),
                  pl.BlockSpec((1, tn), lambda j: (0, j))],
        out_specs=pl.BlockSpec((rows, tn), lambda j: (0, j)),
        out_shape=jax.ShapeDtypeStruct((rows, n), F32),
        compiler_params=pltpu.CompilerParams(
            dimension_semantics=("arbitrary",), vmem_limit_bytes=32 * 1024 * 1024),
        name="adaln",
    )(cvec, w, b)


def _lower_bound_kernel(l_ref, o_ref):
    x = l_ref[...]
    n_rows = x.shape[0]
    m = jnp.max(x, axis=0, keepdims=True)
    e = jnp.exp(x - m)
    tot = jnp.sum(e, axis=0, keepdims=True)
    run = jnp.zeros_like(tot)
    for r in range(n_rows - 1):
        run = run + e[r:r + 1]
        o_ref[r:r + 1, :] = run / tot


def _lower_bounds(logits):
    n_rows, w = logits.shape
    return pl.pallas_call(
        _lower_bound_kernel,
        out_shape=jax.ShapeDtypeStruct((n_rows - 1, w), F32),
        name="hgrn_lower_bounds",
    )(logits)


def _prep_weights_kernel(wa_ref, wb_ref, o_ref, r_ref, *, j_tail, shift):
    j = pl.program_id(0)

    @pl.when(j < j_tail)
    def _():
        o_ref[...] = wa_ref[...].astype(BF16)

    @pl.when(j >= j_tail)
    def _():
        o_ref[...] = jnp.concatenate([wa_ref[:, shift:], wb_ref[:, :shift]], axis=1).astype(BF16)

    @pl.when(j == j_tail)
    def _():
        rank_cols = wa_ref[:, :shift].astype(BF16)
        r_ref[...] = jnp.concatenate(
            [rank_cols, jnp.zeros((rank_cols.shape[0], LANES - shift), BF16)], axis=1)


def _prep_weights(w, group_starts, tail_start):
    d, n_in = w.shape
    shift = 2 * B_RANK
    src_tiles = []
    for start, width in group_starts:
        assert start % TN == 0 and width % TN == 0
        src_tiles += list(range(start // TN, (start + width) // TN))
    assert tail_start % TN == 0
    j_tail = len(src_tiles)
    n_tail = (n_in - tail_start - shift) // TN
    n_out = j_tail + n_tail
    last_src = (n_in - 1) // TN

    def src_a(j):
        idx = j - j_tail + tail_start // TN
        for dst in range(j_tail - 1, -1, -1):
            idx = jnp.where(j == dst, src_tiles[dst], idx)
        return idx

    return pl.pallas_call(
        functools.partial(_prep_weights_kernel, j_tail=j_tail, shift=shift),
        grid=(n_out,),
        in_specs=[pl.BlockSpec((d, TN), lambda j: (0, src_a(j))),
                  pl.BlockSpec((d, TN), lambda j: (0, jnp.minimum(src_a(j) + 1, last_src)))],
        out_specs=[pl.BlockSpec((d, TN), lambda j: (0, j)),
                   pl.BlockSpec((d, LANES), lambda j: (0, 0))],
        out_shape=[jax.ShapeDtypeStruct((d, n_out * TN), BF16),
                   jax.ShapeDtypeStruct((d, LANES), BF16)],
        compiler_params=pltpu.CompilerParams(
            dimension_semantics=("arbitrary",), vmem_limit_bytes=40 * 1024 * 1024),
        name="prep_weights",
    )(w, w)


TN = 512
NORM_ROWS = 128
EPILOGUE_PARTS = 8
T_AQI = 4096 // TN
T_ALF = 4096 // TN
T_BQKV = 4096 // TN
T_BLF = 2048 // TN
T_GATES = 8192 // TN
J_ALF = T_AQI
J_BQKV = J_ALF + T_ALF
J_BLF = J_BQKV + T_BQKV
J_GATES = J_BLF + T_BLF
J_END = J_GATES + T_GATES


def _inproj_kernel(x_ref, mod_ref, ng_ref, w_ref, wr_ref, lb_ref, gkw_ref, gkb_ref, *rest, latent):
    if latent:
        aqi_ref, alf_ref, bqkv_ref, blf_ref, gates_ref, h_ref, r_ref = rest
    else:
        aqi_ref, alf_ref, bqkv_ref, blf_ref, h_ref, r_ref = rest
        gates_ref = None
    j = pl.program_id(1)

    def store_gla(out_ref, z):
        if not latent:
            out_ref[...] = z.astype(out_ref.dtype)
            return
        n_rows = z.shape[0] // GRID_W
        zc = jnp.swapaxes(z.reshape(n_rows, GRID_W, z.shape[1]), 0, 1)
        out_ref[...] = zc.astype(out_ref.dtype)

    @pl.when(j == 0)
    def _():
        def norm_rows(t, carry):
            rows = pl.ds(pl.multiple_of(t * NORM_ROWS, NORM_ROWS), NORM_ROWS)
            x = x_ref[rows, :]
            ms = jnp.mean(x * x, axis=-1, keepdims=True)
            y = x * lax.rsqrt(ms + EPS) * ng_ref[...]
            h = (y * (1.0 + mod_ref[0, 1:2, :]) + mod_ref[0, 0:1, :]).astype(BF16)
            h_ref[rows, :] = h
            r_ref[rows, :] = jnp.dot(h, wr_ref[...], preferred_element_type=F32).astype(BF16)
            return carry
        lax.fori_loop(0, x_ref.shape[0] // NORM_ROWS, norm_rows, 0)

    def mm():
        return jnp.dot(h_ref[...], w_ref[...], preferred_element_type=F32)

    def mm_parts(out_ref, epilogue):
        part = h_ref.shape[0] // EPILOGUE_PARTS
        for m in range(EPILOGUE_PARTS):
            rows = slice(m * part, (m + 1) * part)
            z = jnp.dot(h_ref[rows, :], w_ref[...], preferred_element_type=F32)
            out_ref[rows, :] = epilogue(z).astype(out_ref.dtype)

    @pl.when(j < J_ALF)
    def _():
        mm_parts(aqi_ref, lambda z: z)

    @pl.when((j >= J_ALF) & (j < J_BQKV))
    def _():
        lb = lb_ref[...]
        mm_parts(alf_ref, lambda z: jnp.log2(lb + (1.0 - lb) * _sigmoid(z)))

    @pl.when((j >= J_BQKV) & (j < J_BLF))
    def _():
        scale = jnp.where(j < J_BQKV + (B_HEADS * B_DK) // TN, B_DK ** -0.5, 1.0)
        store_gla(bqkv_ref, mm() * scale)

    @pl.when((j >= J_BLF) & (j < J_GATES))
    def _():
        z = jnp.dot(r_ref[...], gkw_ref[...], preferred_element_type=F32) + gkb_ref[...]
        ls = jnp.minimum(z, 0.0) - jnp.log(1.0 + jnp.exp(-jnp.abs(z)))
        store_gla(blf_ref, ls * (LOG2E / GATE_NORMALIZER))

    if latent:
        @pl.when(j >= J_GATES)
        def _():
            silu = j < J_GATES + T_GATES // 2
            mm_parts(gates_ref, lambda z: _sigmoid(z) * jnp.where(silu, z, 1.0))


def _inproj(x2d, mod3, mod_row_of_tile, norm_g, w_main, w_r, lb, gk_pad, gkb, *, tm, latent, n_batch=1):
    n_tok, d = x2d.shape
    n_j = J_END if latent else J_GATES
    tile_rows = tm // GRID_W
    seq_rows = n_tok // n_batch // GRID_W
    tiles_per_batch = n_tok // n_batch // tm

    def cl(j, lo, n):
        return jnp.clip(j - lo, 0, n - 1)

    def gla_spec(j_lo, n):
        if not latent:
            return pl.BlockSpec((tm, TN), lambda i, j: (i, cl(j, j_lo, n)))
        return pl.BlockSpec((None, GRID_W, tile_rows, TN),
                            lambda i, j: (i // tiles_per_batch, 0, i % tiles_per_batch, cl(j, j_lo, n)))

    def gla_shape(n, dtype):
        if not latent:
            return jax.ShapeDtypeStruct((n_tok, n * TN), dtype)
        return jax.ShapeDtypeStruct((n_batch, GRID_W, seq_rows, n * TN), dtype)

    in_specs = [
        pl.BlockSpec((tm, d), lambda i, j: (i, 0)),
        pl.BlockSpec((1, 3, d), lambda i, j: (mod_row_of_tile(i), 0, 0)),
        pl.BlockSpec((1, d), lambda i, j: (0, 0)),
        pl.BlockSpec((d, TN), lambda i, j: (0, j - jnp.clip(j - (J_BLF - 1), 0, T_BLF))),
        pl.BlockSpec((d, LANES), lambda i, j: (0, 0)),
        pl.BlockSpec((1, TN), lambda i, j: (0, cl(j, J_ALF, T_ALF))),
        pl.BlockSpec((LANES, TN), lambda i, j: (0, cl(j, J_BLF, T_BLF))),
        pl.BlockSpec((1, TN), lambda i, j: (0, cl(j, J_BLF, T_BLF))),
    ]
    out_specs = [
        pl.BlockSpec((tm, TN), lambda i, j: (i, cl(j, 0, T_AQI))),
        pl.BlockSpec((tm, TN), lambda i, j: (i, cl(j, J_ALF, T_ALF))),
        gla_spec(J_BQKV, T_BQKV),
        gla_spec(J_BLF, T_BLF),
    ]
    out_shape = [
        jax.ShapeDtypeStruct((n_tok, T_AQI * TN), BF16),
        jax.ShapeDtypeStruct((n_tok, T_ALF * TN), F32),
        gla_shape(T_BQKV, BF16),
        gla_shape(T_BLF, F32),
    ]
    if latent:
        out_specs.append(pl.BlockSpec((tm, TN), lambda i, j: (i, cl(j, J_GATES, T_GATES))))
        out_shape.append(jax.ShapeDtypeStruct((n_tok, T_GATES * TN), BF16))
    return pl.pallas_call(
        functools.partial(_inproj_kernel, latent=latent),
        grid=(n_tok // tm, n_j),
        in_specs=in_specs,
        out_specs=out_specs,
        out_shape=out_shape,
        scratch_shapes=[pltpu.VMEM((tm, d), BF16), pltpu.VMEM((tm, LANES), BF16)],
        compiler_params=pltpu.CompilerParams(
            dimension_semantics=("arbitrary", "arbitrary"), vmem_limit_bytes=VMEM_LIMIT_BYTES),
        name="inproj_latent" if latent else "inproj_ctx",
    )(x2d, mod3, norm_g, w_main, w_r, lb, gk_pad, gkb)


def _scan_consts():
    row = lax.broadcasted_iota(jnp.int32, (CHUNK, CHUNK), 0)
    col = lax.broadcasted_iota(jnp.int32, (CHUNK, CHUNK), 1)
    return dict(
        mask_f=col <= row,
        mask_b=col >= row,
        tri_f=jnp.where(col <= row, 1.0, 0.0).astype(BF16),
        tri_b=jnp.where(col >= row, 1.0, 0.0).astype(BF16),
        lane=lax.broadcasted_iota(jnp.int32, (SUBLANES, CHUNK), 1),
    )


DIAG_FACTOR_MAX_LOG2 = 64.0


def _cumsum_stage(lf, has_k, fwd, cst):
    dk = lf.shape[1]
    tri = cst["tri_f"] if fwd else cst["tri_b"]
    hi = lf.astype(BF16)
    lo = (lf - hi.astype(F32)).astype(BF16)
    cc = jnp.dot(tri, jnp.concatenate([hi, lo], axis=1), preferred_element_type=F32)
    c = cc[:, :dk] + cc[:, dk:]
    a = None if has_k else c - jnp.log2(1.0 - jnp.exp2(lf))
    mag = jnp.abs(lf)
    excess = None
    for r0 in range(0, CHUNK, SUB):
        e = jnp.sum(mag[r0:r0 + SUB], axis=0, keepdims=True)
        excess = e if excess is None else jnp.maximum(excess, e)
    return c, a, excess


def _intra_stage(q, k, v, c, a, st_ref, fwd, cst, keep, exact_diag=True):
    dk = c.shape[1]
    tot = c[CHUNK - 1:CHUNK] if fwd else c[0:1]

    if k is None:
        kf = None

        def kscaled(r0, r1, ref):
            return jnp.exp2(ref - a[r0:r1])
    else:
        a = c
        kf = k.astype(F32)

        def kscaled(r0, r1, ref):
            return kf[r0:r1] * jnp.exp2(ref - c[r0:r1])

    kte = kscaled(0, CHUNK, tot).astype(BF16)
    st = st_ref[...]
    if keep is not None:
        keep.prev_state(st.astype(BF16))
    st_ref[...] = st * jnp.exp2(tot) + lax.dot_general(v, kte, TN_DIMS, preferred_element_type=F32)
    if keep is None:
        return
    yield

    qf = q.astype(F32)
    keep.scaled_q((qf * jnp.exp2(c)).astype(BF16))
    n_sub = CHUNK // SUB
    half = SUB // SUBLANES
    own = 0 if exact_diag else SUB
    soffs = []
    for r in range(n_sub):
        r0 = SUB * r
        if fwd:
            lo_row, hi_row = 0, r0 + own
            ref = c[r0:r0 + 1]
        else:
            lo_row, hi_row = r0 + SUB - own, CHUNK
            ref = c[r0 + SUB - 1:r0 + SUB]
        if hi_row == lo_row:
            soffs.append(None)
            continue
        pieces = [kscaled(lo_row, hi_row, ref).astype(BF16)]
        if lo_row > 0:
            pieces.insert(0, jnp.zeros((lo_row, dk), BF16))
        if hi_row < CHUNK:
            pieces.append(jnp.zeros((CHUNK - hi_row, dk), BF16))
        kr = jnp.concatenate(pieces, axis=0) if len(pieces) > 1 else pieces[0]
        qr = (qf[r0:r0 + SUB] * jnp.exp2(c[r0:r0 + SUB] - ref)).astype(BF16)
        soffs.append(lax.dot_general(qr, kr, NT_DIMS, preferred_element_type=F32))
    yield

    if not exact_diag:
        p = jnp.concatenate(soffs, axis=0)
        keep.scores(jnp.where(cst["mask_f"] if fwd else cst["mask_b"], p, 0.0).astype(BF16))
        return

    rows = []
    for r in range(n_sub):
        r0 = SUB * r
        cb = c[r0:r0 + SUB]
        qb = qf[r0:r0 + SUB]
        acc = [jnp.zeros((SUBLANES, CHUNK), F32) for _ in range(half)]
        for jj in range(SUB):
            aj = a[r0 + jj:r0 + jj + 1]
            jg = jj // SUBLANES
            groups = range(jg, half) if fwd else range(0, jg + 1)
            for g in groups:
                e = cb[g * SUBLANES:(g + 1) * SUBLANES] - aj
                if g == jg:
                    e = jnp.minimum(e, 0.0)
                t = qb[g * SUBLANES:(g + 1) * SUBLANES] * jnp.exp2(e)
                if kf is not None:
                    t = t * kf[r0 + jj:r0 + jj + 1]
                sj = jnp.sum(t, axis=1, keepdims=True)
                acc[g] = jnp.where(cst["lane"] == r0 + jj, sj, acc[g])
        d = jnp.concatenate(acc, axis=0)
        rows.append(d if soffs[r] is None else d + soffs[r])
    p = jnp.concatenate(rows, axis=0)
    keep.scores(jnp.where(cst["mask_f"] if fwd else cst["mask_b"], p, 0.0).astype(BF16))


def _run_staged(units):
    live = list(units)
    while live:
        nxt = []
        for u in live:
            try:
                next(u)
                nxt.append(u)
            except StopIteration:
                pass
        live = nxt


def _scan_kernel(*refs, n_heads, dk, dv, has_k, n_ctx_chunks, n_step_chunks, lat_index):
    n_ctx_in = 4 if has_k else 3
    n_lat_in = 4 if has_k else 3
    ctx_refs = refs[:n_ctx_in]
    lat_refs = (refs[n_ctx_in:n_ctx_in + n_lat_in], refs[n_ctx_in + n_lat_in:n_ctx_in + 2 * n_lat_in])
    rest = refs[n_ctx_in + 2 * n_lat_in:]
    out_refs = rest[0:2]
    if has_k:
        st_ref, c_ref, stb_ref = rest[2:]
        a_ref = None
    else:
        st_ref, c_ref, stb_ref, a_ref = rest[2:]
    s = pl.program_id(2)
    cst = _scan_consts()
    units = [(h, d) for h in range(n_heads) for d in range(2)]

    def kcols(h):
        return slice(h * dk, (h + 1) * dk)

    def vcols(h):
        return slice(h * dv, (h + 1) * dv)

    def ctx_step():
        st_ref[...] = jnp.zeros(st_ref.shape, F32)
        if has_k:
            k_ref, v_ref, lff_ref, lfb_ref = ctx_refs
        else:
            v_ref, lff_ref, lfb_ref = ctx_refs
            k_ref = None

        def unit(h, d, rows):
            lf = (lff_ref, lfb_ref)[d][rows, kcols(h)]
            c, a, _ = _cumsum_stage(lf, has_k, d == 0, cst)
            yield
            k = None if k_ref is None else k_ref[rows, kcols(h)]
            yield from _intra_stage(None, k, v_ref[rows, vcols(h)], c, a, st_ref.at[d, h], d == 0, cst, None)

        def body(i, carry):
            rows = (pl.ds(pl.multiple_of(i * CHUNK, CHUNK), CHUNK),
                    pl.ds(pl.multiple_of((n_ctx_chunks - 1 - i) * CHUNK, CHUNK), CHUNK))
            _run_staged([unit(h, d, rows[d]) for h, d in units])
            return carry
        lax.fori_loop(0, n_ctx_chunks, body, 0)

    def latent_step():
        n = n_step_chunks

        def index(d, pos):
            return lat_index(pos if d == 0 else n - 1 - pos)

        def read(d, pos, h, what):
            refs_d = lat_refs[d]
            ref = refs_d[{"q": 0, "k": 1, "v": 2 if has_k else 1, "lf": 3 if has_k else 2}[what]]
            cols = vcols(h) if what == "v" else kcols(h)
            return ref[index(d, pos) + (cols,)]

        def cumsum_all(pos, slot):
            worst = None
            for u, (h, d) in enumerate(units):
                c, a, excess = _cumsum_stage(read(d, pos, h, "lf"), has_k, d == 0, cst)
                c_ref[slot, u] = c
                if a is not None:
                    a_ref[slot, u] = a
                worst = excess if worst is None else jnp.maximum(worst, excess)
            return jnp.max(worst)

        def output_all(pos, carried):
            for u, (h, d) in enumerate(units):
                p, qd = carried[u]
                o = jnp.dot(p, read(d, pos, h, "v"), preferred_element_type=F32)
                o = o + lax.dot_general(qd, stb_ref[u], NT_DIMS, preferred_element_type=F32)
                out_refs[d][index(d, pos) + (vcols(h),)] = o.astype(out_refs[d].dtype)

        class Keeper:
            def __init__(self, u):
                self.u = u
                self.p = self.qd = None

            def prev_state(self, stb):
                stb_ref[self.u] = stb

            def scaled_q(self, qd):
                self.qd = qd

            def scores(self, p):
                self.p = p

        def intra_all(pos, cs, exact_diag):
            keepers = [Keeper(u) for u in range(len(units))]
            gens = []
            for u, (h, d) in enumerate(units):
                k = read(d, pos, h, "k") if has_k else None
                c, a = cs[u]
                gens.append(_intra_stage(read(d, pos, h, "q"), k, read(d, pos, h, "v"), c, a,
                                         st_ref.at[d, h], d == 0, cst, keepers[u], exact_diag))
            _run_staged(gens)
            return tuple((kp.p, kp.qd) for kp in keepers)

        def load_cumsums(slot):
            return [(c_ref[slot, u], None if has_k else a_ref[slot, u]) for u in range(len(units))]

        def either_path(worst, build):
            return lax.cond(worst <= DIAG_FACTOR_MAX_LOG2,
                            functools.partial(build, False), functools.partial(build, True))

        worst0 = cumsum_all(0, 0)
        cs0 = load_cumsums(0)
        worst1 = cumsum_all(1, 1)
        carried0 = either_path(worst0, lambda exact_diag: intra_all(0, cs0, exact_diag))

        def body(i, state):
            worst, carried = state
            slot = lax.rem(i, 2)

            def iteration(exact_diag):
                cs = load_cumsums(slot)
                output_all(i - 1, carried)
                worst_next = cumsum_all(jnp.minimum(i + 1, n - 1), 1 - slot)
                return worst_next, intra_all(i, cs, exact_diag)
            return either_path(worst, iteration)
        _, carried_last = lax.fori_loop(1, n, body, (worst1, carried0))
        output_all(n - 1, carried_last)

    @pl.when(s == 0)
    def _():
        ctx_step()

    @pl.when(s > 0)
    def _():
        latent_step()


def _scan_call(inputs, in_specs, out_struct, out_specs, grid, *, n_heads, dk, dv, has_k,
               n_ctx_chunks, n_step_chunks, lat_index, name):
    n_units = 2 * n_heads
    scratch = [pltpu.VMEM((2, n_heads, dv, dk), F32),
               pltpu.VMEM((2, n_units, CHUNK, dk), F32),
               pltpu.VMEM((n_units, dv, dk), BF16)]
    if not has_k:
        scratch.append(pltpu.VMEM((2, n_units, CHUNK, dk), F32))
    return pl.pallas_call(
        functools.partial(_scan_kernel, n_heads=n_heads, dk=dk, dv=dv, has_k=has_k,
                          n_ctx_chunks=n_ctx_chunks, n_step_chunks=n_step_chunks, lat_index=lat_index),
        grid=grid,
        in_specs=in_specs,
        out_specs=out_specs,
        out_shape=out_struct,
        scratch_shapes=scratch,
        compiler_params=pltpu.CompilerParams(
            dimension_semantics=("arbitrary", "arbitrary", "arbitrary"),
            vmem_limit_bytes=VMEM_LIMIT_BYTES),
        name=name,
    )(*inputs)


A_SCAN_HEADS = 2
A_SCAN_CHUNKS = 8


def _scan_a(c_qi, c_lf, qi, lf):
    bsz, n_ctx, _ = c_qi.shape
    seq = qi.shape[1]
    g = A_SCAN_HEADS
    gw = g * A_DK
    n_hg = A_HEADS // g
    ts = A_SCAN_CHUNKS * CHUNK
    n_steps = seq // ts

    def fstep(s):
        return jnp.maximum(s - 1, 0)

    def bstep(s):
        return n_steps - 1 - jnp.maximum(s - 1, 0)

    cspec = lambda off: pl.BlockSpec((None, n_ctx, gw), lambda b, h, s: (b, 0, off + h))
    fspec = lambda off: pl.BlockSpec((None, ts, gw), lambda b, h, s: (b, fstep(s), off + h))
    bspec = lambda off: pl.BlockSpec((None, ts, gw), lambda b, h, s: (b, bstep(s), off + h))
    in_specs = [cspec(n_hg), cspec(0), cspec(n_hg),
                fspec(0), fspec(n_hg), fspec(0),
                bspec(0), bspec(n_hg), bspec(n_hg)]
    inputs = [c_qi, c_lf, c_lf, qi, qi, lf, qi, qi, lf]
    out_struct = [jax.ShapeDtypeStruct((bsz, seq, A_HEADS * A_DV), BF16)] * 2
    out_specs = [pl.BlockSpec((None, ts, gw), lambda b, h, s: (b, fstep(s), h)),
                 pl.BlockSpec((None, ts, gw), lambda b, h, s: (b, bstep(s), h))]

    def lat_index(chunk):
        row = chunk * CHUNK
        if not isinstance(row, int):
            row = pl.multiple_of(row, CHUNK)
        return (pl.ds(row, CHUNK),)

    return _scan_call(inputs, in_specs, out_struct, out_specs, (bsz, n_hg, n_steps + 1),
                      n_heads=g, dk=A_DK, dv=A_DV, has_k=False,
                      n_ctx_chunks=n_ctx // CHUNK, n_step_chunks=A_SCAN_CHUNKS,
                      lat_index=lat_index, name="scan_hgrn2")


B_SCAN_COLS = 4


def _scan_b(c_qkv, c_lf, qkv, lf):
    bsz, n_ctx, _ = c_qkv.shape
    n_rows = qkv.shape[2]
    kw = B_HEADS * B_DK
    nc = B_SCAN_COLS
    n_steps = GRID_W // nc
    chunks_per_col = n_rows // CHUNK

    def fblk(s):
        return jnp.maximum(s - 1, 0)

    def bblk(s):
        return n_steps - 1 - jnp.maximum(s - 1, 0)

    def cspec(width, off):
        return pl.BlockSpec((None, n_ctx, width), lambda b, h, s: (b, 0, off + h))

    def lspec(width, off, blk):
        return pl.BlockSpec((None, nc, n_rows, width), lambda b, h, s: (b, blk(s), 0, off + h))

    k_off, v_off = kw // B_DK, (2 * kw) // B_DV
    in_specs = [cspec(B_DK, k_off), cspec(B_DV, v_off), cspec(B_DK, 0), cspec(B_DK, B_HEADS),
                lspec(B_DK, 0, fblk), lspec(B_DK, k_off, fblk), lspec(B_DV, v_off, fblk), lspec(B_DK, 0, fblk),
                lspec(B_DK, 0, bblk), lspec(B_DK, k_off, bblk), lspec(B_DV, v_off, bblk),
                lspec(B_DK, B_HEADS, bblk)]
    inputs = [c_qkv, c_qkv, c_lf, c_lf, qkv, qkv, qkv, lf, qkv, qkv, qkv, lf]
    vw = B_HEADS * B_DV
    out_struct = [jax.ShapeDtypeStruct((bsz, GRID_W, n_rows, vw), BF16)] * 2
    out_specs = [pl.BlockSpec((None, nc, n_rows, B_DV), lambda b, h, s: (b, fblk(s), 0, h)),
                 pl.BlockSpec((None, nc, n_rows, B_DV), lambda b, h, s: (b, bblk(s), 0, h))]

    def lat_index(chunk):
        col = chunk // chunks_per_col
        row = (chunk - col * chunks_per_col) * CHUNK
        if not isinstance(row, int):
            row = pl.multiple_of(row, CHUNK)
        return (col, pl.ds(row, CHUNK))

    return _scan_call(inputs, in_specs, out_struct, out_specs, (bsz, B_HEADS, n_steps + 1),
                      n_heads=1, dk=B_DK, dv=B_DV, has_k=True,
                      n_ctx_chunks=n_ctx // CHUNK, n_step_chunks=nc * chunks_per_col,
                      lat_index=lat_index, name="scan_gla")


GLA_FIN_ROWS = 16


def _gla_finalize_kernel(of_ref, ob_ref, sg_ref, g_ref, o_ref):
    n_cols, n_rows, w = of_ref.shape
    o = of_ref[...].astype(F32) + ob_ref[...].astype(F32)
    ms = jnp.mean(o * o, axis=-1, keepdims=True)
    on = o * lax.rsqrt(ms + EPS) * g_ref[...]
    on = jnp.swapaxes(on, 0, 1).reshape(n_rows * n_cols, w)
    o_ref[...] = (on * sg_ref[...].astype(F32)).astype(o_ref.dtype)


def _gla_finalize(of, ob, gates, gain, *, sg_col_off):
    bsz, _, n_rows, vw = of.shape
    tm = GLA_FIN_ROWS * GRID_W
    tiles = n_rows // GLA_FIN_ROWS
    cm = pl.BlockSpec((None, GRID_W, GLA_FIN_ROWS, B_DV), lambda b, r, h: (b, 0, r, h))
    return pl.pallas_call(
        _gla_finalize_kernel,
        grid=(bsz, tiles, B_HEADS),
        in_specs=[cm, cm,
                  pl.BlockSpec((tm, B_DV), lambda b, r, h: (b * tiles + r, sg_col_off + h)),
                  pl.BlockSpec((1, B_DV), lambda b, r, h: (0, h))],
        out_specs=pl.BlockSpec((tm, B_DV), lambda b, r, h: (b * tiles + r, h)),
        out_shape=jax.ShapeDtypeStruct((bsz * n_rows * GRID_W, vw), BF16),
        compiler_params=pltpu.CompilerParams(
            dimension_semantics=("arbitrary", "arbitrary", "arbitrary"),
            vmem_limit_bytes=32 * 1024 * 1024),
        name="gla_finalize",
    )(of, ob, gates, gain)


def _merge_kernel(oaf, oab, obp, sga, sma, smb, ga, wpa, wpb, y_ref):
    o = oaf[...].astype(F32) + oab[...].astype(F32)
    parts = []
    for h in range(A_HEADS):
        oh = o[:, h * A_DV:(h + 1) * A_DV]
        ms = jnp.mean(oh * oh, axis=-1, keepdims=True)
        parts.append(oh * lax.rsqrt(ms + EPS))
    oa = (jnp.concatenate(parts, axis=1) * ga[...] * sga[...].astype(F32)).astype(BF16)
    ya = jnp.dot(oa, wpa[...], preferred_element_type=F32)
    yb = jnp.dot(obp[...], wpb[...], preferred_element_type=F32)
    y_ref[...] = (sma[...].astype(F32) * ya + smb[...].astype(F32) * yb).astype(BF16)


def _merge(oaf, oab, obp, gates, ga, wpa, wpb, *, tm):
    n_tok, d = oaf.shape
    tok = lambda off: pl.BlockSpec((tm, d), lambda i: (i, off))
    const = lambda shape: pl.BlockSpec(shape, lambda i: (0, 0), pipeline_mode=pl.Buffered(1))
    return pl.pallas_call(
        _merge_kernel,
        grid=(n_tok // tm,),
        in_specs=[tok(0), tok(0), tok(0), tok(0), tok(2), tok(3),
                  const((1, d)), const(wpa.shape), const(wpb.shape)],
        out_specs=pl.BlockSpec((tm, wpa.shape[1]), lambda i: (i, 0)),
        out_shape=jax.ShapeDtypeStruct((n_tok, wpa.shape[1]), BF16),
        compiler_params=pltpu.CompilerParams(
            dimension_semantics=("arbitrary",), vmem_limit_bytes=VMEM_LIMIT_BYTES),
        name="merge_proj",
    )(oaf, oab, obp, gates, gates, gates, ga, wpa, wpb)


def _final_kernel(y_ref, x_ref, mod_ref, w_ref, g_ref, o_ref):
    yo = jnp.dot(y_ref[...], w_ref[...], preferred_element_type=F32)
    z = x_ref[...] + mod_ref[0, 2:3, :] * yo
    ms = jnp.mean(z * z, axis=-1, keepdims=True)
    o_ref[...] = z * lax.rsqrt(ms + EPS) * g_ref[...]


def _final(y, x2d, mod3, w_out, fg, *, tm, tiles_per_batch):
    n_tok, d = x2d.shape
    const = lambda shape: pl.BlockSpec(shape, lambda i: (0, 0), pipeline_mode=pl.Buffered(1))
    return pl.pallas_call(
        _final_kernel,
        grid=(n_tok // tm,),
        in_specs=[pl.BlockSpec((tm, d), lambda i: (i, 0)),
                  pl.BlockSpec((tm, d), lambda i: (i, 0)),
                  pl.BlockSpec((1, 3, d), lambda i: (i // tiles_per_batch, 0, 0)),
                  const(w_out.shape), const((1, d))],
        out_specs=pl.BlockSpec((tm, d), lambda i: (i, 0)),
        out_shape=jax.ShapeDtypeStruct((n_tok, d), F32),
        compiler_params=pltpu.CompilerParams(
            dimension_semantics=("arbitrary",), vmem_limit_bytes=VMEM_LIMIT_BYTES),
        name="out_proj_final",
    )(y, x2d, mod3, w_out, fg)


def kernel(x, c, ctx, c_ctx, w_ada, b_ada, norm_g, w_in, hgrn_lb_logits, gla_w_gk, gla_b_gk,
           hgrn_onorm_g, gla_onorm_g, w_pa, w_pb, w_out, final_norm_g):
    bsz, seq, d = x.shape
    n_ctx = ctx.shape[1]
    depth = w_in.shape[0]
    assert depth == 1, "single-layer trunk"
    a_kw, a_vw = A_HEADS * A_DK, A_HEADS * A_DV
    b_kw, b_vw = B_HEADS * B_DK, B_HEADS * B_DV

    n_rows = -(-(bsz + 1) // SUBLANES) * SUBLANES
    cvec = jnp.zeros((n_rows, d), F32).at[:bsz].set(c).at[bsz].set(c_ctx)
    mod = _adaln(cvec, w_ada[0], b_ada[0].reshape(1, -1))
    mod3 = mod.reshape(n_rows, 3, d)

    lb = _lower_bounds(hgrn_lb_logits)[0:1]

    o_ag = 2 * a_kw + 2 * a_vw
    o_bq = o_ag + a_vw
    o_br = o_bq + 2 * b_kw + b_vw
    w_main, w_r = _prep_weights(
        w_in[0], [(0, o_ag), (o_bq, o_br - o_bq), (o_ag, o_bq - o_ag)], tail_start=o_br)
    gk_pad = jnp.zeros((LANES, 2 * b_kw), F32)
    gk_pad = gk_pad.at[0:B_RANK, 0:b_kw].set(gla_w_gk[0, 0]).at[B_RANK:2 * B_RANK, b_kw:].set(gla_w_gk[0, 1])
    gk_pad = gk_pad.astype(BF16)
    gkb = gla_b_gk[0].reshape(1, 2 * b_kw)
    ng = norm_g[0].reshape(1, d)

    x2d = x.reshape(bsz * seq, d)
    ctx2d = ctx.reshape(bsz * n_ctx, d)
    tm = 1024
    tiles_per_batch = seq // tm

    c_aqi, c_alf, c_bqkv, c_blf = _inproj(
        ctx2d, mod3, lambda i: bsz, ng, w_main, w_r, lb, gk_pad, gkb,
        tm=bsz * n_ctx, latent=False)
    aqi, alf, bqkv, blf, gates = _inproj(
        x2d, mod3, lambda i: i // tiles_per_batch, ng, w_main, w_r, lb, gk_pad, gkb,
        tm=tm, latent=True, n_batch=bsz)

    r3 = lambda t, n: t.reshape(bsz, n, t.shape[-1])
    oaf, oab = _scan_a(r3(c_aqi, n_ctx), r3(c_alf, n_ctx), r3(aqi, seq), r3(alf, seq))
    obf, obb = _scan_b(r3(c_bqkv, n_ctx), r3(c_blf, n_ctx), bqkv, blf)
    obp = _gla_finalize(obf, obb, gates, gla_onorm_g[0].reshape(1, -1), sg_col_off=a_vw // B_DV)

    r2 = lambda t: t.reshape(bsz * seq, t.shape[-1])
    y = _merge(r2(oaf), r2(oab), obp, gates, hgrn_onorm_g[0].reshape(1, -1),
               w_pa[0].astype(BF16), w_pb[0].astype(BF16), tm=256)
    out = _final(y, x2d, mod3, w_out[0].astype(BF16), final_norm_g.reshape(1, d),
                 tm=512, tiles_per_batch=seq // 512)
    return out.reshape(bsz, seq, d)
```

```python
import functools

import jax
import jax.numpy as jnp
from jax import lax
from jax.experimental import pallas as pl
from jax.experimental.pallas import tpu as pltpu

F32 = jnp.float32
BF16 = jnp.bfloat16

CHUNK = 64
SUB = 16
GRID_W = 64
EPS = 1e-6
A_HEADS, A_DK, A_DV = 16, 128, 128
B_HEADS, B_DK, B_DV = 4, 256, 512
B_RANK = 16
GATE_NORMALIZER = 16.0
LOG2E = 1.4426950408889634

VMEM_LIMIT_BYTES = 56 * 1024 * 1024
LANES = 128
SUBLANES = 8

NT_DIMS = (((1,), (1,)), ((), ()))
TN_DIMS = (((0,), (0,)), ((), ()))


def _sigmoid(z):
    return 1.0 / (1.0 + jnp.exp(-z))


def _adaln_kernel(c_ref, w_ref, b_ref, o_ref):
    c = c_ref[...]
    s = c * _sigmoid(c)
    o_ref[...] = jnp.dot(s, w_ref[...], preferred_element_type=F32,
                         precision=lax.Precision.HIGHEST) + b_ref[...]


def _adaln(cvec, w, b):
    rows, d = cvec.shape
    n = w.shape[1]
    tn = 768
    return pl.pallas_call(
        _adaln_kernel,
        grid=(n // tn,),
        in_specs=[pl.BlockSpec((rows, d), lambda j: (0, 0)),
                  pl.BlockSpec((d, tn), lambda j: (0, j)),
                  pl.BlockSpec((1, tn), lambda j: (0, j))],
        out_specs=pl.BlockSpec((rows, tn), lambda j: (0, j)),
        out_shape=jax.ShapeDtypeStruct((rows, n), F32),
        compiler_params=pltpu.CompilerParams(
            dimension_semantics=("arbitrary",), vmem_limit_bytes=32 * 1024 * 1024),
        name="adaln",
    )(cvec, w, b)


def _lower_bound_kernel(l_ref, o_ref):
    x = l_ref[...]
    n_rows = x.shape[0]
    m = jnp.max(x, axis=0, keepdims=True)
    e = jnp.exp(x - m)
    tot = jnp.sum(e, axis=0, keepdims=True)
    run = jnp.zeros_like(tot)
    for r in range(n_rows - 1):
        run = run + e[r:r + 1]
        o_ref[r:r + 1, :] = run / tot


def _lower_bounds(logits):
    n_rows, w = logits.shape
    return pl.pallas_call(
        _lower_bound_kernel,
        out_shape=jax.ShapeDtypeStruct((n_rows - 1, w), F32),
        name="hgrn_lower_bounds",
    )(logits)


def _prep_weights_kernel(wa_ref, wb_ref, o_ref, r_ref, *, j_tail, shift):
    j = pl.program_id(0)

    @pl.when(j < j_tail)
    def _():
        o_ref[...] = wa_ref[...].astype(BF16)

    @pl.when(j >= j_tail)
    def _():
        o_ref[...] = jnp.concatenate([wa_ref[shift:, :], wb_ref[:shift, :]], axis=0).astype(BF16)

    @pl.when(j == j_tail)
    def _():
        rank_rows = wa_ref[:shift, :].astype(BF16)
        r_ref[...] = jnp.concatenate(
            [rank_rows, jnp.zeros((LANES - shift, rank_rows.shape[1]), BF16)], axis=0)


def _prep_weights(wt, group_starts, tail_start):
    n_in, d = wt.shape
    shift = 2 * B_RANK
    src_tiles = []
    for start, width in group_starts:
        assert start % TN == 0 and width % TN == 0
        src_tiles += list(range(start // TN, (start + width) // TN))
    assert tail_start % TN == 0
    j_tail = len(src_tiles)
    n_tail = (n_in - tail_start - shift) // TN
    n_out = j_tail + n_tail
    last_src = (n_in - 1) // TN

    def src_a(j):
        idx = j - j_tail + tail_start // TN
        for dst in range(j_tail - 1, -1, -1):
            idx = jnp.where(j == dst, src_tiles[dst], idx)
        return idx

    return pl.pallas_call(
        functools.partial(_prep_weights_kernel, j_tail=j_tail, shift=shift),
        grid=(n_out,),
        in_specs=[pl.BlockSpec((TN, d), lambda j: (src_a(j), 0)),
                  pl.BlockSpec((TN, d), lambda j: (jnp.minimum(src_a(j) + 1, last_src), 0))],
        out_specs=[pl.BlockSpec((TN, d), lambda j: (j, 0)),
                   pl.BlockSpec((LANES, d), lambda j: (0, 0))],
        out_shape=[jax.ShapeDtypeStruct((n_out * TN, d), BF16),
                   jax.ShapeDtypeStruct((LANES, d), BF16)],
        compiler_params=pltpu.CompilerParams(
            dimension_semantics=("arbitrary",), vmem_limit_bytes=40 * 1024 * 1024),
        name="prep_weights",
    )(wt, wt)


TN = 512
NORM_ROWS = 128
EPILOGUE_PARTS = 4
T_AQI = 4096 // TN
T_ALF = 4096 // TN
T_BQKV = 4096 // TN
T_BLF = 2048 // TN
T_GATES = 8192 // TN
J_ALF = T_AQI
J_BQKV = J_ALF + T_ALF
J_BLF = J_BQKV + T_BQKV
J_GATES = J_BLF + T_BLF
J_END = J_GATES + T_GATES


def _inproj_kernel(x_ref, mod_ref, ng_ref, w_ref, wr_ref, lb_ref, gkw_ref, gkb_ref, *rest, latent):
    if latent:
        aqi_ref, alf_ref, bqkv_ref, blf_ref, gates_ref, h_ref, r_ref = rest
    else:
        aqi_ref, alf_ref, bqkv_ref, blf_ref, h_ref, r_ref = rest
        gates_ref = None
    j = pl.program_id(1)

    def store_gla(out_ref, z):
        if not latent:
            out_ref[...] = z.astype(out_ref.dtype)
            return
        n_rows = z.shape[0] // GRID_W
        zc = jnp.swapaxes(z.reshape(n_rows, GRID_W, z.shape[1]), 0, 1)
        out_ref[...] = zc.astype(out_ref.dtype)

    @pl.when(j == 0)
    def _():
        def norm_rows(t, carry):
            rows = pl.ds(pl.multiple_of(t * NORM_ROWS, NORM_ROWS), NORM_ROWS)
            x = x_ref[rows, :]
            ms = jnp.mean(x * x, axis=-1, keepdims=True)
            y = x * lax.rsqrt(ms + EPS) * ng_ref[...]
            h = (y * (1.0 + mod_ref[0, 1:2, :]) + mod_ref[0, 0:1, :]).astype(BF16)
            h_ref[rows, :] = h
            r_ref[rows, :] = lax.dot_general(h, wr_ref[...], NT_DIMS,
                                             preferred_element_type=F32).astype(BF16)
            return carry
        lax.fori_loop(0, x_ref.shape[0] // NORM_ROWS, norm_rows, 0)

    def mm():
        return lax.dot_general(h_ref[...], w_ref[...], NT_DIMS, preferred_element_type=F32)

    def mm_parts(out_ref, epilogue):
        part = h_ref.shape[0] // EPILOGUE_PARTS
        for m in range(EPILOGUE_PARTS):
            rows = slice(m * part, (m + 1) * part)
            z = lax.dot_general(h_ref[rows, :], w_ref[...], NT_DIMS, preferred_element_type=F32)
            out_ref[rows, :] = epilogue(z).astype(out_ref.dtype)

    @pl.when(j < J_ALF)
    def _():
        mm_parts(aqi_ref, lambda z: z)

    @pl.when((j >= J_ALF) & (j < J_BQKV))
    def _():
        lb = lb_ref[...]
        mm_parts(alf_ref, lambda z: jnp.log2(lb + (1.0 - lb) * _sigmoid(z)))

    @pl.when((j >= J_BQKV) & (j < J_BLF))
    def _():
        scale = jnp.where(j < J_BQKV + (B_HEADS * B_DK) // TN, B_DK ** -0.5, 1.0)
        store_gla(bqkv_ref, mm() * scale)

    @pl.when((j >= J_BLF) & (j < J_GATES))
    def _():
        z = jnp.dot(r_ref[...], gkw_ref[...], preferred_element_type=F32) + gkb_ref[...]
        ls = jnp.minimum(z, 0.0) - jnp.log(1.0 + jnp.exp(-jnp.abs(z)))
        store_gla(blf_ref, ls * (LOG2E / GATE_NORMALIZER))

    if latent:
        @pl.when(j >= J_GATES)
        def _():
            silu = j < J_GATES + T_GATES // 2
            mm_parts(gates_ref, lambda z: _sigmoid(z) * jnp.where(silu, z, 1.0))


def _inproj(x2d, mod3, mod_row_of_tile, norm_g, w_main, w_r, lb, gk_pad, gkb, *, tm, latent, n_batch=1):
    n_tok, d = x2d.shape
    n_j = J_END if latent else J_GATES
    tile_rows = tm // GRID_W
    seq_rows = n_tok // n_batch // GRID_W
    tiles_per_batch = n_tok // n_batch // tm

    def cl(j, lo, n):
        return jnp.clip(j - lo, 0, n - 1)

    def gla_spec(j_lo, n):
        if not latent:
            return pl.BlockSpec((tm, TN), lambda i, j: (i, cl(j, j_lo, n)))
        return pl.BlockSpec((None, GRID_W, tile_rows, TN),
                            lambda i, j: (i // tiles_per_batch, 0, i % tiles_per_batch, cl(j, j_lo, n)))

    def gla_shape(n, dtype):
        if not latent:
            return jax.ShapeDtypeStruct((n_tok, n * TN), dtype)
        return jax.ShapeDtypeStruct((n_batch, GRID_W, seq_rows, n * TN), dtype)

    in_specs = [
        pl.BlockSpec((tm, d), lambda i, j: (i, 0)),
        pl.BlockSpec((1, 3, d), lambda i, j: (mod_row_of_tile(i), 0, 0)),
        pl.BlockSpec((1, d), lambda i, j: (0, 0)),
        pl.BlockSpec((TN, d), lambda i, j: (j - jnp.clip(j - (J_BLF - 1), 0, T_BLF), 0)),
        pl.BlockSpec((LANES, d), lambda i, j: (0, 0)),
        pl.BlockSpec((1, TN), lambda i, j: (0, cl(j, J_ALF, T_ALF))),
        pl.BlockSpec((LANES, TN), lambda i, j: (0, cl(j, J_BLF, T_BLF))),
        pl.BlockSpec((1, TN), lambda i, j: (0, cl(j, J_BLF, T_BLF))),
    ]
    out_specs = [
        pl.BlockSpec((tm, TN), lambda i, j: (i, cl(j, 0, T_AQI))),
        pl.BlockSpec((tm, TN), lambda i, j: (i, cl(j, J_ALF, T_ALF))),
        gla_spec(J_BQKV, T_BQKV),
        gla_spec(J_BLF, T_BLF),
    ]
    out_shape = [
        jax.ShapeDtypeStruct((n_tok, T_AQI * TN), BF16),
        jax.ShapeDtypeStruct((n_tok, T_ALF * TN), F32),
        gla_shape(T_BQKV, BF16),
        gla_shape(T_BLF, F32),
    ]
    if latent:
        out_specs.append(pl.BlockSpec((tm, TN), lambda i, j: (i, cl(j, J_GATES, T_GATES))))
        out_shape.append(jax.ShapeDtypeStruct((n_tok, T_GATES * TN), BF16))
    return pl.pallas_call(
        functools.partial(_inproj_kernel, latent=latent),
        grid=(n_tok // tm, n_j),
        in_specs=in_specs,
        out_specs=out_specs,
        out_shape=out_shape,
        scratch_shapes=[pltpu.VMEM((tm, d), BF16), pltpu.VMEM((tm, LANES), BF16)],
        compiler_params=pltpu.CompilerParams(
            dimension_semantics=("arbitrary", "arbitrary"), vmem_limit_bytes=VMEM_LIMIT_BYTES),
        name="inproj_latent" if latent else "inproj_ctx",
    )(x2d, mod3, norm_g, w_main, w_r, lb, gk_pad, gkb)


def _scan_consts():
    row = lax.broadcasted_iota(jnp.int32, (CHUNK, CHUNK), 0)
    col = lax.broadcasted_iota(jnp.int32, (CHUNK, CHUNK), 1)
    return dict(
        mask_f=col <= row,
        mask_b=col >= row,
        tri_f=jnp.where(col <= row, 1.0, 0.0).astype(BF16),
        tri_b=jnp.where(col >= row, 1.0, 0.0).astype(BF16),
        lane=lax.broadcasted_iota(jnp.int32, (SUBLANES, CHUNK), 1),
    )


DIAG_FACTOR_MAX_LOG2 = 64.0


def _cumsum_stage(lf, has_k, fwd, cst):
    dk = lf.shape[1]
    tri = cst["tri_f"] if fwd else cst["tri_b"]
    hi = lf.astype(BF16)
    lo = (lf - hi.astype(F32)).astype(BF16)
    cc = jnp.dot(tri, jnp.concatenate([hi, lo], axis=1), preferred_element_type=F32)
    c = cc[:, :dk] + cc[:, dk:]
    a = None if has_k else c - jnp.log2(1.0 - jnp.exp2(lf))
    mag = jnp.abs(lf)
    excess = None
    for r0 in range(0, CHUNK, SUB):
        e = jnp.sum(mag[r0:r0 + SUB], axis=0, keepdims=True)
        excess = e if excess is None else jnp.maximum(excess, e)
    return c, a, excess


def _intra_stage(q, k, v, c, a, st_ref, fwd, cst, keep, exact_diag=True):
    dk = c.shape[1]
    tot = c[CHUNK - 1:CHUNK] if fwd else c[0:1]

    if k is None:
        kf = None

        def kscaled(r0, r1, ref):
            return jnp.exp2(ref - a[r0:r1])
    else:
        a = c
        kf = k.astype(F32)

        def kscaled(r0, r1, ref):
            return kf[r0:r1] * jnp.exp2(ref - c[r0:r1])

    kte = kscaled(0, CHUNK, tot).astype(BF16)
    st = st_ref[...]
    if keep is not None:
        keep.prev_state(st.astype(BF16))
    st_ref[...] = st * jnp.exp2(tot) + lax.dot_general(v, kte, TN_DIMS, preferred_element_type=F32)
    if keep is None:
        return
    yield

    qf = q.astype(F32)
    keep.scaled_q((qf * jnp.exp2(c)).astype(BF16))
    n_sub = CHUNK // SUB
    half = SUB // SUBLANES
    own = 0 if exact_diag else SUB
    soffs = []
    for r in range(n_sub):
        r0 = SUB * r
        if fwd:
            lo_row, hi_row = 0, r0 + own
            ref = c[r0:r0 + 1]
        else:
            lo_row, hi_row = r0 + SUB - own, CHUNK
            ref = c[r0 + SUB - 1:r0 + SUB]
        if hi_row == lo_row:
            soffs.append(None)
            continue
        pieces = [kscaled(lo_row, hi_row, ref).astype(BF16)]
        if lo_row > 0:
            pieces.insert(0, jnp.zeros((lo_row, dk), BF16))
        if hi_row < CHUNK:
            pieces.append(jnp.zeros((CHUNK - hi_row, dk), BF16))
        kr = jnp.concatenate(pieces, axis=0) if len(pieces) > 1 else pieces[0]
        qr = (qf[r0:r0 + SUB] * jnp.exp2(c[r0:r0 + SUB] - ref)).astype(BF16)
        soffs.append(lax.dot_general(qr, kr, NT_DIMS, preferred_element_type=F32))
    yield

    if not exact_diag:
        p = jnp.concatenate(soffs, axis=0)
        keep.scores(jnp.where(cst["mask_f"] if fwd else cst["mask_b"], p, 0.0).astype(BF16))
        return

    rows = []
    for r in range(n_sub):
        r0 = SUB * r
        cb = c[r0:r0 + SUB]
        qb = qf[r0:r0 + SUB]
        acc = [jnp.zeros((SUBLANES, CHUNK), F32) for _ in range(half)]
        for jj in range(SUB):
            aj = a[r0 + jj:r0 + jj + 1]
            jg = jj // SUBLANES
            groups = range(jg, half) if fwd else range(0, jg + 1)
            for g in groups:
                e = cb[g * SUBLANES:(g + 1) * SUBLANES] - aj
                if g == jg:
                    e = jnp.minimum(e, 0.0)
                t = qb[g * SUBLANES:(g + 1) * SUBLANES] * jnp.exp2(e)
                if kf is not None:
                    t = t * kf[r0 + jj:r0 + jj + 1]
                sj = jnp.sum(t, axis=1, keepdims=True)
                acc[g] = jnp.where(cst["lane"] == r0 + jj, sj, acc[g])
        d = jnp.concatenate(acc, axis=0)
        rows.append(d if soffs[r] is None else d + soffs[r])
    p = jnp.concatenate(rows, axis=0)
    keep.scores(jnp.where(cst["mask_f"] if fwd else cst["mask_b"], p, 0.0).astype(BF16))


def _run_staged(units):
    live = list(units)
    while live:
        nxt = []
        for u in live:
            try:
                next(u)
                nxt.append(u)
            except StopIteration:
                pass
        live = nxt


def _scan_kernel(*refs, n_heads, dk, dv, has_k, n_ctx_chunks, n_step_chunks, lat_index):
    n_ctx_in = 4 if has_k else 3
    n_lat_in = 4 if has_k else 3
    ctx_refs = refs[:n_ctx_in]
    lat_refs = (refs[n_ctx_in:n_ctx_in + n_lat_in], refs[n_ctx_in + n_lat_in:n_ctx_in + 2 * n_lat_in])
    rest = refs[n_ctx_in + 2 * n_lat_in:]
    out_refs = rest[0:2]
    if has_k:
        st_ref, c_ref, stb_ref = rest[2:]
        a_ref = None
    else:
        st_ref, c_ref, stb_ref, a_ref = rest[2:]
    s = pl.program_id(2)
    cst = _scan_consts()
    units = [(h, d) for h in range(n_heads) for d in range(2)]

    def kcols(h):
        return slice(h * dk, (h + 1) * dk)

    def vcols(h):
        return slice(h * dv, (h + 1) * dv)

    def ctx_step():
        st_ref[...] = jnp.zeros(st_ref.shape, F32)
        if has_k:
            k_ref, v_ref, lff_ref, lfb_ref = ctx_refs
        else:
            v_ref, lff_ref, lfb_ref = ctx_refs
            k_ref = None

        def unit(h, d, rows):
            lf = (lff_ref, lfb_ref)[d][rows, kcols(h)]
            c, a, _ = _cumsum_stage(lf, has_k, d == 0, cst)
            yield
            k = None if k_ref is None else k_ref[rows, kcols(h)]
            yield from _intra_stage(None, k, v_ref[rows, vcols(h)], c, a, st_ref.at[d, h], d == 0, cst, None)

        def body(i, carry):
            rows = (pl.ds(pl.multiple_of(i * CHUNK, CHUNK), CHUNK),
                    pl.ds(pl.multiple_of((n_ctx_chunks - 1 - i) * CHUNK, CHUNK), CHUNK))
            _run_staged([unit(h, d, rows[d]) for h, d in units])
            return carry
        lax.fori_loop(0, n_ctx_chunks, body, 0)

    def latent_step():
        n = n_step_chunks

        def index(d, pos):
            return lat_index(pos if d == 0 else n - 1 - pos)

        def read(d, pos, h, what):
            refs_d = lat_refs[d]
            ref = refs_d[{"q": 0, "k": 1, "v": 2 if has_k else 1, "lf": 3 if has_k else 2}[what]]
            cols = vcols(h) if what == "v" else kcols(h)
            return ref[index(d, pos) + (cols,)]

        def cumsum_all(pos, slot):
            worst = None
            for u, (h, d) in enumerate(units):
                c, a, excess = _cumsum_stage(read(d, pos, h, "lf"), has_k, d == 0, cst)
                c_ref[slot, u] = c
                if a is not None:
                    a_ref[slot, u] = a
                worst = excess if worst is None else jnp.maximum(worst, excess)
            return jnp.max(worst)

        def output_all(pos, carried):
            for u, (h, d) in enumerate(units):
                p, qd = carried[u]
                o = jnp.dot(p, read(d, pos, h, "v"), preferred_element_type=F32)
                o = o + lax.dot_general(qd, stb_ref[u], NT_DIMS, preferred_element_type=F32)
                out_refs[d][index(d, pos) + (vcols(h),)] = o.astype(out_refs[d].dtype)

        class Keeper:
            def __init__(self, u):
                self.u = u
                self.p = self.qd = None

            def prev_state(self, stb):
                stb_ref[self.u] = stb

            def scaled_q(self, qd):
                self.qd = qd

            def scores(self, p):
                self.p = p

        def intra_all(pos, cs, exact_diag):
            keepers = [Keeper(u) for u in range(len(units))]
            gens = []
            for u, (h, d) in enumerate(units):
                k = read(d, pos, h, "k") if has_k else None
                c, a = cs[u]
                gens.append(_intra_stage(read(d, pos, h, "q"), k, read(d, pos, h, "v"), c, a,
                                         st_ref.at[d, h], d == 0, cst, keepers[u], exact_diag))
            _run_staged(gens)
            return tuple((kp.p, kp.qd) for kp in keepers)

        def load_cumsums(slot):
            return [(c_ref[slot, u], None if has_k else a_ref[slot, u]) for u in range(len(units))]

        def either_path(worst, build):
            return lax.cond(worst <= DIAG_FACTOR_MAX_LOG2,
                            functools.partial(build, False), functools.partial(build, True))

        worst0 = cumsum_all(0, 0)
        cs0 = load_cumsums(0)
        worst1 = cumsum_all(1, 1)
        carried0 = either_path(worst0, lambda exact_diag: intra_all(0, cs0, exact_diag))

        def body(i, state):
            worst, carried = state
            slot = lax.rem(i, 2)

            def iteration(exact_diag):
                cs = load_cumsums(slot)
                output_all(i - 1, carried)
                worst_next = cumsum_all(jnp.minimum(i + 1, n - 1), 1 - slot)
                return worst_next, intra_all(i, cs, exact_diag)
            return either_path(worst, iteration)
        _, carried_last = lax.fori_loop(1, n, body, (worst1, carried0))
        output_all(n - 1, carried_last)

    @pl.when(s == 0)
    def _():
        ctx_step()

    @pl.when(s > 0)
    def _():
        latent_step()


def _scan_call(inputs, in_specs, out_struct, out_specs, grid, *, n_heads, dk, dv, has_k,
               n_ctx_chunks, n_step_chunks, lat_index, name):
    n_units = 2 * n_heads
    scratch = [pltpu.VMEM((2, n_heads, dv, dk), F32),
               pltpu.VMEM((2, n_units, CHUNK, dk), F32),
               pltpu.VMEM((n_units, dv, dk), BF16)]
    if not has_k:
        scratch.append(pltpu.VMEM((2, n_units, CHUNK, dk), F32))
    return pl.pallas_call(
        functools.partial(_scan_kernel, n_heads=n_heads, dk=dk, dv=dv, has_k=has_k,
                          n_ctx_chunks=n_ctx_chunks, n_step_chunks=n_step_chunks, lat_index=lat_index),
        grid=grid,
        in_specs=in_specs,
        out_specs=out_specs,
        out_shape=out_struct,
        scratch_shapes=scratch,
        compiler_params=pltpu.CompilerParams(
            dimension_semantics=("arbitrary", "arbitrary", "arbitrary"),
            vmem_limit_bytes=VMEM_LIMIT_BYTES),
        name=name,
    )(*inputs)


A_SCAN_HEADS = 2
A_SCAN_CHUNKS = 8


def _scan_a(c_qi, c_lf, qi, lf):
    bsz, n_ctx, _ = c_qi.shape
    seq = qi.shape[1]
    g = A_SCAN_HEADS
    gw = g * A_DK
    n_hg = A_HEADS // g
    ts = A_SCAN_CHUNKS * CHUNK
    n_steps = seq // ts

    def fstep(s):
        return jnp.maximum(s - 1, 0)

    def bstep(s):
        return n_steps - 1 - jnp.maximum(s - 1, 0)

    cspec = lambda off: pl.BlockSpec((None, n_ctx, gw), lambda b, h, s: (b, 0, off + h))
    fspec = lambda off: pl.BlockSpec((None, ts, gw), lambda b, h, s: (b, fstep(s), off + h))
    bspec = lambda off: pl.BlockSpec((None, ts, gw), lambda b, h, s: (b, bstep(s), off + h))
    in_specs = [cspec(n_hg), cspec(0), cspec(n_hg),
                fspec(0), fspec(n_hg), fspec(0),
                bspec(0), bspec(n_hg), bspec(n_hg)]
    inputs = [c_qi, c_lf, c_lf, qi, qi, lf, qi, qi, lf]
    out_struct = [jax.ShapeDtypeStruct((bsz, seq, A_HEADS * A_DV), BF16)] * 2
    out_specs = [pl.BlockSpec((None, ts, gw), lambda b, h, s: (b, fstep(s), h)),
                 pl.BlockSpec((None, ts, gw), lambda b, h, s: (b, bstep(s), h))]

    def lat_index(chunk):
        row = chunk * CHUNK
        if not isinstance(row, int):
            row = pl.multiple_of(row, CHUNK)
        return (pl.ds(row, CHUNK),)

    return _scan_call(inputs, in_specs, out_struct, out_specs, (bsz, n_hg, n_steps + 1),
                      n_heads=g, dk=A_DK, dv=A_DV, has_k=False,
                      n_ctx_chunks=n_ctx // CHUNK, n_step_chunks=A_SCAN_CHUNKS,
                      lat_index=lat_index, name="scan_hgrn2")


B_SCAN_COLS = 4


def _scan_b(c_qkv, c_lf, qkv, lf):
    bsz, n_ctx, _ = c_qkv.shape
    n_rows = qkv.shape[2]
    kw = B_HEADS * B_DK
    nc = B_SCAN_COLS
    n_steps = GRID_W // nc
    chunks_per_col = n_rows // CHUNK

    def fblk(s):
        return jnp.maximum(s - 1, 0)

    def bblk(s):
        return n_steps - 1 - jnp.maximum(s - 1, 0)

    def cspec(width, off):
        return pl.BlockSpec((None, n_ctx, width), lambda b, h, s: (b, 0, off + h))

    def lspec(width, off, blk):
        return pl.BlockSpec((None, nc, n_rows, width), lambda b, h, s: (b, blk(s), 0, off + h))

    k_off, v_off = kw // B_DK, (2 * kw) // B_DV
    in_specs = [cspec(B_DK, k_off), cspec(B_DV, v_off), cspec(B_DK, 0), cspec(B_DK, B_HEADS),
                lspec(B_DK, 0, fblk), lspec(B_DK, k_off, fblk), lspec(B_DV, v_off, fblk), lspec(B_DK, 0, fblk),
                lspec(B_DK, 0, bblk), lspec(B_DK, k_off, bblk), lspec(B_DV, v_off, bblk),
                lspec(B_DK, B_HEADS, bblk)]
    inputs = [c_qkv, c_qkv, c_lf, c_lf, qkv, qkv, qkv, lf, qkv, qkv, qkv, lf]
    vw = B_HEADS * B_DV
    out_struct = [jax.ShapeDtypeStruct((bsz, GRID_W, n_rows, vw), BF16)] * 2
    out_specs = [pl.BlockSpec((None, nc, n_rows, B_DV), lambda b, h, s: (b, fblk(s), 0, h)),
                 pl.BlockSpec((None, nc, n_rows, B_DV), lambda b, h, s: (b, bblk(s), 0, h))]

    def lat_index(chunk):
        col = chunk // chunks_per_col
        row = (chunk - col * chunks_per_col) * CHUNK
        if not isinstance(row, int):
            row = pl.multiple_of(row, CHUNK)
        return (col, pl.ds(row, CHUNK))

    return _scan_call(inputs, in_specs, out_struct, out_specs, (bsz, B_HEADS, n_steps + 1),
                      n_heads=1, dk=B_DK, dv=B_DV, has_k=True,
                      n_ctx_chunks=n_ctx // CHUNK, n_step_chunks=nc * chunks_per_col,
                      lat_index=lat_index, name="scan_gla")


GLA_FIN_ROWS = 16


def _gla_finalize_kernel(of_ref, ob_ref, sg_ref, g_ref, o_ref):
    n_cols, n_rows, w = of_ref.shape
    o = of_ref[...].astype(F32) + ob_ref[...].astype(F32)
    ms = jnp.mean(o * o, axis=-1, keepdims=True)
    on = o * lax.rsqrt(ms + EPS) * g_ref[...]
    on = jnp.swapaxes(on, 0, 1).reshape(n_rows * n_cols, w)
    o_ref[...] = (on * sg_ref[...].astype(F32)).astype(o_ref.dtype)


def _gla_finalize(of, ob, gates, gain, *, sg_col_off):
    bsz, _, n_rows, vw = of.shape
    tm = GLA_FIN_ROWS * GRID_W
    tiles = n_rows // GLA_FIN_ROWS
    cm = pl.BlockSpec((None, GRID_W, GLA_FIN_ROWS, B_DV), lambda b, r, h: (b, 0, r, h))
    return pl.pallas_call(
        _gla_finalize_kernel,
        grid=(bsz, tiles, B_HEADS),
        in_specs=[cm, cm,
                  pl.BlockSpec((tm, B_DV), lambda b, r, h: (b * tiles + r, sg_col_off + h)),
                  pl.BlockSpec((1, B_DV), lambda b, r, h: (0, h))],
        out_specs=pl.BlockSpec((tm, B_DV), lambda b, r, h: (b * tiles + r, h)),
        out_shape=jax.ShapeDtypeStruct((bsz * n_rows * GRID_W, vw), BF16),
        compiler_params=pltpu.CompilerParams(
            dimension_semantics=("arbitrary", "arbitrary", "arbitrary"),
            vmem_limit_bytes=32 * 1024 * 1024),
        name="gla_finalize",
    )(of, ob, gates, gain)


def _merge_kernel(oaf, oab, obp, sga, sma, smb, ga, wpa, wpb, y_ref):
    o = oaf[...].astype(F32) + oab[...].astype(F32)
    parts = []
    for h in range(A_HEADS):
        oh = o[:, h * A_DV:(h + 1) * A_DV]
        ms = jnp.mean(oh * oh, axis=-1, keepdims=True)
        parts.append(oh * lax.rsqrt(ms + EPS))
    oa = (jnp.concatenate(parts, axis=1) * ga[...] * sga[...].astype(F32)).astype(BF16)
    ya = jnp.dot(oa, wpa[...], preferred_element_type=F32)
    yb = jnp.dot(obp[...], wpb[...], preferred_element_type=F32)
    y_ref[...] = (sma[...].astype(F32) * ya + smb[...].astype(F32) * yb).astype(BF16)


def _merge(oaf, oab, obp, gates, ga, wpa, wpb, *, tm):
    n_tok, d = oaf.shape
    tok = lambda off: pl.BlockSpec((tm, d), lambda i: (i, off))
    const = lambda shape: pl.BlockSpec(shape, lambda i: (0, 0), pipeline_mode=pl.Buffered(1))
    return pl.pallas_call(
        _merge_kernel,
        grid=(n_tok // tm,),
        in_specs=[tok(0), tok(0), tok(0), tok(0), tok(2), tok(3),
                  const((1, d)), const(wpa.shape), const(wpb.shape)],
        out_specs=pl.BlockSpec((tm, wpa.shape[1]), lambda i: (i, 0)),
        out_shape=jax.ShapeDtypeStruct((n_tok, wpa.shape[1]), BF16),
        compiler_params=pltpu.CompilerParams(
            dimension_semantics=("arbitrary",), vmem_limit_bytes=VMEM_LIMIT_BYTES),
        name="merge_proj",
    )(oaf, oab, obp, gates, gates, gates, ga, wpa, wpb)


def _final_kernel(y_ref, x_ref, mod_ref, w_ref, g_ref, o_ref):
    yo = jnp.dot(y_ref[...], w_ref[...], preferred_element_type=F32)
    z = x_ref[...] + mod_ref[0, 2:3, :] * yo
    ms = jnp.mean(z * z, axis=-1, keepdims=True)
    o_ref[...] = z * lax.rsqrt(ms + EPS) * g_ref[...]


def _final(y, x2d, mod3, w_out, fg, *, tm, tiles_per_batch):
    n_tok, d = x2d.shape
    const = lambda shape: pl.BlockSpec(shape, lambda i: (0, 0), pipeline_mode=pl.Buffered(1))
    return pl.pallas_call(
        _final_kernel,
        grid=(n_tok // tm,),
        in_specs=[pl.BlockSpec((tm, d), lambda i: (i, 0)),
                  pl.BlockSpec((tm, d), lambda i: (i, 0)),
                  pl.BlockSpec((1, 3, d), lambda i: (i // tiles_per_batch, 0, 0)),
                  const(w_out.shape), const((1, d))],
        out_specs=pl.BlockSpec((tm, d), lambda i: (i, 0)),
        out_shape=jax.ShapeDtypeStruct((n_tok, d), F32),
        compiler_params=pltpu.CompilerParams(
            dimension_semantics=("arbitrary",), vmem_limit_bytes=VMEM_LIMIT_BYTES),
        name="out_proj_final",
    )(y, x2d, mod3, w_out, fg)


def kernel(x, c, ctx, c_ctx, w_ada, b_ada, norm_g, w_in, hgrn_lb_logits, gla_w_gk, gla_b_gk,
           hgrn_onorm_g, gla_onorm_g, w_pa, w_pb, w_out, final_norm_g):
    bsz, seq, d = x.shape
    n_ctx = ctx.shape[1]
    depth = w_in.shape[0]
    assert depth == 1, "single-layer trunk"
    a_kw, a_vw = A_HEADS * A_DK, A_HEADS * A_DV
    b_kw, b_vw = B_HEADS * B_DK, B_HEADS * B_DV

    n_rows = -(-(bsz + 1) // SUBLANES) * SUBLANES
    cvec = jnp.zeros((n_rows, d), F32).at[:bsz].set(c).at[bsz].set(c_ctx)
    mod = _adaln(cvec, w_ada[0], b_ada[0].reshape(1, -1))
    mod3 = mod.reshape(n_rows, 3, d)

    lb = _lower_bounds(hgrn_lb_logits)[0:1]

    o_ag = 2 * a_kw + 2 * a_vw
    o_bq = o_ag + a_vw
    o_br = o_bq + 2 * b_kw + b_vw
    w_main, w_r = _prep_weights(
        jnp.swapaxes(w_in[0], 0, 1), [(0, o_ag), (o_bq, o_br - o_bq), (o_ag, o_bq - o_ag)],
        tail_start=o_br)
    gk_pad = jnp.zeros((LANES, 2 * b_kw), F32)
    gk_pad = gk_pad.at[0:B_RANK, 0:b_kw].set(gla_w_gk[0, 0]).at[B_RANK:2 * B_RANK, b_kw:].set(gla_w_gk[0, 1])
    gk_pad = gk_pad.astype(BF16)
    gkb = gla_b_gk[0].reshape(1, 2 * b_kw)
    ng = norm_g[0].reshape(1, d)

    x2d = x.reshape(bsz * seq, d)
    ctx2d = ctx.reshape(bsz * n_ctx, d)
    tm = 1024
    tiles_per_batch = seq // tm

    c_aqi, c_alf, c_bqkv, c_blf = _inproj(
        ctx2d, mod3, lambda i: bsz, ng, w_main, w_r, lb, gk_pad, gkb,
        tm=bsz * n_ctx, latent=False)
    aqi, alf, bqkv, blf, gates = _inproj(
        x2d, mod3, lambda i: i // tiles_per_batch, ng, w_main, w_r, lb, gk_pad, gkb,
        tm=tm, latent=True, n_batch=bsz)

    r3 = lambda t, n: t.reshape(bsz, n, t.shape[-1])
    oaf, oab = _scan_a(r3(c_aqi, n_ctx), r3(c_alf, n_ctx), r3(aqi, seq), r3(alf, seq))
    obf, obb = _scan_b(r3(c_bqkv, n_ctx), r3(c_blf, n_ctx), bqkv, blf)
    obp = _gla_finalize(obf, obb, gates, gla_onorm_g[0].reshape(1, -1), sg_col_off=a_vw // B_DV)

    r2 = lambda t: t.reshape(bsz * seq, t.shape[-1])
    y = _merge(r2(oaf), r2(oab), obp, gates, hgrn_onorm_g[0].reshape(1, -1),
               w_pa[0].astype(BF16), w_pb[0].astype(BF16), tm=256)
    out = _final(y, x2d, mod3, w_out[0].astype(BF16), final_norm_g.reshape(1, d),
                 tm=512, tiles_per_batch=seq // 512)
    return out.reshape(bsz, seq, d)
```

```python
import functools

import jax
import jax.numpy as jnp
from jax import lax
from jax.experimental import pallas as pl
from jax.experimental.pallas import tpu as pltpu

F32 = jnp.float32
BF16 = jnp.bfloat16

CHUNK = 64
SUB = 16
GRID_W = 64
EPS = 1e-6
A_HEADS, A_DK, A_DV = 16, 128, 128
B_HEADS, B_DK, B_DV = 4, 256, 512
B_RANK = 16
GATE_NORMALIZER = 16.0
LOG2E = 1.4426950408889634

VMEM_LIMIT_BYTES = 56 * 1024 * 1024
LANES = 128
SUBLANES = 8

NT_DIMS = (((1,), (1,)), ((), ()))
TN_DIMS = (((0,), (0,)), ((), ()))


def _sigmoid(z):
    return 1.0 / (1.0 + jnp.exp(-z))


def _adaln_kernel(c_ref, w_ref, b_ref, o_ref):
    c = c_ref[...]
    s = c * _sigmoid(c)
    o_ref[...] = jnp.dot(s, w_ref[...], preferred_element_type=F32,
                         precision=lax.Precision.HIGHEST) + b_ref[...]


def _adaln(cvec, w, b):
    rows, d = cvec.shape
    n = w.shape[1]
    tn = 768
    return pl.pallas_call(
        _adaln_kernel,
        grid=(n // tn,),
        in_specs=[pl.BlockSpec((rows, d), lambda j: (0, 0)),
                  pl.BlockSpec((d, tn), lambda j: (0, j)),
                  pl.BlockSpec((1, tn), lambda j: (0, j))],
        out_specs=pl.BlockSpec((rows, tn), lambda j: (0, j)),
        out_shape=jax.ShapeDtypeStruct((rows, n), F32),
        compiler_params=pltpu.CompilerParams(
            dimension_semantics=("arbitrary",), vmem_limit_bytes=32 * 1024 * 1024),
        name="adaln",
    )(cvec, w, b)


def _lower_bound_kernel(l_ref, o_ref):
    x = l_ref[...]
    n_rows = x.shape[0]
    m = jnp.max(x, axis=0, keepdims=True)
    e = jnp.exp(x - m)
    tot = jnp.sum(e, axis=0, keepdims=True)
    run = jnp.zeros_like(tot)
    for r in range(n_rows - 1):
        run = run + e[r:r + 1]
        o_ref[r:r + 1, :] = run / tot


def _lower_bounds(logits):
    n_rows, w = logits.shape
    return pl.pallas_call(
        _lower_bound_kernel,
        out_shape=jax.ShapeDtypeStruct((n_rows - 1, w), F32),
        name="hgrn_lower_bounds",
    )(logits)


def _prep_weights_kernel(wa_ref, wb_ref, o_ref, r_ref, *, j_tail, shift):
    j = pl.program_id(0)

    @pl.when(j < j_tail)
    def _():
        o_ref[...] = wa_ref[...].astype(BF16)

    @pl.when(j >= j_tail)
    def _():
        o_ref[...] = jnp.concatenate([wa_ref[shift:, :], wb_ref[:shift, :]], axis=0).astype(BF16)

    @pl.when(j == j_tail)
    def _():
        rank_rows = wa_ref[:shift, :].astype(BF16)
        r_ref[...] = jnp.concatenate(
            [rank_rows, jnp.zeros((LANES - shift, rank_rows.shape[1]), BF16)], axis=0)


def _prep_weights(wt, group_starts, tail_start):
    n_in, d = wt.shape
    shift = 2 * B_RANK
    src_tiles = []
    for start, width in group_starts:
        assert start % TN == 0 and width % TN == 0
        src_tiles += list(range(start // TN, (start + width) // TN))
    assert tail_start % TN == 0
    j_tail = len(src_tiles)
    n_tail = (n_in - tail_start - shift) // TN
    n_out = j_tail + n_tail
    last_src = (n_in - 1) // TN

    def src_a(j):
        idx = j - j_tail + tail_start // TN
        for dst in range(j_tail - 1, -1, -1):
            idx = jnp.where(j == dst, src_tiles[dst], idx)
        return idx

    return pl.pallas_call(
        functools.partial(_prep_weights_kernel, j_tail=j_tail, shift=shift),
        grid=(n_out,),
        in_specs=[pl.BlockSpec((TN, d), lambda j: (src_a(j), 0)),
                  pl.BlockSpec((TN, d), lambda j: (jnp.minimum(src_a(j) + 1, last_src), 0))],
        out_specs=[pl.BlockSpec((TN, d), lambda j: (j, 0)),
                   pl.BlockSpec((LANES, d), lambda j: (0, 0))],
        out_shape=[jax.ShapeDtypeStruct((n_out * TN, d), BF16),
                   jax.ShapeDtypeStruct((LANES, d), BF16)],
        compiler_params=pltpu.CompilerParams(
            dimension_semantics=("arbitrary",), vmem_limit_bytes=40 * 1024 * 1024),
        name="prep_weights",
    )(wt, wt)


TN = 512
NORM_ROWS = 128
EPILOGUE_PARTS = 4
T_AQI = 4096 // TN
T_ALF = 4096 // TN
T_BQKV = 4096 // TN
T_BLF = 2048 // TN
T_GATES = 8192 // TN
J_ALF = T_AQI
J_BQKV = J_ALF + T_ALF
J_BLF = J_BQKV + T_BQKV
J_GATES = J_BLF + T_BLF
J_END = J_GATES + T_GATES


def _inproj_kernel(x_ref, mod_ref, ng_ref, w_ref, wr_ref, lb_ref, gkw_ref, gkb_ref, *rest, latent):
    if latent:
        aqi_ref, alf_ref, bqkv_ref, blf_ref, gates_ref, h_ref, r_ref = rest
    else:
        aqi_ref, alf_ref, bqkv_ref, blf_ref, h_ref, r_ref = rest
        gates_ref = None
    j = pl.program_id(1)

    def store_gla(out_ref, z):
        if not latent:
            out_ref[...] = z.astype(out_ref.dtype)
            return
        n_rows = z.shape[0] // GRID_W
        zc = jnp.swapaxes(z.reshape(n_rows, GRID_W, z.shape[1]), 0, 1)
        out_ref[...] = zc.astype(out_ref.dtype)

    @pl.when(j == 0)
    def _():
        def norm_rows(t, carry):
            rows = pl.ds(pl.multiple_of(t * NORM_ROWS, NORM_ROWS), NORM_ROWS)
            x = x_ref[rows, :]
            ms = jnp.mean(x * x, axis=-1, keepdims=True)
            y = x * lax.rsqrt(ms + EPS) * ng_ref[...]
            h = (y * (1.0 + mod_ref[0, 1:2, :]) + mod_ref[0, 0:1, :]).astype(BF16)
            h_ref[rows, :] = h
            r_ref[rows, :] = lax.dot_general(h, wr_ref[...], NT_DIMS,
                                             preferred_element_type=F32).astype(BF16)
            return carry
        lax.fori_loop(0, x_ref.shape[0] // NORM_ROWS, norm_rows, 0)

    def mm():
        return lax.dot_general(h_ref[...], w_ref[...], NT_DIMS, preferred_element_type=F32)

    def mm_parts(out_ref, epilogue):
        part = h_ref.shape[0] // EPILOGUE_PARTS
        for m in range(EPILOGUE_PARTS):
            rows = slice(m * part, (m + 1) * part)
            z = lax.dot_general(h_ref[rows, :], w_ref[...], NT_DIMS, preferred_element_type=F32)
            out_ref[rows, :] = epilogue(z).astype(out_ref.dtype)

    @pl.when(j < J_ALF)
    def _():
        mm_parts(aqi_ref, lambda z: z)

    @pl.when((j >= J_ALF) & (j < J_BQKV))
    def _():
        lb = lb_ref[...]
        mm_parts(alf_ref, lambda z: jnp.log2(lb + (1.0 - lb) * _sigmoid(z)))

    @pl.when((j >= J_BQKV) & (j < J_BLF))
    def _():
        scale = jnp.where(j < J_BQKV + (B_HEADS * B_DK) // TN, B_DK ** -0.5, 1.0)
        store_gla(bqkv_ref, mm() * scale)

    @pl.when((j >= J_BLF) & (j < J_GATES))
    def _():
        z = jnp.dot(r_ref[...], gkw_ref[...], preferred_element_type=F32) + gkb_ref[...]
        ls = jnp.minimum(z, 0.0) - jnp.log(1.0 + jnp.exp(-jnp.abs(z)))
        store_gla(blf_ref, ls * (LOG2E / GATE_NORMALIZER))

    if latent:
        @pl.when(j >= J_GATES)
        def _():
            silu = j < J_GATES + T_GATES // 2
            mm_parts(gates_ref, lambda z: _sigmoid(z) * jnp.where(silu, z, 1.0))


def _inproj(x2d, mod3, mod_row_of_tile, norm_g, w_main, w_r, lb, gk_pad, gkb, *, tm, latent, n_batch=1):
    n_tok, d = x2d.shape
    n_j = J_END if latent else J_GATES
    tile_rows = tm // GRID_W
    seq_rows = n_tok // n_batch // GRID_W
    tiles_per_batch = n_tok // n_batch // tm

    def cl(j, lo, n):
        return jnp.clip(j - lo, 0, n - 1)

    def gla_spec(j_lo, n):
        if not latent:
            return pl.BlockSpec((tm, TN), lambda i, j: (i, cl(j, j_lo, n)))
        return pl.BlockSpec((None, GRID_W, tile_rows, TN),
                            lambda i, j: (i // tiles_per_batch, 0, i % tiles_per_batch, cl(j, j_lo, n)))

    def gla_shape(n, dtype):
        if not latent:
            return jax.ShapeDtypeStruct((n_tok, n * TN), dtype)
        return jax.ShapeDtypeStruct((n_batch, GRID_W, seq_rows, n * TN), dtype)

    in_specs = [
        pl.BlockSpec((tm, d), lambda i, j: (i, 0)),
        pl.BlockSpec((1, 3, d), lambda i, j: (mod_row_of_tile(i), 0, 0)),
        pl.BlockSpec((1, d), lambda i, j: (0, 0)),
        pl.BlockSpec((TN, d), lambda i, j: (j - jnp.clip(j - (J_BLF - 1), 0, T_BLF), 0)),
        pl.BlockSpec((LANES, d), lambda i, j: (0, 0)),
        pl.BlockSpec((1, TN), lambda i, j: (0, cl(j, J_ALF, T_ALF))),
        pl.BlockSpec((LANES, TN), lambda i, j: (0, cl(j, J_BLF, T_BLF))),
        pl.BlockSpec((1, TN), lambda i, j: (0, cl(j, J_BLF, T_BLF))),
    ]
    out_specs = [
        pl.BlockSpec((tm, TN), lambda i, j: (i, cl(j, 0, T_AQI))),
        pl.BlockSpec((tm, TN), lambda i, j: (i, cl(j, J_ALF, T_ALF))),
        gla_spec(J_BQKV, T_BQKV),
        gla_spec(J_BLF, T_BLF),
    ]
    out_shape = [
        jax.ShapeDtypeStruct((n_tok, T_AQI * TN), BF16),
        jax.ShapeDtypeStruct((n_tok, T_ALF * TN), F32),
        gla_shape(T_BQKV, BF16),
        gla_shape(T_BLF, F32),
    ]
    if latent:
        out_specs.append(pl.BlockSpec((tm, TN), lambda i, j: (i, cl(j, J_GATES, T_GATES))))
        out_shape.append(jax.ShapeDtypeStruct((n_tok, T_GATES * TN), BF16))
    return pl.pallas_call(
        functools.partial(_inproj_kernel, latent=latent),
        grid=(n_tok // tm, n_j),
        in_specs=in_specs,
        out_specs=out_specs,
        out_shape=out_shape,
        scratch_shapes=[pltpu.VMEM((tm, d), BF16), pltpu.VMEM((tm, LANES), BF16)],
        compiler_params=pltpu.CompilerParams(
            dimension_semantics=("arbitrary", "arbitrary"), vmem_limit_bytes=VMEM_LIMIT_BYTES),
        name="inproj_latent" if latent else "inproj_ctx",
    )(x2d, mod3, norm_g, w_main, w_r, lb, gk_pad, gkb)


def _scan_consts():
    row = lax.broadcasted_iota(jnp.int32, (CHUNK, CHUNK), 0)
    col = lax.broadcasted_iota(jnp.int32, (CHUNK, CHUNK), 1)
    return dict(
        mask_f=col <= row,
        mask_b=col >= row,
        tri_f=jnp.where(col <= row, 1.0, 0.0).astype(BF16),
        tri_b=jnp.where(col >= row, 1.0, 0.0).astype(BF16),
        lane=lax.broadcasted_iota(jnp.int32, (SUBLANES, CHUNK), 1),
    )


DIAG_FACTOR_MAX_LOG2 = 96.0
FOLDED_SUB = 32


def _cumsum_stage(lf, has_k, fwd, cst):
    dk = lf.shape[1]
    tri = cst["tri_f"] if fwd else cst["tri_b"]
    hi = lf.astype(BF16)
    lo = (lf - hi.astype(F32)).astype(BF16)
    cc = jnp.dot(tri, jnp.concatenate([hi, lo], axis=1), preferred_element_type=F32)
    c = cc[:, :dk] + cc[:, dk:]
    a = None if has_k else c - jnp.log2(1.0 - jnp.exp2(lf))
    mag = jnp.abs(lf)
    excess = None
    for r0 in range(0, CHUNK, FOLDED_SUB):
        e = jnp.sum(mag[r0:r0 + FOLDED_SUB], axis=0, keepdims=True)
        excess = e if excess is None else jnp.maximum(excess, e)
    return c, a, excess


def _intra_stage(q, k, v, c, a, st_ref, fwd, cst, keep, exact_diag=True):
    dk = c.shape[1]
    tot = c[CHUNK - 1:CHUNK] if fwd else c[0:1]

    if k is None:
        kf = None

        def kscaled(r0, r1, ref):
            return jnp.exp2(ref - a[r0:r1])
    else:
        a = c
        kf = k.astype(F32)

        def kscaled(r0, r1, ref):
            return kf[r0:r1] * jnp.exp2(ref - c[r0:r1])

    kte = kscaled(0, CHUNK, tot).astype(BF16)
    st = st_ref[...]
    if keep is not None:
        keep.prev_state(st.astype(BF16))
    st_ref[...] = st * jnp.exp2(tot) + lax.dot_general(v, kte, TN_DIMS, preferred_element_type=F32)
    if keep is None:
        return
    yield

    qf = q.astype(F32)
    keep.scaled_q((qf * jnp.exp2(c)).astype(BF16))
    half = SUB // SUBLANES
    sub, own = (SUB, 0) if exact_diag else (FOLDED_SUB, FOLDED_SUB)
    soffs = []
    for r0 in range(0, CHUNK, sub):
        if fwd:
            lo_row, hi_row = 0, r0 + own
            ref = c[r0:r0 + 1]
        else:
            lo_row, hi_row = r0 + sub - own, CHUNK
            ref = c[r0 + sub - 1:r0 + sub]
        if hi_row == lo_row:
            soffs.append(None)
            continue
        pieces = [kscaled(lo_row, hi_row, ref).astype(BF16)]
        if lo_row > 0:
            pieces.insert(0, jnp.zeros((lo_row, dk), BF16))
        if hi_row < CHUNK:
            pieces.append(jnp.zeros((CHUNK - hi_row, dk), BF16))
        kr = jnp.concatenate(pieces, axis=0) if len(pieces) > 1 else pieces[0]
        qr = (qf[r0:r0 + sub] * jnp.exp2(c[r0:r0 + sub] - ref)).astype(BF16)
        soffs.append(lax.dot_general(qr, kr, NT_DIMS, preferred_element_type=F32))
    yield

    if not exact_diag:
        p = jnp.concatenate(soffs, axis=0)
        keep.scores(jnp.where(cst["mask_f"] if fwd else cst["mask_b"], p, 0.0).astype(BF16))
        return

    rows = []
    for r, r0 in enumerate(range(0, CHUNK, SUB)):
        cb = c[r0:r0 + SUB]
        qb = qf[r0:r0 + SUB]
        acc = [jnp.zeros((SUBLANES, CHUNK), F32) for _ in range(half)]
        for jj in range(SUB):
            aj = a[r0 + jj:r0 + jj + 1]
            jg = jj // SUBLANES
            groups = range(jg, half) if fwd else range(0, jg + 1)
            for g in groups:
                e = cb[g * SUBLANES:(g + 1) * SUBLANES] - aj
                if g == jg:
                    e = jnp.minimum(e, 0.0)
                t = qb[g * SUBLANES:(g + 1) * SUBLANES] * jnp.exp2(e)
                if kf is not None:
                    t = t * kf[r0 + jj:r0 + jj + 1]
                sj = jnp.sum(t, axis=1, keepdims=True)
                acc[g] = jnp.where(cst["lane"] == r0 + jj, sj, acc[g])
        d = jnp.concatenate(acc, axis=0)
        rows.append(d if soffs[r] is None else d + soffs[r])
    p = jnp.concatenate(rows, axis=0)
    keep.scores(jnp.where(cst["mask_f"] if fwd else cst["mask_b"], p, 0.0).astype(BF16))


def _run_staged(units):
    live = list(units)
    while live:
        nxt = []
        for u in live:
            try:
                next(u)
                nxt.append(u)
            except StopIteration:
                pass
        live = nxt


def _scan_kernel(*refs, n_heads, dk, dv, has_k, n_ctx_chunks, n_step_chunks, lat_index):
    n_ctx_in = 4 if has_k else 3
    n_lat_in = 4 if has_k else 3
    ctx_refs = refs[:n_ctx_in]
    lat_refs = (refs[n_ctx_in:n_ctx_in + n_lat_in], refs[n_ctx_in + n_lat_in:n_ctx_in + 2 * n_lat_in])
    rest = refs[n_ctx_in + 2 * n_lat_in:]
    out_refs = rest[0:2]
    if has_k:
        st_ref, c_ref, stb_ref = rest[2:]
        a_ref = None
    else:
        st_ref, c_ref, stb_ref, a_ref = rest[2:]
    s = pl.program_id(2)
    cst = _scan_consts()
    units = [(h, d) for h in range(n_heads) for d in range(2)]

    def kcols(h):
        return slice(h * dk, (h + 1) * dk)

    def vcols(h):
        return slice(h * dv, (h + 1) * dv)

    def ctx_step():
        st_ref[...] = jnp.zeros(st_ref.shape, F32)
        if has_k:
            k_ref, v_ref, lff_ref, lfb_ref = ctx_refs
        else:
            v_ref, lff_ref, lfb_ref = ctx_refs
            k_ref = None

        def unit(h, d, rows):
            lf = (lff_ref, lfb_ref)[d][rows, kcols(h)]
            c, a, _ = _cumsum_stage(lf, has_k, d == 0, cst)
            yield
            k = None if k_ref is None else k_ref[rows, kcols(h)]
            yield from _intra_stage(None, k, v_ref[rows, vcols(h)], c, a, st_ref.at[d, h], d == 0, cst, None)

        def body(i, carry):
            rows = (pl.ds(pl.multiple_of(i * CHUNK, CHUNK), CHUNK),
                    pl.ds(pl.multiple_of((n_ctx_chunks - 1 - i) * CHUNK, CHUNK), CHUNK))
            _run_staged([unit(h, d, rows[d]) for h, d in units])
            return carry
        lax.fori_loop(0, n_ctx_chunks, body, 0)

    def latent_step():
        n = n_step_chunks

        def index(d, pos):
            return lat_index(pos if d == 0 else n - 1 - pos)

        def read(d, pos, h, what):
            refs_d = lat_refs[d]
            ref = refs_d[{"q": 0, "k": 1, "v": 2 if has_k else 1, "lf": 3 if has_k else 2}[what]]
            cols = vcols(h) if what == "v" else kcols(h)
            return ref[index(d, pos) + (cols,)]

        def cumsum_all(pos, slot):
            worst = None
            for u, (h, d) in enumerate(units):
                c, a, excess = _cumsum_stage(read(d, pos, h, "lf"), has_k, d == 0, cst)
                c_ref[slot, u] = c
                if a is not None:
                    a_ref[slot, u] = a
                worst = excess if worst is None else jnp.maximum(worst, excess)
            return jnp.max(worst)

        def output_all(pos, carried):
            for u, (h, d) in enumerate(units):
                p, qd = carried[u]
                o = jnp.dot(p, read(d, pos, h, "v"), preferred_element_type=F32)
                o = o + lax.dot_general(qd, stb_ref[u], NT_DIMS, preferred_element_type=F32)
                out_refs[d][index(d, pos) + (vcols(h),)] = o.astype(out_refs[d].dtype)

        class Keeper:
            def __init__(self, u):
                self.u = u
                self.p = self.qd = None

            def prev_state(self, stb):
                stb_ref[self.u] = stb

            def scaled_q(self, qd):
                self.qd = qd

            def scores(self, p):
                self.p = p

        def intra_all(pos, cs, exact_diag):
            keepers = [Keeper(u) for u in range(len(units))]
            gens = []
            for u, (h, d) in enumerate(units):
                k = read(d, pos, h, "k") if has_k else None
                c, a = cs[u]
                gens.append(_intra_stage(read(d, pos, h, "q"), k, read(d, pos, h, "v"), c, a,
                                         st_ref.at[d, h], d == 0, cst, keepers[u], exact_diag))
            _run_staged(gens)
            return tuple((kp.p, kp.qd) for kp in keepers)

        def load_cumsums(slot):
            return [(c_ref[slot, u], None if has_k else a_ref[slot, u]) for u in range(len(units))]

        def either_path(worst, build):
            return lax.cond(worst <= DIAG_FACTOR_MAX_LOG2,
                            functools.partial(build, False), functools.partial(build, True))

        worst0 = cumsum_all(0, 0)
        cs0 = load_cumsums(0)
        worst1 = cumsum_all(1, 1)
        carried0 = either_path(worst0, lambda exact_diag: intra_all(0, cs0, exact_diag))

        def body(i, state):
            worst, carried = state
            slot = lax.rem(i, 2)

            def iteration(exact_diag):
                cs = load_cumsums(slot)
                output_all(i - 1, carried)
                worst_next = cumsum_all(jnp.minimum(i + 1, n - 1), 1 - slot)
                return worst_next, intra_all(i, cs, exact_diag)
            return either_path(worst, iteration)
        _, carried_last = lax.fori_loop(1, n, body, (worst1, carried0))
        output_all(n - 1, carried_last)

    @pl.when(s == 0)
    def _():
        ctx_step()

    @pl.when(s > 0)
    def _():
        latent_step()


def _scan_call(inputs, in_specs, out_struct, out_specs, grid, *, n_heads, dk, dv, has_k,
               n_ctx_chunks, n_step_chunks, lat_index, name):
    n_units = 2 * n_heads
    scratch = [pltpu.VMEM((2, n_heads, dv, dk), F32),
               pltpu.VMEM((2, n_units, CHUNK, dk), F32),
               pltpu.VMEM((n_units, dv, dk), BF16)]
    if not has_k:
        scratch.append(pltpu.VMEM((2, n_units, CHUNK, dk), F32))
    return pl.pallas_call(
        functools.partial(_scan_kernel, n_heads=n_heads, dk=dk, dv=dv, has_k=has_k,
                          n_ctx_chunks=n_ctx_chunks, n_step_chunks=n_step_chunks, lat_index=lat_index),
        grid=grid,
        in_specs=in_specs,
        out_specs=out_specs,
        out_shape=out_struct,
        scratch_shapes=scratch,
        compiler_params=pltpu.CompilerParams(
            dimension_semantics=("arbitrary", "arbitrary", "arbitrary"),
            vmem_limit_bytes=VMEM_LIMIT_BYTES),
        name=name,
    )(*inputs)


A_SCAN_HEADS = 2
A_SCAN_CHUNKS = 8


def _scan_a(c_qi, c_lf, qi, lf):
    bsz, n_ctx, _ = c_qi.shape
    seq = qi.shape[1]
    g = A_SCAN_HEADS
    gw = g * A_DK
    n_hg = A_HEADS // g
    ts = A_SCAN_CHUNKS * CHUNK
    n_steps = seq // ts

    def fstep(s):
        return jnp.maximum(s - 1, 0)

    def bstep(s):
        return n_steps - 1 - jnp.maximum(s - 1, 0)

    cspec = lambda off: pl.BlockSpec((None, n_ctx, gw), lambda b, h, s: (b, 0, off + h))
    fspec = lambda off: pl.BlockSpec((None, ts, gw), lambda b, h, s: (b, fstep(s), off + h))
    bspec = lambda off: pl.BlockSpec((None, ts, gw), lambda b, h, s: (b, bstep(s), off + h))
    in_specs = [cspec(n_hg), cspec(0), cspec(n_hg),
                fspec(0), fspec(n_hg), fspec(0),
                bspec(0), bspec(n_hg), bspec(n_hg)]
    inputs = [c_qi, c_lf, c_lf, qi, qi, lf, qi, qi, lf]
    out_struct = [jax.ShapeDtypeStruct((bsz, seq, A_HEADS * A_DV), BF16)] * 2
    out_specs = [pl.BlockSpec((None, ts, gw), lambda b, h, s: (b, fstep(s), h)),
                 pl.BlockSpec((None, ts, gw), lambda b, h, s: (b, bstep(s), h))]

    def lat_index(chunk):
        row = chunk * CHUNK
        if not isinstance(row, int):
            row = pl.multiple_of(row, CHUNK)
        return (pl.ds(row, CHUNK),)

    return _scan_call(inputs, in_specs, out_struct, out_specs, (bsz, n_hg, n_steps + 1),
                      n_heads=g, dk=A_DK, dv=A_DV, has_k=False,
                      n_ctx_chunks=n_ctx // CHUNK, n_step_chunks=A_SCAN_CHUNKS,
                      lat_index=lat_index, name="scan_hgrn2")


B_SCAN_COLS = 4


def _scan_b(c_qkv, c_lf, qkv, lf):
    bsz, n_ctx, _ = c_qkv.shape
    n_rows = qkv.shape[2]
    kw = B_HEADS * B_DK
    nc = B_SCAN_COLS
    n_steps = GRID_W // nc
    chunks_per_col = n_rows // CHUNK

    def fblk(s):
        return jnp.maximum(s - 1, 0)

    def bblk(s):
        return n_steps - 1 - jnp.maximum(s - 1, 0)

    def cspec(width, off):
        return pl.BlockSpec((None, n_ctx, width), lambda b, h, s: (b, 0, off + h))

    def lspec(width, off, blk):
        return pl.BlockSpec((None, nc, n_rows, width), lambda b, h, s: (b, blk(s), 0, off + h))

    k_off, v_off = kw // B_DK, (2 * kw) // B_DV
    in_specs = [cspec(B_DK, k_off), cspec(B_DV, v_off), cspec(B_DK, 0), cspec(B_DK, B_HEADS),
                lspec(B_DK, 0, fblk), lspec(B_DK, k_off, fblk), lspec(B_DV, v_off, fblk), lspec(B_DK, 0, fblk),
                lspec(B_DK, 0, bblk), lspec(B_DK, k_off, bblk), lspec(B_DV, v_off, bblk),
                lspec(B_DK, B_HEADS, bblk)]
    inputs = [c_qkv, c_qkv, c_lf, c_lf, qkv, qkv, qkv, lf, qkv, qkv, qkv, lf]
    vw = B_HEADS * B_DV
    out_struct = [jax.ShapeDtypeStruct((bsz, GRID_W, n_rows, vw), BF16)] * 2
    out_specs = [pl.BlockSpec((None, nc, n_rows, B_DV), lambda b, h, s: (b, fblk(s), 0, h)),
                 pl.BlockSpec((None, nc, n_rows, B_DV), lambda b, h, s: (b, bblk(s), 0, h))]

    def lat_index(chunk):
        col = chunk // chunks_per_col
        row = (chunk - col * chunks_per_col) * CHUNK
        if not isinstance(row, int):
            row = pl.multiple_of(row, CHUNK)
        return (col, pl.ds(row, CHUNK))

    return _scan_call(inputs, in_specs, out_struct, out_specs, (bsz, B_HEADS, n_steps + 1),
                      n_heads=1, dk=B_DK, dv=B_DV, has_k=True,
                      n_ctx_chunks=n_ctx // CHUNK, n_step_chunks=nc * chunks_per_col,
                      lat_index=lat_index, name="scan_gla")


GLA_FIN_ROWS = 16


def _gla_finalize_kernel(of_ref, ob_ref, sg_ref, g_ref, o_ref):
    n_cols, n_rows, w = of_ref.shape
    o = of_ref[...].astype(F32) + ob_ref[...].astype(F32)
    ms = jnp.mean(o * o, axis=-1, keepdims=True)
    on = o * lax.rsqrt(ms + EPS) * g_ref[...]
    on = jnp.swapaxes(on, 0, 1).reshape(n_rows * n_cols, w)
    o_ref[...] = (on * sg_ref[...].astype(F32)).astype(o_ref.dtype)


def _gla_finalize(of, ob, gates, gain, *, sg_col_off):
    bsz, _, n_rows, vw = of.shape
    tm = GLA_FIN_ROWS * GRID_W
    tiles = n_rows // GLA_FIN_ROWS
    cm = pl.BlockSpec((None, GRID_W, GLA_FIN_ROWS, B_DV), lambda b, r, h: (b, 0, r, h))
    return pl.pallas_call(
        _gla_finalize_kernel,
        grid=(bsz, tiles, B_HEADS),
        in_specs=[cm, cm,
                  pl.BlockSpec((tm, B_DV), lambda b, r, h: (b * tiles + r, sg_col_off + h)),
                  pl.BlockSpec((1, B_DV), lambda b, r, h: (0, h))],
        out_specs=pl.BlockSpec((tm, B_DV), lambda b, r, h: (b * tiles + r, h)),
        out_shape=jax.ShapeDtypeStruct((bsz * n_rows * GRID_W, vw), BF16),
        compiler_params=pltpu.CompilerParams(
            dimension_semantics=("arbitrary", "arbitrary", "arbitrary"),
            vmem_limit_bytes=32 * 1024 * 1024),
        name="gla_finalize",
    )(of, ob, gates, gain)


def _merge_kernel(oaf, oab, obp, sga, sma, smb, ga, wpa, wpb, y_ref):
    o = oaf[...].astype(F32) + oab[...].astype(F32)
    parts = []
    for h in range(A_HEADS):
        oh = o[:, h * A_DV:(h + 1) * A_DV]
        ms = jnp.mean(oh * oh, axis=-1, keepdims=True)
        parts.append(oh * lax.rsqrt(ms + EPS))
    oa = (jnp.concatenate(parts, axis=1) * ga[...] * sga[...].astype(F32)).astype(BF16)
    ya = jnp.dot(oa, wpa[...], preferred_element_type=F32)
    yb = jnp.dot(obp[...], wpb[...], preferred_element_type=F32)
    y_ref[...] = (sma[...].astype(F32) * ya + smb[...].astype(F32) * yb).astype(BF16)


def _merge(oaf, oab, obp, gates, ga, wpa, wpb, *, tm):
    n_tok, d = oaf.shape
    tok = lambda off: pl.BlockSpec((tm, d), lambda i: (i, off))
    const = lambda shape: pl.BlockSpec(shape, lambda i: (0, 0), pipeline_mode=pl.Buffered(1))
    return pl.pallas_call(
        _merge_kernel,
        grid=(n_tok // tm,),
        in_specs=[tok(0), tok(0), tok(0), tok(0), tok(2), tok(3),
                  const((1, d)), const(wpa.shape), const(wpb.shape)],
        out_specs=pl.BlockSpec((tm, wpa.shape[1]), lambda i: (i, 0)),
        out_shape=jax.ShapeDtypeStruct((n_tok, wpa.shape[1]), BF16),
        compiler_params=pltpu.CompilerParams(
            dimension_semantics=("arbitrary",), vmem_limit_bytes=VMEM_LIMIT_BYTES),
        name="merge_proj",
    )(oaf, oab, obp, gates, gates, gates, ga, wpa, wpb)


def _final_kernel(y_ref, x_ref, mod_ref, w_ref, g_ref, o_ref):
    yo = jnp.dot(y_ref[...], w_ref[...], preferred_element_type=F32)
    z = x_ref[...] + mod_ref[0, 2:3, :] * yo
    ms = jnp.mean(z * z, axis=-1, keepdims=True)
    o_ref[...] = z * lax.rsqrt(ms + EPS) * g_ref[...]


def _final(y, x2d, mod3, w_out, fg, *, tm, tiles_per_batch):
    n_tok, d = x2d.shape
    const = lambda shape: pl.BlockSpec(shape, lambda i: (0, 0), pipeline_mode=pl.Buffered(1))
    return pl.pallas_call(
        _final_kernel,
        grid=(n_tok // tm,),
        in_specs=[pl.BlockSpec((tm, d), lambda i: (i, 0)),
                  pl.BlockSpec((tm, d), lambda i: (i, 0)),
                  pl.BlockSpec((1, 3, d), lambda i: (i // tiles_per_batch, 0, 0)),
                  const(w_out.shape), const((1, d))],
        out_specs=pl.BlockSpec((tm, d), lambda i: (i, 0)),
        out_shape=jax.ShapeDtypeStruct((n_tok, d), F32),
        compiler_params=pltpu.CompilerParams(
            dimension_semantics=("arbitrary",), vmem_limit_bytes=VMEM_LIMIT_BYTES),
        name="out_proj_final",
    )(y, x2d, mod3, w_out, fg)


def kernel(x, c, ctx, c_ctx, w_ada, b_ada, norm_g, w_in, hgrn_lb_logits, gla_w_gk, gla_b_gk,
           hgrn_onorm_g, gla_onorm_g, w_pa, w_pb, w_out, final_norm_g):
    bsz, seq, d = x.shape
    n_ctx = ctx.shape[1]
    depth = w_in.shape[0]
    assert depth == 1, "single-layer trunk"
    a_kw, a_vw = A_HEADS * A_DK, A_HEADS * A_DV
    b_kw, b_vw = B_HEADS * B_DK, B_HEADS * B_DV

    n_rows = -(-(bsz + 1) // SUBLANES) * SUBLANES
    cvec = jnp.zeros((n_rows, d), F32).at[:bsz].set(c).at[bsz].set(c_ctx)
    mod = _adaln(cvec, w_ada[0], b_ada[0].reshape(1, -1))
    mod3 = mod.reshape(n_rows, 3, d)

    lb = _lower_bounds(hgrn_lb_logits)[0:1]

    o_ag = 2 * a_kw + 2 * a_vw
    o_bq = o_ag + a_vw
    o_br = o_bq + 2 * b_kw + b_vw
    w_main, w_r = _prep_weights(
        jnp.swapaxes(w_in[0], 0, 1), [(0, o_ag), (o_bq, o_br - o_bq), (o_ag, o_bq - o_ag)],
        tail_start=o_br)
    gk_pad = jnp.zeros((LANES, 2 * b_kw), F32)
    gk_pad = gk_pad.at[0:B_RANK, 0:b_kw].set(gla_w_gk[0, 0]).at[B_RANK:2 * B_RANK, b_kw:].set(gla_w_gk[0, 1])
    gk_pad = gk_pad.astype(BF16)
    gkb = gla_b_gk[0].reshape(1, 2 * b_kw)
    ng = norm_g[0].reshape(1, d)

    x2d = x.reshape(bsz * seq, d)
    ctx2d = ctx.reshape(bsz * n_ctx, d)
    tm = 1024
    tiles_per_batch = seq // tm

    c_aqi, c_alf, c_bqkv, c_blf = _inproj(
        ctx2d, mod3, lambda i: bsz, ng, w_main, w_r, lb, gk_pad, gkb,
        tm=bsz * n_ctx, latent=False)
    aqi, alf, bqkv, blf, gates = _inproj(
        x2d, mod3, lambda i: i // tiles_per_batch, ng, w_main, w_r, lb, gk_pad, gkb,
        tm=tm, latent=True, n_batch=bsz)

    r3 = lambda t, n: t.reshape(bsz, n, t.shape[-1])
    oaf, oab = _scan_a(r3(c_aqi, n_ctx), r3(c_alf, n_ctx), r3(aqi, seq), r3(alf, seq))
    obf, obb = _scan_b(r3(c_bqkv, n_ctx), r3(c_blf, n_ctx), bqkv, blf)
    obp = _gla_finalize(obf, obb, gates, gla_onorm_g[0].reshape(1, -1), sg_col_off=a_vw // B_DV)

    r2 = lambda t: t.reshape(bsz * seq, t.shape[-1])
    y = _merge(r2(oaf), r2(oab), obp, gates, hgrn_onorm_g[0].reshape(1, -1),
               w_pa[0].astype(BF16), w_pb[0].astype(BF16), tm=256)
    out = _final(y, x2d, mod3, w_out[0].astype(BF16), final_norm_g.reshape(1, d),
                 tm=512, tiles_per_batch=seq // 512)
    return out.reshape(bsz, seq, d)
```

```python
import functools

import jax
import jax.numpy as jnp
from jax import lax
from jax.experimental import pallas as pl
from jax.experimental.pallas import tpu as pltpu

F32 = jnp.float32
BF16 = jnp.bfloat16

CHUNK = 64
SUB = 16
GRID_W = 64
EPS = 1e-6
A_HEADS, A_DK, A_DV = 16, 128, 128
B_HEADS, B_DK, B_DV = 4, 256, 512
B_RANK = 16
GATE_NORMALIZER = 16.0
LOG2E = 1.4426950408889634

VMEM_LIMIT_BYTES = 56 * 1024 * 1024
LANES = 128
SUBLANES = 8

NT_DIMS = (((1,), (1,)), ((), ()))
TN_DIMS = (((0,), (0,)), ((), ()))


def _sigmoid(z):
    return 1.0 / (1.0 + jnp.exp(-z))


def _adaln_kernel(c_ref, w_ref, b_ref, o_ref):
    c = c_ref[...]
    s = c * _sigmoid(c)
    o_ref[...] = jnp.dot(s, w_ref[...], preferred_element_type=F32,
                         precision=lax.Precision.HIGHEST) + b_ref[...]


def _adaln(cvec, w, b):
    rows, d = cvec.shape
    n = w.shape[1]
    tn = 768
    return pl.pallas_call(
        _adaln_kernel,
        grid=(n // tn,),
        in_specs=[pl.BlockSpec((rows, d), lambda j: (0, 0)),
                  pl.BlockSpec((d, tn), lambda j: (0, j)),
                  pl.BlockSpec((1, tn), lambda j: (0, j))],
        out_specs=pl.BlockSpec((rows, tn), lambda j: (0, j)),
        out_shape=jax.ShapeDtypeStruct((rows, n), F32),
        compiler_params=pltpu.CompilerParams(
            dimension_semantics=("arbitrary",), vmem_limit_bytes=32 * 1024 * 1024),
        name="adaln",
    )(cvec, w, b)


def _lower_bound_kernel(l_ref, o_ref):
    x = l_ref[...]
    n_rows = x.shape[0]
    m = jnp.max(x, axis=0, keepdims=True)
    e = jnp.exp(x - m)
    tot = jnp.sum(e, axis=0, keepdims=True)
    run = jnp.zeros_like(tot)
    for r in range(n_rows - 1):
        run = run + e[r:r + 1]
        o_ref[r:r + 1, :] = run / tot


def _lower_bounds(logits):
    n_rows, w = logits.shape
    return pl.pallas_call(
        _lower_bound_kernel,
        out_shape=jax.ShapeDtypeStruct((n_rows - 1, w), F32),
        name="hgrn_lower_bounds",
    )(logits)


def _prep_weights_kernel(wa_ref, wb_ref, o_ref, r_ref, *, j_tail, shift):
    j = pl.program_id(0)

    @pl.when(j < j_tail)
    def _():
        o_ref[...] = wa_ref[...].T.astype(BF16)

    @pl.when(j >= j_tail)
    def _():
        o_ref[...] = jnp.concatenate([wa_ref[shift:, :], wb_ref[:shift, :]], axis=0).T.astype(BF16)

    @pl.when(j == j_tail)
    def _():
        rank_rows = jnp.concatenate(
            [wa_ref[:shift, :], jnp.zeros((LANES - shift, wa_ref.shape[1]), F32)], axis=0)
        r_ref[...] = rank_rows.T.astype(BF16)


def _prep_weights(wt, group_starts, tail_start):
    n_in, d = wt.shape
    shift = 2 * B_RANK
    src_tiles = []
    for start, width in group_starts:
        assert start % TN == 0 and width % TN == 0
        src_tiles += list(range(start // TN, (start + width) // TN))
    assert tail_start % TN == 0
    j_tail = len(src_tiles)
    n_tail = (n_in - tail_start - shift) // TN
    n_out = j_tail + n_tail
    last_src = (n_in - 1) // TN

    def src_a(j):
        idx = j - j_tail + tail_start // TN
        for dst in range(j_tail - 1, -1, -1):
            idx = jnp.where(j == dst, src_tiles[dst], idx)
        return idx

    return pl.pallas_call(
        functools.partial(_prep_weights_kernel, j_tail=j_tail, shift=shift),
        grid=(n_out,),
        in_specs=[pl.BlockSpec((TN, d), lambda j: (src_a(j), 0)),
                  pl.BlockSpec((TN, d), lambda j: (jnp.minimum(src_a(j) + 1, last_src), 0))],
        out_specs=[pl.BlockSpec((d, TN), lambda j: (0, j)),
                   pl.BlockSpec((d, LANES), lambda j: (0, 0))],
        out_shape=[jax.ShapeDtypeStruct((d, n_out * TN), BF16),
                   jax.ShapeDtypeStruct((d, LANES), BF16)],
        compiler_params=pltpu.CompilerParams(
            dimension_semantics=("arbitrary",), vmem_limit_bytes=40 * 1024 * 1024),
        name="prep_weights",
    )(wt, wt)


TN = 512
NORM_ROWS = 128
EPILOGUE_PARTS = 8
T_AQI = 4096 // TN
T_ALF = 4096 // TN
T_BQKV = 4096 // TN
T_BLF = 2048 // TN
T_GATES = 8192 // TN
J_ALF = T_AQI
J_BQKV = J_ALF + T_ALF
J_BLF = J_BQKV + T_BQKV
J_GATES = J_BLF + T_BLF
J_END = J_GATES + T_GATES


def _inproj_kernel(x_ref, mod_ref, ng_ref, w_ref, wr_ref, lb_ref, gkw_ref, gkb_ref, *rest, latent):
    if latent:
        aqi_ref, alf_ref, bqkv_ref, blf_ref, gates_ref, h_ref, r_ref = rest
    else:
        aqi_ref, alf_ref, bqkv_ref, blf_ref, h_ref, r_ref = rest
        gates_ref = None
    j = pl.program_id(1)

    def store_gla(out_ref, z):
        if not latent:
            out_ref[...] = z.astype(out_ref.dtype)
            return
        n_rows = z.shape[0] // GRID_W
        zc = jnp.swapaxes(z.reshape(n_rows, GRID_W, z.shape[1]), 0, 1)
        out_ref[...] = zc.astype(out_ref.dtype)

    @pl.when(j == 0)
    def _():
        def norm_rows(t, carry):
            rows = pl.ds(pl.multiple_of(t * NORM_ROWS, NORM_ROWS), NORM_ROWS)
            x = x_ref[rows, :]
            ms = jnp.mean(x * x, axis=-1, keepdims=True)
            y = x * lax.rsqrt(ms + EPS) * ng_ref[...]
            h = (y * (1.0 + mod_ref[0, 1:2, :]) + mod_ref[0, 0:1, :]).astype(BF16)
            h_ref[rows, :] = h
            r_ref[rows, :] = jnp.dot(h, wr_ref[...], preferred_element_type=F32).astype(BF16)
            return carry
        lax.fori_loop(0, x_ref.shape[0] // NORM_ROWS, norm_rows, 0)

    def mm():
        return jnp.dot(h_ref[...], w_ref[...], preferred_element_type=F32)

    def mm_parts(out_ref, epilogue):
        part = h_ref.shape[0] // EPILOGUE_PARTS
        for m in range(EPILOGUE_PARTS):
            rows = slice(m * part, (m + 1) * part)
            z = jnp.dot(h_ref[rows, :], w_ref[...], preferred_element_type=F32)
            out_ref[rows, :] = epilogue(z).astype(out_ref.dtype)

    @pl.when(j < J_ALF)
    def _():
        mm_parts(aqi_ref, lambda z: z)

    @pl.when((j >= J_ALF) & (j < J_BQKV))
    def _():
        lb = lb_ref[...]
        mm_parts(alf_ref, lambda z: jnp.log2(lb + (1.0 - lb) * _sigmoid(z)))

    @pl.when((j >= J_BQKV) & (j < J_BLF))
    def _():
        scale = jnp.where(j < J_BQKV + (B_HEADS * B_DK) // TN, B_DK ** -0.5, 1.0)
        store_gla(bqkv_ref, mm() * scale)

    @pl.when((j >= J_BLF) & (j < J_GATES))
    def _():
        z = jnp.dot(r_ref[...], gkw_ref[...], preferred_element_type=F32) + gkb_ref[...]
        ls = jnp.minimum(z, 0.0) - jnp.log(1.0 + jnp.exp(-jnp.abs(z)))
        store_gla(blf_ref, ls * (LOG2E / GATE_NORMALIZER))

    if latent:
        @pl.when(j >= J_GATES)
        def _():
            silu = j < J_GATES + T_GATES // 2
            mm_parts(gates_ref, lambda z: _sigmoid(z) * jnp.where(silu, z, 1.0))


def _inproj(x2d, mod3, mod_row_of_tile, norm_g, w_main, w_r, lb, gk_pad, gkb, *, tm, latent, n_batch=1):
    n_tok, d = x2d.shape
    n_j = J_END if latent else J_GATES
    tile_rows = tm // GRID_W
    seq_rows = n_tok // n_batch // GRID_W
    tiles_per_batch = n_tok // n_batch // tm

    def cl(j, lo, n):
        return jnp.clip(j - lo, 0, n - 1)

    def gla_spec(j_lo, n):
        if not latent:
            return pl.BlockSpec((tm, TN), lambda i, j: (i, cl(j, j_lo, n)))
        return pl.BlockSpec((None, GRID_W, tile_rows, TN),
                            lambda i, j: (i // tiles_per_batch, 0, i % tiles_per_batch, cl(j, j_lo, n)))

    def gla_shape(n, dtype):
        if not latent:
            return jax.ShapeDtypeStruct((n_tok, n * TN), dtype)
        return jax.ShapeDtypeStruct((n_batch, GRID_W, seq_rows, n * TN), dtype)

    in_specs = [
        pl.BlockSpec((tm, d), lambda i, j: (i, 0)),
        pl.BlockSpec((1, 3, d), lambda i, j: (mod_row_of_tile(i), 0, 0)),
        pl.BlockSpec((1, d), lambda i, j: (0, 0)),
        pl.BlockSpec((d, TN), lambda i, j: (0, j - jnp.clip(j - (J_BLF - 1), 0, T_BLF))),
        pl.BlockSpec((d, LANES), lambda i, j: (0, 0)),
        pl.BlockSpec((1, TN), lambda i, j: (0, cl(j, J_ALF, T_ALF))),
        pl.BlockSpec((LANES, TN), lambda i, j: (0, cl(j, J_BLF, T_BLF))),
        pl.BlockSpec((1, TN), lambda i, j: (0, cl(j, J_BLF, T_BLF))),
    ]
    out_specs = [
        pl.BlockSpec((tm, TN), lambda i, j: (i, cl(j, 0, T_AQI))),
        pl.BlockSpec((tm, TN), lambda i, j: (i, cl(j, J_ALF, T_ALF))),
        gla_spec(J_BQKV, T_BQKV),
        gla_spec(J_BLF, T_BLF),
    ]
    out_shape = [
        jax.ShapeDtypeStruct((n_tok, T_AQI * TN), BF16),
        jax.ShapeDtypeStruct((n_tok, T_ALF * TN), F32),
        gla_shape(T_BQKV, BF16),
        gla_shape(T_BLF, F32),
    ]
    if latent:
        out_specs.append(pl.BlockSpec((tm, TN), lambda i, j: (i, cl(j, J_GATES, T_GATES))))
        out_shape.append(jax.ShapeDtypeStruct((n_tok, T_GATES * TN), BF16))
    return pl.pallas_call(
        functools.partial(_inproj_kernel, latent=latent),
        grid=(n_tok // tm, n_j),
        in_specs=in_specs,
        out_specs=out_specs,
        out_shape=out_shape,
        scratch_shapes=[pltpu.VMEM((tm, d), BF16), pltpu.VMEM((tm, LANES), BF16)],
        compiler_params=pltpu.CompilerParams(
            dimension_semantics=("arbitrary", "arbitrary"), vmem_limit_bytes=VMEM_LIMIT_BYTES),
        name="inproj_latent" if latent else "inproj_ctx",
    )(x2d, mod3, norm_g, w_main, w_r, lb, gk_pad, gkb)


def _scan_consts():
    row = lax.broadcasted_iota(jnp.int32, (CHUNK, CHUNK), 0)
    col = lax.broadcasted_iota(jnp.int32, (CHUNK, CHUNK), 1)
    return dict(
        mask_f=col <= row,
        mask_b=col >= row,
        tri_f=jnp.where(col <= row, 1.0, 0.0).astype(BF16),
        tri_b=jnp.where(col >= row, 1.0, 0.0).astype(BF16),
        lane=lax.broadcasted_iota(jnp.int32, (SUBLANES, CHUNK), 1),
    )


DIAG_FACTOR_MAX_LOG2 = 96.0
FOLDED_SUB = 32


def _cumsum_stage(lf, has_k, fwd, cst):
    dk = lf.shape[1]
    tri = cst["tri_f"] if fwd else cst["tri_b"]
    hi = lf.astype(BF16)
    lo = (lf - hi.astype(F32)).astype(BF16)
    cc = jnp.dot(tri, jnp.concatenate([hi, lo], axis=1), preferred_element_type=F32)
    c = cc[:, :dk] + cc[:, dk:]
    a = None if has_k else c - jnp.log2(1.0 - jnp.exp2(lf))
    mag = jnp.abs(lf)
    excess = None
    for r0 in range(0, CHUNK, FOLDED_SUB):
        e = jnp.sum(mag[r0:r0 + FOLDED_SUB], axis=0, keepdims=True)
        excess = e if excess is None else jnp.maximum(excess, e)
    return c, a, excess


def _intra_stage(q, k, v, c, a, st_ref, fwd, cst, keep, exact_diag=True):
    dk = c.shape[1]
    tot = c[CHUNK - 1:CHUNK] if fwd else c[0:1]

    if k is None:
        kf = None

        def kscaled(r0, r1, ref):
            return jnp.exp2(ref - a[r0:r1])
    else:
        a = c
        kf = k.astype(F32)

        def kscaled(r0, r1, ref):
            return kf[r0:r1] * jnp.exp2(ref - c[r0:r1])

    kte = kscaled(0, CHUNK, tot).astype(BF16)
    st = st_ref[...]
    if keep is not None:
        keep.prev_state(st.astype(BF16))
    st_ref[...] = st * jnp.exp2(tot) + lax.dot_general(v, kte, TN_DIMS, preferred_element_type=F32)
    if keep is None:
        return
    yield

    qf = q.astype(F32)
    keep.scaled_q((qf * jnp.exp2(c)).astype(BF16))
    half = SUB // SUBLANES
    sub, own = (SUB, 0) if exact_diag else (FOLDED_SUB, FOLDED_SUB)
    soffs = []
    for r0 in range(0, CHUNK, sub):
        if fwd:
            lo_row, hi_row = 0, r0 + own
            ref = c[r0:r0 + 1]
        else:
            lo_row, hi_row = r0 + sub - own, CHUNK
            ref = c[r0 + sub - 1:r0 + sub]
        if hi_row == lo_row:
            soffs.append(None)
            continue
        pieces = [kscaled(lo_row, hi_row, ref).astype(BF16)]
        if lo_row > 0:
            pieces.insert(0, jnp.zeros((lo_row, dk), BF16))
        if hi_row < CHUNK:
            pieces.append(jnp.zeros((CHUNK - hi_row, dk), BF16))
        kr = jnp.concatenate(pieces, axis=0) if len(pieces) > 1 else pieces[0]
        qr = (qf[r0:r0 + sub] * jnp.exp2(c[r0:r0 + sub] - ref)).astype(BF16)
        soffs.append(lax.dot_general(qr, kr, NT_DIMS, preferred_element_type=F32))
    yield

    if not exact_diag:
        p = jnp.concatenate(soffs, axis=0)
        keep.scores(jnp.where(cst["mask_f"] if fwd else cst["mask_b"], p, 0.0).astype(BF16))
        return

    rows = []
    for r, r0 in enumerate(range(0, CHUNK, SUB)):
        cb = c[r0:r0 + SUB]
        qb = qf[r0:r0 + SUB]
        acc = [jnp.zeros((SUBLANES, CHUNK), F32) for _ in range(half)]
        for jj in range(SUB):
            aj = a[r0 + jj:r0 + jj + 1]
            jg = jj // SUBLANES
            groups = range(jg, half) if fwd else range(0, jg + 1)
            for g in groups:
                e = cb[g * SUBLANES:(g + 1) * SUBLANES] - aj
                if g == jg:
                    e = jnp.minimum(e, 0.0)
                t = qb[g * SUBLANES:(g + 1) * SUBLANES] * jnp.exp2(e)
                if kf is not None:
                    t = t * kf[r0 + jj:r0 + jj + 1]
                sj = jnp.sum(t, axis=1, keepdims=True)
                acc[g] = jnp.where(cst["lane"] == r0 + jj, sj, acc[g])
        d = jnp.concatenate(acc, axis=0)
        rows.append(d if soffs[r] is None else d + soffs[r])
    p = jnp.concatenate(rows, axis=0)
    keep.scores(jnp.where(cst["mask_f"] if fwd else cst["mask_b"], p, 0.0).astype(BF16))


def _run_staged(units):
    live = list(units)
    while live:
        nxt = []
        for u in live:
            try:
                next(u)
                nxt.append(u)
            except StopIteration:
                pass
        live = nxt


def _scan_kernel(*refs, n_heads, dk, dv, has_k, n_ctx_chunks, n_step_chunks, lat_index):
    n_ctx_in = 4 if has_k else 3
    n_lat_in = 4 if has_k else 3
    ctx_refs = refs[:n_ctx_in]
    lat_refs = (refs[n_ctx_in:n_ctx_in + n_lat_in], refs[n_ctx_in + n_lat_in:n_ctx_in + 2 * n_lat_in])
    rest = refs[n_ctx_in + 2 * n_lat_in:]
    out_refs = rest[0:2]
    if has_k:
        st_ref, c_ref, stb_ref = rest[2:]
        a_ref = None
    else:
        st_ref, c_ref, stb_ref, a_ref = rest[2:]
    s = pl.program_id(2)
    cst = _scan_consts()
    units = [(h, d) for h in range(n_heads) for d in range(2)]

    def kcols(h):
        return slice(h * dk, (h + 1) * dk)

    def vcols(h):
        return slice(h * dv, (h + 1) * dv)

    def ctx_step():
        st_ref[...] = jnp.zeros(st_ref.shape, F32)
        if has_k:
            k_ref, v_ref, lff_ref, lfb_ref = ctx_refs
        else:
            v_ref, lff_ref, lfb_ref = ctx_refs
            k_ref = None

        def unit(h, d, rows):
            lf = (lff_ref, lfb_ref)[d][rows, kcols(h)]
            c, a, _ = _cumsum_stage(lf, has_k, d == 0, cst)
            yield
            k = None if k_ref is None else k_ref[rows, kcols(h)]
            yield from _intra_stage(None, k, v_ref[rows, vcols(h)], c, a, st_ref.at[d, h], d == 0, cst, None)

        def body(i, carry):
            rows = (pl.ds(pl.multiple_of(i * CHUNK, CHUNK), CHUNK),
                    pl.ds(pl.multiple_of((n_ctx_chunks - 1 - i) * CHUNK, CHUNK), CHUNK))
            _run_staged([unit(h, d, rows[d]) for h, d in units])
            return carry
        lax.fori_loop(0, n_ctx_chunks, body, 0)

    def latent_step():
        n = n_step_chunks

        def index(d, pos):
            return lat_index(pos if d == 0 else n - 1 - pos)

        def read(d, pos, h, what):
            refs_d = lat_refs[d]
            ref = refs_d[{"q": 0, "k": 1, "v": 2 if has_k else 1, "lf": 3 if has_k else 2}[what]]
            cols = vcols(h) if what == "v" else kcols(h)
            return ref[index(d, pos) + (cols,)]

        def cumsum_all(pos, slot):
            worst = None
            for u, (h, d) in enumerate(units):
                c, a, excess = _cumsum_stage(read(d, pos, h, "lf"), has_k, d == 0, cst)
                c_ref[slot, u] = c
                if a is not None:
                    a_ref[slot, u] = a
                worst = excess if worst is None else jnp.maximum(worst, excess)
            return jnp.max(worst)

        def output_all(pos, carried):
            for u, (h, d) in enumerate(units):
                p, qd = carried[u]
                o = jnp.dot(p, read(d, pos, h, "v"), preferred_element_type=F32)
                o = o + lax.dot_general(qd, stb_ref[u], NT_DIMS, preferred_element_type=F32)
                out_refs[d][index(d, pos) + (vcols(h),)] = o.astype(out_refs[d].dtype)

        class Keeper:
            def __init__(self, u):
                self.u = u
                self.p = self.qd = None

            def prev_state(self, stb):
                stb_ref[self.u] = stb

            def scaled_q(self, qd):
                self.qd = qd

            def scores(self, p):
                self.p = p

        def intra_all(pos, cs, exact_diag):
            keepers = [Keeper(u) for u in range(len(units))]
            gens = []
            for u, (h, d) in enumerate(units):
                k = read(d, pos, h, "k") if has_k else None
                c, a = cs[u]
                gens.append(_intra_stage(read(d, pos, h, "q"), k, read(d, pos, h, "v"), c, a,
                                         st_ref.at[d, h], d == 0, cst, keepers[u], exact_diag))
            _run_staged(gens)
            return tuple((kp.p, kp.qd) for kp in keepers)

        def load_cumsums(slot):
            return [(c_ref[slot, u], None if has_k else a_ref[slot, u]) for u in range(len(units))]

        def either_path(worst, build):
            return lax.cond(worst <= DIAG_FACTOR_MAX_LOG2,
                            functools.partial(build, False), functools.partial(build, True))

        worst0 = cumsum_all(0, 0)
        cs0 = load_cumsums(0)
        worst1 = cumsum_all(1, 1)
        carried0 = either_path(worst0, lambda exact_diag: intra_all(0, cs0, exact_diag))

        def body(i, state):
            worst, carried = state
            slot = lax.rem(i, 2)

            def iteration(exact_diag):
                cs = load_cumsums(slot)
                output_all(i - 1, carried)
                worst_next = cumsum_all(jnp.minimum(i + 1, n - 1), 1 - slot)
                return worst_next, intra_all(i, cs, exact_diag)
            return either_path(worst, iteration)
        _, carried_last = lax.fori_loop(1, n, body, (worst1, carried0))
        output_all(n - 1, carried_last)

    @pl.when(s == 0)
    def _():
        ctx_step()

    @pl.when(s > 0)
    def _():
        latent_step()


def _scan_call(inputs, in_specs, out_struct, out_specs, grid, *, n_heads, dk, dv, has_k,
               n_ctx_chunks, n_step_chunks, lat_index, name):
    n_units = 2 * n_heads
    scratch = [pltpu.VMEM((2, n_heads, dv, dk), F32),
               pltpu.VMEM((2, n_units, CHUNK, dk), F32),
               pltpu.VMEM((n_units, dv, dk), BF16)]
    if not has_k:
        scratch.append(pltpu.VMEM((2, n_units, CHUNK, dk), F32))
    return pl.pallas_call(
        functools.partial(_scan_kernel, n_heads=n_heads, dk=dk, dv=dv, has_k=has_k,
                          n_ctx_chunks=n_ctx_chunks, n_step_chunks=n_step_chunks, lat_index=lat_index),
        grid=grid,
        in_specs=in_specs,
        out_specs=out_specs,
        out_shape=out_struct,
        scratch_shapes=scratch,
        compiler_params=pltpu.CompilerParams(
            dimension_semantics=("arbitrary", "arbitrary", "arbitrary"),
            vmem_limit_bytes=VMEM_LIMIT_BYTES),
        name=name,
    )(*inputs)


A_SCAN_HEADS = 2
A_SCAN_CHUNKS = 8


def _scan_a(c_qi, c_lf, qi, lf):
    bsz, n_ctx, _ = c_qi.shape
    seq = qi.shape[1]
    g = A_SCAN_HEADS
    gw = g * A_DK
    n_hg = A_HEADS // g
    ts = A_SCAN_CHUNKS * CHUNK
    n_steps = seq // ts

    def fstep(s):
        return jnp.maximum(s - 1, 0)

    def bstep(s):
        return n_steps - 1 - jnp.maximum(s - 1, 0)

    cspec = lambda off: pl.BlockSpec((None, n_ctx, gw), lambda b, h, s: (b, 0, off + h))
    fspec = lambda off: pl.BlockSpec((None, ts, gw), lambda b, h, s: (b, fstep(s), off + h))
    bspec = lambda off: pl.BlockSpec((None, ts, gw), lambda b, h, s: (b, bstep(s), off + h))
    in_specs = [cspec(n_hg), cspec(0), cspec(n_hg),
                fspec(0), fspec(n_hg), fspec(0),
                bspec(0), bspec(n_hg), bspec(n_hg)]
    inputs = [c_qi, c_lf, c_lf, qi, qi, lf, qi, qi, lf]
    out_struct = [jax.ShapeDtypeStruct((bsz, seq, A_HEADS * A_DV), BF16)] * 2
    out_specs = [pl.BlockSpec((None, ts, gw), lambda b, h, s: (b, fstep(s), h)),
                 pl.BlockSpec((None, ts, gw), lambda b, h, s: (b, bstep(s), h))]

    def lat_index(chunk):
        row = chunk * CHUNK
        if not isinstance(row, int):
            row = pl.multiple_of(row, CHUNK)
        return (pl.ds(row, CHUNK),)

    return _scan_call(inputs, in_specs, out_struct, out_specs, (bsz, n_hg, n_steps + 1),
                      n_heads=g, dk=A_DK, dv=A_DV, has_k=False,
                      n_ctx_chunks=n_ctx // CHUNK, n_step_chunks=A_SCAN_CHUNKS,
                      lat_index=lat_index, name="scan_hgrn2")


B_SCAN_COLS = 4


def _scan_b(c_qkv, c_lf, qkv, lf):
    bsz, n_ctx, _ = c_qkv.shape
    n_rows = qkv.shape[2]
    kw = B_HEADS * B_DK
    nc = B_SCAN_COLS
    n_steps = GRID_W // nc
    chunks_per_col = n_rows // CHUNK

    def fblk(s):
        return jnp.maximum(s - 1, 0)

    def bblk(s):
        return n_steps - 1 - jnp.maximum(s - 1, 0)

    def cspec(width, off):
        return pl.BlockSpec((None, n_ctx, width), lambda b, h, s: (b, 0, off + h))

    def lspec(width, off, blk):
        return pl.BlockSpec((None, nc, n_rows, width), lambda b, h, s: (b, blk(s), 0, off + h))

    k_off, v_off = kw // B_DK, (2 * kw) // B_DV
    in_specs = [cspec(B_DK, k_off), cspec(B_DV, v_off), cspec(B_DK, 0), cspec(B_DK, B_HEADS),
                lspec(B_DK, 0, fblk), lspec(B_DK, k_off, fblk), lspec(B_DV, v_off, fblk), lspec(B_DK, 0, fblk),
                lspec(B_DK, 0, bblk), lspec(B_DK, k_off, bblk), lspec(B_DV, v_off, bblk),
                lspec(B_DK, B_HEADS, bblk)]
    inputs = [c_qkv, c_qkv, c_lf, c_lf, qkv, qkv, qkv, lf, qkv, qkv, qkv, lf]
    vw = B_HEADS * B_DV
    out_struct = [jax.ShapeDtypeStruct((bsz, GRID_W, n_rows, vw), BF16)] * 2
    out_specs = [pl.BlockSpec((None, nc, n_rows, B_DV), lambda b, h, s: (b, fblk(s), 0, h)),
                 pl.BlockSpec((None, nc, n_rows, B_DV), lambda b, h, s: (b, bblk(s), 0, h))]

    def lat_index(chunk):
        col = chunk // chunks_per_col
        row = (chunk - col * chunks_per_col) * CHUNK
        if not isinstance(row, int):
            row = pl.multiple_of(row, CHUNK)
        return (col, pl.ds(row, CHUNK))

    return _scan_call(inputs, in_specs, out_struct, out_specs, (bsz, B_HEADS, n_steps + 1),
                      n_heads=1, dk=B_DK, dv=B_DV, has_k=True,
                      n_ctx_chunks=n_ctx // CHUNK, n_step_chunks=nc * chunks_per_col,
                      lat_index=lat_index, name="scan_gla")


GLA_FIN_ROWS = 16


def _gla_finalize_kernel(of_ref, ob_ref, sg_ref, g_ref, o_ref):
    n_cols, n_rows, w = of_ref.shape
    o = of_ref[...].astype(F32) + ob_ref[...].astype(F32)
    ms = jnp.mean(o * o, axis=-1, keepdims=True)
    on = o * lax.rsqrt(ms + EPS) * g_ref[...]
    on = jnp.swapaxes(on, 0, 1).reshape(n_rows * n_cols, w)
    o_ref[...] = (on * sg_ref[...].astype(F32)).astype(o_ref.dtype)


def _gla_finalize(of, ob, gates, gain, *, sg_col_off):
    bsz, _, n_rows, vw = of.shape
    tm = GLA_FIN_ROWS * GRID_W
    tiles = n_rows // GLA_FIN_ROWS
    cm = pl.BlockSpec((None, GRID_W, GLA_FIN_ROWS, B_DV), lambda b, r, h: (b, 0, r, h))
    return pl.pallas_call(
        _gla_finalize_kernel,
        grid=(bsz, tiles, B_HEADS),
        in_specs=[cm, cm,
                  pl.BlockSpec((tm, B_DV), lambda b, r, h: (b * tiles + r, sg_col_off + h)),
                  pl.BlockSpec((1, B_DV), lambda b, r, h: (0, h))],
        out_specs=pl.BlockSpec((tm, B_DV), lambda b, r, h: (b * tiles + r, h)),
        out_shape=jax.ShapeDtypeStruct((bsz * n_rows * GRID_W, vw), BF16),
        compiler_params=pltpu.CompilerParams(
            dimension_semantics=("arbitrary", "arbitrary", "arbitrary"),
            vmem_limit_bytes=32 * 1024 * 1024),
        name="gla_finalize",
    )(of, ob, gates, gain)


def _merge_kernel(oaf, oab, obp, sga, sma, smb, ga, wpa, wpb, y_ref):
    o = oaf[...].astype(F32) + oab[...].astype(F32)
    parts = []
    for h in range(A_HEADS):
        oh = o[:, h * A_DV:(h + 1) * A_DV]
        ms = jnp.mean(oh * oh, axis=-1, keepdims=True)
        parts.append(oh * lax.rsqrt(ms + EPS))
    oa = (jnp.concatenate(parts, axis=1) * ga[...] * sga[...].astype(F32)).astype(BF16)
    ya = jnp.dot(oa, wpa[...], preferred_element_type=F32)
    yb = jnp.dot(obp[...], wpb[...], preferred_element_type=F32)
    y_ref[...] = (sma[...].astype(F32) * ya + smb[...].astype(F32) * yb).astype(BF16)


def _merge(oaf, oab, obp, gates, ga, wpa, wpb, *, tm):
    n_tok, d = oaf.shape
    tok = lambda off: pl.BlockSpec((tm, d), lambda i: (i, off))
    const = lambda shape: pl.BlockSpec(shape, lambda i: (0, 0), pipeline_mode=pl.Buffered(1))
    return pl.pallas_call(
        _merge_kernel,
        grid=(n_tok // tm,),
        in_specs=[tok(0), tok(0), tok(0), tok(0), tok(2), tok(3),
                  const((1, d)), const(wpa.shape), const(wpb.shape)],
        out_specs=pl.BlockSpec((tm, wpa.shape[1]), lambda i: (i, 0)),
        out_shape=jax.ShapeDtypeStruct((n_tok, wpa.shape[1]), BF16),
        compiler_params=pltpu.CompilerParams(
            dimension_semantics=("arbitrary",), vmem_limit_bytes=VMEM_LIMIT_BYTES),
        name="merge_proj",
    )(oaf, oab, obp, gates, gates, gates, ga, wpa, wpb)


def _final_kernel(y_ref, x_ref, mod_ref, w_ref, g_ref, o_ref):
    yo = jnp.dot(y_ref[...], w_ref[...], preferred_element_type=F32)
    z = x_ref[...] + mod_ref[0, 2:3, :] * yo
    ms = jnp.mean(z * z, axis=-1, keepdims=True)
    o_ref[...] = z * lax.rsqrt(ms + EPS) * g_ref[...]


def _final(y, x2d, mod3, w_out, fg, *, tm, tiles_per_batch):
    n_tok, d = x2d.shape
    const = lambda shape: pl.BlockSpec(shape, lambda i: (0, 0), pipeline_mode=pl.Buffered(1))
    return pl.pallas_call(
        _final_kernel,
        grid=(n_tok // tm,),
        in_specs=[pl.BlockSpec((tm, d), lambda i: (i, 0)),
                  pl.BlockSpec((tm, d), lambda i: (i, 0)),
                  pl.BlockSpec((1, 3, d), lambda i: (i // tiles_per_batch, 0, 0)),
                  const(w_out.shape), const((1, d))],
        out_specs=pl.BlockSpec((tm, d), lambda i: (i, 0)),
        out_shape=jax.ShapeDtypeStruct((n_tok, d), F32),
        compiler_params=pltpu.CompilerParams(
            dimension_semantics=("arbitrary",), vmem_limit_bytes=VMEM_LIMIT_BYTES),
        name="out_proj_final",
    )(y, x2d, mod3, w_out, fg)


def kernel(x, c, ctx, c_ctx, w_ada, b_ada, norm_g, w_in, hgrn_lb_logits, gla_w_gk, gla_b_gk,
           hgrn_onorm_g, gla_onorm_g, w_pa, w_pb, w_out, final_norm_g):
    bsz, seq, d = x.shape
    n_ctx = ctx.shape[1]
    depth = w_in.shape[0]
    assert depth == 1, "single-layer trunk"
    a_kw, a_vw = A_HEADS * A_DK, A_HEADS * A_DV
    b_kw, b_vw = B_HEADS * B_DK, B_HEADS * B_DV

    n_rows = -(-(bsz + 1) // SUBLANES) * SUBLANES
    cvec = jnp.zeros((n_rows, d), F32).at[:bsz].set(c).at[bsz].set(c_ctx)
    mod = _adaln(cvec, w_ada[0], b_ada[0].reshape(1, -1))
    mod3 = mod.reshape(n_rows, 3, d)

    lb = _lower_bounds(hgrn_lb_logits)[0:1]

    o_ag = 2 * a_kw + 2 * a_vw
    o_bq = o_ag + a_vw
    o_br = o_bq + 2 * b_kw + b_vw
    w_main, w_r = _prep_weights(
        jnp.swapaxes(w_in[0], 0, 1), [(0, o_ag), (o_bq, o_br - o_bq), (o_ag, o_bq - o_ag)],
        tail_start=o_br)
    gk_pad = jnp.zeros((LANES, 2 * b_kw), F32)
    gk_pad = gk_pad.at[0:B_RANK, 0:b_kw].set(gla_w_gk[0, 0]).at[B_RANK:2 * B_RANK, b_kw:].set(gla_w_gk[0, 1])
    gk_pad = gk_pad.astype(BF16)
    gkb = gla_b_gk[0].reshape(1, 2 * b_kw)
    ng = norm_g[0].reshape(1, d)

    x2d = x.reshape(bsz * seq, d)
    ctx2d = ctx.reshape(bsz * n_ctx, d)
    tm = 1024
    tiles_per_batch = seq // tm

    c_aqi, c_alf, c_bqkv, c_blf = _inproj(
        ctx2d, mod3, lambda i: bsz, ng, w_main, w_r, lb, gk_pad, gkb,
        tm=bsz * n_ctx, latent=False)
    aqi, alf, bqkv, blf, gates = _inproj(
        x2d, mod3, lambda i: i // tiles_per_batch, ng, w_main, w_r, lb, gk_pad, gkb,
        tm=tm, latent=True, n_batch=bsz)

    r3 = lambda t, n: t.reshape(bsz, n, t.shape[-1])
    oaf, oab = _scan_a(r3(c_aqi, n_ctx), r3(c_alf, n_ctx), r3(aqi, seq), r3(alf, seq))
    obf, obb = _scan_b(r3(c_bqkv, n_ctx), r3(c_blf, n_ctx), bqkv, blf)
    obp = _gla_finalize(obf, obb, gates, gla_onorm_g[0].reshape(1, -1), sg_col_off=a_vw // B_DV)

    r2 = lambda t: t.reshape(bsz * seq, t.shape[-1])
    y = _merge(r2(oaf), r2(oab), obp, gates, hgrn_onorm_g[0].reshape(1, -1),
               w_pa[0].astype(BF16), w_pb[0].astype(BF16), tm=256)
    out = _final(y, x2d, mod3, w_out[0].astype(BF16), final_norm_g.reshape(1, d),
                 tm=512, tiles_per_batch=seq // 512)
    return out.reshape(bsz, seq, d)
```

```python
import functools

import jax
import jax.numpy as jnp
from jax import lax
from jax.experimental import pallas as pl
from jax.experimental.pallas import tpu as pltpu

F32 = jnp.float32
BF16 = jnp.bfloat16

CHUNK = 64
SUB = 16
GRID_W = 64
EPS = 1e-6
A_HEADS, A_DK, A_DV = 16, 128, 128
B_HEADS, B_DK, B_DV = 4, 256, 512
B_RANK = 16
GATE_NORMALIZER = 16.0
LOG2E = 1.4426950408889634

VMEM_LIMIT_BYTES = 56 * 1024 * 1024
LANES = 128
SUBLANES = 8

NT_DIMS = (((1,), (1,)), ((), ()))
TN_DIMS = (((0,), (0,)), ((), ()))


def _sigmoid(z):
    return 1.0 / (1.0 + jnp.exp(-z))


def _adaln_kernel(c_ref, w_ref, b_ref, o_ref):
    c = c_ref[...]
    s = c * _sigmoid(c)
    o_ref[...] = jnp.dot(s, w_ref[...], preferred_element_type=F32,
                         precision=lax.Precision.HIGHEST) + b_ref[...]


def _adaln(cvec, w, b):
    rows, d = cvec.shape
    n = w.shape[1]
    tn = 768
    return pl.pallas_call(
        _adaln_kernel,
        grid=(n // tn,),
        in_specs=[pl.BlockSpec((rows, d), lambda j: (0, 0)),
                  pl.BlockSpec((d, tn), lambda j: (0, j)),
                  pl.BlockSpec((1, tn), lambda j: (0, j))],
        out_specs=pl.BlockSpec((rows, tn), lambda j: (0, j)),
        out_shape=jax.ShapeDtypeStruct((rows, n), F32),
        compiler_params=pltpu.CompilerParams(
            dimension_semantics=("arbitrary",), vmem_limit_bytes=32 * 1024 * 1024),
        name="adaln",
    )(cvec, w, b)


def _lower_bound_kernel(l_ref, o_ref):
    x = l_ref[...]
    n_rows = x.shape[0]
    m = jnp.max(x, axis=0, keepdims=True)
    e = jnp.exp(x - m)
    tot = jnp.sum(e, axis=0, keepdims=True)
    run = jnp.zeros_like(tot)
    for r in range(n_rows - 1):
        run = run + e[r:r + 1]
        o_ref[r:r + 1, :] = run / tot


def _lower_bounds(logits):
    n_rows, w = logits.shape
    return pl.pallas_call(
        _lower_bound_kernel,
        out_shape=jax.ShapeDtypeStruct((n_rows - 1, w), F32),
        name="hgrn_lower_bounds",
    )(logits)


def _prep_weights_kernel(wa_ref, wb_ref, o_ref, r_ref, *, j_tail, shift):
    j = pl.program_id(0)

    @pl.when(j < j_tail)
    def _():
        o_ref[...] = wa_ref[...].T.astype(BF16)

    @pl.when(j >= j_tail)
    def _():
        o_ref[...] = jnp.concatenate([wa_ref[shift:, :], wb_ref[:shift, :]], axis=0).T.astype(BF16)

    @pl.when(j == j_tail)
    def _():
        rank_rows = jnp.concatenate(
            [wa_ref[:shift, :], jnp.zeros((LANES - shift, wa_ref.shape[1]), F32)], axis=0)
        r_ref[...] = rank_rows.T.astype(BF16)


def _prep_weights(wt, group_starts, tail_start):
    n_in, d = wt.shape
    shift = 2 * B_RANK
    src_tiles = []
    for start, width in group_starts:
        assert start % TN == 0 and width % TN == 0
        src_tiles += list(range(start // TN, (start + width) // TN))
    assert tail_start % TN == 0
    j_tail = len(src_tiles)
    n_tail = (n_in - tail_start - shift) // TN
    n_out = j_tail + n_tail
    last_src = (n_in - 1) // TN

    def src_a(j):
        idx = j - j_tail + tail_start // TN
        for dst in range(j_tail - 1, -1, -1):
            idx = jnp.where(j == dst, src_tiles[dst], idx)
        return idx

    return pl.pallas_call(
        functools.partial(_prep_weights_kernel, j_tail=j_tail, shift=shift),
        grid=(n_out,),
        in_specs=[pl.BlockSpec((TN, d), lambda j: (src_a(j), 0)),
                  pl.BlockSpec((TN, d), lambda j: (jnp.minimum(src_a(j) + 1, last_src), 0))],
        out_specs=[pl.BlockSpec((d, TN), lambda j: (0, j)),
                   pl.BlockSpec((d, LANES), lambda j: (0, 0))],
        out_shape=[jax.ShapeDtypeStruct((d, n_out * TN), BF16),
                   jax.ShapeDtypeStruct((d, LANES), BF16)],
        compiler_params=pltpu.CompilerParams(
            dimension_semantics=("arbitrary",), vmem_limit_bytes=VMEM_LIMIT_BYTES),
        name="prep_weights",
    )(wt, wt)


TN = 1024
NORM_ROWS = 128
EPILOGUE_PARTS = 8
T_AQI = 4096 // TN
T_ALF = 4096 // TN
T_BQKV = 4096 // TN
T_BLF = 2048 // TN
T_GATES = 8192 // TN
J_ALF = T_AQI
J_BQKV = J_ALF + T_ALF
J_BLF = J_BQKV + T_BQKV
J_GATES = J_BLF + T_BLF
J_END = J_GATES + T_GATES


def _inproj_kernel(x_ref, mod_ref, ng_ref, w_ref, wr_ref, lb_ref, gkw_ref, gkb_ref, *rest, latent):
    if latent:
        aqi_ref, alf_ref, bqkv_ref, blf_ref, gates_ref, h_ref, r_ref = rest
    else:
        aqi_ref, alf_ref, bqkv_ref, blf_ref, h_ref, r_ref = rest
        gates_ref = None
    j = pl.program_id(1)

    def store_gla(out_ref, z):
        if not latent:
            out_ref[...] = z.astype(out_ref.dtype)
            return
        n_rows = z.shape[0] // GRID_W
        zc = jnp.swapaxes(z.reshape(n_rows, GRID_W, z.shape[1]), 0, 1)
        out_ref[...] = zc.astype(out_ref.dtype)

    @pl.when(j == 0)
    def _():
        def norm_rows(t, carry):
            rows = pl.ds(pl.multiple_of(t * NORM_ROWS, NORM_ROWS), NORM_ROWS)
            x = x_ref[rows, :]
            ms = jnp.mean(x * x, axis=-1, keepdims=True)
            y = x * lax.rsqrt(ms + EPS) * ng_ref[...]
            h = (y * (1.0 + mod_ref[0, 1:2, :]) + mod_ref[0, 0:1, :]).astype(BF16)
            h_ref[rows, :] = h
            r_ref[rows, :] = jnp.dot(h, wr_ref[...], preferred_element_type=F32).astype(BF16)
            return carry
        lax.fori_loop(0, x_ref.shape[0] // NORM_ROWS, norm_rows, 0)

    def mm():
        return jnp.dot(h_ref[...], w_ref[...], preferred_element_type=F32)

    def mm_parts(out_ref, epilogue):
        part = h_ref.shape[0] // EPILOGUE_PARTS
        for m in range(EPILOGUE_PARTS):
            rows = slice(m * part, (m + 1) * part)
            z = jnp.dot(h_ref[rows, :], w_ref[...], preferred_element_type=F32)
            out_ref[rows, :] = epilogue(z).astype(out_ref.dtype)

    @pl.when(j < J_ALF)
    def _():
        mm_parts(aqi_ref, lambda z: z)

    @pl.when((j >= J_ALF) & (j < J_BQKV))
    def _():
        lb = lb_ref[...]
        mm_parts(alf_ref, lambda z: jnp.log2(lb + (1.0 - lb) * _sigmoid(z)))

    @pl.when((j >= J_BQKV) & (j < J_BLF))
    def _():
        scale = jnp.where(j < J_BQKV + (B_HEADS * B_DK) // TN, B_DK ** -0.5, 1.0)
        store_gla(bqkv_ref, mm() * scale)

    @pl.when((j >= J_BLF) & (j < J_GATES))
    def _():
        z = jnp.dot(r_ref[...], gkw_ref[...], preferred_element_type=F32) + gkb_ref[...]
        ls = jnp.minimum(z, 0.0) - jnp.log(1.0 + jnp.exp(-jnp.abs(z)))
        store_gla(blf_ref, ls * (LOG2E / GATE_NORMALIZER))

    if latent:
        @pl.when(j >= J_GATES)
        def _():
            silu = j < J_GATES + T_GATES // 2
            mm_parts(gates_ref, lambda z: _sigmoid(z) * jnp.where(silu, z, 1.0))


def _inproj(x2d, mod3, mod_row_of_tile, norm_g, w_main, w_r, lb, gk_pad, gkb, *, tm, latent, n_batch=1):
    n_tok, d = x2d.shape
    n_j = J_END if latent else J_GATES
    tile_rows = tm // GRID_W
    seq_rows = n_tok // n_batch // GRID_W
    tiles_per_batch = n_tok // n_batch // tm

    def cl(j, lo, n):
        return jnp.clip(j - lo, 0, n - 1)

    def gla_spec(j_lo, n):
        if not latent:
            return pl.BlockSpec((tm, TN), lambda i, j: (i, cl(j, j_lo, n)))
        return pl.BlockSpec((None, GRID_W, tile_rows, TN),
                            lambda i, j: (i // tiles_per_batch, 0, i % tiles_per_batch, cl(j, j_lo, n)))

    def gla_shape(n, dtype):
        if not latent:
            return jax.ShapeDtypeStruct((n_tok, n * TN), dtype)
        return jax.ShapeDtypeStruct((n_batch, GRID_W, seq_rows, n * TN), dtype)

    in_specs = [
        pl.BlockSpec((tm, d), lambda i, j: (i, 0), pipeline_mode=pl.Buffered(1)),
        pl.BlockSpec((1, 3, d), lambda i, j: (mod_row_of_tile(i), 0, 0)),
        pl.BlockSpec((1, d), lambda i, j: (0, 0)),
        pl.BlockSpec((d, TN), lambda i, j: (0, j - jnp.clip(j - (J_BLF - 1), 0, T_BLF))),
        pl.BlockSpec((d, LANES), lambda i, j: (0, 0)),
        pl.BlockSpec((1, TN), lambda i, j: (0, cl(j, J_ALF, T_ALF))),
        pl.BlockSpec((LANES, TN), lambda i, j: (0, cl(j, J_BLF, T_BLF))),
        pl.BlockSpec((1, TN), lambda i, j: (0, cl(j, J_BLF, T_BLF))),
    ]
    out_specs = [
        pl.BlockSpec((tm, TN), lambda i, j: (i, cl(j, 0, T_AQI))),
        pl.BlockSpec((tm, TN), lambda i, j: (i, cl(j, J_ALF, T_ALF))),
        gla_spec(J_BQKV, T_BQKV),
        gla_spec(J_BLF, T_BLF),
    ]
    out_shape = [
        jax.ShapeDtypeStruct((n_tok, T_AQI * TN), BF16),
        jax.ShapeDtypeStruct((n_tok, T_ALF * TN), F32),
        gla_shape(T_BQKV, BF16),
        gla_shape(T_BLF, F32),
    ]
    if latent:
        out_specs.append(pl.BlockSpec((tm, TN), lambda i, j: (i, cl(j, J_GATES, T_GATES))))
        out_shape.append(jax.ShapeDtypeStruct((n_tok, T_GATES * TN), BF16))
    return pl.pallas_call(
        functools.partial(_inproj_kernel, latent=latent),
        grid=(n_tok // tm, n_j),
        in_specs=in_specs,
        out_specs=out_specs,
        out_shape=out_shape,
        scratch_shapes=[pltpu.VMEM((tm, d), BF16), pltpu.VMEM((tm, LANES), BF16)],
        compiler_params=pltpu.CompilerParams(
            dimension_semantics=("arbitrary", "arbitrary"), vmem_limit_bytes=VMEM_LIMIT_BYTES),
        name="inproj_latent" if latent else "inproj_ctx",
    )(x2d, mod3, norm_g, w_main, w_r, lb, gk_pad, gkb)


def _scan_consts():
    row = lax.broadcasted_iota(jnp.int32, (CHUNK, CHUNK), 0)
    col = lax.broadcasted_iota(jnp.int32, (CHUNK, CHUNK), 1)
    return dict(
        mask_f=col <= row,
        mask_b=col >= row,
        tri_f=jnp.where(col <= row, 1.0, 0.0).astype(BF16),
        tri_b=jnp.where(col >= row, 1.0, 0.0).astype(BF16),
        lane=lax.broadcasted_iota(jnp.int32, (SUBLANES, CHUNK), 1),
    )


DIAG_FACTOR_MAX_LOG2 = 96.0
FOLDED_SUB = 32


def _cumsum_stage(lf, has_k, fwd, cst):
    dk = lf.shape[1]
    tri = cst["tri_f"] if fwd else cst["tri_b"]
    hi = lf.astype(BF16)
    lo = (lf - hi.astype(F32)).astype(BF16)
    cc = jnp.dot(tri, jnp.concatenate([hi, lo], axis=1), preferred_element_type=F32)
    c = cc[:, :dk] + cc[:, dk:]
    a = None if has_k else c - jnp.log2(1.0 - jnp.exp2(lf))
    mag = jnp.abs(lf)
    excess = None
    for r0 in range(0, CHUNK, FOLDED_SUB):
        e = jnp.sum(mag[r0:r0 + FOLDED_SUB], axis=0, keepdims=True)
        excess = e if excess is None else jnp.maximum(excess, e)
    return c, a, excess


def _intra_stage(q, k, v, c, a, st_ref, fwd, cst, keep, exact_diag=True):
    dk = c.shape[1]
    tot = c[CHUNK - 1:CHUNK] if fwd else c[0:1]

    if k is None:
        kf = None

        def kscaled(r0, r1, ref):
            return jnp.exp2(ref - a[r0:r1])
    else:
        a = c
        kf = k.astype(F32)

        def kscaled(r0, r1, ref):
            return kf[r0:r1] * jnp.exp2(ref - c[r0:r1])

    kte = kscaled(0, CHUNK, tot).astype(BF16)
    st = st_ref[...]
    if keep is not None:
        keep.prev_state(st.astype(BF16))
    st_ref[...] = st * jnp.exp2(tot) + lax.dot_general(v, kte, TN_DIMS, preferred_element_type=F32)
    if keep is None:
        return
    yield

    qf = q.astype(F32)
    keep.scaled_q((qf * jnp.exp2(c)).astype(BF16))
    half = SUB // SUBLANES
    sub, own = (SUB, 0) if exact_diag else (FOLDED_SUB, FOLDED_SUB)
    soffs = []
    for r0 in range(0, CHUNK, sub):
        if fwd:
            lo_row, hi_row = 0, r0 + own
            ref = c[r0:r0 + 1]
        else:
            lo_row, hi_row = r0 + sub - own, CHUNK
            ref = c[r0 + sub - 1:r0 + sub]
        if hi_row == lo_row:
            soffs.append(None)
            continue
        pieces = [kscaled(lo_row, hi_row, ref).astype(BF16)]
        if lo_row > 0:
            pieces.insert(0, jnp.zeros((lo_row, dk), BF16))
        if hi_row < CHUNK:
            pieces.append(jnp.zeros((CHUNK - hi_row, dk), BF16))
        kr = jnp.concatenate(pieces, axis=0) if len(pieces) > 1 else pieces[0]
        qr = (qf[r0:r0 + sub] * jnp.exp2(c[r0:r0 + sub] - ref)).astype(BF16)
        soffs.append(lax.dot_general(qr, kr, NT_DIMS, preferred_element_type=F32))
    yield

    if not exact_diag:
        p = jnp.concatenate(soffs, axis=0)
        keep.scores(jnp.where(cst["mask_f"] if fwd else cst["mask_b"], p, 0.0).astype(BF16))
        return

    rows = []
    for r, r0 in enumerate(range(0, CHUNK, SUB)):
        cb = c[r0:r0 + SUB]
        qb = qf[r0:r0 + SUB]
        acc = [jnp.zeros((SUBLANES, CHUNK), F32) for _ in range(half)]
        for jj in range(SUB):
            aj = a[r0 + jj:r0 + jj + 1]
            jg = jj // SUBLANES
            groups = range(jg, half) if fwd else range(0, jg + 1)
            for g in groups:
                e = cb[g * SUBLANES:(g + 1) * SUBLANES] - aj
                if g == jg:
                    e = jnp.minimum(e, 0.0)
                t = qb[g * SUBLANES:(g + 1) * SUBLANES] * jnp.exp2(e)
                if kf is not None:
                    t = t * kf[r0 + jj:r0 + jj + 1]
                sj = jnp.sum(t, axis=1, keepdims=True)
                acc[g] = jnp.where(cst["lane"] == r0 + jj, sj, acc[g])
        d = jnp.concatenate(acc, axis=0)
        rows.append(d if soffs[r] is None else d + soffs[r])
    p = jnp.concatenate(rows, axis=0)
    keep.scores(jnp.where(cst["mask_f"] if fwd else cst["mask_b"], p, 0.0).astype(BF16))


def _run_staged(units):
    live = list(units)
    while live:
        nxt = []
        for u in live:
            try:
                next(u)
                nxt.append(u)
            except StopIteration:
                pass
        live = nxt


def _scan_kernel(*refs, n_heads, dk, dv, has_k, n_ctx_chunks, n_step_chunks, lat_index):
    n_ctx_in = 4 if has_k else 3
    n_lat_in = 4 if has_k else 3
    ctx_refs = refs[:n_ctx_in]
    lat_refs = (refs[n_ctx_in:n_ctx_in + n_lat_in], refs[n_ctx_in + n_lat_in:n_ctx_in + 2 * n_lat_in])
    rest = refs[n_ctx_in + 2 * n_lat_in:]
    out_refs = rest[0:2]
    if has_k:
        st_ref, c_ref, stb_ref = rest[2:]
        a_ref = None
    else:
        st_ref, c_ref, stb_ref, a_ref = rest[2:]
    s = pl.program_id(2)
    cst = _scan_consts()
    units = [(h, d) for h in range(n_heads) for d in range(2)]

    def kcols(h):
        return slice(h * dk, (h + 1) * dk)

    def vcols(h):
        return slice(h * dv, (h + 1) * dv)

    def ctx_step():
        st_ref[...] = jnp.zeros(st_ref.shape, F32)
        if has_k:
            k_ref, v_ref, lff_ref, lfb_ref = ctx_refs
        else:
            v_ref, lff_ref, lfb_ref = ctx_refs
            k_ref = None

        def unit(h, d, rows):
            lf = (lff_ref, lfb_ref)[d][rows, kcols(h)]
            c, a, _ = _cumsum_stage(lf, has_k, d == 0, cst)
            yield
            k = None if k_ref is None else k_ref[rows, kcols(h)]
            yield from _intra_stage(None, k, v_ref[rows, vcols(h)], c, a, st_ref.at[d, h], d == 0, cst, None)

        def body(i, carry):
            rows = (pl.ds(pl.multiple_of(i * CHUNK, CHUNK), CHUNK),
                    pl.ds(pl.multiple_of((n_ctx_chunks - 1 - i) * CHUNK, CHUNK), CHUNK))
            _run_staged([unit(h, d, rows[d]) for h, d in units])
            return carry
        lax.fori_loop(0, n_ctx_chunks, body, 0)

    def latent_step():
        n = n_step_chunks

        def index(d, pos):
            return lat_index(pos if d == 0 else n - 1 - pos)

        def read(d, pos, h, what):
            refs_d = lat_refs[d]
            ref = refs_d[{"q": 0, "k": 1, "v": 2 if has_k else 1, "lf": 3 if has_k else 2}[what]]
            cols = vcols(h) if what == "v" else kcols(h)
            return ref[index(d, pos) + (cols,)]

        def cumsum_all(pos, slot):
            worst = None
            for u, (h, d) in enumerate(units):
                c, a, excess = _cumsum_stage(read(d, pos, h, "lf"), has_k, d == 0, cst)
                c_ref[slot, u] = c
                if a is not None:
                    a_ref[slot, u] = a
                worst = excess if worst is None else jnp.maximum(worst, excess)
            return jnp.max(worst)

        def output_all(pos, carried):
            for u, (h, d) in enumerate(units):
                p, qd = carried[u]
                o = jnp.dot(p, read(d, pos, h, "v"), preferred_element_type=F32)
                o = o + lax.dot_general(qd, stb_ref[u], NT_DIMS, preferred_element_type=F32)
                out_refs[d][index(d, pos) + (vcols(h),)] = o.astype(out_refs[d].dtype)

        class Keeper:
            def __init__(self, u):
                self.u = u
                self.p = self.qd = None

            def prev_state(self, stb):
                stb_ref[self.u] = stb

            def scaled_q(self, qd):
                self.qd = qd

            def scores(self, p):
                self.p = p

        def intra_all(pos, cs, exact_diag):
            keepers = [Keeper(u) for u in range(len(units))]
            gens = []
            for u, (h, d) in enumerate(units):
                k = read(d, pos, h, "k") if has_k else None
                c, a = cs[u]
                gens.append(_intra_stage(read(d, pos, h, "q"), k, read(d, pos, h, "v"), c, a,
                                         st_ref.at[d, h], d == 0, cst, keepers[u], exact_diag))
            _run_staged(gens)
            return tuple((kp.p, kp.qd) for kp in keepers)

        def load_cumsums(slot):
            return [(c_ref[slot, u], None if has_k else a_ref[slot, u]) for u in range(len(units))]

        def either_path(worst, build):
            return lax.cond(worst <= DIAG_FACTOR_MAX_LOG2,
                            functools.partial(build, False), functools.partial(build, True))

        worst0 = cumsum_all(0, 0)
        cs0 = load_cumsums(0)
        worst1 = cumsum_all(1, 1)
        carried0 = either_path(worst0, lambda exact_diag: intra_all(0, cs0, exact_diag))

        def body(i, state):
            worst, carried = state
            slot = lax.rem(i, 2)

            def iteration(exact_diag):
                cs = load_cumsums(slot)
                output_all(i - 1, carried)
                worst_next = cumsum_all(jnp.minimum(i + 1, n - 1), 1 - slot)
                return worst_next, intra_all(i, cs, exact_diag)
            return either_path(worst, iteration)
        _, carried_last = lax.fori_loop(1, n, body, (worst1, carried0))
        output_all(n - 1, carried_last)

    @pl.when(s == 0)
    def _():
        ctx_step()

    @pl.when(s > 0)
    def _():
        latent_step()


def _scan_call(inputs, in_specs, out_struct, out_specs, grid, *, n_heads, dk, dv, has_k,
               n_ctx_chunks, n_step_chunks, lat_index, name):
    n_units = 2 * n_heads
    scratch = [pltpu.VMEM((2, n_heads, dv, dk), F32),
               pltpu.VMEM((2, n_units, CHUNK, dk), F32),
               pltpu.VMEM((n_units, dv, dk), BF16)]
    if not has_k:
        scratch.append(pltpu.VMEM((2, n_units, CHUNK, dk), F32))
    return pl.pallas_call(
        functools.partial(_scan_kernel, n_heads=n_heads, dk=dk, dv=dv, has_k=has_k,
                          n_ctx_chunks=n_ctx_chunks, n_step_chunks=n_step_chunks, lat_index=lat_index),
        grid=grid,
        in_specs=in_specs,
        out_specs=out_specs,
        out_shape=out_struct,
        scratch_shapes=scratch,
        compiler_params=pltpu.CompilerParams(
            dimension_semantics=("arbitrary", "arbitrary", "arbitrary"),
            vmem_limit_bytes=VMEM_LIMIT_BYTES),
        name=name,
    )(*inputs)


A_SCAN_HEADS = 2
A_SCAN_CHUNKS = 8


def _scan_a(c_qi, c_lf, qi, lf):
    bsz, n_ctx, _ = c_qi.shape
    seq = qi.shape[1]
    g = A_SCAN_HEADS
    gw = g * A_DK
    n_hg = A_HEADS // g
    ts = A_SCAN_CHUNKS * CHUNK
    n_steps = seq // ts

    def fstep(s):
        return jnp.maximum(s - 1, 0)

    def bstep(s):
        return n_steps - 1 - jnp.maximum(s - 1, 0)

    cspec = lambda off: pl.BlockSpec((None, n_ctx, gw), lambda b, h, s: (b, 0, off + h))
    fspec = lambda off: pl.BlockSpec((None, ts, gw), lambda b, h, s: (b, fstep(s), off + h))
    bspec = lambda off: pl.BlockSpec((None, ts, gw), lambda b, h, s: (b, bstep(s), off + h))
    in_specs = [cspec(n_hg), cspec(0), cspec(n_hg),
                fspec(0), fspec(n_hg), fspec(0),
                bspec(0), bspec(n_hg), bspec(n_hg)]
    inputs = [c_qi, c_lf, c_lf, qi, qi, lf, qi, qi, lf]
    out_struct = [jax.ShapeDtypeStruct((bsz, seq, A_HEADS * A_DV), BF16)] * 2
    out_specs = [pl.BlockSpec((None, ts, gw), lambda b, h, s: (b, fstep(s), h)),
                 pl.BlockSpec((None, ts, gw), lambda b, h, s: (b, bstep(s), h))]

    def lat_index(chunk):
        row = chunk * CHUNK
        if not isinstance(row, int):
            row = pl.multiple_of(row, CHUNK)
        return (pl.ds(row, CHUNK),)

    return _scan_call(inputs, in_specs, out_struct, out_specs, (bsz, n_hg, n_steps + 1),
                      n_heads=g, dk=A_DK, dv=A_DV, has_k=False,
                      n_ctx_chunks=n_ctx // CHUNK, n_step_chunks=A_SCAN_CHUNKS,
                      lat_index=lat_index, name="scan_hgrn2")


B_SCAN_COLS = 4


def _scan_b(c_qkv, c_lf, qkv, lf):
    bsz, n_ctx, _ = c_qkv.shape
    n_rows = qkv.shape[2]
    kw = B_HEADS * B_DK
    nc = B_SCAN_COLS
    n_steps = GRID_W // nc
    chunks_per_col = n_rows // CHUNK

    def fblk(s):
        return jnp.maximum(s - 1, 0)

    def bblk(s):
        return n_steps - 1 - jnp.maximum(s - 1, 0)

    def cspec(width, off):
        return pl.BlockSpec((None, n_ctx, width), lambda b, h, s: (b, 0, off + h))

    def lspec(width, off, blk):
        return pl.BlockSpec((None, nc, n_rows, width), lambda b, h, s: (b, blk(s), 0, off + h))

    k_off, v_off = kw // B_DK, (2 * kw) // B_DV
    in_specs = [cspec(B_DK, k_off), cspec(B_DV, v_off), cspec(B_DK, 0), cspec(B_DK, B_HEADS),
                lspec(B_DK, 0, fblk), lspec(B_DK, k_off, fblk), lspec(B_DV, v_off, fblk), lspec(B_DK, 0, fblk),
                lspec(B_DK, 0, bblk), lspec(B_DK, k_off, bblk), lspec(B_DV, v_off, bblk),
                lspec(B_DK, B_HEADS, bblk)]
    inputs = [c_qkv, c_qkv, c_lf, c_lf, qkv, qkv, qkv, lf, qkv, qkv, qkv, lf]
    vw = B_HEADS * B_DV
    out_struct = [jax.ShapeDtypeStruct((bsz, GRID_W, n_rows, vw), BF16)] * 2
    out_specs = [pl.BlockSpec((None, nc, n_rows, B_DV), lambda b, h, s: (b, fblk(s), 0, h)),
                 pl.BlockSpec((None, nc, n_rows, B_DV), lambda b, h, s: (b, bblk(s), 0, h))]

    def lat_index(chunk):
        col = chunk // chunks_per_col
        row = (chunk - col * chunks_per_col) * CHUNK
        if not isinstance(row, int):
            row = pl.multiple_of(row, CHUNK)
        return (col, pl.ds(row, CHUNK))

    return _scan_call(inputs, in_specs, out_struct, out_specs, (bsz, B_HEADS, n_steps + 1),
                      n_heads=1, dk=B_DK, dv=B_DV, has_k=True,
                      n_ctx_chunks=n_ctx // CHUNK, n_step_chunks=nc * chunks_per_col,
                      lat_index=lat_index, name="scan_gla")


GLA_FIN_ROWS = 16


def _gla_finalize_kernel(of_ref, ob_ref, sg_ref, g_ref, o_ref):
    n_cols, n_rows, w = of_ref.shape
    o = of_ref[...].astype(F32) + ob_ref[...].astype(F32)
    ms = jnp.mean(o * o, axis=-1, keepdims=True)
    on = o * lax.rsqrt(ms + EPS) * g_ref[...]
    on = jnp.swapaxes(on, 0, 1).reshape(n_rows * n_cols, w)
    o_ref[...] = (on * sg_ref[...].astype(F32)).astype(o_ref.dtype)


def _gla_finalize(of, ob, gates, gain, *, sg_col_off):
    bsz, _, n_rows, vw = of.shape
    tm = GLA_FIN_ROWS * GRID_W
    tiles = n_rows // GLA_FIN_ROWS
    cm = pl.BlockSpec((None, GRID_W, GLA_FIN_ROWS, B_DV), lambda b, r, h: (b, 0, r, h))
    return pl.pallas_call(
        _gla_finalize_kernel,
        grid=(bsz, tiles, B_HEADS),
        in_specs=[cm, cm,
                  pl.BlockSpec((tm, B_DV), lambda b, r, h: (b * tiles + r, sg_col_off + h)),
                  pl.BlockSpec((1, B_DV), lambda b, r, h: (0, h))],
        out_specs=pl.BlockSpec((tm, B_DV), lambda b, r, h: (b * tiles + r, h)),
        out_shape=jax.ShapeDtypeStruct((bsz * n_rows * GRID_W, vw), BF16),
        compiler_params=pltpu.CompilerParams(
            dimension_semantics=("arbitrary", "arbitrary", "arbitrary"),
            vmem_limit_bytes=32 * 1024 * 1024),
        name="gla_finalize",
    )(of, ob, gates, gain)


def _merge_kernel(oaf, oab, obp, sga, sma, smb, ga, wpa, wpb, y_ref):
    o = oaf[...].astype(F32) + oab[...].astype(F32)
    parts = []
    for h in range(A_HEADS):
        oh = o[:, h * A_DV:(h + 1) * A_DV]
        ms = jnp.mean(oh * oh, axis=-1, keepdims=True)
        parts.append(oh * lax.rsqrt(ms + EPS))
    oa = (jnp.concatenate(parts, axis=1) * ga[...] * sga[...].astype(F32)).astype(BF16)
    ya = jnp.dot(oa, wpa[...], preferred_element_type=F32)
    yb = jnp.dot(obp[...], wpb[...], preferred_element_type=F32)
    y_ref[...] = (sma[...].astype(F32) * ya + smb[...].astype(F32) * yb).astype(BF16)


def _merge(oaf, oab, obp, gates, ga, wpa, wpb, *, tm):
    n_tok, d = oaf.shape
    tok = lambda off: pl.BlockSpec((tm, d), lambda i: (i, off))
    const = lambda shape: pl.BlockSpec(shape, lambda i: (0, 0), pipeline_mode=pl.Buffered(1))
    return pl.pallas_call(
        _merge_kernel,
        grid=(n_tok // tm,),
        in_specs=[tok(0), tok(0), tok(0), tok(0), tok(2), tok(3),
                  const((1, d)), const(wpa.shape), const(wpb.shape)],
        out_specs=pl.BlockSpec((tm, wpa.shape[1]), lambda i: (i, 0)),
        out_shape=jax.ShapeDtypeStruct((n_tok, wpa.shape[1]), BF16),
        compiler_params=pltpu.CompilerParams(
            dimension_semantics=("arbitrary",), vmem_limit_bytes=VMEM_LIMIT_BYTES),
        name="merge_proj",
    )(oaf, oab, obp, gates, gates, gates, ga, wpa, wpb)


def _final_kernel(y_ref, x_ref, mod_ref, w_ref, g_ref, o_ref):
    yo = jnp.dot(y_ref[...], w_ref[...], preferred_element_type=F32)
    z = x_ref[...] + mod_ref[0, 2:3, :] * yo
    ms = jnp.mean(z * z, axis=-1, keepdims=True)
    o_ref[...] = z * lax.rsqrt(ms + EPS) * g_ref[...]


def _final(y, x2d, mod3, w_out, fg, *, tm, tiles_per_batch):
    n_tok, d = x2d.shape
    const = lambda shape: pl.BlockSpec(shape, lambda i: (0, 0), pipeline_mode=pl.Buffered(1))
    return pl.pallas_call(
        _final_kernel,
        grid=(n_tok // tm,),
        in_specs=[pl.BlockSpec((tm, d), lambda i: (i, 0)),
                  pl.BlockSpec((tm, d), lambda i: (i, 0)),
                  pl.BlockSpec((1, 3, d), lambda i: (i // tiles_per_batch, 0, 0)),
                  const(w_out.shape), const((1, d))],
        out_specs=pl.BlockSpec((tm, d), lambda i: (i, 0)),
        out_shape=jax.ShapeDtypeStruct((n_tok, d), F32),
        compiler_params=pltpu.CompilerParams(
            dimension_semantics=("arbitrary",), vmem_limit_bytes=VMEM_LIMIT_BYTES),
        name="out_proj_final",
    )(y, x2d, mod3, w_out, fg)


def kernel(x, c, ctx, c_ctx, w_ada, b_ada, norm_g, w_in, hgrn_lb_logits, gla_w_gk, gla_b_gk,
           hgrn_onorm_g, gla_onorm_g, w_pa, w_pb, w_out, final_norm_g):
    bsz, seq, d = x.shape
    n_ctx = ctx.shape[1]
    depth = w_in.shape[0]
    assert depth == 1, "single-layer trunk"
    a_kw, a_vw = A_HEADS * A_DK, A_HEADS * A_DV
    b_kw, b_vw = B_HEADS * B_DK, B_HEADS * B_DV

    n_rows = -(-(bsz + 1) // SUBLANES) * SUBLANES
    cvec = jnp.zeros((n_rows, d), F32).at[:bsz].set(c).at[bsz].set(c_ctx)
    mod = _adaln(cvec, w_ada[0], b_ada[0].reshape(1, -1))
    mod3 = mod.reshape(n_rows, 3, d)

    lb = _lower_bounds(hgrn_lb_logits)[0:1]

    o_ag = 2 * a_kw + 2 * a_vw
    o_bq = o_ag + a_vw
    o_br = o_bq + 2 * b_kw + b_vw
    w_main, w_r = _prep_weights(
        jnp.swapaxes(w_in[0], 0, 1), [(0, o_ag), (o_bq, o_br - o_bq), (o_ag, o_bq - o_ag)],
        tail_start=o_br)
    gk_pad = jnp.zeros((LANES, 2 * b_kw), F32)
    gk_pad = gk_pad.at[0:B_RANK, 0:b_kw].set(gla_w_gk[0, 0]).at[B_RANK:2 * B_RANK, b_kw:].set(gla_w_gk[0, 1])
    gk_pad = gk_pad.astype(BF16)
    gkb = gla_b_gk[0].reshape(1, 2 * b_kw)
    ng = norm_g[0].reshape(1, d)

    x2d = x.reshape(bsz * seq, d)
    ctx2d = ctx.reshape(bsz * n_ctx, d)
    tm = 1024
    tiles_per_batch = seq // tm

    c_aqi, c_alf, c_bqkv, c_blf = _inproj(
        ctx2d, mod3, lambda i: bsz, ng, w_main, w_r, lb, gk_pad, gkb,
        tm=bsz * n_ctx, latent=False)
    aqi, alf, bqkv, blf, gates = _inproj(
        x2d, mod3, lambda i: i // tiles_per_batch, ng, w_main, w_r, lb, gk_pad, gkb,
        tm=tm, latent=True, n_batch=bsz)

    r3 = lambda t, n: t.reshape(bsz, n, t.shape[-1])
    oaf, oab = _scan_a(r3(c_aqi, n_ctx), r3(c_alf, n_ctx), r3(aqi, seq), r3(alf, seq))
    obf, obb = _scan_b(r3(c_bqkv, n_ctx), r3(c_blf, n_ctx), bqkv, blf)
    obp = _gla_finalize(obf, obb, gates, gla_onorm_g[0].reshape(1, -1), sg_col_off=a_vw // B_DV)

    r2 = lambda t: t.reshape(bsz * seq, t.shape[-1])
    y = _merge(r2(oaf), r2(oab), obp, gates, hgrn_onorm_g[0].reshape(1, -1),
               w_pa[0].astype(BF16), w_pb[0].astype(BF16), tm=256)
    out = _final(y, x2d, mod3, w_out[0].astype(BF16), final_norm_g.reshape(1, d),
                 tm=512, tiles_per_batch=seq // 512)
    return out.reshape(bsz, seq, d)
```

```python
import functools

import jax
import jax.numpy as jnp
from jax import lax
from jax.experimental import pallas as pl
from jax.experimental.pallas import tpu as pltpu

F32 = jnp.float32
BF16 = jnp.bfloat16

CHUNK = 64
SUB = 16
GRID_W = 64
EPS = 1e-6
A_HEADS, A_DK, A_DV = 16, 128, 128
B_HEADS, B_DK, B_DV = 4, 256, 512
B_RANK = 16
GATE_NORMALIZER = 16.0
LOG2E = 1.4426950408889634

VMEM_LIMIT_BYTES = 56 * 1024 * 1024
LANES = 128
SUBLANES = 8

NT_DIMS = (((1,), (1,)), ((), ()))
TN_DIMS = (((0,), (0,)), ((), ()))


def _sigmoid(z):
    return 1.0 / (1.0 + jnp.exp(-z))


def _adaln_kernel(c_ref, w_ref, b_ref, o_ref):
    c = c_ref[...]
    s = c * _sigmoid(c)
    o_ref[...] = jnp.dot(s, w_ref[...], preferred_element_type=F32,
                         precision=lax.Precision.HIGHEST) + b_ref[...]


def _adaln(cvec, w, b):
    rows, d = cvec.shape
    n = w.shape[1]
    tn = 768
    return pl.pallas_call(
        _adaln_kernel,
        grid=(n // tn,),
        in_specs=[pl.BlockSpec((rows, d), lambda j: (0, 0)),
                  pl.BlockSpec((d, tn), lambda j: (0, j)),
                  pl.BlockSpec((1, tn), lambda j: (0, j))],
        out_specs=pl.BlockSpec((rows, tn), lambda j: (0, j)),
        out_shape=jax.ShapeDtypeStruct((rows, n), F32),
        compiler_params=pltpu.CompilerParams(
            dimension_semantics=("arbitrary",), vmem_limit_bytes=32 * 1024 * 1024),
        name="adaln",
    )(cvec, w, b)


def _lower_bound_kernel(l_ref, o_ref):
    x = l_ref[...]
    n_rows = x.shape[0]
    m = jnp.max(x, axis=0, keepdims=True)
    e = jnp.exp(x - m)
    tot = jnp.sum(e, axis=0, keepdims=True)
    run = jnp.zeros_like(tot)
    for r in range(n_rows - 1):
        run = run + e[r:r + 1]
        o_ref[r:r + 1, :] = run / tot


def _lower_bounds(logits):
    n_rows, w = logits.shape
    return pl.pallas_call(
        _lower_bound_kernel,
        out_shape=jax.ShapeDtypeStruct((n_rows - 1, w), F32),
        name="hgrn_lower_bounds",
    )(logits)


def _prep_weights_kernel(wa_ref, wb_ref, o_ref, r_ref, *, j_tail, shift):
    j = pl.program_id(0)

    @pl.when(j < j_tail)
    def _():
        o_ref[...] = wa_ref[...].T.astype(BF16)

    @pl.when(j >= j_tail)
    def _():
        o_ref[...] = jnp.concatenate([wa_ref[shift:, :], wb_ref[:shift, :]], axis=0).T.astype(BF16)

    @pl.when(j == j_tail)
    def _():
        rank_rows = jnp.concatenate(
            [wa_ref[:shift, :], jnp.zeros((LANES - shift, wa_ref.shape[1]), F32)], axis=0)
        r_ref[...] = rank_rows.T.astype(BF16)


def _prep_weights(wt, group_starts, tail_start):
    n_in, d = wt.shape
    shift = 2 * B_RANK
    src_tiles = []
    for start, width in group_starts:
        assert start % TN == 0 and width % TN == 0
        src_tiles += list(range(start // TN, (start + width) // TN))
    assert tail_start % TN == 0
    j_tail = len(src_tiles)
    n_tail = (n_in - tail_start - shift) // TN
    n_out = j_tail + n_tail
    last_src = (n_in - 1) // TN

    def src_a(j):
        idx = j - j_tail + tail_start // TN
        for dst in range(j_tail - 1, -1, -1):
            idx = jnp.where(j == dst, src_tiles[dst], idx)
        return idx

    return pl.pallas_call(
        functools.partial(_prep_weights_kernel, j_tail=j_tail, shift=shift),
        grid=(n_out,),
        in_specs=[pl.BlockSpec((TN, d), lambda j: (src_a(j), 0)),
                  pl.BlockSpec((TN, d), lambda j: (jnp.minimum(src_a(j) + 1, last_src), 0))],
        out_specs=[pl.BlockSpec((d, TN), lambda j: (0, j)),
                   pl.BlockSpec((d, LANES), lambda j: (0, 0))],
        out_shape=[jax.ShapeDtypeStruct((d, n_out * TN), BF16),
                   jax.ShapeDtypeStruct((d, LANES), BF16)],
        compiler_params=pltpu.CompilerParams(
            dimension_semantics=("arbitrary",), vmem_limit_bytes=VMEM_LIMIT_BYTES),
        name="prep_weights",
    )(wt, wt)


TN = 1024
NORM_ROWS = 128
EPILOGUE_PARTS = 8
T_AQI = 4096 // TN
T_ALF = 4096 // TN
T_BQKV = 4096 // TN
T_BLF = 2048 // TN
T_GATES = 8192 // TN
J_ALF = T_AQI
J_BQKV = J_ALF + T_ALF
J_BLF = J_BQKV + T_BQKV
J_GATES = J_BLF + T_BLF
J_END = J_GATES + T_GATES


def _inproj_kernel(x_ref, mod_ref, ng_ref, w_ref, wr_ref, lb_ref, gkw_ref, gkb_ref, *rest, latent):
    if latent:
        aqi_ref, alf_ref, bqkv_ref, blf_ref, gates_ref, h_ref, r_ref = rest
    else:
        aqi_ref, alf_ref, bqkv_ref, blf_ref, h_ref, r_ref = rest
        gates_ref = None
    j = pl.program_id(1)

    def store_gla(out_ref, z):
        if not latent:
            out_ref[...] = z.astype(out_ref.dtype)
            return
        n_rows = z.shape[0] // GRID_W
        zc = jnp.swapaxes(z.reshape(n_rows, GRID_W, z.shape[1]), 0, 1)
        out_ref[...] = zc.astype(out_ref.dtype)

    @pl.when(j == 0)
    def _():
        def norm_rows(t, carry):
            rows = pl.ds(pl.multiple_of(t * NORM_ROWS, NORM_ROWS), NORM_ROWS)
            x = x_ref[rows, :]
            ms = jnp.mean(x * x, axis=-1, keepdims=True)
            y = x * lax.rsqrt(ms + EPS) * ng_ref[...]
            h = (y * (1.0 + mod_ref[0, 1:2, :]) + mod_ref[0, 0:1, :]).astype(BF16)
            h_ref[rows, :] = h
            r_ref[rows, :] = jnp.dot(h, wr_ref[...], preferred_element_type=F32).astype(BF16)
            return carry
        lax.fori_loop(0, x_ref.shape[0] // NORM_ROWS, norm_rows, 0)

    def mm():
        return jnp.dot(h_ref[...], w_ref[...], preferred_element_type=F32)

    def mm_parts(out_ref, epilogue):
        part = h_ref.shape[0] // EPILOGUE_PARTS
        for m in range(EPILOGUE_PARTS):
            rows = slice(m * part, (m + 1) * part)
            z = jnp.dot(h_ref[rows, :], w_ref[...], preferred_element_type=F32)
            out_ref[rows, :] = epilogue(z).astype(out_ref.dtype)

    @pl.when(j < J_ALF)
    def _():
        mm_parts(aqi_ref, lambda z: z)

    @pl.when((j >= J_ALF) & (j < J_BQKV))
    def _():
        lb = lb_ref[...]
        mm_parts(alf_ref, lambda z: jnp.log2(lb + (1.0 - lb) * _sigmoid(z)))

    @pl.when((j >= J_BQKV) & (j < J_BLF))
    def _():
        scale = jnp.where(j < J_BQKV + (B_HEADS * B_DK) // TN, B_DK ** -0.5, 1.0)
        store_gla(bqkv_ref, mm() * scale)

    @pl.when((j >= J_BLF) & (j < J_GATES))
    def _():
        z = jnp.dot(r_ref[...], gkw_ref[...], preferred_element_type=F32) + gkb_ref[...]
        ls = jnp.minimum(z, 0.0) - jnp.log(1.0 + jnp.exp(-jnp.abs(z)))
        store_gla(blf_ref, ls * (LOG2E / GATE_NORMALIZER))

    if latent:
        @pl.when(j >= J_GATES)
        def _():
            silu = j < J_GATES + T_GATES // 2
            mm_parts(gates_ref, lambda z: _sigmoid(z) * jnp.where(silu, z, 1.0))


def _inproj(x2d, mod3, mod_row_of_tile, norm_g, w_main, w_r, lb, gk_pad, gkb, *, tm, latent, n_batch=1):
    n_tok, d = x2d.shape
    n_j = J_END if latent else J_GATES
    tile_rows = tm // GRID_W
    seq_rows = n_tok // n_batch // GRID_W
    tiles_per_batch = n_tok // n_batch // tm

    def cl(j, lo, n):
        return jnp.clip(j - lo, 0, n - 1)

    def gla_spec(j_lo, n):
        if not latent:
            return pl.BlockSpec((tm, TN), lambda i, j: (i, cl(j, j_lo, n)))
        return pl.BlockSpec((None, GRID_W, tile_rows, TN),
                            lambda i, j: (i // tiles_per_batch, 0, i % tiles_per_batch, cl(j, j_lo, n)))

    def gla_shape(n, dtype):
        if not latent:
            return jax.ShapeDtypeStruct((n_tok, n * TN), dtype)
        return jax.ShapeDtypeStruct((n_batch, GRID_W, seq_rows, n * TN), dtype)

    in_specs = [
        pl.BlockSpec((tm, d), lambda i, j: (i, 0), pipeline_mode=pl.Buffered(1)),
        pl.BlockSpec((1, 3, d), lambda i, j: (mod_row_of_tile(i), 0, 0)),
        pl.BlockSpec((1, d), lambda i, j: (0, 0)),
        pl.BlockSpec((d, TN), lambda i, j: (0, j - jnp.clip(j - (J_BLF - 1), 0, T_BLF))),
        pl.BlockSpec((d, LANES), lambda i, j: (0, 0)),
        pl.BlockSpec((1, TN), lambda i, j: (0, cl(j, J_ALF, T_ALF))),
        pl.BlockSpec((LANES, TN), lambda i, j: (0, cl(j, J_BLF, T_BLF))),
        pl.BlockSpec((1, TN), lambda i, j: (0, cl(j, J_BLF, T_BLF))),
    ]
    out_specs = [
        pl.BlockSpec((tm, TN), lambda i, j: (i, cl(j, 0, T_AQI))),
        pl.BlockSpec((tm, TN), lambda i, j: (i, cl(j, J_ALF, T_ALF))),
        gla_spec(J_BQKV, T_BQKV),
        gla_spec(J_BLF, T_BLF),
    ]
    out_shape = [
        jax.ShapeDtypeStruct((n_tok, T_AQI * TN), BF16),
        jax.ShapeDtypeStruct((n_tok, T_ALF * TN), F32),
        gla_shape(T_BQKV, BF16),
        gla_shape(T_BLF, F32),
    ]
    if latent:
        out_specs.append(pl.BlockSpec((tm, TN), lambda i, j: (i, cl(j, J_GATES, T_GATES))))
        out_shape.append(jax.ShapeDtypeStruct((n_tok, T_GATES * TN), BF16))
    return pl.pallas_call(
        functools.partial(_inproj_kernel, latent=latent),
        grid=(n_tok // tm, n_j),
        in_specs=in_specs,
        out_specs=out_specs,
        out_shape=out_shape,
        scratch_shapes=[pltpu.VMEM((tm, d), BF16), pltpu.VMEM((tm, LANES), BF16)],
        compiler_params=pltpu.CompilerParams(
            dimension_semantics=("arbitrary", "arbitrary"), vmem_limit_bytes=VMEM_LIMIT_BYTES),
        name="inproj_latent" if latent else "inproj_ctx",
    )(x2d, mod3, norm_g, w_main, w_r, lb, gk_pad, gkb)


def _scan_consts():
    row = lax.broadcasted_iota(jnp.int32, (CHUNK, CHUNK), 0)
    col = lax.broadcasted_iota(jnp.int32, (CHUNK, CHUNK), 1)
    return dict(
        mask_f=col <= row,
        mask_b=col >= row,
        tri_f=jnp.where(col <= row, 1.0, 0.0).astype(BF16),
        tri_b=jnp.where(col >= row, 1.0, 0.0).astype(BF16),
        lane=lax.broadcasted_iota(jnp.int32, (SUBLANES, CHUNK), 1),
    )


DIAG_FACTOR_MAX_LOG2 = 96.0
FOLDED_SUB = 32


def _cumsum_stage(lf, has_k, fwd, cst):
    dk = lf.shape[1]
    tri = cst["tri_f"] if fwd else cst["tri_b"]
    hi = lf.astype(BF16)
    lo = (lf - hi.astype(F32)).astype(BF16)
    cc = jnp.dot(tri, jnp.concatenate([hi, lo], axis=1), preferred_element_type=F32)
    c = cc[:, :dk] + cc[:, dk:]
    a = None if has_k else c - jnp.log2(1.0 - jnp.exp2(lf))
    mag = jnp.abs(lf)
    excess = None
    for r0 in range(0, CHUNK, FOLDED_SUB):
        e = jnp.sum(mag[r0:r0 + FOLDED_SUB], axis=0, keepdims=True)
        excess = e if excess is None else jnp.maximum(excess, e)
    return c, a, excess


def _intra_stage(q, k, v, c, a, st_ref, fwd, cst, keep, exact_diag=True):
    dk = c.shape[1]
    tot = c[CHUNK - 1:CHUNK] if fwd else c[0:1]

    if k is None:
        kf = None

        def kscaled(r0, r1, ref):
            return jnp.exp2(ref - a[r0:r1])
    else:
        a = c
        kf = k.astype(F32)

        def kscaled(r0, r1, ref):
            return kf[r0:r1] * jnp.exp2(ref - c[r0:r1])

    kte = kscaled(0, CHUNK, tot).astype(BF16)
    st = st_ref[...]
    if keep is not None:
        keep.prev_state(st.astype(BF16))
    st_ref[...] = st * jnp.exp2(tot) + lax.dot_general(v, kte, TN_DIMS, preferred_element_type=F32)
    if keep is None:
        return
    yield

    qf = q.astype(F32)
    keep.scaled_q((qf * jnp.exp2(c)).astype(BF16))
    half = SUB // SUBLANES
    sub, own = (SUB, 0) if exact_diag else (FOLDED_SUB, FOLDED_SUB)
    soffs = []
    for r0 in range(0, CHUNK, sub):
        if fwd:
            lo_row, hi_row = 0, r0 + own
            ref = c[r0:r0 + 1]
        else:
            lo_row, hi_row = r0 + sub - own, CHUNK
            ref = c[r0 + sub - 1:r0 + sub]
        if hi_row == lo_row:
            soffs.append(None)
            continue
        pieces = [kscaled(lo_row, hi_row, ref).astype(BF16)]
        if lo_row > 0:
            pieces.insert(0, jnp.zeros((lo_row, dk), BF16))
        if hi_row < CHUNK:
            pieces.append(jnp.zeros((CHUNK - hi_row, dk), BF16))
        kr = jnp.concatenate(pieces, axis=0) if len(pieces) > 1 else pieces[0]
        qr = (qf[r0:r0 + sub] * jnp.exp2(c[r0:r0 + sub] - ref)).astype(BF16)
        soffs.append(lax.dot_general(qr, kr, NT_DIMS, preferred_element_type=F32))
    yield

    if not exact_diag:
        p = jnp.concatenate(soffs, axis=0)
        keep.scores(jnp.where(cst["mask_f"] if fwd else cst["mask_b"], p, 0.0).astype(BF16))
        return

    rows = []
    for r, r0 in enumerate(range(0, CHUNK, SUB)):
        cb = c[r0:r0 + SUB]
        qb = qf[r0:r0 + SUB]
        acc = [jnp.zeros((SUBLANES, CHUNK), F32) for _ in range(half)]
        for jj in range(SUB):
            aj = a[r0 + jj:r0 + jj + 1]
            jg = jj // SUBLANES
            groups = range(jg, half) if fwd else range(0, jg + 1)
            for g in groups:
                e = cb[g * SUBLANES:(g + 1) * SUBLANES] - aj
                if g == jg:
                    e = jnp.minimum(e, 0.0)
                t = qb[g * SUBLANES:(g + 1) * SUBLANES] * jnp.exp2(e)
                if kf is not None:
                    t = t * kf[r0 + jj:r0 + jj + 1]
                sj = jnp.sum(t, axis=1, keepdims=True)
                acc[g] = jnp.where(cst["lane"] == r0 + jj, sj, acc[g])
        d = jnp.concatenate(acc, axis=0)
        rows.append(d if soffs[r] is None else d + soffs[r])
    p = jnp.concatenate(rows, axis=0)
    keep.scores(jnp.where(cst["mask_f"] if fwd else cst["mask_b"], p, 0.0).astype(BF16))


def _run_staged(units):
    live = list(units)
    while live:
        nxt = []
        for u in live:
            try:
                next(u)
                nxt.append(u)
            except StopIteration:
                pass
        live = nxt


def _scan_kernel(*refs, n_heads, dk, dv, has_k, n_ctx_chunks, n_step_chunks, lat_index):
    n_ctx_in = 4 if has_k else 3
    n_lat_in = 4 if has_k else 3
    ctx_refs = refs[:n_ctx_in]
    lat_refs = (refs[n_ctx_in:n_ctx_in + n_lat_in], refs[n_ctx_in + n_lat_in:n_ctx_in + 2 * n_lat_in])
    rest = refs[n_ctx_in + 2 * n_lat_in:]
    out_refs = rest[0:2]
    if has_k:
        st_ref, c_ref, stb_ref = rest[2:]
        a_ref = None
    else:
        st_ref, c_ref, stb_ref, a_ref = rest[2:]
    s = pl.program_id(2)
    cst = _scan_consts()
    units = [(h, d) for h in range(n_heads) for d in range(2)]

    def kcols(h):
        return slice(h * dk, (h + 1) * dk)

    def vcols(h):
        return slice(h * dv, (h + 1) * dv)

    def ctx_step():
        st_ref[...] = jnp.zeros(st_ref.shape, F32)
        if has_k:
            k_ref, v_ref, lff_ref, lfb_ref = ctx_refs
        else:
            v_ref, lff_ref, lfb_ref = ctx_refs
            k_ref = None

        def unit(h, d, rows):
            lf = (lff_ref, lfb_ref)[d][rows, kcols(h)]
            c, a, _ = _cumsum_stage(lf, has_k, d == 0, cst)
            yield
            k = None if k_ref is None else k_ref[rows, kcols(h)]
            yield from _intra_stage(None, k, v_ref[rows, vcols(h)], c, a, st_ref.at[d, h], d == 0, cst, None)

        def body(i, carry):
            rows = (pl.ds(pl.multiple_of(i * CHUNK, CHUNK), CHUNK),
                    pl.ds(pl.multiple_of((n_ctx_chunks - 1 - i) * CHUNK, CHUNK), CHUNK))
            _run_staged([unit(h, d, rows[d]) for h, d in units])
            return carry
        lax.fori_loop(0, n_ctx_chunks, body, 0)

    def latent_step():
        n = n_step_chunks

        def index(d, pos):
            return lat_index(pos if d == 0 else n - 1 - pos)

        def read(d, pos, h, what):
            refs_d = lat_refs[d]
            ref = refs_d[{"q": 0, "k": 1, "v": 2 if has_k else 1, "lf": 3 if has_k else 2}[what]]
            cols = vcols(h) if what == "v" else kcols(h)
            return ref[index(d, pos) + (cols,)]

        def cumsum_all(pos, slot):
            worst = None
            for u, (h, d) in enumerate(units):
                c, a, excess = _cumsum_stage(read(d, pos, h, "lf"), has_k, d == 0, cst)
                c_ref[slot, u] = c
                if a is not None:
                    a_ref[slot, u] = a
                worst = excess if worst is None else jnp.maximum(worst, excess)
            return jnp.max(worst)

        def output_all(pos, carried):
            for u, (h, d) in enumerate(units):
                p, qd = carried[u]
                o = jnp.dot(p, read(d, pos, h, "v"), preferred_element_type=F32)
                o = o + lax.dot_general(qd, stb_ref[u], NT_DIMS, preferred_element_type=F32)
                out_refs[d][index(d, pos) + (vcols(h),)] = o.astype(out_refs[d].dtype)

        class Keeper:
            def __init__(self, u):
                self.u = u
                self.p = self.qd = None

            def prev_state(self, stb):
                stb_ref[self.u] = stb

            def scaled_q(self, qd):
                self.qd = qd

            def scores(self, p):
                self.p = p

        def intra_all(pos, cs, exact_diag):
            keepers = [Keeper(u) for u in range(len(units))]
            gens = []
            for u, (h, d) in enumerate(units):
                k = read(d, pos, h, "k") if has_k else None
                c, a = cs[u]
                gens.append(_intra_stage(read(d, pos, h, "q"), k, read(d, pos, h, "v"), c, a,
                                         st_ref.at[d, h], d == 0, cst, keepers[u], exact_diag))
            _run_staged(gens)
            return tuple((kp.p, kp.qd) for kp in keepers)

        def load_cumsums(slot):
            return [(c_ref[slot, u], None if has_k else a_ref[slot, u]) for u in range(len(units))]

        def either_path(worst, build):
            return lax.cond(worst <= DIAG_FACTOR_MAX_LOG2,
                            functools.partial(build, False), functools.partial(build, True))

        worst0 = cumsum_all(0, 0)
        cs0 = load_cumsums(0)
        worst1 = cumsum_all(1, 1)
        carried0 = either_path(worst0, lambda exact_diag: intra_all(0, cs0, exact_diag))

        def body(i, state):
            worst, carried = state
            slot = lax.rem(i, 2)

            def iteration(exact_diag):
                cs = load_cumsums(slot)
                output_all(i - 1, carried)
                worst_next = cumsum_all(jnp.minimum(i + 1, n - 1), 1 - slot)
                return worst_next, intra_all(i, cs, exact_diag)
            return either_path(worst, iteration)
        _, carried_last = lax.fori_loop(1, n, body, (worst1, carried0))
        output_all(n - 1, carried_last)

    @pl.when(s == 0)
    def _():
        ctx_step()

    @pl.when(s > 0)
    def _():
        latent_step()


def _scan_call(inputs, in_specs, out_struct, out_specs, grid, *, n_heads, dk, dv, has_k,
               n_ctx_chunks, n_step_chunks, lat_index, name):
    n_units = 2 * n_heads
    scratch = [pltpu.VMEM((2, n_heads, dv, dk), F32),
               pltpu.VMEM((2, n_units, CHUNK, dk), F32),
               pltpu.VMEM((n_units, dv, dk), BF16)]
    if not has_k:
        scratch.append(pltpu.VMEM((2, n_units, CHUNK, dk), F32))
    return pl.pallas_call(
        functools.partial(_scan_kernel, n_heads=n_heads, dk=dk, dv=dv, has_k=has_k,
                          n_ctx_chunks=n_ctx_chunks, n_step_chunks=n_step_chunks, lat_index=lat_index),
        grid=grid,
        in_specs=in_specs,
        out_specs=out_specs,
        out_shape=out_struct,
        scratch_shapes=scratch,
        compiler_params=pltpu.CompilerParams(
            dimension_semantics=("arbitrary", "arbitrary", "arbitrary"),
            vmem_limit_bytes=VMEM_LIMIT_BYTES),
        name=name,
    )(*inputs)


A_SCAN_HEADS = 2
A_SCAN_CHUNKS = 16


def _scan_a(c_qi, c_lf, qi, lf):
    bsz, n_ctx, _ = c_qi.shape
    seq = qi.shape[1]
    g = A_SCAN_HEADS
    gw = g * A_DK
    n_hg = A_HEADS // g
    ts = A_SCAN_CHUNKS * CHUNK
    n_steps = seq // ts

    def fstep(s):
        return jnp.maximum(s - 1, 0)

    def bstep(s):
        return n_steps - 1 - jnp.maximum(s - 1, 0)

    cspec = lambda off: pl.BlockSpec((None, n_ctx, gw), lambda b, h, s: (b, 0, off + h))
    fspec = lambda off: pl.BlockSpec((None, ts, gw), lambda b, h, s: (b, fstep(s), off + h))
    bspec = lambda off: pl.BlockSpec((None, ts, gw), lambda b, h, s: (b, bstep(s), off + h))
    in_specs = [cspec(n_hg), cspec(0), cspec(n_hg),
                fspec(0), fspec(n_hg), fspec(0),
                bspec(0), bspec(n_hg), bspec(n_hg)]
    inputs = [c_qi, c_lf, c_lf, qi, qi, lf, qi, qi, lf]
    out_struct = [jax.ShapeDtypeStruct((bsz, seq, A_HEADS * A_DV), BF16)] * 2
    out_specs = [pl.BlockSpec((None, ts, gw), lambda b, h, s: (b, fstep(s), h)),
                 pl.BlockSpec((None, ts, gw), lambda b, h, s: (b, bstep(s), h))]

    def lat_index(chunk):
        row = chunk * CHUNK
        if not isinstance(row, int):
            row = pl.multiple_of(row, CHUNK)
        return (pl.ds(row, CHUNK),)

    return _scan_call(inputs, in_specs, out_struct, out_specs, (bsz, n_hg, n_steps + 1),
                      n_heads=g, dk=A_DK, dv=A_DV, has_k=False,
                      n_ctx_chunks=n_ctx // CHUNK, n_step_chunks=A_SCAN_CHUNKS,
                      lat_index=lat_index, name="scan_hgrn2")


B_SCAN_COLS = 8


def _scan_b(c_qkv, c_lf, qkv, lf):
    bsz, n_ctx, _ = c_qkv.shape
    n_rows = qkv.shape[2]
    kw = B_HEADS * B_DK
    nc = B_SCAN_COLS
    n_steps = GRID_W // nc
    chunks_per_col = n_rows // CHUNK

    def fblk(s):
        return jnp.maximum(s - 1, 0)

    def bblk(s):
        return n_steps - 1 - jnp.maximum(s - 1, 0)

    def cspec(width, off):
        return pl.BlockSpec((None, n_ctx, width), lambda b, h, s: (b, 0, off + h))

    def lspec(width, off, blk):
        return pl.BlockSpec((None, nc, n_rows, width), lambda b, h, s: (b, blk(s), 0, off + h))

    k_off, v_off = kw // B_DK, (2 * kw) // B_DV
    in_specs = [cspec(B_DK, k_off), cspec(B_DV, v_off), cspec(B_DK, 0), cspec(B_DK, B_HEADS),
                lspec(B_DK, 0, fblk), lspec(B_DK, k_off, fblk), lspec(B_DV, v_off, fblk), lspec(B_DK, 0, fblk),
                lspec(B_DK, 0, bblk), lspec(B_DK, k_off, bblk), lspec(B_DV, v_off, bblk),
                lspec(B_DK, B_HEADS, bblk)]
    inputs = [c_qkv, c_qkv, c_lf, c_lf, qkv, qkv, qkv, lf, qkv, qkv, qkv, lf]
    vw = B_HEADS * B_DV
    out_struct = [jax.ShapeDtypeStruct((bsz, GRID_W, n_rows, vw), BF16)] * 2
    out_specs = [pl.BlockSpec((None, nc, n_rows, B_DV), lambda b, h, s: (b, fblk(s), 0, h)),
                 pl.BlockSpec((None, nc, n_rows, B_DV), lambda b, h, s: (b, bblk(s), 0, h))]

    def lat_index(chunk):
        col = chunk // chunks_per_col
        row = (chunk - col * chunks_per_col) * CHUNK
        if not isinstance(row, int):
            row = pl.multiple_of(row, CHUNK)
        return (col, pl.ds(row, CHUNK))

    return _scan_call(inputs, in_specs, out_struct, out_specs, (bsz, B_HEADS, n_steps + 1),
                      n_heads=1, dk=B_DK, dv=B_DV, has_k=True,
                      n_ctx_chunks=n_ctx // CHUNK, n_step_chunks=nc * chunks_per_col,
                      lat_index=lat_index, name="scan_gla")


GLA_FIN_ROWS = 16


def _gla_finalize_kernel(of_ref, ob_ref, sg_ref, g_ref, o_ref):
    n_cols, n_rows, w = of_ref.shape
    o = of_ref[...].astype(F32) + ob_ref[...].astype(F32)
    ms = jnp.mean(o * o, axis=-1, keepdims=True)
    on = o * lax.rsqrt(ms + EPS) * g_ref[...]
    on = jnp.swapaxes(on, 0, 1).reshape(n_rows * n_cols, w)
    o_ref[...] = (on * sg_ref[...].astype(F32)).astype(o_ref.dtype)


def _gla_finalize(of, ob, gates, gain, *, sg_col_off):
    bsz, _, n_rows, vw = of.shape
    tm = GLA_FIN_ROWS * GRID_W
    tiles = n_rows // GLA_FIN_ROWS
    cm = pl.BlockSpec((None, GRID_W, GLA_FIN_ROWS, B_DV), lambda b, r, h: (b, 0, r, h))
    return pl.pallas_call(
        _gla_finalize_kernel,
        grid=(bsz, tiles, B_HEADS),
        in_specs=[cm, cm,
                  pl.BlockSpec((tm, B_DV), lambda b, r, h: (b * tiles + r, sg_col_off + h)),
                  pl.BlockSpec((1, B_DV), lambda b, r, h: (0, h))],
        out_specs=pl.BlockSpec((tm, B_DV), lambda b, r, h: (b * tiles + r, h)),
        out_shape=jax.ShapeDtypeStruct((bsz * n_rows * GRID_W, vw), BF16),
        compiler_params=pltpu.CompilerParams(
            dimension_semantics=("arbitrary", "arbitrary", "arbitrary"),
            vmem_limit_bytes=32 * 1024 * 1024),
        name="gla_finalize",
    )(of, ob, gates, gain)


def _merge_kernel(oaf, oab, obp, sga, sma, smb, ga, wpa, wpb, y_ref):
    o = oaf[...].astype(F32) + oab[...].astype(F32)
    parts = []
    for h in range(A_HEADS):
        oh = o[:, h * A_DV:(h + 1) * A_DV]
        ms = jnp.mean(oh * oh, axis=-1, keepdims=True)
        parts.append(oh * lax.rsqrt(ms + EPS))
    oa = (jnp.concatenate(parts, axis=1) * ga[...] * sga[...].astype(F32)).astype(BF16)
    ya = jnp.dot(oa, wpa[...], preferred_element_type=F32)
    yb = jnp.dot(obp[...], wpb[...], preferred_element_type=F32)
    y_ref[...] = (sma[...].astype(F32) * ya + smb[...].astype(F32) * yb).astype(BF16)


def _merge(oaf, oab, obp, gates, ga, wpa, wpb, *, tm):
    n_tok, d = oaf.shape
    tok = lambda off: pl.BlockSpec((tm, d), lambda i: (i, off))
    const = lambda shape: pl.BlockSpec(shape, lambda i: (0, 0), pipeline_mode=pl.Buffered(1))
    return pl.pallas_call(
        _merge_kernel,
        grid=(n_tok // tm,),
        in_specs=[tok(0), tok(0), tok(0), tok(0), tok(2), tok(3),
                  const((1, d)), const(wpa.shape), const(wpb.shape)],
        out_specs=pl.BlockSpec((tm, wpa.shape[1]), lambda i: (i, 0)),
        out_shape=jax.ShapeDtypeStruct((n_tok, wpa.shape[1]), BF16),
        compiler_params=pltpu.CompilerParams(
            dimension_semantics=("arbitrary",), vmem_limit_bytes=VMEM_LIMIT_BYTES),
        name="merge_proj",
    )(oaf, oab, obp, gates, gates, gates, ga, wpa, wpb)


def _final_kernel(y_ref, x_ref, mod_ref, w_ref, g_ref, o_ref):
    yo = jnp.dot(y_ref[...], w_ref[...], preferred_element_type=F32)
    z = x_ref[...] + mod_ref[0, 2:3, :] * yo
    ms = jnp.mean(z * z, axis=-1, keepdims=True)
    o_ref[...] = z * lax.rsqrt(ms + EPS) * g_ref[...]


def _final(y, x2d, mod3, w_out, fg, *, tm, tiles_per_batch):
    n_tok, d = x2d.shape
    const = lambda shape: pl.BlockSpec(shape, lambda i: (0, 0), pipeline_mode=pl.Buffered(1))
    return pl.pallas_call(
        _final_kernel,
        grid=(n_tok // tm,),
        in_specs=[pl.BlockSpec((tm, d), lambda i: (i, 0)),
                  pl.BlockSpec((tm, d), lambda i: (i, 0)),
                  pl.BlockSpec((1, 3, d), lambda i: (i // tiles_per_batch, 0, 0)),
                  const(w_out.shape), const((1, d))],
        out_specs=pl.BlockSpec((tm, d), lambda i: (i, 0)),
        out_shape=jax.ShapeDtypeStruct((n_tok, d), F32),
        compiler_params=pltpu.CompilerParams(
            dimension_semantics=("arbitrary",), vmem_limit_bytes=VMEM_LIMIT_BYTES),
        name="out_proj_final",
    )(y, x2d, mod3, w_out, fg)


def kernel(x, c, ctx, c_ctx, w_ada, b_ada, norm_g, w_in, hgrn_lb_logits, gla_w_gk, gla_b_gk,
           hgrn_onorm_g, gla_onorm_g, w_pa, w_pb, w_out, final_norm_g):
    bsz, seq, d = x.shape
    n_ctx = ctx.shape[1]
    depth = w_in.shape[0]
    assert depth == 1, "single-layer trunk"
    a_kw, a_vw = A_HEADS * A_DK, A_HEADS * A_DV
    b_kw, b_vw = B_HEADS * B_DK, B_HEADS * B_DV

    n_rows = -(-(bsz + 1) // SUBLANES) * SUBLANES
    cvec = jnp.zeros((n_rows, d), F32).at[:bsz].set(c).at[bsz].set(c_ctx)
    mod = _adaln(cvec, w_ada[0], b_ada[0].reshape(1, -1))
    mod3 = mod.reshape(n_rows, 3, d)

    lb = _lower_bounds(hgrn_lb_logits)[0:1]

    o_ag = 2 * a_kw + 2 * a_vw
    o_bq = o_ag + a_vw
    o_br = o_bq + 2 * b_kw + b_vw
    w_main, w_r = _prep_weights(
        jnp.swapaxes(w_in[0], 0, 1), [(0, o_ag), (o_bq, o_br - o_bq), (o_ag, o_bq - o_ag)],
        tail_start=o_br)
    gk_pad = jnp.zeros((LANES, 2 * b_kw), F32)
    gk_pad = gk_pad.at[0:B_RANK, 0:b_kw].set(gla_w_gk[0, 0]).at[B_RANK:2 * B_RANK, b_kw:].set(gla_w_gk[0, 1])
    gk_pad = gk_pad.astype(BF16)
    gkb = gla_b_gk[0].reshape(1, 2 * b_kw)
    ng = norm_g[0].reshape(1, d)

    x2d = x.reshape(bsz * seq, d)
    ctx2d = ctx.reshape(bsz * n_ctx, d)
    tm = 1024
    tiles_per_batch = seq // tm

    c_aqi, c_alf, c_bqkv, c_blf = _inproj(
        ctx2d, mod3, lambda i: bsz, ng, w_main, w_r, lb, gk_pad, gkb,
        tm=bsz * n_ctx, latent=False)
    aqi, alf, bqkv, blf, gates = _inproj(
        x2d, mod3, lambda i: i // tiles_per_batch, ng, w_main, w_r, lb, gk_pad, gkb,
        tm=tm, latent=True, n_batch=bsz)

    r3 = lambda t, n: t.reshape(bsz, n, t.shape[-1])
    oaf, oab = _scan_a(r3(c_aqi, n_ctx), r3(c_alf, n_ctx), r3(aqi, seq), r3(alf, seq))
    obf, obb = _scan_b(r3(c_bqkv, n_ctx), r3(c_blf, n_ctx), bqkv, blf)
    obp = _gla_finalize(obf, obb, gates, gla_onorm_g[0].reshape(1, -1), sg_col_off=a_vw // B_DV)

    r2 = lambda t: t.reshape(bsz * seq, t.shape[-1])
    y = _merge(r2(oaf), r2(oab), obp, gates, hgrn_onorm_g[0].reshape(1, -1),
               w_pa[0].astype(BF16), w_pb[0].astype(BF16), tm=512)
    out = _final(y, x2d, mod3, w_out[0].astype(BF16), final_norm_g.reshape(1, d),
                 tm=512, tiles_per_batch=seq // 512)
    return out.reshape(bsz, seq, d)
```

```python
import functools

import jax
import jax.numpy as jnp
from jax import lax
from jax.experimental import pallas as pl
from jax.experimental.pallas import tpu as pltpu

F32 = jnp.float32
BF16 = jnp.bfloat16

CHUNK = 64
SUB = 16
GRID_W = 64
EPS = 1e-6
A_HEADS, A_DK, A_DV = 16, 128, 128
B_HEADS, B_DK, B_DV = 4, 256, 512
B_RANK = 16
GATE_NORMALIZER = 16.0
LOG2E = 1.4426950408889634

VMEM_LIMIT_BYTES = 56 * 1024 * 1024
LANES = 128
SUBLANES = 8

NT_DIMS = (((1,), (1,)), ((), ()))
TN_DIMS = (((0,), (0,)), ((), ()))


def _sigmoid(z):
    return 1.0 / (1.0 + jnp.exp(-z))


def _adaln_kernel(c_ref, w_ref, b_ref, o_ref):
    c = c_ref[...]
    s = c * _sigmoid(c)
    o_ref[...] = jnp.dot(s, w_ref[...], preferred_element_type=F32,
                         precision=lax.Precision.HIGHEST) + b_ref[...]


def _adaln(cvec, w, b):
    rows, d = cvec.shape
    n = w.shape[1]
    tn = 768
    return pl.pallas_call(
        _adaln_kernel,
        grid=(n // tn,),
        in_specs=[pl.BlockSpec((rows, d), lambda j: (0, 0)),
                  pl.BlockSpec((d, tn), lambda j: (0, j)),
                  pl.BlockSpec((1, tn), lambda j: (0, j))],
        out_specs=pl.BlockSpec((rows, tn), lambda j: (0, j)),
        out_shape=jax.ShapeDtypeStruct((rows, n), F32),
        compiler_params=pltpu.CompilerParams(
            dimension_semantics=("arbitrary",), vmem_limit_bytes=32 * 1024 * 1024),
        name="adaln",
    )(cvec, w, b)


def _lower_bound_kernel(l_ref, o_ref):
    x = l_ref[...]
    n_rows = x.shape[0]
    m = jnp.max(x, axis=0, keepdims=True)
    e = jnp.exp(x - m)
    tot = jnp.sum(e, axis=0, keepdims=True)
    run = jnp.zeros_like(tot)
    for r in range(n_rows - 1):
        run = run + e[r:r + 1]
        o_ref[r:r + 1, :] = run / tot


def _lower_bounds(logits):
    n_rows, w = logits.shape
    return pl.pallas_call(
        _lower_bound_kernel,
        out_shape=jax.ShapeDtypeStruct((n_rows - 1, w), F32),
        name="hgrn_lower_bounds",
    )(logits)


def _prep_weights_kernel(wa_ref, wb_ref, o_ref, r_ref, *, j_tail, shift):
    j = pl.program_id(0)

    @pl.when(j < j_tail)
    def _():
        o_ref[...] = wa_ref[...].T.astype(BF16)

    @pl.when(j >= j_tail)
    def _():
        o_ref[...] = jnp.concatenate([wa_ref[shift:, :], wb_ref[...]], axis=0).T.astype(BF16)

    @pl.when(j == j_tail)
    def _():
        rank_rows = jnp.concatenate(
            [wa_ref[:shift, :], jnp.zeros((LANES - shift, wa_ref.shape[1]), F32)], axis=0)
        r_ref[...] = rank_rows.T.astype(BF16)


def _prep_weights(wt, group_starts, tail_start):
    n_in, d = wt.shape
    shift = 2 * B_RANK
    src_tiles = []
    for start, width in group_starts:
        assert start % TN == 0 and width % TN == 0
        src_tiles += list(range(start // TN, (start + width) // TN))
    assert tail_start % TN == 0
    j_tail = len(src_tiles)
    n_tail = (n_in - tail_start - shift) // TN
    n_out = j_tail + n_tail
    assert tail_start + shift + n_tail * TN == n_in and TN % shift == 0

    def src_a(j):
        idx = j - j_tail + tail_start // TN
        for dst in range(j_tail - 1, -1, -1):
            idx = jnp.where(j == dst, src_tiles[dst], idx)
        return idx

    def src_b(j):
        return (jnp.maximum(src_a(j), tail_start // TN) + 1) * (TN // shift)

    return pl.pallas_call(
        functools.partial(_prep_weights_kernel, j_tail=j_tail, shift=shift),
        grid=(n_out,),
        in_specs=[pl.BlockSpec((TN, d), lambda j: (src_a(j), 0)),
                  pl.BlockSpec((shift, d), lambda j: (src_b(j), 0))],
        out_specs=[pl.BlockSpec((d, TN), lambda j: (0, j)),
                   pl.BlockSpec((d, LANES), lambda j: (0, 0))],
        out_shape=[jax.ShapeDtypeStruct((d, n_out * TN), BF16),
                   jax.ShapeDtypeStruct((d, LANES), BF16)],
        compiler_params=pltpu.CompilerParams(
            dimension_semantics=("arbitrary",), vmem_limit_bytes=VMEM_LIMIT_BYTES),
        name="prep_weights",
    )(wt, wt)


TN = 1024
EPILOGUE_PARTS = 8
T_AQI = 4096 // TN
T_ALF = 4096 // TN
T_BQKV = 4096 // TN
T_GATES = 8192 // TN
J_ALF = T_AQI
J_BQKV = J_ALF + T_ALF
J_GATES = J_BQKV + T_BQKV
J_END = J_GATES + T_GATES
BLF_W = 2048 // T_BQKV


def _inproj_kernel(x_ref, mod_ref, ng_ref, w_ref, wr_ref, lb_ref, gkw_ref, gkb_ref, *rest, latent):
    if latent:
        aqi_ref, alf_ref, bqkv_ref, blf_ref, gates_ref, h_ref, r_ref = rest
    else:
        aqi_ref, alf_ref, bqkv_ref, blf_ref, h_ref, r_ref = rest
        gates_ref = None
    j = pl.program_id(1)

    def store_gla(out_ref, z):
        if not latent:
            out_ref[...] = z.astype(out_ref.dtype)
            return
        n_rows = z.shape[0] // GRID_W
        zc = jnp.swapaxes(z.reshape(n_rows, GRID_W, z.shape[1]), 0, 1)
        out_ref[...] = zc.astype(out_ref.dtype)

    @pl.when(j == 0)
    def _():
        part = x_ref.shape[0] // EPILOGUE_PARTS
        for m in range(EPILOGUE_PARTS):
            rows = slice(m * part, (m + 1) * part)
            x = x_ref[rows, :]
            ms = jnp.mean(x * x, axis=-1, keepdims=True)
            y = x * lax.rsqrt(ms + EPS) * ng_ref[...]
            h = (y * (1.0 + mod_ref[0, 1:2, :]) + mod_ref[0, 0:1, :]).astype(BF16)
            h_ref[rows, :] = h
            r_ref[rows, :] = jnp.dot(h, wr_ref[...], preferred_element_type=F32).astype(BF16)
            aqi_ref[rows, :] = jnp.dot(h, w_ref[...], preferred_element_type=F32).astype(BF16)

    def mm():
        return jnp.dot(h_ref[...], w_ref[...], preferred_element_type=F32)

    def mm_parts(out_ref, epilogue):
        part = h_ref.shape[0] // EPILOGUE_PARTS
        for m in range(EPILOGUE_PARTS):
            rows = slice(m * part, (m + 1) * part)
            z = jnp.dot(h_ref[rows, :], w_ref[...], preferred_element_type=F32)
            out_ref[rows, :] = epilogue(z).astype(out_ref.dtype)

    @pl.when((j > 0) & (j < J_ALF))
    def _():
        mm_parts(aqi_ref, lambda z: z)

    @pl.when((j >= J_ALF) & (j < J_BQKV))
    def _():
        lb = lb_ref[...]
        mm_parts(alf_ref, lambda z: jnp.log2(lb + (1.0 - lb) * _sigmoid(z)))

    @pl.when((j >= J_BQKV) & (j < J_GATES))
    def _():
        z = jnp.dot(r_ref[...], gkw_ref[...], preferred_element_type=F32) + gkb_ref[...]
        ls = jnp.minimum(z, 0.0) - jnp.log(1.0 + jnp.exp(-jnp.abs(z)))
        store_gla(blf_ref, ls * (LOG2E / GATE_NORMALIZER))
        scale = jnp.where(j < J_BQKV + (B_HEADS * B_DK) // TN, B_DK ** -0.5, 1.0)
        store_gla(bqkv_ref, mm() * scale)

    if latent:
        @pl.when(j >= J_GATES)
        def _():
            silu = j < J_GATES + T_GATES // 2
            mm_parts(gates_ref, lambda z: _sigmoid(z) * jnp.where(silu, z, 1.0))


def _inproj(x2d, mod3, mod_row_of_tile, norm_g, w_main, w_r, lb, gk_pad, gkb, *, tm, latent, n_batch=1):
    n_tok, d = x2d.shape
    n_j = J_END if latent else J_GATES
    tile_rows = tm // GRID_W
    seq_rows = n_tok // n_batch // GRID_W
    tiles_per_batch = n_tok // n_batch // tm

    def cl(j, lo, n):
        return jnp.clip(j - lo, 0, n - 1)

    def gla_spec(width):
        col = lambda j: cl(j, J_BQKV, T_BQKV)
        if not latent:
            return pl.BlockSpec((tm, width), lambda i, j: (i, col(j)))
        return pl.BlockSpec((None, GRID_W, tile_rows, width),
                            lambda i, j: (i // tiles_per_batch, 0, i % tiles_per_batch, col(j)))

    def gla_shape(width, dtype):
        if not latent:
            return jax.ShapeDtypeStruct((n_tok, T_BQKV * width), dtype)
        return jax.ShapeDtypeStruct((n_batch, GRID_W, seq_rows, T_BQKV * width), dtype)

    in_specs = [
        pl.BlockSpec((tm, d), lambda i, j: (i, 0), pipeline_mode=pl.Buffered(1)),
        pl.BlockSpec((1, 3, d), lambda i, j: (mod_row_of_tile(i), 0, 0)),
        pl.BlockSpec((1, d), lambda i, j: (0, 0)),
        pl.BlockSpec((d, TN), lambda i, j: (0, j)),
        pl.BlockSpec((d, LANES), lambda i, j: (0, 0)),
        pl.BlockSpec((1, TN), lambda i, j: (0, cl(j, J_ALF, T_ALF))),
        pl.BlockSpec((LANES, BLF_W), lambda i, j: (0, cl(j, J_BQKV, T_BQKV))),
        pl.BlockSpec((1, BLF_W), lambda i, j: (0, cl(j, J_BQKV, T_BQKV))),
    ]
    out_specs = [
        pl.BlockSpec((tm, TN), lambda i, j: (i, cl(j, 0, T_AQI))),
        pl.BlockSpec((tm, TN), lambda i, j: (i, cl(j, J_ALF, T_ALF))),
        gla_spec(TN),
        gla_spec(BLF_W),
    ]
    out_shape = [
        jax.ShapeDtypeStruct((n_tok, T_AQI * TN), BF16),
        jax.ShapeDtypeStruct((n_tok, T_ALF * TN), F32),
        gla_shape(TN, BF16),
        gla_shape(BLF_W, F32),
    ]
    if latent:
        out_specs.append(pl.BlockSpec((tm, TN), lambda i, j: (i, cl(j, J_GATES, T_GATES))))
        out_shape.append(jax.ShapeDtypeStruct((n_tok, T_GATES * TN), BF16))
    return pl.pallas_call(
        functools.partial(_inproj_kernel, latent=latent),
        grid=(n_tok // tm, n_j),
        in_specs=in_specs,
        out_specs=out_specs,
        out_shape=out_shape,
        scratch_shapes=[pltpu.VMEM((tm, d), BF16), pltpu.VMEM((tm, LANES), BF16)],
        compiler_params=pltpu.CompilerParams(
            dimension_semantics=("arbitrary", "arbitrary"), vmem_limit_bytes=VMEM_LIMIT_BYTES),
        name="inproj_latent" if latent else "inproj_ctx",
    )(x2d, mod3, norm_g, w_main, w_r, lb, gk_pad, gkb)


def _scan_consts():
    row = lax.broadcasted_iota(jnp.int32, (CHUNK, CHUNK), 0)
    col = lax.broadcasted_iota(jnp.int32, (CHUNK, CHUNK), 1)
    return dict(
        mask_f=col <= row,
        mask_b=col >= row,
        tri_f=jnp.where(col <= row, 1.0, 0.0).astype(BF16),
        tri_b=jnp.where(col >= row, 1.0, 0.0).astype(BF16),
        lane=lax.broadcasted_iota(jnp.int32, (SUBLANES, CHUNK), 1),
    )


DIAG_FACTOR_MAX_LOG2 = 96.0
FOLDED_SUB = 32


def _cumsum_stage(lf, has_k, fwd, cst):
    dk = lf.shape[1]
    tri = cst["tri_f"] if fwd else cst["tri_b"]
    hi = lf.astype(BF16)
    lo = (lf - hi.astype(F32)).astype(BF16)
    cc = jnp.dot(tri, jnp.concatenate([hi, lo], axis=1), preferred_element_type=F32)
    c = cc[:, :dk] + cc[:, dk:]
    a = None if has_k else c - jnp.log2(1.0 - jnp.exp2(lf))
    mag = jnp.abs(lf)
    excess = None
    for r0 in range(0, CHUNK, FOLDED_SUB):
        e = jnp.sum(mag[r0:r0 + FOLDED_SUB], axis=0, keepdims=True)
        excess = e if excess is None else jnp.maximum(excess, e)
    return c, a, excess


def _intra_stage(q, k, v, c, a, st_ref, fwd, cst, keep, exact_diag=True):
    dk = c.shape[1]
    tot = c[CHUNK - 1:CHUNK] if fwd else c[0:1]

    if k is None:
        kf = None

        def kscaled(r0, r1, ref):
            return jnp.exp2(ref - a[r0:r1])
    else:
        a = c
        kf = k.astype(F32)

        def kscaled(r0, r1, ref):
            return kf[r0:r1] * jnp.exp2(ref - c[r0:r1])

    kte = kscaled(0, CHUNK, tot).astype(BF16)
    st = st_ref[...]
    if keep is not None:
        keep.prev_state(st.astype(BF16))
    st_ref[...] = st * jnp.exp2(tot) + lax.dot_general(v, kte, TN_DIMS, preferred_element_type=F32)
    if keep is None:
        return
    yield

    qf = q.astype(F32)
    keep.scaled_q((qf * jnp.exp2(c)).astype(BF16))
    half = SUB // SUBLANES
    sub, own = (SUB, 0) if exact_diag else (FOLDED_SUB, FOLDED_SUB)
    soffs = []
    for r0 in range(0, CHUNK, sub):
        if fwd:
            lo_row, hi_row = 0, r0 + own
            ref = c[r0:r0 + 1]
        else:
            lo_row, hi_row = r0 + sub - own, CHUNK
            ref = c[r0 + sub - 1:r0 + sub]
        if hi_row == lo_row:
            soffs.append(None)
            continue
        pieces = [kscaled(lo_row, hi_row, ref).astype(BF16)]
        if lo_row > 0:
            pieces.insert(0, jnp.zeros((lo_row, dk), BF16))
        if hi_row < CHUNK:
            pieces.append(jnp.zeros((CHUNK - hi_row, dk), BF16))
        kr = jnp.concatenate(pieces, axis=0) if len(pieces) > 1 else pieces[0]
        qr = (qf[r0:r0 + sub] * jnp.exp2(c[r0:r0 + sub] - ref)).astype(BF16)
        soffs.append(lax.dot_general(qr, kr, NT_DIMS, preferred_element_type=F32))
    yield

    if not exact_diag:
        p = jnp.concatenate(soffs, axis=0)
        keep.scores(jnp.where(cst["mask_f"] if fwd else cst["mask_b"], p, 0.0).astype(BF16))
        return

    rows = []
    for r, r0 in enumerate(range(0, CHUNK, SUB)):
        cb = c[r0:r0 + SUB]
        qb = qf[r0:r0 + SUB]
        acc = [jnp.zeros((SUBLANES, CHUNK), F32) for _ in range(half)]
        for jj in range(SUB):
            aj = a[r0 + jj:r0 + jj + 1]
            jg = jj // SUBLANES
            groups = range(jg, half) if fwd else range(0, jg + 1)
            for g in groups:
                e = cb[g * SUBLANES:(g + 1) * SUBLANES] - aj
                if g == jg:
                    e = jnp.minimum(e, 0.0)
                t = qb[g * SUBLANES:(g + 1) * SUBLANES] * jnp.exp2(e)
                if kf is not None:
                    t = t * kf[r0 + jj:r0 + jj + 1]
                sj = jnp.sum(t, axis=1, keepdims=True)
                acc[g] = jnp.where(cst["lane"] == r0 + jj, sj, acc[g])
        d = jnp.concatenate(acc, axis=0)
        rows.append(d if soffs[r] is None else d + soffs[r])
    p = jnp.concatenate(rows, axis=0)
    keep.scores(jnp.where(cst["mask_f"] if fwd else cst["mask_b"], p, 0.0).astype(BF16))


def _run_staged(units):
    live = list(units)
    while live:
        nxt = []
        for u in live:
            try:
                next(u)
                nxt.append(u)
            except StopIteration:
                pass
        live = nxt


def _scan_kernel(*refs, n_heads, dk, dv, has_k, n_ctx_chunks, n_step_chunks, lat_index):
    n_ctx_in = 4 if has_k else 3
    n_lat_in = 4 if has_k else 3
    ctx_refs = refs[:n_ctx_in]
    lat_refs = (refs[n_ctx_in:n_ctx_in + n_lat_in], refs[n_ctx_in + n_lat_in:n_ctx_in + 2 * n_lat_in])
    rest = refs[n_ctx_in + 2 * n_lat_in:]
    out_refs = rest[0:2]
    if has_k:
        st_ref, c_ref, stb_ref = rest[2:]
        a_ref = None
    else:
        st_ref, c_ref, stb_ref, a_ref = rest[2:]
    s = pl.program_id(2)
    cst = _scan_consts()
    units = [(h, d) for h in range(n_heads) for d in range(2)]

    def kcols(h):
        return slice(h * dk, (h + 1) * dk)

    def vcols(h):
        return slice(h * dv, (h + 1) * dv)

    def ctx_step():
        st_ref[...] = jnp.zeros(st_ref.shape, F32)
        if has_k:
            k_ref, v_ref, lff_ref, lfb_ref = ctx_refs
        else:
            v_ref, lff_ref, lfb_ref = ctx_refs
            k_ref = None

        def unit(h, d, rows):
            lf = (lff_ref, lfb_ref)[d][rows, kcols(h)]
            c, a, _ = _cumsum_stage(lf, has_k, d == 0, cst)
            yield
            k = None if k_ref is None else k_ref[rows, kcols(h)]
            yield from _intra_stage(None, k, v_ref[rows, vcols(h)], c, a, st_ref.at[d, h], d == 0, cst, None)

        def body(i, carry):
            rows = (pl.ds(pl.multiple_of(i * CHUNK, CHUNK), CHUNK),
                    pl.ds(pl.multiple_of((n_ctx_chunks - 1 - i) * CHUNK, CHUNK), CHUNK))
            _run_staged([unit(h, d, rows[d]) for h, d in units])
            return carry
        lax.fori_loop(0, n_ctx_chunks, body, 0)

    def latent_step():
        n = n_step_chunks

        def index(d, pos):
            return lat_index(pos if d == 0 else n - 1 - pos)

        def read(d, pos, h, what):
            refs_d = lat_refs[d]
            ref = refs_d[{"q": 0, "k": 1, "v": 2 if has_k else 1, "lf": 3 if has_k else 2}[what]]
            cols = vcols(h) if what == "v" else kcols(h)
            return ref[index(d, pos) + (cols,)]

        def cumsum_all(pos, slot):
            worst = None
            for u, (h, d) in enumerate(units):
                c, a, excess = _cumsum_stage(read(d, pos, h, "lf"), has_k, d == 0, cst)
                c_ref[slot, u] = c
                if a is not None:
                    a_ref[slot, u] = a
                worst = excess if worst is None else jnp.maximum(worst, excess)
            return jnp.max(worst)

        def output_all(pos, carried):
            for u, (h, d) in enumerate(units):
                p, qd = carried[u]
                o = jnp.dot(p, read(d, pos, h, "v"), preferred_element_type=F32)
                o = o + lax.dot_general(qd, stb_ref[u], NT_DIMS, preferred_element_type=F32)
                out_refs[d][index(d, pos) + (vcols(h),)] = o.astype(out_refs[d].dtype)

        class Keeper:
            def __init__(self, u):
                self.u = u
                self.p = self.qd = None

            def prev_state(self, stb):
                stb_ref[self.u] = stb

            def scaled_q(self, qd):
                self.qd = qd

            def scores(self, p):
                self.p = p

        def intra_all(pos, cs, exact_diag):
            keepers = [Keeper(u) for u in range(len(units))]
            gens = []
            for u, (h, d) in enumerate(units):
                k = read(d, pos, h, "k") if has_k else None
                c, a = cs[u]
                gens.append(_intra_stage(read(d, pos, h, "q"), k, read(d, pos, h, "v"), c, a,
                                         st_ref.at[d, h], d == 0, cst, keepers[u], exact_diag))
            _run_staged(gens)
            return tuple((kp.p, kp.qd) for kp in keepers)

        def load_cumsums(slot):
            return [(c_ref[slot, u], None if has_k else a_ref[slot, u]) for u in range(len(units))]

        def either_path(worst, build):
            return lax.cond(worst <= DIAG_FACTOR_MAX_LOG2,
                            functools.partial(build, False), functools.partial(build, True))

        worst0 = cumsum_all(0, 0)
        cs0 = load_cumsums(0)
        worst1 = cumsum_all(1, 1)
        carried0 = either_path(worst0, lambda exact_diag: intra_all(0, cs0, exact_diag))

        def body(i, state):
            worst, carried = state
            slot = lax.rem(i, 2)

            def iteration(exact_diag):
                cs = load_cumsums(slot)
                output_all(i - 1, carried)
                worst_next = cumsum_all(jnp.minimum(i + 1, n - 1), 1 - slot)
                return worst_next, intra_all(i, cs, exact_diag)
            return either_path(worst, iteration)
        _, carried_last = lax.fori_loop(1, n, body, (worst1, carried0))
        output_all(n - 1, carried_last)

    @pl.when(s == 0)
    def _():
        ctx_step()

    @pl.when(s > 0)
    def _():
        latent_step()


def _scan_call(inputs, in_specs, out_struct, out_specs, grid, *, n_heads, dk, dv, has_k,
               n_ctx_chunks, n_step_chunks, lat_index, name):
    n_units = 2 * n_heads
    scratch = [pltpu.VMEM((2, n_heads, dv, dk), F32),
               pltpu.VMEM((2, n_units, CHUNK, dk), F32),
               pltpu.VMEM((n_units, dv, dk), BF16)]
    if not has_k:
        scratch.append(pltpu.VMEM((2, n_units, CHUNK, dk), F32))
    return pl.pallas_call(
        functools.partial(_scan_kernel, n_heads=n_heads, dk=dk, dv=dv, has_k=has_k,
                          n_ctx_chunks=n_ctx_chunks, n_step_chunks=n_step_chunks, lat_index=lat_index),
        grid=grid,
        in_specs=in_specs,
        out_specs=out_specs,
        out_shape=out_struct,
        scratch_shapes=scratch,
        compiler_params=pltpu.CompilerParams(
            dimension_semantics=("arbitrary", "arbitrary", "arbitrary"),
            vmem_limit_bytes=VMEM_LIMIT_BYTES),
        name=name,
    )(*inputs)


A_SCAN_HEADS = 2
A_SCAN_CHUNKS = 16


def _scan_a(c_qi, c_lf, qi, lf):
    bsz, n_ctx, _ = c_qi.shape
    seq = qi.shape[1]
    g = A_SCAN_HEADS
    gw = g * A_DK
    n_hg = A_HEADS // g
    ts = A_SCAN_CHUNKS * CHUNK
    n_steps = seq // ts

    def fstep(s):
        return jnp.maximum(s - 1, 0)

    def bstep(s):
        return n_steps - 1 - jnp.maximum(s - 1, 0)

    cspec = lambda off: pl.BlockSpec((None, n_ctx, gw), lambda b, h, s: (b, 0, off + h))
    fspec = lambda off: pl.BlockSpec((None, ts, gw), lambda b, h, s: (b, fstep(s), off + h))
    bspec = lambda off: pl.BlockSpec((None, ts, gw), lambda b, h, s: (b, bstep(s), off + h))
    in_specs = [cspec(n_hg), cspec(0), cspec(n_hg),
                fspec(0), fspec(n_hg), fspec(0),
                bspec(0), bspec(n_hg), bspec(n_hg)]
    inputs = [c_qi, c_lf, c_lf, qi, qi, lf, qi, qi, lf]
    out_struct = [jax.ShapeDtypeStruct((bsz, seq, A_HEADS * A_DV), BF16)] * 2
    out_specs = [pl.BlockSpec((None, ts, gw), lambda b, h, s: (b, fstep(s), h)),
                 pl.BlockSpec((None, ts, gw), lambda b, h, s: (b, bstep(s), h))]

    def lat_index(chunk):
        row = chunk * CHUNK
        if not isinstance(row, int):
            row = pl.multiple_of(row, CHUNK)
        return (pl.ds(row, CHUNK),)

    return _scan_call(inputs, in_specs, out_struct, out_specs, (bsz, n_hg, n_steps + 1),
                      n_heads=g, dk=A_DK, dv=A_DV, has_k=False,
                      n_ctx_chunks=n_ctx // CHUNK, n_step_chunks=A_SCAN_CHUNKS,
                      lat_index=lat_index, name="scan_hgrn2")


B_SCAN_COLS = 8


def _scan_b(c_qkv, c_lf, qkv, lf):
    bsz, n_ctx, _ = c_qkv.shape
    n_rows = qkv.shape[2]
    kw = B_HEADS * B_DK
    nc = B_SCAN_COLS
    n_steps = GRID_W // nc
    chunks_per_col = n_rows // CHUNK

    def fblk(s):
        return jnp.maximum(s - 1, 0)

    def bblk(s):
        return n_steps - 1 - jnp.maximum(s - 1, 0)

    def cspec(width, off):
        return pl.BlockSpec((None, n_ctx, width), lambda b, h, s: (b, 0, off + h))

    def lspec(width, off, blk):
        return pl.BlockSpec((None, nc, n_rows, width), lambda b, h, s: (b, blk(s), 0, off + h))

    k_off, v_off = kw // B_DK, (2 * kw) // B_DV
    in_specs = [cspec(B_DK, k_off), cspec(B_DV, v_off), cspec(B_DK, 0), cspec(B_DK, B_HEADS),
                lspec(B_DK, 0, fblk), lspec(B_DK, k_off, fblk), lspec(B_DV, v_off, fblk), lspec(B_DK, 0, fblk),
                lspec(B_DK, 0, bblk), lspec(B_DK, k_off, bblk), lspec(B_DV, v_off, bblk),
                lspec(B_DK, B_HEADS, bblk)]
    inputs = [c_qkv, c_qkv, c_lf, c_lf, qkv, qkv, qkv, lf, qkv, qkv, qkv, lf]
    vw = B_HEADS * B_DV
    out_struct = [jax.ShapeDtypeStruct((bsz, GRID_W, n_rows, vw), BF16)] * 2
    out_specs = [pl.BlockSpec((None, nc, n_rows, B_DV), lambda b, h, s: (b, fblk(s), 0, h)),
                 pl.BlockSpec((None, nc, n_rows, B_DV), lambda b, h, s: (b, bblk(s), 0, h))]

    def lat_index(chunk):
        col = chunk // chunks_per_col
        row = (chunk - col * chunks_per_col) * CHUNK
        if not isinstance(row, int):
            row = pl.multiple_of(row, CHUNK)
        return (col, pl.ds(row, CHUNK))

    return _scan_call(inputs, in_specs, out_struct, out_specs, (bsz, B_HEADS, n_steps + 1),
                      n_heads=1, dk=B_DK, dv=B_DV, has_k=True,
                      n_ctx_chunks=n_ctx // CHUNK, n_step_chunks=nc * chunks_per_col,
                      lat_index=lat_index, name="scan_gla")


GLA_FIN_ROWS = 16


def _gla_finalize_kernel(of_ref, ob_ref, sg_ref, g_ref, o_ref):
    n_cols, n_rows, w = of_ref.shape
    o = of_ref[...].astype(F32) + ob_ref[...].astype(F32)
    ms = jnp.mean(o * o, axis=-1, keepdims=True)
    on = o * lax.rsqrt(ms + EPS) * g_ref[...]
    on = jnp.swapaxes(on, 0, 1).reshape(n_rows * n_cols, w)
    o_ref[...] = (on * sg_ref[...].astype(F32)).astype(o_ref.dtype)


def _gla_finalize(of, ob, gates, gain, *, sg_col_off):
    bsz, _, n_rows, vw = of.shape
    tm = GLA_FIN_ROWS * GRID_W
    tiles = n_rows // GLA_FIN_ROWS
    cm = pl.BlockSpec((None, GRID_W, GLA_FIN_ROWS, B_DV), lambda b, r, h: (b, 0, r, h))
    return pl.pallas_call(
        _gla_finalize_kernel,
        grid=(bsz, tiles, B_HEADS),
        in_specs=[cm, cm,
                  pl.BlockSpec((tm, B_DV), lambda b, r, h: (b * tiles + r, sg_col_off + h)),
                  pl.BlockSpec((1, B_DV), lambda b, r, h: (0, h))],
        out_specs=pl.BlockSpec((tm, B_DV), lambda b, r, h: (b * tiles + r, h)),
        out_shape=jax.ShapeDtypeStruct((bsz * n_rows * GRID_W, vw), BF16),
        compiler_params=pltpu.CompilerParams(
            dimension_semantics=("arbitrary", "arbitrary", "arbitrary"),
            vmem_limit_bytes=32 * 1024 * 1024),
        name="gla_finalize",
    )(of, ob, gates, gain)


def _merge_kernel(oaf, oab, obp, sga, sma, smb, ga, wpa, wpb, y_ref):
    o = oaf[...].astype(F32) + oab[...].astype(F32)
    parts = []
    for h in range(A_HEADS):
        oh = o[:, h * A_DV:(h + 1) * A_DV]
        ms = jnp.mean(oh * oh, axis=-1, keepdims=True)
        parts.append(oh * lax.rsqrt(ms + EPS))
    oa = (jnp.concatenate(parts, axis=1) * ga[...] * sga[...].astype(F32)).astype(BF16)
    ya = jnp.dot(oa, wpa[...], preferred_element_type=F32)
    yb = jnp.dot(obp[...], wpb[...], preferred_element_type=F32)
    y_ref[...] = (sma[...].astype(F32) * ya + smb[...].astype(F32) * yb).astype(BF16)


def _merge(oaf, oab, obp, gates, ga, wpa, wpb, *, tm):
    n_tok, d = oaf.shape
    tok = lambda off: pl.BlockSpec((tm, d), lambda i: (i, off))
    const = lambda shape: pl.BlockSpec(shape, lambda i: (0, 0), pipeline_mode=pl.Buffered(1))
    return pl.pallas_call(
        _merge_kernel,
        grid=(n_tok // tm,),
        in_specs=[tok(0), tok(0), tok(0), tok(0), tok(2), tok(3),
                  const((1, d)), const(wpa.shape), const(wpb.shape)],
        out_specs=pl.BlockSpec((tm, wpa.shape[1]), lambda i: (i, 0)),
        out_shape=jax.ShapeDtypeStruct((n_tok, wpa.shape[1]), BF16),
        compiler_params=pltpu.CompilerParams(
            dimension_semantics=("arbitrary",), vmem_limit_bytes=VMEM_LIMIT_BYTES),
        name="merge_proj",
    )(oaf, oab, obp, gates, gates, gates, ga, wpa, wpb)


def _final_kernel(y_ref, x_ref, mod_ref, w_ref, g_ref, o_ref):
    yo = jnp.dot(y_ref[...], w_ref[...], preferred_element_type=F32)
    z = x_ref[...] + mod_ref[0, 2:3, :] * yo
    ms = jnp.mean(z * z, axis=-1, keepdims=True)
    o_ref[...] = z * lax.rsqrt(ms + EPS) * g_ref[...]


def _final(y, x2d, mod3, w_out, fg, *, tm, tiles_per_batch):
    n_tok, d = x2d.shape
    const = lambda shape: pl.BlockSpec(shape, lambda i: (0, 0), pipeline_mode=pl.Buffered(1))
    return pl.pallas_call(
        _final_kernel,
        grid=(n_tok // tm,),
        in_specs=[pl.BlockSpec((tm, d), lambda i: (i, 0)),
                  pl.BlockSpec((tm, d), lambda i: (i, 0)),
                  pl.BlockSpec((1, 3, d), lambda i: (i // tiles_per_batch, 0, 0)),
                  const(w_out.shape), const((1, d))],
        out_specs=pl.BlockSpec((tm, d), lambda i: (i, 0)),
        out_shape=jax.ShapeDtypeStruct((n_tok, d), F32),
        compiler_params=pltpu.CompilerParams(
            dimension_semantics=("arbitrary",), vmem_limit_bytes=VMEM_LIMIT_BYTES),
        name="out_proj_final",
    )(y, x2d, mod3, w_out, fg)


def kernel(x, c, ctx, c_ctx, w_ada, b_ada, norm_g, w_in, hgrn_lb_logits, gla_w_gk, gla_b_gk,
           hgrn_onorm_g, gla_onorm_g, w_pa, w_pb, w_out, final_norm_g):
    bsz, seq, d = x.shape
    n_ctx = ctx.shape[1]
    depth = w_in.shape[0]
    assert depth == 1, "single-layer trunk"
    a_kw, a_vw = A_HEADS * A_DK, A_HEADS * A_DV
    b_kw, b_vw = B_HEADS * B_DK, B_HEADS * B_DV

    n_rows = -(-(bsz + 1) // SUBLANES) * SUBLANES
    cvec = jnp.zeros((n_rows, d), F32).at[:bsz].set(c).at[bsz].set(c_ctx)
    mod = _adaln(cvec, w_ada[0], b_ada[0].reshape(1, -1))
    mod3 = mod.reshape(n_rows, 3, d)

    lb = _lower_bounds(hgrn_lb_logits)[0:1]

    o_ag = 2 * a_kw + 2 * a_vw
    o_bq = o_ag + a_vw
    o_br = o_bq + 2 * b_kw + b_vw
    w_main, w_r = _prep_weights(
        jnp.swapaxes(w_in[0], 0, 1), [(0, o_ag), (o_bq, o_br - o_bq), (o_ag, o_bq - o_ag)],
        tail_start=o_br)
    gk_pad = jnp.zeros((LANES, 2 * b_kw), F32)
    gk_pad = gk_pad.at[0:B_RANK, 0:b_kw].set(gla_w_gk[0, 0]).at[B_RANK:2 * B_RANK, b_kw:].set(gla_w_gk[0, 1])
    gk_pad = gk_pad.astype(BF16)
    gkb = gla_b_gk[0].reshape(1, 2 * b_kw)
    ng = norm_g[0].reshape(1, d)

    x2d = x.reshape(bsz * seq, d)
    ctx2d = ctx.reshape(bsz * n_ctx, d)
    tm = 1024
    tiles_per_batch = seq // tm

    c_aqi, c_alf, c_bqkv, c_blf = _inproj(
        ctx2d, mod3, lambda i: bsz, ng, w_main, w_r, lb, gk_pad, gkb,
        tm=bsz * n_ctx, latent=False)
    aqi, alf, bqkv, blf, gates = _inproj(
        x2d, mod3, lambda i: i // tiles_per_batch, ng, w_main, w_r, lb, gk_pad, gkb,
        tm=tm, latent=True, n_batch=bsz)

    r3 = lambda t, n: t.reshape(bsz, n, t.shape[-1])
    oaf, oab = _scan_a(r3(c_aqi, n_ctx), r3(c_alf, n_ctx), r3(aqi, seq), r3(alf, seq))
    obf, obb = _scan_b(r3(c_bqkv, n_ctx), r3(c_blf, n_ctx), bqkv, blf)
    obp = _gla_finalize(obf, obb, gates, gla_onorm_g[0].reshape(1, -1), sg_col_off=a_vw // B_DV)

    r2 = lambda t: t.reshape(bsz * seq, t.shape[-1])
    y = _merge(r2(oaf), r2(oab), obp, gates, hgrn_onorm_g[0].reshape(1, -1),
               w_pa[0].astype(BF16), w_pb[0].astype(BF16), tm=512)
    out = _final(y, x2d, mod3, w_out[0].astype(BF16), final_norm_g.reshape(1, d),
                 tm=512, tiles_per_batch=seq // 512)
    return out.reshape(bsz, seq, d)
```

```python
import functools

import jax
import jax.numpy as jnp
from jax import lax
from jax.experimental import pallas as pl
from jax.experimental.pallas import tpu as pltpu

F32 = jnp.float32
BF16 = jnp.bfloat16

CHUNK = 64
SUB = 16
GRID_W = 64
EPS = 1e-6
A_HEADS, A_DK, A_DV = 16, 128, 128
B_HEADS, B_DK, B_DV = 4, 256, 512
B_RANK = 16
GATE_NORMALIZER = 16.0
LOG2E = 1.4426950408889634

VMEM_LIMIT_BYTES = 56 * 1024 * 1024
LANES = 128
SUBLANES = 8

NT_DIMS = (((1,), (1,)), ((), ()))
TN_DIMS = (((0,), (0,)), ((), ()))


def _sigmoid(z):
    return 1.0 / (1.0 + jnp.exp(-z))


def _adaln_kernel(c_ref, w_ref, b_ref, o_ref):
    c = c_ref[...]
    s = c * _sigmoid(c)
    o_ref[...] = jnp.dot(s, w_ref[...], preferred_element_type=F32,
                         precision=lax.Precision.HIGHEST) + b_ref[...]


def _adaln(cvec, w, b):
    rows, d = cvec.shape
    n = w.shape[1]
    tn = 768
    return pl.pallas_call(
        _adaln_kernel,
        grid=(n // tn,),
        in_specs=[pl.BlockSpec((rows, d), lambda j: (0, 0)),
                  pl.BlockSpec((d, tn), lambda j: (0, j)),
                  pl.BlockSpec((1, tn), lambda j: (0, j))],
        out_specs=pl.BlockSpec((rows, tn), lambda j: (0, j)),
        out_shape=jax.ShapeDtypeStruct((rows, n), F32),
        compiler_params=pltpu.CompilerParams(
            dimension_semantics=("arbitrary",), vmem_limit_bytes=32 * 1024 * 1024),
        name="adaln",
    )(cvec, w, b)


def _lower_bound_kernel(l_ref, o_ref):
    x = l_ref[...]
    n_rows = x.shape[0]
    m = jnp.max(x, axis=0, keepdims=True)
    e = jnp.exp(x - m)
    tot = jnp.sum(e, axis=0, keepdims=True)
    run = jnp.zeros_like(tot)
    for r in range(n_rows - 1):
        run = run + e[r:r + 1]
        o_ref[r:r + 1, :] = run / tot


def _lower_bounds(logits):
    n_rows, w = logits.shape
    return pl.pallas_call(
        _lower_bound_kernel,
        out_shape=jax.ShapeDtypeStruct((n_rows - 1, w), F32),
        name="hgrn_lower_bounds",
    )(logits)


def _prep_weights_kernel(wa_ref, wb_ref, o_ref, r_ref, *, j_tail, shift):
    j = pl.program_id(0)

    @pl.when(j < j_tail)
    def _():
        o_ref[...] = wa_ref[...].T.astype(BF16)

    @pl.when(j >= j_tail)
    def _():
        o_ref[...] = jnp.concatenate([wa_ref[shift:, :], wb_ref[...]], axis=0).T.astype(BF16)

    @pl.when(j == j_tail)
    def _():
        rank_rows = jnp.concatenate(
            [wa_ref[:shift, :], jnp.zeros((LANES - shift, wa_ref.shape[1]), F32)], axis=0)
        r_ref[...] = rank_rows.T.astype(BF16)


def _prep_weights(wt, group_starts, tail_start):
    n_in, d = wt.shape
    shift = 2 * B_RANK
    src_tiles = []
    for start, width in group_starts:
        assert start % TN == 0 and width % TN == 0
        src_tiles += list(range(start // TN, (start + width) // TN))
    assert tail_start % TN == 0
    j_tail = len(src_tiles)
    n_tail = (n_in - tail_start - shift) // TN
    n_out = j_tail + n_tail
    assert tail_start + shift + n_tail * TN == n_in and TN % shift == 0

    def src_a(j):
        idx = j - j_tail + tail_start // TN
        for dst in range(j_tail - 1, -1, -1):
            idx = jnp.where(j == dst, src_tiles[dst], idx)
        return idx

    def src_b(j):
        return (jnp.maximum(src_a(j), tail_start // TN) + 1) * (TN // shift)

    return pl.pallas_call(
        functools.partial(_prep_weights_kernel, j_tail=j_tail, shift=shift),
        grid=(n_out,),
        in_specs=[pl.BlockSpec((TN, d), lambda j: (src_a(j), 0)),
                  pl.BlockSpec((shift, d), lambda j: (src_b(j), 0))],
        out_specs=[pl.BlockSpec((d, TN), lambda j: (0, j)),
                   pl.BlockSpec((d, LANES), lambda j: (0, 0))],
        out_shape=[jax.ShapeDtypeStruct((d, n_out * TN), BF16),
                   jax.ShapeDtypeStruct((d, LANES), BF16)],
        compiler_params=pltpu.CompilerParams(
            dimension_semantics=("arbitrary",), vmem_limit_bytes=VMEM_LIMIT_BYTES),
        name="prep_weights",
    )(wt, wt)


TN = 1024
EPILOGUE_PARTS = 8
T_AQI = 4096 // TN
T_ALF = 4096 // TN
T_BQKV = 4096 // TN
T_GATES = 8192 // TN
J_ALF = T_AQI
J_BQKV = J_ALF + T_ALF
J_GATES = J_BQKV + T_BQKV
J_END = J_GATES + T_GATES
BLF_W = 2048 // T_BQKV


def _inproj_kernel(x_ref, mod_ref, ng_ref, w_ref, wr_ref, lb_ref, gkw_ref, gkb_ref, *rest, latent):
    if latent:
        aqi_ref, alf_ref, bqkv_ref, blf_ref, gates_ref, h_ref, r_ref = rest
    else:
        aqi_ref, alf_ref, bqkv_ref, blf_ref, h_ref, r_ref = rest
        gates_ref = None
    j = pl.program_id(1)

    def store_gla(out_ref, z):
        if not latent:
            out_ref[...] = z.astype(out_ref.dtype)
            return
        n_rows = z.shape[0] // GRID_W
        zc = jnp.swapaxes(z.reshape(n_rows, GRID_W, z.shape[1]), 0, 1)
        out_ref[...] = zc.astype(out_ref.dtype)

    @pl.when(j == 0)
    def _():
        part = x_ref.shape[0] // EPILOGUE_PARTS
        for m in range(EPILOGUE_PARTS):
            rows = slice(m * part, (m + 1) * part)
            x = x_ref[rows, :]
            ms = jnp.mean(x * x, axis=-1, keepdims=True)
            y = x * lax.rsqrt(ms + EPS) * ng_ref[...]
            h = (y * (1.0 + mod_ref[0, 1:2, :]) + mod_ref[0, 0:1, :]).astype(BF16)
            h_ref[rows, :] = h
            r_ref[rows, :] = jnp.dot(h, wr_ref[...], preferred_element_type=F32).astype(BF16)
            aqi_ref[rows, :] = jnp.dot(h, w_ref[...], preferred_element_type=F32).astype(BF16)

    def mm():
        return jnp.dot(h_ref[...], w_ref[...], preferred_element_type=F32)

    def mm_parts(out_ref, epilogue):
        part = h_ref.shape[0] // EPILOGUE_PARTS
        for m in range(EPILOGUE_PARTS):
            rows = slice(m * part, (m + 1) * part)
            z = jnp.dot(h_ref[rows, :], w_ref[...], preferred_element_type=F32)
            out_ref[rows, :] = epilogue(z).astype(out_ref.dtype)

    @pl.when((j > 0) & (j < J_ALF))
    def _():
        mm_parts(aqi_ref, lambda z: z)

    @pl.when((j >= J_ALF) & (j < J_BQKV))
    def _():
        lb = lb_ref[...]
        mm_parts(alf_ref, lambda z: jnp.log2(lb + (1.0 - lb) * _sigmoid(z)))

    @pl.when((j >= J_BQKV) & (j < J_GATES))
    def _():
        z = jnp.dot(r_ref[...], gkw_ref[...], preferred_element_type=F32) + gkb_ref[...]
        ls = jnp.minimum(z, 0.0) - jnp.log(1.0 + jnp.exp(-jnp.abs(z)))
        store_gla(blf_ref, ls * (LOG2E / GATE_NORMALIZER))
        scale = jnp.where(j < J_BQKV + (B_HEADS * B_DK) // TN, B_DK ** -0.5, 1.0)
        store_gla(bqkv_ref, mm() * scale)

    if latent:
        @pl.when(j >= J_GATES)
        def _():
            silu = j < J_GATES + T_GATES // 2
            mm_parts(gates_ref, lambda z: _sigmoid(z) * jnp.where(silu, z, 1.0))


def _inproj(x2d, mod3, mod_row_of_tile, norm_g, w_main, w_r, lb, gk_pad, gkb, *, tm, latent, n_batch=1):
    n_tok, d = x2d.shape
    n_j = J_END if latent else J_GATES
    tile_rows = tm // GRID_W
    seq_rows = n_tok // n_batch // GRID_W
    tiles_per_batch = n_tok // n_batch // tm

    def cl(j, lo, n):
        return jnp.clip(j - lo, 0, n - 1)

    def gla_spec(width):
        col = lambda j: cl(j, J_BQKV, T_BQKV)
        if not latent:
            return pl.BlockSpec((tm, width), lambda i, j: (i, col(j)))
        return pl.BlockSpec((None, GRID_W, tile_rows, width),
                            lambda i, j: (i // tiles_per_batch, 0, i % tiles_per_batch, col(j)))

    def gla_shape(width, dtype):
        if not latent:
            return jax.ShapeDtypeStruct((n_tok, T_BQKV * width), dtype)
        return jax.ShapeDtypeStruct((n_batch, GRID_W, seq_rows, T_BQKV * width), dtype)

    in_specs = [
        pl.BlockSpec((tm, d), lambda i, j: (i, 0), pipeline_mode=pl.Buffered(1)),
        pl.BlockSpec((1, 3, d), lambda i, j: (mod_row_of_tile(i), 0, 0)),
        pl.BlockSpec((1, d), lambda i, j: (0, 0)),
        pl.BlockSpec((d, TN), lambda i, j: (0, j)),
        pl.BlockSpec((d, LANES), lambda i, j: (0, 0)),
        pl.BlockSpec((1, TN), lambda i, j: (0, cl(j, J_ALF, T_ALF))),
        pl.BlockSpec((LANES, BLF_W), lambda i, j: (0, cl(j, J_BQKV, T_BQKV))),
        pl.BlockSpec((1, BLF_W), lambda i, j: (0, cl(j, J_BQKV, T_BQKV))),
    ]
    out_specs = [
        pl.BlockSpec((tm, TN), lambda i, j: (i, cl(j, 0, T_AQI))),
        pl.BlockSpec((tm, TN), lambda i, j: (i, cl(j, J_ALF, T_ALF))),
        gla_spec(TN),
        gla_spec(BLF_W),
    ]
    out_shape = [
        jax.ShapeDtypeStruct((n_tok, T_AQI * TN), BF16),
        jax.ShapeDtypeStruct((n_tok, T_ALF * TN), F32),
        gla_shape(TN, BF16),
        gla_shape(BLF_W, F32),
    ]
    if latent:
        out_specs.append(pl.BlockSpec((tm, TN), lambda i, j: (i, cl(j, J_GATES, T_GATES))))
        out_shape.append(jax.ShapeDtypeStruct((n_tok, T_GATES * TN), BF16))
    return pl.pallas_call(
        functools.partial(_inproj_kernel, latent=latent),
        grid=(n_tok // tm, n_j),
        in_specs=in_specs,
        out_specs=out_specs,
        out_shape=out_shape,
        scratch_shapes=[pltpu.VMEM((tm, d), BF16), pltpu.VMEM((tm, LANES), BF16)],
        compiler_params=pltpu.CompilerParams(
            dimension_semantics=("arbitrary", "arbitrary"), vmem_limit_bytes=VMEM_LIMIT_BYTES),
        name="inproj_latent" if latent else "inproj_ctx",
    )(x2d, mod3, norm_g, w_main, w_r, lb, gk_pad, gkb)


def _scan_consts():
    row = lax.broadcasted_iota(jnp.int32, (CHUNK, CHUNK), 0)
    col = lax.broadcasted_iota(jnp.int32, (CHUNK, CHUNK), 1)
    return dict(
        mask_f=col <= row,
        mask_b=col >= row,
        tri_f=jnp.where(col <= row, 1.0, 0.0).astype(BF16),
        tri_b=jnp.where(col >= row, 1.0, 0.0).astype(BF16),
        lane=lax.broadcasted_iota(jnp.int32, (SUBLANES, CHUNK), 1),
    )


DIAG_FACTOR_MAX_LOG2 = 96.0
FOLDED_SUB = 32


def _cumsum_stage(lf, has_k, fwd, cst):
    dk = lf.shape[1]
    tri = cst["tri_f"] if fwd else cst["tri_b"]
    hi = lf.astype(BF16)
    lo = (lf - hi.astype(F32)).astype(BF16)
    cc = jnp.dot(tri, jnp.concatenate([hi, lo], axis=1), preferred_element_type=F32)
    c = cc[:, :dk] + cc[:, dk:]
    a = None if has_k else c - jnp.log2(1.0 - jnp.exp2(lf))
    mag = jnp.abs(lf)
    excess = None
    for r0 in range(0, CHUNK, FOLDED_SUB):
        e = jnp.sum(mag[r0:r0 + FOLDED_SUB], axis=0, keepdims=True)
        excess = e if excess is None else jnp.maximum(excess, e)
    return c, a, excess


def _intra_stage(q, k, v, c, a, st_ref, fwd, cst, keep, exact_diag=True):
    dk = c.shape[1]
    tot = c[CHUNK - 1:CHUNK] if fwd else c[0:1]

    if k is None:
        kf = None

        def kscaled(r0, r1, ref):
            return jnp.exp2(ref - a[r0:r1])
    else:
        a = c
        kf = k.astype(F32)

        def kscaled(r0, r1, ref):
            return kf[r0:r1] * jnp.exp2(ref - c[r0:r1])

    kte = kscaled(0, CHUNK, tot).astype(BF16)
    st = st_ref[...]
    if keep is not None:
        keep.prev_state(st.astype(BF16))
    st_ref[...] = st * jnp.exp2(tot) + lax.dot_general(v, kte, TN_DIMS, preferred_element_type=F32)
    if keep is None:
        return
    yield

    qf = q.astype(F32)
    keep.scaled_q((qf * jnp.exp2(c)).astype(BF16))
    half = SUB // SUBLANES
    sub, own = (SUB, 0) if exact_diag else (FOLDED_SUB, FOLDED_SUB)
    soffs = []
    for r0 in range(0, CHUNK, sub):
        if fwd:
            lo_row, hi_row = 0, r0 + own
            ref = c[r0:r0 + 1]
        else:
            lo_row, hi_row = r0 + sub - own, CHUNK
            ref = c[r0 + sub - 1:r0 + sub]
        if hi_row == lo_row:
            soffs.append(None)
            continue
        pieces = [kscaled(lo_row, hi_row, ref).astype(BF16)]
        if lo_row > 0:
            pieces.insert(0, jnp.zeros((lo_row, dk), BF16))
        if hi_row < CHUNK:
            pieces.append(jnp.zeros((CHUNK - hi_row, dk), BF16))
        kr = jnp.concatenate(pieces, axis=0) if len(pieces) > 1 else pieces[0]
        qr = (qf[r0:r0 + sub] * jnp.exp2(c[r0:r0 + sub] - ref)).astype(BF16)
        soffs.append(lax.dot_general(qr, kr, NT_DIMS, preferred_element_type=F32))
    yield

    if not exact_diag:
        p = jnp.concatenate(soffs, axis=0)
        keep.scores(jnp.where(cst["mask_f"] if fwd else cst["mask_b"], p, 0.0).astype(BF16))
        return

    rows = []
    for r, r0 in enumerate(range(0, CHUNK, SUB)):
        cb = c[r0:r0 + SUB]
        qb = qf[r0:r0 + SUB]
        acc = [jnp.zeros((SUBLANES, CHUNK), F32) for _ in range(half)]
        for jj in range(SUB):
            aj = a[r0 + jj:r0 + jj + 1]
            jg = jj // SUBLANES
            groups = range(jg, half) if fwd else range(0, jg + 1)
            for g in groups:
                e = cb[g * SUBLANES:(g + 1) * SUBLANES] - aj
                if g == jg:
                    e = jnp.minimum(e, 0.0)
                t = qb[g * SUBLANES:(g + 1) * SUBLANES] * jnp.exp2(e)
                if kf is not None:
                    t = t * kf[r0 + jj:r0 + jj + 1]
                sj = jnp.sum(t, axis=1, keepdims=True)
                acc[g] = jnp.where(cst["lane"] == r0 + jj, sj, acc[g])
        d = jnp.concatenate(acc, axis=0)
        rows.append(d if soffs[r] is None else d + soffs[r])
    p = jnp.concatenate(rows, axis=0)
    keep.scores(jnp.where(cst["mask_f"] if fwd else cst["mask_b"], p, 0.0).astype(BF16))


def _run_staged(units):
    live = list(units)
    while live:
        nxt = []
        for u in live:
            try:
                next(u)
                nxt.append(u)
            except StopIteration:
                pass
        live = nxt


def _scan_kernel(*refs, n_heads, dk, dv, has_k, n_ctx_chunks, n_step_chunks, lat_index):
    n_ctx_in = 4 if has_k else 3
    n_lat_in = 4 if has_k else 3
    ctx_refs = refs[:n_ctx_in]
    lat_refs = (refs[n_ctx_in:n_ctx_in + n_lat_in], refs[n_ctx_in + n_lat_in:n_ctx_in + 2 * n_lat_in])
    rest = refs[n_ctx_in + 2 * n_lat_in:]
    out_refs = rest[0:2]
    if has_k:
        st_ref, c_ref, stb_ref = rest[2:]
        a_ref = None
    else:
        st_ref, c_ref, stb_ref, a_ref = rest[2:]
    s = pl.program_id(2)
    cst = _scan_consts()
    units = [(h, d) for h in range(n_heads) for d in range(2)]

    def kcols(h):
        return slice(h * dk, (h + 1) * dk)

    def vcols(h):
        return slice(h * dv, (h + 1) * dv)

    def ctx_step():
        st_ref[...] = jnp.zeros(st_ref.shape, F32)
        if has_k:
            k_ref, v_ref, lff_ref, lfb_ref = ctx_refs
        else:
            v_ref, lff_ref, lfb_ref = ctx_refs
            k_ref = None

        def unit(h, d, rows):
            lf = (lff_ref, lfb_ref)[d][rows, kcols(h)]
            c, a, _ = _cumsum_stage(lf, has_k, d == 0, cst)
            yield
            k = None if k_ref is None else k_ref[rows, kcols(h)]
            yield from _intra_stage(None, k, v_ref[rows, vcols(h)], c, a, st_ref.at[d, h], d == 0, cst, None)

        def body(i, carry):
            rows = (pl.ds(pl.multiple_of(i * CHUNK, CHUNK), CHUNK),
                    pl.ds(pl.multiple_of((n_ctx_chunks - 1 - i) * CHUNK, CHUNK), CHUNK))
            _run_staged([unit(h, d, rows[d]) for h, d in units])
            return carry
        lax.fori_loop(0, n_ctx_chunks, body, 0)

    def latent_step():
        n = n_step_chunks

        def index(d, pos):
            return lat_index(pos if d == 0 else n - 1 - pos)

        def read(d, pos, h, what):
            refs_d = lat_refs[d]
            ref = refs_d[{"q": 0, "k": 1, "v": 2 if has_k else 1, "lf": 3 if has_k else 2}[what]]
            cols = vcols(h) if what == "v" else kcols(h)
            return ref[index(d, pos) + (cols,)]

        def cumsum_all(pos, slot):
            worst = None
            for u, (h, d) in enumerate(units):
                c, a, excess = _cumsum_stage(read(d, pos, h, "lf"), has_k, d == 0, cst)
                c_ref[slot, u] = c
                if a is not None:
                    a_ref[slot, u] = a
                worst = excess if worst is None else jnp.maximum(worst, excess)
            return jnp.max(worst)

        def output_all(pos, carried):
            for u, (h, d) in enumerate(units):
                p, qd = carried[u]
                o = jnp.dot(p, read(d, pos, h, "v"), preferred_element_type=F32)
                o = o + lax.dot_general(qd, stb_ref[u], NT_DIMS, preferred_element_type=F32)
                out_refs[d][index(d, pos) + (vcols(h),)] = o.astype(out_refs[d].dtype)

        class Keeper:
            def __init__(self, u):
                self.u = u
                self.p = self.qd = None

            def prev_state(self, stb):
                stb_ref[self.u] = stb

            def scaled_q(self, qd):
                self.qd = qd

            def scores(self, p):
                self.p = p

        def intra_all(pos, cs, exact_diag):
            keepers = [Keeper(u) for u in range(len(units))]
            gens = []
            for u, (h, d) in enumerate(units):
                k = read(d, pos, h, "k") if has_k else None
                c, a = cs[u]
                gens.append(_intra_stage(read(d, pos, h, "q"), k, read(d, pos, h, "v"), c, a,
                                         st_ref.at[d, h], d == 0, cst, keepers[u], exact_diag))
            _run_staged(gens)
            return tuple((kp.p, kp.qd) for kp in keepers)

        def load_cumsums(slot):
            return [(c_ref[slot, u], None if has_k else a_ref[slot, u]) for u in range(len(units))]

        def either_path(worst, build):
            return lax.cond(worst <= DIAG_FACTOR_MAX_LOG2,
                            functools.partial(build, False), functools.partial(build, True))

        worst0 = cumsum_all(0, 0)
        cs0 = load_cumsums(0)
        worst1 = cumsum_all(1, 1)
        carried0 = either_path(worst0, lambda exact_diag: intra_all(0, cs0, exact_diag))

        def body(i, state):
            worst, carried = state
            slot = lax.rem(i, 2)

            def iteration(exact_diag):
                cs = load_cumsums(slot)
                output_all(i - 1, carried)
                worst_next = cumsum_all(jnp.minimum(i + 1, n - 1), 1 - slot)
                return worst_next, intra_all(i, cs, exact_diag)
            return either_path(worst, iteration)
        _, carried_last = lax.fori_loop(1, n, body, (worst1, carried0))
        output_all(n - 1, carried_last)

    @pl.when(s == 0)
    def _():
        ctx_step()

    @pl.when(s > 0)
    def _():
        latent_step()


def _scan_call(inputs, in_specs, out_struct, out_specs, grid, *, n_heads, dk, dv, has_k,
               n_ctx_chunks, n_step_chunks, lat_index, name):
    n_units = 2 * n_heads
    scratch = [pltpu.VMEM((2, n_heads, dv, dk), F32),
               pltpu.VMEM((2, n_units, CHUNK, dk), F32),
               pltpu.VMEM((n_units, dv, dk), BF16)]
    if not has_k:
        scratch.append(pltpu.VMEM((2, n_units, CHUNK, dk), F32))
    return pl.pallas_call(
        functools.partial(_scan_kernel, n_heads=n_heads, dk=dk, dv=dv, has_k=has_k,
                          n_ctx_chunks=n_ctx_chunks, n_step_chunks=n_step_chunks, lat_index=lat_index),
        grid=grid,
        in_specs=in_specs,
        out_specs=out_specs,
        out_shape=out_struct,
        scratch_shapes=scratch,
        compiler_params=pltpu.CompilerParams(
            dimension_semantics=("arbitrary", "arbitrary", "arbitrary"),
            vmem_limit_bytes=VMEM_LIMIT_BYTES),
        name=name,
    )(*inputs)


A_SCAN_HEADS = 2
A_SCAN_CHUNKS = 32


def _scan_a(c_qi, c_lf, qi, lf):
    bsz, n_ctx, _ = c_qi.shape
    seq = qi.shape[1]
    g = A_SCAN_HEADS
    gw = g * A_DK
    n_hg = A_HEADS // g
    ts = A_SCAN_CHUNKS * CHUNK
    n_steps = seq // ts

    def fstep(s):
        return jnp.maximum(s - 1, 0)

    def bstep(s):
        return n_steps - 1 - jnp.maximum(s - 1, 0)

    cspec = lambda off: pl.BlockSpec((None, n_ctx, gw), lambda b, h, s: (b, 0, off + h))
    fspec = lambda off: pl.BlockSpec((None, ts, gw), lambda b, h, s: (b, fstep(s), off + h))
    bspec = lambda off: pl.BlockSpec((None, ts, gw), lambda b, h, s: (b, bstep(s), off + h))
    in_specs = [cspec(n_hg), cspec(0), cspec(n_hg),
                fspec(0), fspec(n_hg), fspec(0),
                bspec(0), bspec(n_hg), bspec(n_hg)]
    inputs = [c_qi, c_lf, c_lf, qi, qi, lf, qi, qi, lf]
    out_struct = [jax.ShapeDtypeStruct((bsz, seq, A_HEADS * A_DV), BF16)] * 2
    out_specs = [pl.BlockSpec((None, ts, gw), lambda b, h, s: (b, fstep(s), h)),
                 pl.BlockSpec((None, ts, gw), lambda b, h, s: (b, bstep(s), h))]

    def lat_index(chunk):
        row = chunk * CHUNK
        if not isinstance(row, int):
            row = pl.multiple_of(row, CHUNK)
        return (pl.ds(row, CHUNK),)

    return _scan_call(inputs, in_specs, out_struct, out_specs, (bsz, n_hg, n_steps + 1),
                      n_heads=g, dk=A_DK, dv=A_DV, has_k=False,
                      n_ctx_chunks=n_ctx // CHUNK, n_step_chunks=A_SCAN_CHUNKS,
                      lat_index=lat_index, name="scan_hgrn2")


B_SCAN_COLS = 16


def _scan_b(c_qkv, c_lf, qkv, lf):
    bsz, n_ctx, _ = c_qkv.shape
    n_rows = qkv.shape[2]
    kw = B_HEADS * B_DK
    nc = B_SCAN_COLS
    n_steps = GRID_W // nc
    chunks_per_col = n_rows // CHUNK

    def fblk(s):
        return jnp.maximum(s - 1, 0)

    def bblk(s):
        return n_steps - 1 - jnp.maximum(s - 1, 0)

    def cspec(width, off):
        return pl.BlockSpec((None, n_ctx, width), lambda b, h, s: (b, 0, off + h))

    def lspec(width, off, blk):
        return pl.BlockSpec((None, nc, n_rows, width), lambda b, h, s: (b, blk(s), 0, off + h))

    k_off, v_off = kw // B_DK, (2 * kw) // B_DV
    in_specs = [cspec(B_DK, k_off), cspec(B_DV, v_off), cspec(B_DK, 0), cspec(B_DK, B_HEADS),
                lspec(B_DK, 0, fblk), lspec(B_DK, k_off, fblk), lspec(B_DV, v_off, fblk), lspec(B_DK, 0, fblk),
                lspec(B_DK, 0, bblk), lspec(B_DK, k_off, bblk), lspec(B_DV, v_off, bblk),
                lspec(B_DK, B_HEADS, bblk)]
    inputs = [c_qkv, c_qkv, c_lf, c_lf, qkv, qkv, qkv, lf, qkv, qkv, qkv, lf]
    vw = B_HEADS * B_DV
    out_struct = [jax.ShapeDtypeStruct((bsz, GRID_W, n_rows, vw), BF16)] * 2
    out_specs = [pl.BlockSpec((None, nc, n_rows, B_DV), lambda b, h, s: (b, fblk(s), 0, h)),
                 pl.BlockSpec((None, nc, n_rows, B_DV), lambda b, h, s: (b, bblk(s), 0, h))]

    def lat_index(chunk):
        col = chunk // chunks_per_col
        row = (chunk - col * chunks_per_col) * CHUNK
        if not isinstance(row, int):
            row = pl.multiple_of(row, CHUNK)
        return (col, pl.ds(row, CHUNK))

    return _scan_call(inputs, in_specs, out_struct, out_specs, (bsz, B_HEADS, n_steps + 1),
                      n_heads=1, dk=B_DK, dv=B_DV, has_k=True,
                      n_ctx_chunks=n_ctx // CHUNK, n_step_chunks=nc * chunks_per_col,
                      lat_index=lat_index, name="scan_gla")


GLA_FIN_ROWS = 16


def _gla_finalize_kernel(of_ref, ob_ref, sg_ref, g_ref, o_ref):
    n_cols, n_rows, w = of_ref.shape
    o = of_ref[...].astype(F32) + ob_ref[...].astype(F32)
    ms = jnp.mean(o * o, axis=-1, keepdims=True)
    on = o * lax.rsqrt(ms + EPS) * g_ref[...]
    on = jnp.swapaxes(on, 0, 1).reshape(n_rows * n_cols, w)
    o_ref[...] = (on * sg_ref[...].astype(F32)).astype(o_ref.dtype)


def _gla_finalize(of, ob, gates, gain, *, sg_col_off):
    bsz, _, n_rows, vw = of.shape
    tm = GLA_FIN_ROWS * GRID_W
    tiles = n_rows // GLA_FIN_ROWS
    cm = pl.BlockSpec((None, GRID_W, GLA_FIN_ROWS, B_DV), lambda b, r, h: (b, 0, r, h))
    return pl.pallas_call(
        _gla_finalize_kernel,
        grid=(bsz, tiles, B_HEADS),
        in_specs=[cm, cm,
                  pl.BlockSpec((tm, B_DV), lambda b, r, h: (b * tiles + r, sg_col_off + h)),
                  pl.BlockSpec((1, B_DV), lambda b, r, h: (0, h))],
        out_specs=pl.BlockSpec((tm, B_DV), lambda b, r, h: (b * tiles + r, h)),
        out_shape=jax.ShapeDtypeStruct((bsz * n_rows * GRID_W, vw), BF16),
        compiler_params=pltpu.CompilerParams(
            dimension_semantics=("arbitrary", "arbitrary", "arbitrary"),
            vmem_limit_bytes=32 * 1024 * 1024),
        name="gla_finalize",
    )(of, ob, gates, gain)


def _merge_kernel(oaf, oab, obp, sga, sma, smb, ga, wpa, wpb, y_ref):
    o = oaf[...].astype(F32) + oab[...].astype(F32)
    parts = []
    for h in range(A_HEADS):
        oh = o[:, h * A_DV:(h + 1) * A_DV]
        ms = jnp.mean(oh * oh, axis=-1, keepdims=True)
        parts.append(oh * lax.rsqrt(ms + EPS))
    oa = (jnp.concatenate(parts, axis=1) * ga[...] * sga[...].astype(F32)).astype(BF16)
    ya = jnp.dot(oa, wpa[...], preferred_element_type=F32)
    yb = jnp.dot(obp[...], wpb[...], preferred_element_type=F32)
    y_ref[...] = (sma[...].astype(F32) * ya + smb[...].astype(F32) * yb).astype(BF16)


def _merge(oaf, oab, obp, gates, ga, wpa, wpb, *, tm):
    n_tok, d = oaf.shape
    tok = lambda off: pl.BlockSpec((tm, d), lambda i: (i, off))
    const = lambda shape: pl.BlockSpec(shape, lambda i: (0, 0), pipeline_mode=pl.Buffered(1))
    return pl.pallas_call(
        _merge_kernel,
        grid=(n_tok // tm,),
        in_specs=[tok(0), tok(0), tok(0), tok(0), tok(2), tok(3),
                  const((1, d)), const(wpa.shape), const(wpb.shape)],
        out_specs=pl.BlockSpec((tm, wpa.shape[1]), lambda i: (i, 0)),
        out_shape=jax.ShapeDtypeStruct((n_tok, wpa.shape[1]), BF16),
        compiler_params=pltpu.CompilerParams(
            dimension_semantics=("arbitrary",), vmem_limit_bytes=VMEM_LIMIT_BYTES),
        name="merge_proj",
    )(oaf, oab, obp, gates, gates, gates, ga, wpa, wpb)


def _final_kernel(y_ref, x_ref, mod_ref, w_ref, g_ref, o_ref):
    yo = jnp.dot(y_ref[...], w_ref[...], preferred_element_type=F32)
    z = x_ref[...] + mod_ref[0, 2:3, :] * yo
    ms = jnp.mean(z * z, axis=-1, keepdims=True)
    o_ref[...] = z * lax.rsqrt(ms + EPS) * g_ref[...]


def _final(y, x2d, mod3, w_out, fg, *, tm, tiles_per_batch):
    n_tok, d = x2d.shape
    const = lambda shape: pl.BlockSpec(shape, lambda i: (0, 0), pipeline_mode=pl.Buffered(1))
    return pl.pallas_call(
        _final_kernel,
        grid=(n_tok // tm,),
        in_specs=[pl.BlockSpec((tm, d), lambda i: (i, 0)),
                  pl.BlockSpec((tm, d), lambda i: (i, 0)),
                  pl.BlockSpec((1, 3, d), lambda i: (i // tiles_per_batch, 0, 0)),
                  const(w_out.shape), const((1, d))],
        out_specs=pl.BlockSpec((tm, d), lambda i: (i, 0)),
        out_shape=jax.ShapeDtypeStruct((n_tok, d), F32),
        compiler_params=pltpu.CompilerParams(
            dimension_semantics=("arbitrary",), vmem_limit_bytes=VMEM_LIMIT_BYTES),
        name="out_proj_final",
    )(y, x2d, mod3, w_out, fg)


def kernel(x, c, ctx, c_ctx, w_ada, b_ada, norm_g, w_in, hgrn_lb_logits, gla_w_gk, gla_b_gk,
           hgrn_onorm_g, gla_onorm_g, w_pa, w_pb, w_out, final_norm_g):
    bsz, seq, d = x.shape
    n_ctx = ctx.shape[1]
    depth = w_in.shape[0]
    assert depth == 1, "single-layer trunk"
    a_kw, a_vw = A_HEADS * A_DK, A_HEADS * A_DV
    b_kw, b_vw = B_HEADS * B_DK, B_HEADS * B_DV

    n_rows = -(-(bsz + 1) // SUBLANES) * SUBLANES
    cvec = jnp.zeros((n_rows, d), F32).at[:bsz].set(c).at[bsz].set(c_ctx)
    mod = _adaln(cvec, w_ada[0], b_ada[0].reshape(1, -1))
    mod3 = mod.reshape(n_rows, 3, d)

    lb = _lower_bounds(hgrn_lb_logits)[0:1]

    o_ag = 2 * a_kw + 2 * a_vw
    o_bq = o_ag + a_vw
    o_br = o_bq + 2 * b_kw + b_vw
    w_main, w_r = _prep_weights(
        jnp.swapaxes(w_in[0], 0, 1), [(0, o_ag), (o_bq, o_br - o_bq), (o_ag, o_bq - o_ag)],
        tail_start=o_br)
    gk_pad = jnp.zeros((LANES, 2 * b_kw), F32)
    gk_pad = gk_pad.at[0:B_RANK, 0:b_kw].set(gla_w_gk[0, 0]).at[B_RANK:2 * B_RANK, b_kw:].set(gla_w_gk[0, 1])
    gk_pad = gk_pad.astype(BF16)
    gkb = gla_b_gk[0].reshape(1, 2 * b_kw)
    ng = norm_g[0].reshape(1, d)

    x2d = x.reshape(bsz * seq, d)
    ctx2d = ctx.reshape(bsz * n_ctx, d)
    tm = 1024
    tiles_per_batch = seq // tm

    c_aqi, c_alf, c_bqkv, c_blf = _inproj(
        ctx2d, mod3, lambda i: bsz, ng, w_main, w_r, lb, gk_pad, gkb,
        tm=bsz * n_ctx, latent=False)
    aqi, alf, bqkv, blf, gates = _inproj(
        x2d, mod3, lambda i: i // tiles_per_batch, ng, w_main, w_r, lb, gk_pad, gkb,
        tm=tm, latent=True, n_batch=bsz)

    r3 = lambda t, n: t.reshape(bsz, n, t.shape[-1])
    oaf, oab = _scan_a(r3(c_aqi, n_ctx), r3(c_alf, n_ctx), r3(aqi, seq), r3(alf, seq))
    obf, obb = _scan_b(r3(c_bqkv, n_ctx), r3(c_blf, n_ctx), bqkv, blf)
    obp = _gla_finalize(obf, obb, gates, gla_onorm_g[0].reshape(1, -1), sg_col_off=a_vw // B_DV)

    r2 = lambda t: t.reshape(bsz * seq, t.shape[-1])
    y = _merge(r2(oaf), r2(oab), obp, gates, hgrn_onorm_g[0].reshape(1, -1),
               w_pa[0].astype(BF16), w_pb[0].astype(BF16), tm=512)
    out = _final(y, x2d, mod3, w_out[0].astype(BF16), final_norm_g.reshape(1, d),
                 tm=512, tiles_per_batch=seq // 512)
    return out.reshape(bsz, seq, d)
```

```python
import functools

import jax
import jax.numpy as jnp
from jax import lax
from jax.experimental import pallas as pl
from jax.experimental.pallas import tpu as pltpu

F32 = jnp.float32
BF16 = jnp.bfloat16

CHUNK = 64
SUB = 16
GRID_W = 64
EPS = 1e-6
A_HEADS, A_DK, A_DV = 16, 128, 128
B_HEADS, B_DK, B_DV = 4, 256, 512
B_RANK = 16
GATE_NORMALIZER = 16.0
LOG2E = 1.4426950408889634

VMEM_LIMIT_BYTES = 56 * 1024 * 1024
LANES = 128
SUBLANES = 8

NT_DIMS = (((1,), (1,)), ((), ()))
TN_DIMS = (((0,), (0,)), ((), ()))


def _sigmoid(z):
    return 1.0 / (1.0 + jnp.exp(-z))


def _adaln_kernel(c_ref, w_ref, b_ref, o_ref):
    c = c_ref[...]
    s = c * _sigmoid(c)
    o_ref[...] = jnp.dot(s, w_ref[...], preferred_element_type=F32,
                         precision=lax.Precision.HIGHEST) + b_ref[...]


def _adaln(cvec, w, b):
    rows, d = cvec.shape
    n = w.shape[1]
    tn = 768
    return pl.pallas_call(
        _adaln_kernel,
        grid=(n // tn,),
        in_specs=[pl.BlockSpec((rows, d), lambda j: (0, 0)),
                  pl.BlockSpec((d, tn), lambda j: (0, j)),
                  pl.BlockSpec((1, tn), lambda j: (0, j))],
        out_specs=pl.BlockSpec((rows, tn), lambda j: (0, j)),
        out_shape=jax.ShapeDtypeStruct((rows, n), F32),
        compiler_params=pltpu.CompilerParams(
            dimension_semantics=("arbitrary",), vmem_limit_bytes=32 * 1024 * 1024),
        name="adaln",
    )(cvec, w, b)


def _lower_bound_kernel(l_ref, o_ref):
    x = l_ref[...]
    n_rows = x.shape[0]
    m = jnp.max(x, axis=0, keepdims=True)
    e = jnp.exp(x - m)
    tot = jnp.sum(e, axis=0, keepdims=True)
    run = jnp.zeros_like(tot)
    for r in range(n_rows - 1):
        run = run + e[r:r + 1]
        o_ref[r:r + 1, :] = run / tot


def _lower_bounds(logits):
    n_rows, w = logits.shape
    return pl.pallas_call(
        _lower_bound_kernel,
        out_shape=jax.ShapeDtypeStruct((n_rows - 1, w), F32),
        name="hgrn_lower_bounds",
    )(logits)


def _prep_weights_kernel(wa_ref, wb_ref, o_ref, r_ref, *, j_tail, shift):
    j = pl.program_id(0)

    @pl.when(j < j_tail)
    def _():
        o_ref[...] = wa_ref[...].T.astype(BF16)

    @pl.when(j >= j_tail)
    def _():
        o_ref[...] = jnp.concatenate([wa_ref[shift:, :], wb_ref[...]], axis=0).T.astype(BF16)

    @pl.when(j == j_tail)
    def _():
        rank_rows = jnp.concatenate(
            [wa_ref[:shift, :], jnp.zeros((LANES - shift, wa_ref.shape[1]), F32)], axis=0)
        r_ref[...] = rank_rows.T.astype(BF16)


def _prep_weights(wt, group_starts, tail_start):
    n_in, d = wt.shape
    shift = 2 * B_RANK
    src_tiles = []
    for start, width in group_starts:
        assert start % TN == 0 and width % TN == 0
        src_tiles += list(range(start // TN, (start + width) // TN))
    assert tail_start % TN == 0
    j_tail = len(src_tiles)
    n_tail = (n_in - tail_start - shift) // TN
    n_out = j_tail + n_tail
    assert tail_start + shift + n_tail * TN == n_in and TN % shift == 0

    def src_a(j):
        idx = j - j_tail + tail_start // TN
        for dst in range(j_tail - 1, -1, -1):
            idx = jnp.where(j == dst, src_tiles[dst], idx)
        return idx

    def src_b(j):
        return (jnp.maximum(src_a(j), tail_start // TN) + 1) * (TN // shift)

    return pl.pallas_call(
        functools.partial(_prep_weights_kernel, j_tail=j_tail, shift=shift),
        grid=(n_out,),
        in_specs=[pl.BlockSpec((TN, d), lambda j: (src_a(j), 0)),
                  pl.BlockSpec((shift, d), lambda j: (src_b(j), 0))],
        out_specs=[pl.BlockSpec((d, TN), lambda j: (0, j)),
                   pl.BlockSpec((d, LANES), lambda j: (0, 0))],
        out_shape=[jax.ShapeDtypeStruct((d, n_out * TN), BF16),
                   jax.ShapeDtypeStruct((d, LANES), BF16)],
        compiler_params=pltpu.CompilerParams(
            dimension_semantics=("arbitrary",), vmem_limit_bytes=VMEM_LIMIT_BYTES),
        name="prep_weights",
    )(wt, wt)


TN = 1024
EPILOGUE_PARTS = 8
T_AQI = 4096 // TN
T_ALF = 4096 // TN
T_BQKV = 4096 // TN
T_GATES = 8192 // TN
J_ALF = T_AQI
J_BQKV = J_ALF + T_ALF
J_GATES = J_BQKV + T_BQKV
J_END = J_GATES + T_GATES
BLF_W = 2048 // T_BQKV


def _inproj_kernel(x_ref, mod_ref, ng_ref, w_ref, wr_ref, lb_ref, gkw_ref, gkb_ref, *rest, latent):
    if latent:
        aqi_ref, alf_ref, bqkv_ref, blf_ref, gates_ref, h_ref, r_ref = rest
    else:
        aqi_ref, alf_ref, bqkv_ref, blf_ref, h_ref, r_ref = rest
        gates_ref = None
    j = pl.program_id(1)

    def store_gla(out_ref, z):
        if not latent:
            out_ref[...] = z.astype(out_ref.dtype)
            return
        n_rows = z.shape[0] // GRID_W
        zc = jnp.swapaxes(z.reshape(n_rows, GRID_W, z.shape[1]), 0, 1)
        out_ref[...] = zc.astype(out_ref.dtype)

    @pl.when(j == 0)
    def _():
        part = x_ref.shape[0] // EPILOGUE_PARTS
        for m in range(EPILOGUE_PARTS):
            rows = slice(m * part, (m + 1) * part)
            x = x_ref[rows, :]
            ms = jnp.mean(x * x, axis=-1, keepdims=True)
            y = x * lax.rsqrt(ms + EPS) * ng_ref[...]
            h = (y * (1.0 + mod_ref[0, 1:2, :]) + mod_ref[0, 0:1, :]).astype(BF16)
            h_ref[rows, :] = h
            r_ref[rows, :] = jnp.dot(h, wr_ref[...], preferred_element_type=F32).astype(BF16)
            aqi_ref[rows, :] = jnp.dot(h, w_ref[...], preferred_element_type=F32).astype(BF16)

    def mm():
        return jnp.dot(h_ref[...], w_ref[...], preferred_element_type=F32)

    def mm_parts(out_ref, epilogue):
        part = h_ref.shape[0] // EPILOGUE_PARTS
        for m in range(EPILOGUE_PARTS):
            rows = slice(m * part, (m + 1) * part)
            z = jnp.dot(h_ref[rows, :], w_ref[...], preferred_element_type=F32)
            out_ref[rows, :] = epilogue(z).astype(out_ref.dtype)

    @pl.when((j > 0) & (j < J_ALF))
    def _():
        mm_parts(aqi_ref, lambda z: z)

    @pl.when((j >= J_ALF) & (j < J_BQKV))
    def _():
        lb = lb_ref[...]
        mm_parts(alf_ref, lambda z: jnp.minimum(jnp.log2(lb + (1.0 - lb) * _sigmoid(z)), 0.0))

    @pl.when((j >= J_BQKV) & (j < J_GATES))
    def _():
        z = jnp.dot(r_ref[...], gkw_ref[...], preferred_element_type=F32) + gkb_ref[...]
        ls = jnp.minimum(z, 0.0) - jnp.log(1.0 + jnp.exp(-jnp.abs(z)))
        store_gla(blf_ref, ls * (LOG2E / GATE_NORMALIZER))
        scale = jnp.where(j < J_BQKV + (B_HEADS * B_DK) // TN, B_DK ** -0.5, 1.0)
        store_gla(bqkv_ref, mm() * scale)

    if latent:
        @pl.when(j >= J_GATES)
        def _():
            silu = j < J_GATES + T_GATES // 2
            mm_parts(gates_ref, lambda z: _sigmoid(z) * jnp.where(silu, z, 1.0))


def _inproj(x2d, mod3, mod_row_of_tile, norm_g, w_main, w_r, lb, gk_pad, gkb, *, tm, latent, n_batch=1):
    n_tok, d = x2d.shape
    n_j = J_END if latent else J_GATES
    tile_rows = tm // GRID_W
    seq_rows = n_tok // n_batch // GRID_W
    tiles_per_batch = n_tok // n_batch // tm

    def cl(j, lo, n):
        return jnp.clip(j - lo, 0, n - 1)

    def gla_spec(width):
        col = lambda j: cl(j, J_BQKV, T_BQKV)
        if not latent:
            return pl.BlockSpec((tm, width), lambda i, j: (i, col(j)))
        return pl.BlockSpec((None, GRID_W, tile_rows, width),
                            lambda i, j: (i // tiles_per_batch, 0, i % tiles_per_batch, col(j)))

    def gla_shape(width, dtype):
        if not latent:
            return jax.ShapeDtypeStruct((n_tok, T_BQKV * width), dtype)
        return jax.ShapeDtypeStruct((n_batch, GRID_W, seq_rows, T_BQKV * width), dtype)

    in_specs = [
        pl.BlockSpec((tm, d), lambda i, j: (i, 0), pipeline_mode=pl.Buffered(1)),
        pl.BlockSpec((1, 3, d), lambda i, j: (mod_row_of_tile(i), 0, 0)),
        pl.BlockSpec((1, d), lambda i, j: (0, 0)),
        pl.BlockSpec((d, TN), lambda i, j: (0, j)),
        pl.BlockSpec((d, LANES), lambda i, j: (0, 0)),
        pl.BlockSpec((1, TN), lambda i, j: (0, cl(j, J_ALF, T_ALF))),
        pl.BlockSpec((LANES, BLF_W), lambda i, j: (0, cl(j, J_BQKV, T_BQKV))),
        pl.BlockSpec((1, BLF_W), lambda i, j: (0, cl(j, J_BQKV, T_BQKV))),
    ]
    out_specs = [
        pl.BlockSpec((tm, TN), lambda i, j: (i, cl(j, 0, T_AQI))),
        pl.BlockSpec((tm, TN), lambda i, j: (i, cl(j, J_ALF, T_ALF))),
        gla_spec(TN),
        gla_spec(BLF_W),
    ]
    out_shape = [
        jax.ShapeDtypeStruct((n_tok, T_AQI * TN), BF16),
        jax.ShapeDtypeStruct((n_tok, T_ALF * TN), F32),
        gla_shape(TN, BF16),
        gla_shape(BLF_W, F32),
    ]
    if latent:
        out_specs.append(pl.BlockSpec((tm, TN), lambda i, j: (i, cl(j, J_GATES, T_GATES))))
        out_shape.append(jax.ShapeDtypeStruct((n_tok, T_GATES * TN), BF16))
    return pl.pallas_call(
        functools.partial(_inproj_kernel, latent=latent),
        grid=(n_tok // tm, n_j),
        in_specs=in_specs,
        out_specs=out_specs,
        out_shape=out_shape,
        scratch_shapes=[pltpu.VMEM((tm, d), BF16), pltpu.VMEM((tm, LANES), BF16)],
        compiler_params=pltpu.CompilerParams(
            dimension_semantics=("arbitrary", "arbitrary"), vmem_limit_bytes=VMEM_LIMIT_BYTES),
        name="inproj_latent" if latent else "inproj_ctx",
    )(x2d, mod3, norm_g, w_main, w_r, lb, gk_pad, gkb)


def _scan_consts():
    row = lax.broadcasted_iota(jnp.int32, (CHUNK, CHUNK), 0)
    col = lax.broadcasted_iota(jnp.int32, (CHUNK, CHUNK), 1)
    return dict(
        mask_f=col <= row,
        mask_b=col >= row,
        tri_f=jnp.where(col <= row, 1.0, 0.0).astype(BF16),
        tri_b=jnp.where(col >= row, 1.0, 0.0).astype(BF16),
        lane=lax.broadcasted_iota(jnp.int32, (SUBLANES, CHUNK), 1),
    )


DIAG_FACTOR_MAX_LOG2 = 96.0
FOLDED_SUB = 32


def _cumsum_stage(lf, has_k, fwd, cst):
    dk = lf.shape[1]
    tri = cst["tri_f"] if fwd else cst["tri_b"]
    hi = lf.astype(BF16)
    lo = (lf - hi.astype(F32)).astype(BF16)
    cc = jnp.dot(tri, jnp.concatenate([hi, lo], axis=1), preferred_element_type=F32)
    c = cc[:, :dk] + cc[:, dk:]
    a = None if has_k else c - jnp.log2(1.0 - jnp.exp2(lf))
    mag = jnp.abs(lf)
    excess = None
    for r0 in range(0, CHUNK, FOLDED_SUB):
        e = jnp.sum(mag[r0:r0 + FOLDED_SUB], axis=0, keepdims=True)
        excess = e if excess is None else jnp.maximum(excess, e)
    return c, a, excess


def _intra_stage(q, k, v, c, a, st_ref, fwd, cst, keep, exact_diag=True):
    dk = c.shape[1]
    tot = c[CHUNK - 1:CHUNK] if fwd else c[0:1]

    if k is None:
        kf = None

        def kscaled(r0, r1, ref):
            return jnp.exp2(ref - a[r0:r1])
    else:
        a = c
        kf = k.astype(F32)

        def kscaled(r0, r1, ref):
            return kf[r0:r1] * jnp.exp2(ref - c[r0:r1])

    kte = kscaled(0, CHUNK, tot).astype(BF16)
    st = st_ref[...]
    if keep is not None:
        keep.prev_state(st.astype(BF16))
    st_ref[...] = st * jnp.exp2(tot) + lax.dot_general(v, kte, TN_DIMS, preferred_element_type=F32)
    if keep is None:
        return
    yield

    qf = q.astype(F32)
    keep.scaled_q((qf * jnp.exp2(c)).astype(BF16))
    half = SUB // SUBLANES
    sub, own = (SUB, 0) if exact_diag else (FOLDED_SUB, FOLDED_SUB)
    soffs = []
    for r0 in range(0, CHUNK, sub):
        if fwd:
            lo_row, hi_row = 0, r0 + own
            ref = c[r0:r0 + 1]
        else:
            lo_row, hi_row = r0 + sub - own, CHUNK
            ref = c[r0 + sub - 1:r0 + sub]
        if hi_row == lo_row:
            soffs.append(None)
            continue
        pieces = [kscaled(lo_row, hi_row, ref).astype(BF16)]
        if lo_row > 0:
            pieces.insert(0, jnp.zeros((lo_row, dk), BF16))
        if hi_row < CHUNK:
            pieces.append(jnp.zeros((CHUNK - hi_row, dk), BF16))
        kr = jnp.concatenate(pieces, axis=0) if len(pieces) > 1 else pieces[0]
        qr = (qf[r0:r0 + sub] * jnp.exp2(c[r0:r0 + sub] - ref)).astype(BF16)
        soffs.append(lax.dot_general(qr, kr, NT_DIMS, preferred_element_type=F32))
    yield

    if not exact_diag:
        p = jnp.concatenate(soffs, axis=0)
        keep.scores(jnp.where(cst["mask_f"] if fwd else cst["mask_b"], p, 0.0).astype(BF16))
        return

    rows = []
    for r, r0 in enumerate(range(0, CHUNK, SUB)):
        cb = c[r0:r0 + SUB]
        qb = qf[r0:r0 + SUB]
        acc = [jnp.zeros((SUBLANES, CHUNK), F32) for _ in range(half)]
        for jj in range(SUB):
            aj = a[r0 + jj:r0 + jj + 1]
            jg = jj // SUBLANES
            groups = range(jg, half) if fwd else range(0, jg + 1)
            for g in groups:
                e = cb[g * SUBLANES:(g + 1) * SUBLANES] - aj
                if g == jg:
                    e = jnp.minimum(e, 0.0)
                t = qb[g * SUBLANES:(g + 1) * SUBLANES] * jnp.exp2(e)
                if kf is not None:
                    t = t * kf[r0 + jj:r0 + jj + 1]
                sj = jnp.sum(t, axis=1, keepdims=True)
                acc[g] = jnp.where(cst["lane"] == r0 + jj, sj, acc[g])
        d = jnp.concatenate(acc, axis=0)
        rows.append(d if soffs[r] is None else d + soffs[r])
    p = jnp.concatenate(rows, axis=0)
    keep.scores(jnp.where(cst["mask_f"] if fwd else cst["mask_b"], p, 0.0).astype(BF16))


def _run_staged(units):
    live = list(units)
    while live:
        nxt = []
        for u in live:
            try:
                next(u)
                nxt.append(u)
            except StopIteration:
                pass
        live = nxt


def _scan_kernel(*refs, n_heads, dk, dv, has_k, n_ctx_chunks, n_step_chunks, lat_index):
    n_ctx_in = 4 if has_k else 3
    n_lat_in = 4 if has_k else 3
    ctx_refs = refs[:n_ctx_in]
    lat_refs = (refs[n_ctx_in:n_ctx_in + n_lat_in], refs[n_ctx_in + n_lat_in:n_ctx_in + 2 * n_lat_in])
    rest = refs[n_ctx_in + 2 * n_lat_in:]
    out_refs = rest[0:2]
    if has_k:
        st_ref, c_ref, stb_ref = rest[2:]
        a_ref = None
    else:
        st_ref, c_ref, stb_ref, a_ref = rest[2:]
    s = pl.program_id(2)
    cst = _scan_consts()
    units = [(h, d) for h in range(n_heads) for d in range(2)]

    def kcols(h):
        return slice(h * dk, (h + 1) * dk)

    def vcols(h):
        return slice(h * dv, (h + 1) * dv)

    def ctx_step():
        st_ref[...] = jnp.zeros(st_ref.shape, F32)
        if has_k:
            k_ref, v_ref, lff_ref, lfb_ref = ctx_refs
        else:
            v_ref, lff_ref, lfb_ref = ctx_refs
            k_ref = None

        def unit(h, d, rows):
            lf = (lff_ref, lfb_ref)[d][rows, kcols(h)]
            c, a, _ = _cumsum_stage(lf, has_k, d == 0, cst)
            yield
            k = None if k_ref is None else k_ref[rows, kcols(h)]
            yield from _intra_stage(None, k, v_ref[rows, vcols(h)], c, a, st_ref.at[d, h], d == 0, cst, None)

        def body(i, carry):
            rows = (pl.ds(pl.multiple_of(i * CHUNK, CHUNK), CHUNK),
                    pl.ds(pl.multiple_of((n_ctx_chunks - 1 - i) * CHUNK, CHUNK), CHUNK))
            _run_staged([unit(h, d, rows[d]) for h, d in units])
            return carry
        lax.fori_loop(0, n_ctx_chunks, body, 0)

    def latent_step():
        n = n_step_chunks

        def index(d, pos):
            return lat_index(pos if d == 0 else n - 1 - pos)

        def read(d, pos, h, what):
            refs_d = lat_refs[d]
            ref = refs_d[{"q": 0, "k": 1, "v": 2 if has_k else 1, "lf": 3 if has_k else 2}[what]]
            cols = vcols(h) if what == "v" else kcols(h)
            return ref[index(d, pos) + (cols,)]

        def cumsum_all(pos, slot):
            worst = None
            for u, (h, d) in enumerate(units):
                c, a, excess = _cumsum_stage(read(d, pos, h, "lf"), has_k, d == 0, cst)
                c_ref[slot, u] = c
                if a is not None:
                    a_ref[slot, u] = a
                worst = excess if worst is None else jnp.maximum(worst, excess)
            return jnp.max(worst)

        def output_all(pos, carried):
            for u, (h, d) in enumerate(units):
                p, qd = carried[u]
                o = jnp.dot(p, read(d, pos, h, "v"), preferred_element_type=F32)
                o = o + lax.dot_general(qd, stb_ref[u], NT_DIMS, preferred_element_type=F32)
                out_refs[d][index(d, pos) + (vcols(h),)] = o.astype(out_refs[d].dtype)

        class Keeper:
            def __init__(self, u):
                self.u = u
                self.p = self.qd = None

            def prev_state(self, stb):
                stb_ref[self.u] = stb

            def scaled_q(self, qd):
                self.qd = qd

            def scores(self, p):
                self.p = p

        def intra_all(pos, cs, exact_diag):
            keepers = [Keeper(u) for u in range(len(units))]
            gens = []
            for u, (h, d) in enumerate(units):
                k = read(d, pos, h, "k") if has_k else None
                c, a = cs[u]
                gens.append(_intra_stage(read(d, pos, h, "q"), k, read(d, pos, h, "v"), c, a,
                                         st_ref.at[d, h], d == 0, cst, keepers[u], exact_diag))
            _run_staged(gens)
            return tuple((kp.p, kp.qd) for kp in keepers)

        def load_cumsums(slot):
            return [(c_ref[slot, u], None if has_k else a_ref[slot, u]) for u in range(len(units))]

        def either_path(worst, build):
            return lax.cond(worst <= DIAG_FACTOR_MAX_LOG2,
                            functools.partial(build, False), functools.partial(build, True))

        worst0 = cumsum_all(0, 0)
        cs0 = load_cumsums(0)
        worst1 = cumsum_all(1, 1)
        carried0 = either_path(worst0, lambda exact_diag: intra_all(0, cs0, exact_diag))

        def body(i, state):
            worst, carried = state
            slot = lax.rem(i, 2)

            def iteration(exact_diag):
                cs = load_cumsums(slot)
                output_all(i - 1, carried)
                worst_next = cumsum_all(jnp.minimum(i + 1, n - 1), 1 - slot)
                return worst_next, intra_all(i, cs, exact_diag)
            return either_path(worst, iteration)
        _, carried_last = lax.fori_loop(1, n, body, (worst1, carried0))
        output_all(n - 1, carried_last)

    @pl.when(s == 0)
    def _():
        ctx_step()

    @pl.when(s > 0)
    def _():
        latent_step()


def _scan_call(inputs, in_specs, out_struct, out_specs, grid, *, n_heads, dk, dv, has_k,
               n_ctx_chunks, n_step_chunks, lat_index, name):
    n_units = 2 * n_heads
    scratch = [pltpu.VMEM((2, n_heads, dv, dk), F32),
               pltpu.VMEM((2, n_units, CHUNK, dk), F32),
               pltpu.VMEM((n_units, dv, dk), BF16)]
    if not has_k:
        scratch.append(pltpu.VMEM((2, n_units, CHUNK, dk), F32))
    return pl.pallas_call(
        functools.partial(_scan_kernel, n_heads=n_heads, dk=dk, dv=dv, has_k=has_k,
                          n_ctx_chunks=n_ctx_chunks, n_step_chunks=n_step_chunks, lat_index=lat_index),
        grid=grid,
        in_specs=in_specs,
        out_specs=out_specs,
        out_shape=out_struct,
        scratch_shapes=scratch,
        compiler_params=pltpu.CompilerParams(
            dimension_semantics=("arbitrary", "arbitrary", "arbitrary"),
            vmem_limit_bytes=VMEM_LIMIT_BYTES),
        name=name,
    )(*inputs)


A_SCAN_HEADS = 2
A_SCAN_CHUNKS = 32


def _scan_a(c_qi, c_lf, qi, lf):
    bsz, n_ctx, _ = c_qi.shape
    seq = qi.shape[1]
    g = A_SCAN_HEADS
    gw = g * A_DK
    n_hg = A_HEADS // g
    ts = A_SCAN_CHUNKS * CHUNK
    n_steps = seq // ts

    def fstep(s):
        return jnp.maximum(s - 1, 0)

    def bstep(s):
        return n_steps - 1 - jnp.maximum(s - 1, 0)

    cspec = lambda off: pl.BlockSpec((None, n_ctx, gw), lambda b, h, s: (b, 0, off + h))
    fspec = lambda off: pl.BlockSpec((None, ts, gw), lambda b, h, s: (b, fstep(s), off + h))
    bspec = lambda off: pl.BlockSpec((None, ts, gw), lambda b, h, s: (b, bstep(s), off + h))
    in_specs = [cspec(n_hg), cspec(0), cspec(n_hg),
                fspec(0), fspec(n_hg), fspec(0),
                bspec(0), bspec(n_hg), bspec(n_hg)]
    inputs = [c_qi, c_lf, c_lf, qi, qi, lf, qi, qi, lf]
    out_struct = [jax.ShapeDtypeStruct((bsz, seq, A_HEADS * A_DV), BF16)] * 2
    out_specs = [pl.BlockSpec((None, ts, gw), lambda b, h, s: (b, fstep(s), h)),
                 pl.BlockSpec((None, ts, gw), lambda b, h, s: (b, bstep(s), h))]

    def lat_index(chunk):
        row = chunk * CHUNK
        if not isinstance(row, int):
            row = pl.multiple_of(row, CHUNK)
        return (pl.ds(row, CHUNK),)

    return _scan_call(inputs, in_specs, out_struct, out_specs, (bsz, n_hg, n_steps + 1),
                      n_heads=g, dk=A_DK, dv=A_DV, has_k=False,
                      n_ctx_chunks=n_ctx // CHUNK, n_step_chunks=A_SCAN_CHUNKS,
                      lat_index=lat_index, name="scan_hgrn2")


B_SCAN_COLS = 16


def _scan_b(c_qkv, c_lf, qkv, lf):
    bsz, n_ctx, _ = c_qkv.shape
    n_rows = qkv.shape[2]
    kw = B_HEADS * B_DK
    nc = B_SCAN_COLS
    n_steps = GRID_W // nc
    chunks_per_col = n_rows // CHUNK

    def fblk(s):
        return jnp.maximum(s - 1, 0)

    def bblk(s):
        return n_steps - 1 - jnp.maximum(s - 1, 0)

    def cspec(width, off):
        return pl.BlockSpec((None, n_ctx, width), lambda b, h, s: (b, 0, off + h))

    def lspec(width, off, blk):
        return pl.BlockSpec((None, nc, n_rows, width), lambda b, h, s: (b, blk(s), 0, off + h))

    k_off, v_off = kw // B_DK, (2 * kw) // B_DV
    in_specs = [cspec(B_DK, k_off), cspec(B_DV, v_off), cspec(B_DK, 0), cspec(B_DK, B_HEADS),
                lspec(B_DK, 0, fblk), lspec(B_DK, k_off, fblk), lspec(B_DV, v_off, fblk), lspec(B_DK, 0, fblk),
                lspec(B_DK, 0, bblk), lspec(B_DK, k_off, bblk), lspec(B_DV, v_off, bblk),
                lspec(B_DK, B_HEADS, bblk)]
    inputs = [c_qkv, c_qkv, c_lf, c_lf, qkv, qkv, qkv, lf, qkv, qkv, qkv, lf]
    vw = B_HEADS * B_DV
    out_struct = [jax.ShapeDtypeStruct((bsz, GRID_W, n_rows, vw), BF16)] * 2
    out_specs = [pl.BlockSpec((None, nc, n_rows, B_DV), lambda b, h, s: (b, fblk(s), 0, h)),
                 pl.BlockSpec((None, nc, n_rows, B_DV), lambda b, h, s: (b, bblk(s), 0, h))]

    def lat_index(chunk):
        col = chunk // chunks_per_col
        row = (chunk - col * chunks_per_col) * CHUNK
        if not isinstance(row, int):
            row = pl.multiple_of(row, CHUNK)
        return (col, pl.ds(row, CHUNK))

    return _scan_call(inputs, in_specs, out_struct, out_specs, (bsz, B_HEADS, n_steps + 1),
                      n_heads=1, dk=B_DK, dv=B_DV, has_k=True,
                      n_ctx_chunks=n_ctx // CHUNK, n_step_chunks=nc * chunks_per_col,
                      lat_index=lat_index, name="scan_gla")


GLA_FIN_ROWS = 16


def _gla_finalize_kernel(of_ref, ob_ref, sg_ref, g_ref, o_ref):
    n_cols, n_rows, w = of_ref.shape
    o = of_ref[...].astype(F32) + ob_ref[...].astype(F32)
    ms = jnp.mean(o * o, axis=-1, keepdims=True)
    on = o * lax.rsqrt(ms + EPS) * g_ref[...]
    on = jnp.swapaxes(on, 0, 1).reshape(n_rows * n_cols, w)
    o_ref[...] = (on * sg_ref[...].astype(F32)).astype(o_ref.dtype)


def _gla_finalize(of, ob, gates, gain, *, sg_col_off):
    bsz, _, n_rows, vw = of.shape
    tm = GLA_FIN_ROWS * GRID_W
    tiles = n_rows // GLA_FIN_ROWS
    cm = pl.BlockSpec((None, GRID_W, GLA_FIN_ROWS, B_DV), lambda b, r, h: (b, 0, r, h))
    return pl.pallas_call(
        _gla_finalize_kernel,
        grid=(bsz, tiles, B_HEADS),
        in_specs=[cm, cm,
                  pl.BlockSpec((tm, B_DV), lambda b, r, h: (b * tiles + r, sg_col_off + h)),
                  pl.BlockSpec((1, B_DV), lambda b, r, h: (0, h))],
        out_specs=pl.BlockSpec((tm, B_DV), lambda b, r, h: (b * tiles + r, h)),
        out_shape=jax.ShapeDtypeStruct((bsz * n_rows * GRID_W, vw), BF16),
        compiler_params=pltpu.CompilerParams(
            dimension_semantics=("arbitrary", "arbitrary", "arbitrary"),
            vmem_limit_bytes=32 * 1024 * 1024),
        name="gla_finalize",
    )(of, ob, gates, gain)


def _merge_kernel(oaf, oab, obp, sga, sma, smb, ga, wpa, wpb, y_ref):
    o = oaf[...].astype(F32) + oab[...].astype(F32)
    parts = []
    for h in range(A_HEADS):
        oh = o[:, h * A_DV:(h + 1) * A_DV]
        ms = jnp.mean(oh * oh, axis=-1, keepdims=True)
        parts.append(oh * lax.rsqrt(ms + EPS))
    oa = (jnp.concatenate(parts, axis=1) * ga[...] * sga[...].astype(F32)).astype(BF16)
    ya = jnp.dot(oa, wpa[...], preferred_element_type=F32)
    yb = jnp.dot(obp[...], wpb[...], preferred_element_type=F32)
    y_ref[...] = (sma[...].astype(F32) * ya + smb[...].astype(F32) * yb).astype(BF16)


def _merge(oaf, oab, obp, gates, ga, wpa, wpb, *, tm):
    n_tok, d = oaf.shape
    tok = lambda off: pl.BlockSpec((tm, d), lambda i: (i, off))
    const = lambda shape: pl.BlockSpec(shape, lambda i: (0, 0), pipeline_mode=pl.Buffered(1))
    return pl.pallas_call(
        _merge_kernel,
        grid=(n_tok // tm,),
        in_specs=[tok(0), tok(0), tok(0), tok(0), tok(2), tok(3),
                  const((1, d)), const(wpa.shape), const(wpb.shape)],
        out_specs=pl.BlockSpec((tm, wpa.shape[1]), lambda i: (i, 0)),
        out_shape=jax.ShapeDtypeStruct((n_tok, wpa.shape[1]), BF16),
        compiler_params=pltpu.CompilerParams(
            dimension_semantics=("arbitrary",), vmem_limit_bytes=VMEM_LIMIT_BYTES),
        name="merge_proj",
    )(oaf, oab, obp, gates, gates, gates, ga, wpa, wpb)


def _final_kernel(y_ref, x_ref, mod_ref, w_ref, g_ref, o_ref):
    yo = jnp.dot(y_ref[...], w_ref[...], preferred_element_type=F32)
    z = x_ref[...] + mod_ref[0, 2:3, :] * yo
    ms = jnp.mean(z * z, axis=-1, keepdims=True)
    o_ref[...] = z * lax.rsqrt(ms + EPS) * g_ref[...]


def _final(y, x2d, mod3, w_out, fg, *, tm, tiles_per_batch):
    n_tok, d = x2d.shape
    const = lambda shape: pl.BlockSpec(shape, lambda i: (0, 0), pipeline_mode=pl.Buffered(1))
    return pl.pallas_call(
        _final_kernel,
        grid=(n_tok // tm,),
        in_specs=[pl.BlockSpec((tm, d), lambda i: (i, 0)),
                  pl.BlockSpec((tm, d), lambda i: (i, 0)),
                  pl.BlockSpec((1, 3, d), lambda i: (i // tiles_per_batch, 0, 0)),
                  const(w_out.shape), const((1, d))],
        out_specs=pl.BlockSpec((tm, d), lambda i: (i, 0)),
        out_shape=jax.ShapeDtypeStruct((n_tok, d), F32),
        compiler_params=pltpu.CompilerParams(
            dimension_semantics=("arbitrary",), vmem_limit_bytes=VMEM_LIMIT_BYTES),
        name="out_proj_final",
    )(y, x2d, mod3, w_out, fg)


def kernel(x, c, ctx, c_ctx, w_ada, b_ada, norm_g, w_in, hgrn_lb_logits, gla_w_gk, gla_b_gk,
           hgrn_onorm_g, gla_onorm_g, w_pa, w_pb, w_out, final_norm_g):
    bsz, seq, d = x.shape
    n_ctx = ctx.shape[1]
    depth = w_in.shape[0]
    assert depth == 1, "single-layer trunk"
    a_kw, a_vw = A_HEADS * A_DK, A_HEADS * A_DV
    b_kw, b_vw = B_HEADS * B_DK, B_HEADS * B_DV

    n_rows = -(-(bsz + 1) // SUBLANES) * SUBLANES
    cvec = jnp.zeros((n_rows, d), F32).at[:bsz].set(c).at[bsz].set(c_ctx)
    mod = _adaln(cvec, w_ada[0], b_ada[0].reshape(1, -1))
    mod3 = mod.reshape(n_rows, 3, d)

    lb = _lower_bounds(hgrn_lb_logits)[0:1]

    o_ag = 2 * a_kw + 2 * a_vw
    o_bq = o_ag + a_vw
    o_br = o_bq + 2 * b_kw + b_vw
    w_main, w_r = _prep_weights(
        jnp.swapaxes(w_in[0], 0, 1), [(0, o_ag), (o_bq, o_br - o_bq), (o_ag, o_bq - o_ag)],
        tail_start=o_br)
    gk_pad = jnp.zeros((LANES, 2 * b_kw), F32)
    gk_pad = gk_pad.at[0:B_RANK, 0:b_kw].set(gla_w_gk[0, 0]).at[B_RANK:2 * B_RANK, b_kw:].set(gla_w_gk[0, 1])
    gk_pad = gk_pad.astype(BF16)
    gkb = gla_b_gk[0].reshape(1, 2 * b_kw)
    ng = norm_g[0].reshape(1, d)

    x2d = x.reshape(bsz * seq, d)
    ctx2d = ctx.reshape(bsz * n_ctx, d)
    tm = 1024
    tiles_per_batch = seq // tm

    c_aqi, c_alf, c_bqkv, c_blf = _inproj(
        ctx2d, mod3, lambda i: bsz, ng, w_main, w_r, lb, gk_pad, gkb,
        tm=bsz * n_ctx, latent=False)
    aqi, alf, bqkv, blf, gates = _inproj(
        x2d, mod3, lambda i: i // tiles_per_batch, ng, w_main, w_r, lb, gk_pad, gkb,
        tm=tm, latent=True, n_batch=bsz)

    r3 = lambda t, n: t.reshape(bsz, n, t.shape[-1])
    oaf, oab = _scan_a(r3(c_aqi, n_ctx), r3(c_alf, n_ctx), r3(aqi, seq), r3(alf, seq))
    obf, obb = _scan_b(r3(c_bqkv, n_ctx), r3(c_blf, n_ctx), bqkv, blf)
    obp = _gla_finalize(obf, obb, gates, gla_onorm_g[0].reshape(1, -1), sg_col_off=a_vw // B_DV)

    r2 = lambda t: t.reshape(bsz * seq, t.shape[-1])
    y = _merge(r2(oaf), r2(oab), obp, gates, hgrn_onorm_g[0].reshape(1, -1),
               w_pa[0].astype(BF16), w_pb[0].astype(BF16), tm=512)
    out = _final(y, x2d, mod3, w_out[0].astype(BF16), final_norm_g.reshape(1, d),
                 tm=512, tiles_per_batch=seq // 512)
    return out.reshape(bsz, seq, d)
```

```python
import functools

import jax
import jax.numpy as jnp
from jax import lax
from jax.experimental import pallas as pl
from jax.experimental.pallas import tpu as pltpu

F32 = jnp.float32
BF16 = jnp.bfloat16

CHUNK = 64
SUB = 16
GRID_W = 64
EPS = 1e-6
A_HEADS, A_DK, A_DV = 16, 128, 128
B_HEADS, B_DK, B_DV = 4, 256, 512
B_RANK = 16
GATE_NORMALIZER = 16.0
LOG2E = 1.4426950408889634

VMEM_LIMIT_BYTES = 56 * 1024 * 1024
LANES = 128
SUBLANES = 8

NT_DIMS = (((1,), (1,)), ((), ()))
TN_DIMS = (((0,), (0,)), ((), ()))


def _sigmoid(z):
    return 1.0 / (1.0 + jnp.exp(-z))


def _adaln_kernel(c_ref, w_ref, b_ref, o_ref):
    c = c_ref[...]
    s = c * _sigmoid(c)
    o_ref[...] = jnp.dot(s, w_ref[...], preferred_element_type=F32,
                         precision=lax.Precision.HIGHEST) + b_ref[...]


def _adaln(cvec, w, b):
    rows, d = cvec.shape
    n = w.shape[1]
    tn = 768
    return pl.pallas_call(
        _adaln_kernel,
        grid=(n // tn,),
        in_specs=[pl.BlockSpec((rows, d), lambda j: (0, 0)),
                  pl.BlockSpec((d, tn), lambda j: (0, j)),
                  pl.BlockSpec((1, tn), lambda j: (0, j))],
        out_specs=pl.BlockSpec((rows, tn), lambda j: (0, j)),
        out_shape=jax.ShapeDtypeStruct((rows, n), F32),
        compiler_params=pltpu.CompilerParams(
            dimension_semantics=("arbitrary",), vmem_limit_bytes=32 * 1024 * 1024),
        name="adaln",
    )(cvec, w, b)


def _lower_bound_kernel(l_ref, o_ref):
    x = l_ref[...]
    n_rows = x.shape[0]
    m = jnp.max(x, axis=0, keepdims=True)
    e = jnp.exp(x - m)
    tot = jnp.sum(e, axis=0, keepdims=True)
    run = jnp.zeros_like(tot)
    for r in range(n_rows - 1):
        run = run + e[r:r + 1]
        o_ref[r:r + 1, :] = run / tot


def _lower_bounds(logits):
    n_rows, w = logits.shape
    return pl.pallas_call(
        _lower_bound_kernel,
        out_shape=jax.ShapeDtypeStruct((n_rows - 1, w), F32),
        name="hgrn_lower_bounds",
    )(logits)


def _prep_weights_kernel(wa_ref, wb_ref, o_ref, r_ref, *, j_tail, shift):
    j = pl.program_id(0)

    @pl.when(j < j_tail)
    def _():
        o_ref[...] = wa_ref[...].T.astype(BF16)

    @pl.when(j >= j_tail)
    def _():
        o_ref[...] = jnp.concatenate([wa_ref[shift:, :], wb_ref[...]], axis=0).T.astype(BF16)

    @pl.when(j == j_tail)
    def _():
        rank_rows = jnp.concatenate(
            [wa_ref[:shift, :], jnp.zeros((LANES - shift, wa_ref.shape[1]), F32)], axis=0)
        r_ref[...] = rank_rows.T.astype(BF16)


def _prep_weights(wt, group_starts, tail_start):
    n_in, d = wt.shape
    shift = 2 * B_RANK
    src_tiles = []
    for start, width in group_starts:
        assert start % TN == 0 and width % TN == 0
        src_tiles += list(range(start // TN, (start + width) // TN))
    assert tail_start % TN == 0
    j_tail = len(src_tiles)
    n_tail = (n_in - tail_start - shift) // TN
    n_out = j_tail + n_tail
    assert tail_start + shift + n_tail * TN == n_in and TN % shift == 0

    def src_a(j):
        idx = j - j_tail + tail_start // TN
        for dst in range(j_tail - 1, -1, -1):
            idx = jnp.where(j == dst, src_tiles[dst], idx)
        return idx

    def src_b(j):
        return (jnp.maximum(src_a(j), tail_start // TN) + 1) * (TN // shift)

    return pl.pallas_call(
        functools.partial(_prep_weights_kernel, j_tail=j_tail, shift=shift),
        grid=(n_out,),
        in_specs=[pl.BlockSpec((TN, d), lambda j: (src_a(j), 0)),
                  pl.BlockSpec((shift, d), lambda j: (src_b(j), 0))],
        out_specs=[pl.BlockSpec((d, TN), lambda j: (0, j)),
                   pl.BlockSpec((d, LANES), lambda j: (0, 0))],
        out_shape=[jax.ShapeDtypeStruct((d, n_out * TN), BF16),
                   jax.ShapeDtypeStruct((d, LANES), BF16)],
        compiler_params=pltpu.CompilerParams(
            dimension_semantics=("arbitrary",), vmem_limit_bytes=VMEM_LIMIT_BYTES),
        name="prep_weights",
    )(wt, wt)


TN = 1024
EPILOGUE_PARTS = 8
T_AQI = 4096 // TN
T_ALF = 4096 // TN
T_BQKV = 4096 // TN
T_GATES = 8192 // TN
J_ALF = T_AQI
J_BQKV = J_ALF + T_ALF
J_GATES = J_BQKV + T_BQKV
J_END = J_GATES + T_GATES
BLF_W = 2048 // T_BQKV


def _inproj_kernel(x_ref, mod_ref, ng_ref, w_ref, wr_ref, lb_ref, gkw_ref, gkb_ref, *rest, latent):
    if latent:
        aqi_ref, alf_ref, bqkv_ref, blf_ref, gates_ref, h_ref, r_ref = rest
    else:
        aqi_ref, alf_ref, bqkv_ref, blf_ref, h_ref, r_ref = rest
        gates_ref = None
    j = pl.program_id(1)

    def store_gla(out_ref, z):
        if not latent:
            out_ref[...] = z.astype(out_ref.dtype)
            return
        n_rows = z.shape[0] // GRID_W
        zc = jnp.swapaxes(z.reshape(n_rows, GRID_W, z.shape[1]), 0, 1)
        out_ref[...] = zc.astype(out_ref.dtype)

    @pl.when(j == 0)
    def _():
        part = x_ref.shape[0] // EPILOGUE_PARTS
        for m in range(EPILOGUE_PARTS):
            rows = slice(m * part, (m + 1) * part)
            x = x_ref[rows, :]
            ms = jnp.mean(x * x, axis=-1, keepdims=True)
            y = x * lax.rsqrt(ms + EPS) * ng_ref[...]
            h = (y * (1.0 + mod_ref[0, 1:2, :]) + mod_ref[0, 0:1, :]).astype(BF16)
            h_ref[rows, :] = h
            r_ref[rows, :] = jnp.dot(h, wr_ref[...], preferred_element_type=F32).astype(BF16)
            aqi_ref[rows, :] = jnp.dot(h, w_ref[...], preferred_element_type=F32).astype(BF16)

    def mm():
        return jnp.dot(h_ref[...], w_ref[...], preferred_element_type=F32)

    def mm_parts(out_ref, epilogue):
        part = h_ref.shape[0] // EPILOGUE_PARTS
        for m in range(EPILOGUE_PARTS):
            rows = slice(m * part, (m + 1) * part)
            z = jnp.dot(h_ref[rows, :], w_ref[...], preferred_element_type=F32)
            out_ref[rows, :] = epilogue(z).astype(out_ref.dtype)

    @pl.when((j > 0) & (j < J_ALF))
    def _():
        mm_parts(aqi_ref, lambda z: z)

    @pl.when((j >= J_ALF) & (j < J_BQKV))
    def _():
        lb = lb_ref[...]
        mm_parts(alf_ref, lambda z: jnp.log2(lb + (1.0 - lb) * _sigmoid(z)))

    @pl.when((j >= J_BQKV) & (j < J_GATES))
    def _():
        z = jnp.dot(r_ref[...], gkw_ref[...], preferred_element_type=F32) + gkb_ref[...]
        ls = jnp.minimum(z, 0.0) - jnp.log(1.0 + jnp.exp(-jnp.abs(z)))
        store_gla(blf_ref, ls * (LOG2E / GATE_NORMALIZER))
        scale = jnp.where(j < J_BQKV + (B_HEADS * B_DK) // TN, B_DK ** -0.5, 1.0)
        store_gla(bqkv_ref, mm() * scale)

    if latent:
        @pl.when(j >= J_GATES)
        def _():
            silu = j < J_GATES + T_GATES // 2
            mm_parts(gates_ref, lambda z: _sigmoid(z) * jnp.where(silu, z, 1.0))


def _inproj(x2d, mod3, mod_row_of_tile, norm_g, w_main, w_r, lb, gk_pad, gkb, *, tm, latent, n_batch=1):
    n_tok, d = x2d.shape
    n_j = J_END if latent else J_GATES
    tile_rows = tm // GRID_W
    seq_rows = n_tok // n_batch // GRID_W
    tiles_per_batch = n_tok // n_batch // tm

    def cl(j, lo, n):
        return jnp.clip(j - lo, 0, n - 1)

    def gla_spec(width):
        col = lambda j: cl(j, J_BQKV, T_BQKV)
        if not latent:
            return pl.BlockSpec((tm, width), lambda i, j: (i, col(j)))
        return pl.BlockSpec((None, GRID_W, tile_rows, width),
                            lambda i, j: (i // tiles_per_batch, 0, i % tiles_per_batch, col(j)))

    def gla_shape(width, dtype):
        if not latent:
            return jax.ShapeDtypeStruct((n_tok, T_BQKV * width), dtype)
        return jax.ShapeDtypeStruct((n_batch, GRID_W, seq_rows, T_BQKV * width), dtype)

    in_specs = [
        pl.BlockSpec((tm, d), lambda i, j: (i, 0), pipeline_mode=pl.Buffered(1)),
        pl.BlockSpec((1, 3, d), lambda i, j: (mod_row_of_tile(i), 0, 0)),
        pl.BlockSpec((1, d), lambda i, j: (0, 0)),
        pl.BlockSpec((d, TN), lambda i, j: (0, j)),
        pl.BlockSpec((d, LANES), lambda i, j: (0, 0)),
        pl.BlockSpec((1, TN), lambda i, j: (0, cl(j, J_ALF, T_ALF))),
        pl.BlockSpec((LANES, BLF_W), lambda i, j: (0, cl(j, J_BQKV, T_BQKV))),
        pl.BlockSpec((1, BLF_W), lambda i, j: (0, cl(j, J_BQKV, T_BQKV))),
    ]
    out_specs = [
        pl.BlockSpec((tm, TN), lambda i, j: (i, cl(j, 0, T_AQI))),
        pl.BlockSpec((tm, TN), lambda i, j: (i, cl(j, J_ALF, T_ALF))),
        gla_spec(TN),
        gla_spec(BLF_W),
    ]
    out_shape = [
        jax.ShapeDtypeStruct((n_tok, T_AQI * TN), BF16),
        jax.ShapeDtypeStruct((n_tok, T_ALF * TN), F32),
        gla_shape(TN, BF16),
        gla_shape(BLF_W, F32),
    ]
    if latent:
        out_specs.append(pl.BlockSpec((tm, TN), lambda i, j: (i, cl(j, J_GATES, T_GATES))))
        out_shape.append(jax.ShapeDtypeStruct((n_tok, T_GATES * TN), BF16))
    return pl.pallas_call(
        functools.partial(_inproj_kernel, latent=latent),
        grid=(n_tok // tm, n_j),
        in_specs=in_specs,
        out_specs=out_specs,
        out_shape=out_shape,
        scratch_shapes=[pltpu.VMEM((tm, d), BF16), pltpu.VMEM((tm, LANES), BF16)],
        compiler_params=pltpu.CompilerParams(
            dimension_semantics=("arbitrary", "arbitrary"), vmem_limit_bytes=VMEM_LIMIT_BYTES),
        name="inproj_latent" if latent else "inproj_ctx",
    )(x2d, mod3, norm_g, w_main, w_r, lb, gk_pad, gkb)


def _scan_consts():
    row = lax.broadcasted_iota(jnp.int32, (CHUNK, CHUNK), 0)
    col = lax.broadcasted_iota(jnp.int32, (CHUNK, CHUNK), 1)
    return dict(
        mask_f=col <= row,
        mask_b=col >= row,
        tri_f=jnp.where(col <= row, 1.0, 0.0).astype(BF16),
        tri_b=jnp.where(col >= row, 1.0, 0.0).astype(BF16),
        lane=lax.broadcasted_iota(jnp.int32, (SUBLANES, CHUNK), 1),
    )


DIAG_FACTOR_MAX_LOG2 = 96.0
FOLDED_SUB = 32


def _cumsum_stage(lf, has_k, fwd, cst):
    dk = lf.shape[1]
    tri = cst["tri_f"] if fwd else cst["tri_b"]
    hi = lf.astype(BF16)
    lo = (lf - hi.astype(F32)).astype(BF16)
    cc = jnp.dot(tri, jnp.concatenate([hi, lo], axis=1), preferred_element_type=F32)
    c = cc[:, :dk] + cc[:, dk:]
    a = None if has_k else c - jnp.log2(jnp.maximum(1.0 - jnp.exp2(lf), 0.0))
    mag = jnp.abs(lf)
    excess = None
    for r0 in range(0, CHUNK, FOLDED_SUB):
        e = jnp.sum(mag[r0:r0 + FOLDED_SUB], axis=0, keepdims=True)
        excess = e if excess is None else jnp.maximum(excess, e)
    return c, a, excess


def _intra_stage(q, k, v, c, a, st_ref, fwd, cst, keep, exact_diag=True):
    dk = c.shape[1]
    tot = c[CHUNK - 1:CHUNK] if fwd else c[0:1]

    if k is None:
        kf = None

        def kscaled(r0, r1, ref):
            return jnp.exp2(ref - a[r0:r1])
    else:
        a = c
        kf = k.astype(F32)

        def kscaled(r0, r1, ref):
            return kf[r0:r1] * jnp.exp2(ref - c[r0:r1])

    kte = kscaled(0, CHUNK, tot).astype(BF16)
    st = st_ref[...]
    if keep is not None:
        keep.prev_state(st.astype(BF16))
    st_ref[...] = st * jnp.exp2(tot) + lax.dot_general(v, kte, TN_DIMS, preferred_element_type=F32)
    if keep is None:
        return
    yield

    qf = q.astype(F32)
    keep.scaled_q((qf * jnp.exp2(c)).astype(BF16))
    half = SUB // SUBLANES
    sub, own = (SUB, 0) if exact_diag else (FOLDED_SUB, FOLDED_SUB)
    soffs = []
    for r0 in range(0, CHUNK, sub):
        if fwd:
            lo_row, hi_row = 0, r0 + own
            ref = c[r0:r0 + 1]
        else:
            lo_row, hi_row = r0 + sub - own, CHUNK
            ref = c[r0 + sub - 1:r0 + sub]
        if hi_row == lo_row:
            soffs.append(None)
            continue
        pieces = [kscaled(lo_row, hi_row, ref).astype(BF16)]
        if lo_row > 0:
            pieces.insert(0, jnp.zeros((lo_row, dk), BF16))
        if hi_row < CHUNK:
            pieces.append(jnp.zeros((CHUNK - hi_row, dk), BF16))
        kr = jnp.concatenate(pieces, axis=0) if len(pieces) > 1 else pieces[0]
        qr = (qf[r0:r0 + sub] * jnp.exp2(c[r0:r0 + sub] - ref)).astype(BF16)
        soffs.append(lax.dot_general(qr, kr, NT_DIMS, preferred_element_type=F32))
    yield

    if not exact_diag:
        p = jnp.concatenate(soffs, axis=0)
        keep.scores(jnp.where(cst["mask_f"] if fwd else cst["mask_b"], p, 0.0).astype(BF16))
        return

    rows = []
    for r, r0 in enumerate(range(0, CHUNK, SUB)):
        cb = c[r0:r0 + SUB]
        qb = qf[r0:r0 + SUB]
        acc = [jnp.zeros((SUBLANES, CHUNK), F32) for _ in range(half)]
        for jj in range(SUB):
            aj = a[r0 + jj:r0 + jj + 1]
            jg = jj // SUBLANES
            groups = range(jg, half) if fwd else range(0, jg + 1)
            for g in groups:
                e = cb[g * SUBLANES:(g + 1) * SUBLANES] - aj
                if g == jg:
                    e = jnp.minimum(e, 0.0)
                t = qb[g * SUBLANES:(g + 1) * SUBLANES] * jnp.exp2(e)
                if kf is not None:
                    t = t * kf[r0 + jj:r0 + jj + 1]
                sj = jnp.sum(t, axis=1, keepdims=True)
                acc[g] = jnp.where(cst["lane"] == r0 + jj, sj, acc[g])
        d = jnp.concatenate(acc, axis=0)
        rows.append(d if soffs[r] is None else d + soffs[r])
    p = jnp.concatenate(rows, axis=0)
    keep.scores(jnp.where(cst["mask_f"] if fwd else cst["mask_b"], p, 0.0).astype(BF16))


def _run_staged(units):
    live = list(units)
    while live:
        nxt = []
        for u in live:
            try:
                next(u)
                nxt.append(u)
            except StopIteration:
                pass
        live = nxt


def _scan_kernel(*refs, n_heads, dk, dv, has_k, n_ctx_chunks, n_step_chunks, lat_index):
    n_ctx_in = 4 if has_k else 3
    n_lat_in = 4 if has_k else 3
    ctx_refs = refs[:n_ctx_in]
    lat_refs = (refs[n_ctx_in:n_ctx_in + n_lat_in], refs[n_ctx_in + n_lat_in:n_ctx_in + 2 * n_lat_in])
    rest = refs[n_ctx_in + 2 * n_lat_in:]
    out_refs = rest[0:2]
    if has_k:
        st_ref, c_ref, stb_ref = rest[2:]
        a_ref = None
    else:
        st_ref, c_ref, stb_ref, a_ref = rest[2:]
    s = pl.program_id(2)
    cst = _scan_consts()
    units = [(h, d) for h in range(n_heads) for d in range(2)]

    def kcols(h):
        return slice(h * dk, (h + 1) * dk)

    def vcols(h):
        return slice(h * dv, (h + 1) * dv)

    def ctx_step():
        st_ref[...] = jnp.zeros(st_ref.shape, F32)
        if has_k:
            k_ref, v_ref, lff_ref, lfb_ref = ctx_refs
        else:
            v_ref, lff_ref, lfb_ref = ctx_refs
            k_ref = None

        def unit(h, d, rows):
            lf = (lff_ref, lfb_ref)[d][rows, kcols(h)]
            c, a, _ = _cumsum_stage(lf, has_k, d == 0, cst)
            yield
            k = None if k_ref is None else k_ref[rows, kcols(h)]
            yield from _intra_stage(None, k, v_ref[rows, vcols(h)], c, a, st_ref.at[d, h], d == 0, cst, None)

        def body(i, carry):
            rows = (pl.ds(pl.multiple_of(i * CHUNK, CHUNK), CHUNK),
                    pl.ds(pl.multiple_of((n_ctx_chunks - 1 - i) * CHUNK, CHUNK), CHUNK))
            _run_staged([unit(h, d, rows[d]) for h, d in units])
            return carry
        lax.fori_loop(0, n_ctx_chunks, body, 0)

    def latent_step():
        n = n_step_chunks

        def index(d, pos):
            return lat_index(pos if d == 0 else n - 1 - pos)

        def read(d, pos, h, what):
            refs_d = lat_refs[d]
            ref = refs_d[{"q": 0, "k": 1, "v": 2 if has_k else 1, "lf": 3 if has_k else 2}[what]]
            cols = vcols(h) if what == "v" else kcols(h)
            return ref[index(d, pos) + (cols,)]

        def cumsum_all(pos, slot):
            worst = None
            for u, (h, d) in enumerate(units):
                c, a, excess = _cumsum_stage(read(d, pos, h, "lf"), has_k, d == 0, cst)
                c_ref[slot, u] = c
                if a is not None:
                    a_ref[slot, u] = a
                worst = excess if worst is None else jnp.maximum(worst, excess)
            return jnp.max(worst)

        def output_all(pos, carried):
            for u, (h, d) in enumerate(units):
                p, qd = carried[u]
                o = jnp.dot(p, read(d, pos, h, "v"), preferred_element_type=F32)
                o = o + lax.dot_general(qd, stb_ref[u], NT_DIMS, preferred_element_type=F32)
                out_refs[d][index(d, pos) + (vcols(h),)] = o.astype(out_refs[d].dtype)

        class Keeper:
            def __init__(self, u):
                self.u = u
                self.p = self.qd = None

            def prev_state(self, stb):
                stb_ref[self.u] = stb

            def scaled_q(self, qd):
                self.qd = qd

            def scores(self, p):
                self.p = p

        def intra_all(pos, cs, exact_diag):
            keepers = [Keeper(u) for u in range(len(units))]
            gens = []
            for u, (h, d) in enumerate(units):
                k = read(d, pos, h, "k") if has_k else None
                c, a = cs[u]
                gens.append(_intra_stage(read(d, pos, h, "q"), k, read(d, pos, h, "v"), c, a,
                                         st_ref.at[d, h], d == 0, cst, keepers[u], exact_diag))
            _run_staged(gens)
            return tuple((kp.p, kp.qd) for kp in keepers)

        def load_cumsums(slot):
            return [(c_ref[slot, u], None if has_k else a_ref[slot, u]) for u in range(len(units))]

        def either_path(worst, build):
            return lax.cond(worst <= DIAG_FACTOR_MAX_LOG2,
                            functools.partial(build, False), functools.partial(build, True))

        worst0 = cumsum_all(0, 0)
        cs0 = load_cumsums(0)
        worst1 = cumsum_all(1, 1)
        carried0 = either_path(worst0, lambda exact_diag: intra_all(0, cs0, exact_diag))

        def body(i, state):
            worst, carried = state
            slot = lax.rem(i, 2)

            def iteration(exact_diag):
                cs = load_cumsums(slot)
                output_all(i - 1, carried)
                worst_next = cumsum_all(jnp.minimum(i + 1, n - 1), 1 - slot)
                return worst_next, intra_all(i, cs, exact_diag)
            return either_path(worst, iteration)
        _, carried_last = lax.fori_loop(1, n, body, (worst1, carried0))
        output_all(n - 1, carried_last)

    @pl.when(s == 0)
    def _():
        ctx_step()

    @pl.when(s > 0)
    def _():
        latent_step()


def _scan_call(inputs, in_specs, out_struct, out_specs, grid, *, n_heads, dk, dv, has_k,
               n_ctx_chunks, n_step_chunks, lat_index, name):
    n_units = 2 * n_heads
    scratch = [pltpu.VMEM((2, n_heads, dv, dk), F32),
               pltpu.VMEM((2, n_units, CHUNK, dk), F32),
               pltpu.VMEM((n_units, dv, dk), BF16)]
    if not has_k:
        scratch.append(pltpu.VMEM((2, n_units, CHUNK, dk), F32))
    return pl.pallas_call(
        functools.partial(_scan_kernel, n_heads=n_heads, dk=dk, dv=dv, has_k=has_k,
                          n_ctx_chunks=n_ctx_chunks, n_step_chunks=n_step_chunks, lat_index=lat_index),
        grid=grid,
        in_specs=in_specs,
        out_specs=out_specs,
        out_shape=out_struct,
        scratch_shapes=scratch,
        compiler_params=pltpu.CompilerParams(
            dimension_semantics=("arbitrary", "arbitrary", "arbitrary"),
            vmem_limit_bytes=VMEM_LIMIT_BYTES),
        name=name,
    )(*inputs)


A_SCAN_HEADS = 2
A_SCAN_CHUNKS = 32


def _scan_a(c_qi, c_lf, qi, lf):
    bsz, n_ctx, _ = c_qi.shape
    seq = qi.shape[1]
    g = A_SCAN_HEADS
    gw = g * A_DK
    n_hg = A_HEADS // g
    ts = A_SCAN_CHUNKS * CHUNK
    n_steps = seq // ts

    def fstep(s):
        return jnp.maximum(s - 1, 0)

    def bstep(s):
        return n_steps - 1 - jnp.maximum(s - 1, 0)

    cspec = lambda off: pl.BlockSpec((None, n_ctx, gw), lambda b, h, s: (b, 0, off + h))
    fspec = lambda off: pl.BlockSpec((None, ts, gw), lambda b, h, s: (b, fstep(s), off + h))
    bspec = lambda off: pl.BlockSpec((None, ts, gw), lambda b, h, s: (b, bstep(s), off + h))
    in_specs = [cspec(n_hg), cspec(0), cspec(n_hg),
                fspec(0), fspec(n_hg), fspec(0),
                bspec(0), bspec(n_hg), bspec(n_hg)]
    inputs = [c_qi, c_lf, c_lf, qi, qi, lf, qi, qi, lf]
    out_struct = [jax.ShapeDtypeStruct((bsz, seq, A_HEADS * A_DV), BF16)] * 2
    out_specs = [pl.BlockSpec((None, ts, gw), lambda b, h, s: (b, fstep(s), h)),
                 pl.BlockSpec((None, ts, gw), lambda b, h, s: (b, bstep(s), h))]

    def lat_index(chunk):
        row = chunk * CHUNK
        if not isinstance(row, int):
            row = pl.multiple_of(row, CHUNK)
        return (pl.ds(row, CHUNK),)

    return _scan_call(inputs, in_specs, out_struct, out_specs, (bsz, n_hg, n_steps + 1),
                      n_heads=g, dk=A_DK, dv=A_DV, has_k=False,
                      n_ctx_chunks=n_ctx // CHUNK, n_step_chunks=A_SCAN_CHUNKS,
                      lat_index=lat_index, name="scan_hgrn2")


B_SCAN_COLS = 16


def _scan_b(c_qkv, c_lf, qkv, lf):
    bsz, n_ctx, _ = c_qkv.shape
    n_rows = qkv.shape[2]
    kw = B_HEADS * B_DK
    nc = B_SCAN_COLS
    n_steps = GRID_W // nc
    chunks_per_col = n_rows // CHUNK

    def fblk(s):
        return jnp.maximum(s - 1, 0)

    def bblk(s):
        return n_steps - 1 - jnp.maximum(s - 1, 0)

    def cspec(width, off):
        return pl.BlockSpec((None, n_ctx, width), lambda b, h, s: (b, 0, off + h))

    def lspec(width, off, blk):
        return pl.BlockSpec((None, nc, n_rows, width), lambda b, h, s: (b, blk(s), 0, off + h))

    k_off, v_off = kw // B_DK, (2 * kw) // B_DV
    in_specs = [cspec(B_DK, k_off), cspec(B_DV, v_off), cspec(B_DK, 0), cspec(B_DK, B_HEADS),
                lspec(B_DK, 0, fblk), lspec(B_DK, k_off, fblk), lspec(B_DV, v_off, fblk), lspec(B_DK, 0, fblk),
                lspec(B_DK, 0, bblk), lspec(B_DK, k_off, bblk), lspec(B_DV, v_off, bblk),
                lspec(B_DK, B_HEADS, bblk)]
    inputs = [c_qkv, c_qkv, c_lf, c_lf, qkv, qkv, qkv, lf, qkv, qkv, qkv, lf]
    vw = B_HEADS * B_DV
    out_struct = [jax.ShapeDtypeStruct((bsz, GRID_W, n_rows, vw), BF16)] * 2
    out_specs = [pl.BlockSpec((None, nc, n_rows, B_DV), lambda b, h, s: (b, fblk(s), 0, h)),
                 pl.BlockSpec((None, nc, n_rows, B_DV), lambda b, h, s: (b, bblk(s), 0, h))]

    def lat_index(chunk):
        col = chunk // chunks_per_col
        row = (chunk - col * chunks_per_col) * CHUNK
        if not isinstance(row, int):
            row = pl.multiple_of(row, CHUNK)
        return (col, pl.ds(row, CHUNK))

    return _scan_call(inputs, in_specs, out_struct, out_specs, (bsz, B_HEADS, n_steps + 1),
                      n_heads=1, dk=B_DK, dv=B_DV, has_k=True,
                      n_ctx_chunks=n_ctx // CHUNK, n_step_chunks=nc * chunks_per_col,
                      lat_index=lat_index, name="scan_gla")


GLA_FIN_ROWS = 16


def _gla_finalize_kernel(of_ref, ob_ref, sg_ref, g_ref, o_ref):
    n_cols, n_rows, w = of_ref.shape
    o = of_ref[...].astype(F32) + ob_ref[...].astype(F32)
    ms = jnp.mean(o * o, axis=-1, keepdims=True)
    on = o * lax.rsqrt(ms + EPS) * g_ref[...]
    on = jnp.swapaxes(on, 0, 1).reshape(n_rows * n_cols, w)
    o_ref[...] = (on * sg_ref[...].astype(F32)).astype(o_ref.dtype)


def _gla_finalize(of, ob, gates, gain, *, sg_col_off):
    bsz, _, n_rows, vw = of.shape
    tm = GLA_FIN_ROWS * GRID_W
    tiles = n_rows // GLA_FIN_ROWS
    cm = pl.BlockSpec((None, GRID_W, GLA_FIN_ROWS, B_DV), lambda b, r, h: (b, 0, r, h))
    return pl.pallas_call(
        _gla_finalize_kernel,
        grid=(bsz, tiles, B_HEADS),
        in_specs=[cm, cm,
                  pl.BlockSpec((tm, B_DV), lambda b, r, h: (b * tiles + r, sg_col_off + h)),
                  pl.BlockSpec((1, B_DV), lambda b, r, h: (0, h))],
        out_specs=pl.BlockSpec((tm, B_DV), lambda b, r, h: (b * tiles + r, h)),
        out_shape=jax.ShapeDtypeStruct((bsz * n_rows * GRID_W, vw), BF16),
        compiler_params=pltpu.CompilerParams(
            dimension_semantics=("arbitrary", "arbitrary", "arbitrary"),
            vmem_limit_bytes=32 * 1024 * 1024),
        name="gla_finalize",
    )(of, ob, gates, gain)


def _merge_kernel(oaf, oab, obp, sga, sma, smb, ga, wpa, wpb, y_ref):
    o = oaf[...].astype(F32) + oab[...].astype(F32)
    parts = []
    for h in range(A_HEADS):
        oh = o[:, h * A_DV:(h + 1) * A_DV]
        ms = jnp.mean(oh * oh, axis=-1, keepdims=True)
        parts.append(oh * lax.rsqrt(ms + EPS))
    oa = (jnp.concatenate(parts, axis=1) * ga[...] * sga[...].astype(F32)).astype(BF16)
    ya = jnp.dot(oa, wpa[...], preferred_element_type=F32)
    yb = jnp.dot(obp[...], wpb[...], preferred_element_type=F32)
    y_ref[...] = (sma[...].astype(F32) * ya + smb[...].astype(F32) * yb).astype(BF16)


def _merge(oaf, oab, obp, gates, ga, wpa, wpb, *, tm):
    n_tok, d = oaf.shape
    tok = lambda off: pl.BlockSpec((tm, d), lambda i: (i, off))
    const = lambda shape: pl.BlockSpec(shape, lambda i: (0, 0), pipeline_mode=pl.Buffered(1))
    return pl.pallas_call(
        _merge_kernel,
        grid=(n_tok // tm,),
        in_specs=[tok(0), tok(0), tok(0), tok(0), tok(2), tok(3),
                  const((1, d)), const(wpa.shape), const(wpb.shape)],
        out_specs=pl.BlockSpec((tm, wpa.shape[1]), lambda i: (i, 0)),
        out_shape=jax.ShapeDtypeStruct((n_tok, wpa.shape[1]), BF16),
        compiler_params=pltpu.CompilerParams(
            dimension_semantics=("arbitrary",), vmem_limit_bytes=VMEM_LIMIT_BYTES),
        name="merge_proj",
    )(oaf, oab, obp, gates, gates, gates, ga, wpa, wpb)


def _final_kernel(y_ref, x_ref, mod_ref, w_ref, g_ref, o_ref):
    yo = jnp.dot(y_ref[...], w_ref[...], preferred_element_type=F32)
    z = x_ref[...] + mod_ref[0, 2:3, :] * yo
    ms = jnp.mean(z * z, axis=-1, keepdims=True)
    o_ref[...] = z * lax.rsqrt(ms + EPS) * g_ref[...]


def _final(y, x2d, mod3, w_out, fg, *, tm, tiles_per_batch):
    n_tok, d = x2d.shape
    const = lambda shape: pl.BlockSpec(shape, lambda i: (0, 0), pipeline_mode=pl.Buffered(1))
    return pl.pallas_call(
        _final_kernel,
        grid=(n_tok // tm,),
        in_specs=[pl.BlockSpec((tm, d), lambda i: (i, 0)),
                  pl.BlockSpec((tm, d), lambda i: (i, 0)),
                  pl.BlockSpec((1, 3, d), lambda i: (i // tiles_per_batch, 0, 0)),
                  const(w_out.shape), const((1, d))],
        out_specs=pl.BlockSpec((tm, d), lambda i: (i, 0)),
        out_shape=jax.ShapeDtypeStruct((n_tok, d), F32),
        compiler_params=pltpu.CompilerParams(
            dimension_semantics=("arbitrary",), vmem_limit_bytes=VMEM_LIMIT_BYTES),
        name="out_proj_final",
    )(y, x2d, mod3, w_out, fg)


def kernel(x, c, ctx, c_ctx, w_ada, b_ada, norm_g, w_in, hgrn_lb_logits, gla_w_gk, gla_b_gk,
           hgrn_onorm_g, gla_onorm_g, w_pa, w_pb, w_out, final_norm_g):
    bsz, seq, d = x.shape
    n_ctx = ctx.shape[1]
    depth = w_in.shape[0]
    assert depth == 1, "single-layer trunk"
    a_kw, a_vw = A_HEADS * A_DK, A_HEADS * A_DV
    b_kw, b_vw = B_HEADS * B_DK, B_HEADS * B_DV

    n_rows = -(-(bsz + 1) // SUBLANES) * SUBLANES
    cvec = jnp.zeros((n_rows, d), F32).at[:bsz].set(c).at[bsz].set(c_ctx)
    mod = _adaln(cvec, w_ada[0], b_ada[0].reshape(1, -1))
    mod3 = mod.reshape(n_rows, 3, d)

    lb = _lower_bounds(hgrn_lb_logits)[0:1]

    o_ag = 2 * a_kw + 2 * a_vw
    o_bq = o_ag + a_vw
    o_br = o_bq + 2 * b_kw + b_vw
    w_main, w_r = _prep_weights(
        jnp.swapaxes(w_in[0], 0, 1), [(0, o_ag), (o_bq, o_br - o_bq), (o_ag, o_bq - o_ag)],
        tail_start=o_br)
    gk_pad = jnp.zeros((LANES, 2 * b_kw), F32)
    gk_pad = gk_pad.at[0:B_RANK, 0:b_kw].set(gla_w_gk[0, 0]).at[B_RANK:2 * B_RANK, b_kw:].set(gla_w_gk[0, 1])
    gk_pad = gk_pad.astype(BF16)
    gkb = gla_b_gk[0].reshape(1, 2 * b_kw)
    ng = norm_g[0].reshape(1, d)

    x2d = x.reshape(bsz * seq, d)
    ctx2d = ctx.reshape(bsz * n_ctx, d)
    tm = 1024
    tiles_per_batch = seq // tm

    c_aqi, c_alf, c_bqkv, c_blf = _inproj(
        ctx2d, mod3, lambda i: bsz, ng, w_main, w_r, lb, gk_pad, gkb,
        tm=bsz * n_ctx, latent=False)
    aqi, alf, bqkv, blf, gates = _inproj(
        x2d, mod3, lambda i: i // tiles_per_batch, ng, w_main, w_r, lb, gk_pad, gkb,
        tm=tm, latent=True, n_batch=bsz)

    r3 = lambda t, n: t.reshape(bsz, n, t.shape[-1])
    oaf, oab = _scan_a(r3(c_aqi, n_ctx), r3(c_alf, n_ctx), r3(aqi, seq), r3(alf, seq))
    obf, obb = _scan_b(r3(c_bqkv, n_ctx), r3(c_blf, n_ctx), bqkv, blf)
    obp = _gla_finalize(obf, obb, gates, gla_onorm_g[0].reshape(1, -1), sg_col_off=a_vw // B_DV)

    r2 = lambda t: t.reshape(bsz * seq, t.shape[-1])
    y = _merge(r2(oaf), r2(oab), obp, gates, hgrn_onorm_g[0].reshape(1, -1),
               w_pa[0].astype(BF16), w_pb[0].astype(BF16), tm=512)
    out = _final(y, x2d, mod3, w_out[0].astype(BF16), final_norm_g.reshape(1, d),
                 tm=512, tiles_per_batch=seq // 512)
    return out.reshape(bsz, seq, d)
```

```python
import functools

import jax
import jax.numpy as jnp
from jax import lax
from jax.experimental import pallas as pl
from jax.experimental.pallas import tpu as pltpu

F32 = jnp.float32
BF16 = jnp.bfloat16

CHUNK = 64
SUB = 16
GRID_W = 64
EPS = 1e-6
A_HEADS, A_DK, A_DV = 16, 128, 128
B_HEADS, B_DK, B_DV = 4, 256, 512
B_RANK = 16
GATE_NORMALIZER = 16.0
LOG2E = 1.4426950408889634

VMEM_LIMIT_BYTES = 56 * 1024 * 1024
LANES = 128
SUBLANES = 8

NT_DIMS = (((1,), (1,)), ((), ()))
TN_DIMS = (((0,), (0,)), ((), ()))


def _sigmoid(z):
    return 1.0 / (1.0 + jnp.exp(-z))


def _adaln_kernel(c_ref, w_ref, b_ref, o_ref):
    c = c_ref[...]
    s = c * _sigmoid(c)
    o_ref[...] = jnp.dot(s, w_ref[...], preferred_element_type=F32,
                         precision=lax.Precision.HIGHEST) + b_ref[...]


def _adaln(cvec, w, b):
    rows, d = cvec.shape
    n = w.shape[1]
    tn = 768
    return pl.pallas_call(
        _adaln_kernel,
        grid=(n // tn,),
        in_specs=[pl.BlockSpec((rows, d), lambda j: (0, 0)),
                  pl.BlockSpec((d, tn), lambda j: (0, j)),
                  pl.BlockSpec((1, tn), lambda j: (0, j))],
        out_specs=pl.BlockSpec((rows, tn), lambda j: (0, j)),
        out_shape=jax.ShapeDtypeStruct((rows, n), F32),
        compiler_params=pltpu.CompilerParams(
            dimension_semantics=("arbitrary",), vmem_limit_bytes=32 * 1024 * 1024),
        name="adaln",
    )(cvec, w, b)


def _lower_bound_kernel(l_ref, o_ref):
    x = l_ref[...]
    n_rows = x.shape[0]
    m = jnp.max(x, axis=0, keepdims=True)
    e = jnp.exp(x - m)
    tot = jnp.sum(e, axis=0, keepdims=True)
    run = jnp.zeros_like(tot)
    for r in range(n_rows - 1):
        run = run + e[r:r + 1]
        o_ref[r:r + 1, :] = run / tot


def _lower_bounds(logits):
    n_rows, w = logits.shape
    return pl.pallas_call(
        _lower_bound_kernel,
        out_shape=jax.ShapeDtypeStruct((n_rows - 1, w), F32),
        name="hgrn_lower_bounds",
    )(logits)


def _prep_weights_kernel(wa_ref, wb_ref, o_ref, r_ref, *, j_tail, shift):
    j = pl.program_id(0)

    @pl.when(j < j_tail)
    def _():
        o_ref[...] = wa_ref[...].T.astype(BF16)

    @pl.when(j >= j_tail)
    def _():
        o_ref[...] = jnp.concatenate([wa_ref[shift:, :], wb_ref[...]], axis=0).T.astype(BF16)

    @pl.when(j == j_tail)
    def _():
        rank_rows = jnp.concatenate(
            [wa_ref[:shift, :], jnp.zeros((LANES - shift, wa_ref.shape[1]), F32)], axis=0)
        r_ref[...] = rank_rows.T.astype(BF16)


def _prep_weights(wt, group_starts, tail_start):
    n_in, d = wt.shape
    shift = 2 * B_RANK
    src_tiles = []
    for start, width in group_starts:
        assert start % TN == 0 and width % TN == 0
        src_tiles += list(range(start // TN, (start + width) // TN))
    assert tail_start % TN == 0
    j_tail = len(src_tiles)
    n_tail = (n_in - tail_start - shift) // TN
    n_out = j_tail + n_tail
    assert tail_start + shift + n_tail * TN == n_in and TN % shift == 0

    def src_a(j):
        idx = j - j_tail + tail_start // TN
        for dst in range(j_tail - 1, -1, -1):
            idx = jnp.where(j == dst, src_tiles[dst], idx)
        return idx

    def src_b(j):
        return (jnp.maximum(src_a(j), tail_start // TN) + 1) * (TN // shift)

    return pl.pallas_call(
        functools.partial(_prep_weights_kernel, j_tail=j_tail, shift=shift),
        grid=(n_out,),
        in_specs=[pl.BlockSpec((TN, d), lambda j: (src_a(j), 0)),
                  pl.BlockSpec((shift, d), lambda j: (src_b(j), 0))],
        out_specs=[pl.BlockSpec((d, TN), lambda j: (0, j)),
                   pl.BlockSpec((d, LANES), lambda j: (0, 0))],
        out_shape=[jax.ShapeDtypeStruct((d, n_out * TN), BF16),
                   jax.ShapeDtypeStruct((d, LANES), BF16)],
        compiler_params=pltpu.CompilerParams(
            dimension_semantics=("arbitrary",), vmem_limit_bytes=VMEM_LIMIT_BYTES),
        name="prep_weights",
    )(wt, wt)


TN = 1024
EPILOGUE_PARTS = 8
T_AQI = 4096 // TN
T_ALF = 4096 // TN
T_BQKV = 4096 // TN
T_GATES = 8192 // TN
J_ALF = T_AQI
J_BQKV = J_ALF + T_ALF
J_GATES = J_BQKV + T_BQKV
J_END = J_GATES + T_GATES
BLF_W = 2048 // T_BQKV


def _inproj_kernel(x_ref, mod_ref, ng_ref, w_ref, wr_ref, lb_ref, gkw_ref, gkb_ref, *rest, latent):
    if latent:
        aqi_ref, alf_ref, bqkv_ref, blf_ref, gates_ref, h_ref, r_ref = rest
    else:
        aqi_ref, alf_ref, bqkv_ref, blf_ref, h_ref, r_ref = rest
        gates_ref = None
    j = pl.program_id(1)

    def store_gla(out_ref, z):
        if not latent:
            out_ref[...] = z.astype(out_ref.dtype)
            return
        n_rows = z.shape[0] // GRID_W
        zc = jnp.swapaxes(z.reshape(n_rows, GRID_W, z.shape[1]), 0, 1)
        out_ref[...] = zc.astype(out_ref.dtype)

    @pl.when(j == 0)
    def _():
        part = x_ref.shape[0] // EPILOGUE_PARTS
        for m in range(EPILOGUE_PARTS):
            rows = slice(m * part, (m + 1) * part)
            x = x_ref[rows, :]
            ms = jnp.mean(x * x, axis=-1, keepdims=True)
            y = x * lax.rsqrt(ms + EPS) * ng_ref[...]
            h = (y * (1.0 + mod_ref[0, 1:2, :]) + mod_ref[0, 0:1, :]).astype(BF16)
            h_ref[rows, :] = h
            r_ref[rows, :] = jnp.dot(h, wr_ref[...], preferred_element_type=F32).astype(BF16)
            aqi_ref[rows, :] = jnp.dot(h, w_ref[...], preferred_element_type=F32).astype(BF16)

    def mm():
        return jnp.dot(h_ref[...], w_ref[...], preferred_element_type=F32)

    def mm_parts(out_ref, epilogue):
        part = h_ref.shape[0] // EPILOGUE_PARTS
        for m in range(EPILOGUE_PARTS):
            rows = slice(m * part, (m + 1) * part)
            z = jnp.dot(h_ref[rows, :], w_ref[...], preferred_element_type=F32)
            out_ref[rows, :] = epilogue(z).astype(out_ref.dtype)

    @pl.when((j > 0) & (j < J_ALF))
    def _():
        mm_parts(aqi_ref, lambda z: z)

    @pl.when((j >= J_ALF) & (j < J_BQKV))
    def _():
        lb = lb_ref[...]
        mm_parts(alf_ref, lambda z: jnp.log2(lb + (1.0 - lb) * _sigmoid(z)))

    @pl.when((j >= J_BQKV) & (j < J_GATES))
    def _():
        z = jnp.dot(r_ref[...], gkw_ref[...], preferred_element_type=F32) + gkb_ref[...]
        ls = jnp.minimum(z, 0.0) - jnp.log(1.0 + jnp.exp(-jnp.abs(z)))
        store_gla(blf_ref, ls * (LOG2E / GATE_NORMALIZER))
        scale = jnp.where(j < J_BQKV + (B_HEADS * B_DK) // TN, B_DK ** -0.5, 1.0)
        store_gla(bqkv_ref, mm() * scale)

    if latent:
        @pl.when(j >= J_GATES)
        def _():
            silu = j < J_GATES + T_GATES // 2
            mm_parts(gates_ref, lambda z: _sigmoid(z) * jnp.where(silu, z, 1.0))


def _inproj(x2d, mod3, mod_row_of_tile, norm_g, w_main, w_r, lb, gk_pad, gkb, *, tm, latent, n_batch=1):
    n_tok, d = x2d.shape
    n_j = J_END if latent else J_GATES
    tile_rows = tm // GRID_W
    seq_rows = n_tok // n_batch // GRID_W
    tiles_per_batch = n_tok // n_batch // tm

    def cl(j, lo, n):
        return jnp.clip(j - lo, 0, n - 1)

    def gla_spec(width):
        col = lambda j: cl(j, J_BQKV, T_BQKV)
        if not latent:
            return pl.BlockSpec((tm, width), lambda i, j: (i, col(j)))
        return pl.BlockSpec((None, GRID_W, tile_rows, width),
                            lambda i, j: (i // tiles_per_batch, 0, i % tiles_per_batch, col(j)))

    def gla_shape(width, dtype):
        if not latent:
            return jax.ShapeDtypeStruct((n_tok, T_BQKV * width), dtype)
        return jax.ShapeDtypeStruct((n_batch, GRID_W, seq_rows, T_BQKV * width), dtype)

    in_specs = [
        pl.BlockSpec((tm, d), lambda i, j: (i, 0), pipeline_mode=pl.Buffered(1)),
        pl.BlockSpec((1, 3, d), lambda i, j: (mod_row_of_tile(i), 0, 0)),
        pl.BlockSpec((1, d), lambda i, j: (0, 0)),
        pl.BlockSpec((d, TN), lambda i, j: (0, j)),
        pl.BlockSpec((d, LANES), lambda i, j: (0, 0)),
        pl.BlockSpec((1, TN), lambda i, j: (0, cl(j, J_ALF, T_ALF))),
        pl.BlockSpec((LANES, BLF_W), lambda i, j: (0, cl(j, J_BQKV, T_BQKV))),
        pl.BlockSpec((1, BLF_W), lambda i, j: (0, cl(j, J_BQKV, T_BQKV))),
    ]
    out_specs = [
        pl.BlockSpec((tm, TN), lambda i, j: (i, cl(j, 0, T_AQI))),
        pl.BlockSpec((tm, TN), lambda i, j: (i, cl(j, J_ALF, T_ALF))),
        gla_spec(TN),
        gla_spec(BLF_W),
    ]
    out_shape = [
        jax.ShapeDtypeStruct((n_tok, T_AQI * TN), BF16),
        jax.ShapeDtypeStruct((n_tok, T_ALF * TN), F32),
        gla_shape(TN, BF16),
        gla_shape(BLF_W, F32),
    ]
    if latent:
        out_specs.append(pl.BlockSpec((tm, TN), lambda i, j: (i, cl(j, J_GATES, T_GATES))))
        out_shape.append(jax.ShapeDtypeStruct((n_tok, T_GATES * TN), BF16))
    return pl.pallas_call(
        functools.partial(_inproj_kernel, latent=latent),
        grid=(n_tok // tm, n_j),
        in_specs=in_specs,
        out_specs=out_specs,
        out_shape=out_shape,
        scratch_shapes=[pltpu.VMEM((tm, d), BF16), pltpu.VMEM((tm, LANES), BF16)],
        compiler_params=pltpu.CompilerParams(
            dimension_semantics=("arbitrary", "arbitrary"), vmem_limit_bytes=VMEM_LIMIT_BYTES),
        name="inproj_latent" if latent else "inproj_ctx",
    )(x2d, mod3, norm_g, w_main, w_r, lb, gk_pad, gkb)


def _scan_consts():
    row = lax.broadcasted_iota(jnp.int32, (CHUNK, CHUNK), 0)
    col = lax.broadcasted_iota(jnp.int32, (CHUNK, CHUNK), 1)
    return dict(
        mask_f=col <= row,
        mask_b=col >= row,
        tri_f=jnp.where(col <= row, 1.0, 0.0).astype(BF16),
        tri_b=jnp.where(col >= row, 1.0, 0.0).astype(BF16),
        lane=lax.broadcasted_iota(jnp.int32, (SUBLANES, CHUNK), 1),
    )


DIAG_FACTOR_MAX_LOG2 = 96.0
FOLDED_SUB = 32


def _cumsum_stage(lf, has_k, fwd, cst):
    dk = lf.shape[1]
    tri = cst["tri_f"] if fwd else cst["tri_b"]
    hi = lf.astype(BF16)
    lo = (lf - hi.astype(F32)).astype(BF16)
    cc = jnp.dot(tri, jnp.concatenate([hi, lo], axis=1), preferred_element_type=F32)
    c = cc[:, :dk] + cc[:, dk:]
    a = None if has_k else c - jnp.log2(jnp.maximum(1.0 - jnp.exp2(lf), 0.0))
    mag = jnp.abs(lf)
    excess = None
    for r0 in range(0, CHUNK, FOLDED_SUB):
        e = jnp.sum(mag[r0:r0 + FOLDED_SUB], axis=0, keepdims=True)
        excess = e if excess is None else jnp.maximum(excess, e)
    return c, a, excess


def _intra_stage(q, k, v, c, a, st_ref, fwd, cst, keep, exact_diag=True):
    dk = c.shape[1]
    tot = c[CHUNK - 1:CHUNK] if fwd else c[0:1]

    if k is None:
        kf = None

        def kscaled(r0, r1, ref):
            return jnp.exp2(ref - a[r0:r1])
    else:
        a = c
        kf = k.astype(F32)

        def kscaled(r0, r1, ref):
            return kf[r0:r1] * jnp.exp2(ref - c[r0:r1])

    kte = kscaled(0, CHUNK, tot).astype(BF16)
    st = st_ref[...]
    if keep is not None:
        keep.prev_state(st.astype(BF16))
    st_ref[...] = st * jnp.exp2(tot) + lax.dot_general(v, kte, TN_DIMS, preferred_element_type=F32)
    if keep is None:
        return
    yield

    qf = q.astype(F32)
    keep.scaled_q((qf * jnp.exp2(c)).astype(BF16))
    half = SUB // SUBLANES
    sub, own = (SUB, 0) if exact_diag else (FOLDED_SUB, FOLDED_SUB)
    soffs = []
    for r0 in range(0, CHUNK, sub):
        if fwd:
            lo_row, hi_row = 0, r0 + own
            ref = c[r0:r0 + 1]
        else:
            lo_row, hi_row = r0 + sub - own, CHUNK
            ref = c[r0 + sub - 1:r0 + sub]
        if hi_row == lo_row:
            soffs.append(None)
            continue
        pieces = [kscaled(lo_row, hi_row, ref).astype(BF16)]
        if lo_row > 0:
            pieces.insert(0, jnp.zeros((lo_row, dk), BF16))
        if hi_row < CHUNK:
            pieces.append(jnp.zeros((CHUNK - hi_row, dk), BF16))
        kr = jnp.concatenate(pieces, axis=0) if len(pieces) > 1 else pieces[0]
        qr = (qf[r0:r0 + sub] * jnp.exp2(c[r0:r0 + sub] - ref)).astype(BF16)
        soffs.append(lax.dot_general(qr, kr, NT_DIMS, preferred_element_type=F32))
    yield

    if not exact_diag:
        p = jnp.concatenate(soffs, axis=0)
        keep.scores(jnp.where(cst["mask_f"] if fwd else cst["mask_b"], p, 0.0).astype(BF16))
        return

    rows = []
    for r, r0 in enumerate(range(0, CHUNK, SUB)):
        cb = c[r0:r0 + SUB]
        qb = qf[r0:r0 + SUB]
        acc = [jnp.zeros((SUBLANES, CHUNK), F32) for _ in range(half)]
        for jj in range(SUB):
            aj = a[r0 + jj:r0 + jj + 1]
            jg = jj // SUBLANES
            groups = range(jg, half) if fwd else range(0, jg + 1)
            for g in groups:
                e = cb[g * SUBLANES:(g + 1) * SUBLANES] - aj
                if g == jg:
                    e = jnp.minimum(e, 0.0)
                t = qb[g * SUBLANES:(g + 1) * SUBLANES] * jnp.exp2(e)
                if kf is not None:
                    t = t * kf[r0 + jj:r0 + jj + 1]
                sj = jnp.sum(t, axis=1, keepdims=True)
                acc[g] = jnp.where(cst["lane"] == r0 + jj, sj, acc[g])
        d = jnp.concatenate(acc, axis=0)
        rows.append(d if soffs[r] is None else d + soffs[r])
    p = jnp.concatenate(rows, axis=0)
    keep.scores(jnp.where(cst["mask_f"] if fwd else cst["mask_b"], p, 0.0).astype(BF16))


def _run_staged(units):
    live = list(units)
    while live:
        nxt = []
        for u in live:
            try:
                next(u)
                nxt.append(u)
            except StopIteration:
                pass
        live = nxt


def _scan_kernel(*refs, n_heads, dk, dv, has_k, n_ctx_chunks, n_step_chunks, lat_index):
    n_ctx_in = 4 if has_k else 3
    n_lat_in = 4 if has_k else 3
    ctx_refs = refs[:n_ctx_in]
    lat_refs = (refs[n_ctx_in:n_ctx_in + n_lat_in], refs[n_ctx_in + n_lat_in:n_ctx_in + 2 * n_lat_in])
    rest = refs[n_ctx_in + 2 * n_lat_in:]
    out_refs = rest[0:2]
    if has_k:
        st_ref, c_ref, stb_ref = rest[2:]
        a_ref = None
    else:
        st_ref, c_ref, stb_ref, a_ref = rest[2:]
    s = pl.program_id(2)
    cst = _scan_consts()
    units = [(h, d) for h in range(n_heads) for d in range(2)]

    def kcols(h):
        return slice(h * dk, (h + 1) * dk)

    def vcols(h):
        return slice(h * dv, (h + 1) * dv)

    def ctx_step():
        st_ref[...] = jnp.zeros(st_ref.shape, F32)
        if has_k:
            k_ref, v_ref, lff_ref, lfb_ref = ctx_refs
        else:
            v_ref, lff_ref, lfb_ref = ctx_refs
            k_ref = None

        def unit(h, d, rows):
            lf = (lff_ref, lfb_ref)[d][rows, kcols(h)]
            c, a, _ = _cumsum_stage(lf, has_k, d == 0, cst)
            yield
            k = None if k_ref is None else k_ref[rows, kcols(h)]
            yield from _intra_stage(None, k, v_ref[rows, vcols(h)], c, a, st_ref.at[d, h], d == 0, cst, None)

        def body(i, carry):
            rows = (pl.ds(pl.multiple_of(i * CHUNK, CHUNK), CHUNK),
                    pl.ds(pl.multiple_of((n_ctx_chunks - 1 - i) * CHUNK, CHUNK), CHUNK))
            _run_staged([unit(h, d, rows[d]) for h, d in units])
            return carry
        lax.fori_loop(0, n_ctx_chunks, body, 0)

    def latent_step():
        n = n_step_chunks

        def index(d, pos):
            return lat_index(pos if d == 0 else n - 1 - pos)

        def read(d, pos, h, what):
            refs_d = lat_refs[d]
            ref = refs_d[{"q": 0, "k": 1, "v": 2 if has_k else 1, "lf": 3 if has_k else 2}[what]]
            cols = vcols(h) if what == "v" else kcols(h)
            return ref[index(d, pos) + (cols,)]

        def cumsum_all(pos, slot):
            worst = None
            for u, (h, d) in enumerate(units):
                c, a, excess = _cumsum_stage(read(d, pos, h, "lf"), has_k, d == 0, cst)
                c_ref[slot, u] = c
                if a is not None:
                    a_ref[slot, u] = a
                worst = excess if worst is None else jnp.maximum(worst, excess)
            return jnp.max(worst)

        def output_all(pos, carried):
            for u, (h, d) in enumerate(units):
                p, qd = carried[u]
                o = jnp.dot(p, read(d, pos, h, "v"), preferred_element_type=F32)
                o = o + lax.dot_general(qd, stb_ref[u], NT_DIMS, preferred_element_type=F32)
                out_refs[d][index(d, pos) + (vcols(h),)] = o.astype(out_refs[d].dtype)

        class Keeper:
            def __init__(self, u):
                self.u = u
                self.p = self.qd = None

            def prev_state(self, stb):
                stb_ref[self.u] = stb

            def scaled_q(self, qd):
                self.qd = qd

            def scores(self, p):
                self.p = p

        def intra_all(pos, cs, exact_diag):
            keepers = [Keeper(u) for u in range(len(units))]
            gens = []
            for u, (h, d) in enumerate(units):
                k = read(d, pos, h, "k") if has_k else None
                c, a = cs[u]
                gens.append(_intra_stage(read(d, pos, h, "q"), k, read(d, pos, h, "v"), c, a,
                                         st_ref.at[d, h], d == 0, cst, keepers[u], exact_diag))
            _run_staged(gens)
            return tuple((kp.p, kp.qd) for kp in keepers)

        def load_cumsums(slot):
            return [(c_ref[slot, u], None if has_k else a_ref[slot, u]) for u in range(len(units))]

        def either_path(worst, build):
            return lax.cond(worst <= DIAG_FACTOR_MAX_LOG2,
                            functools.partial(build, False), functools.partial(build, True))

        worst0 = cumsum_all(0, 0)
        cs0 = load_cumsums(0)
        worst1 = cumsum_all(1, 1)
        carried0 = either_path(worst0, lambda exact_diag: intra_all(0, cs0, exact_diag))

        def body(i, state):
            worst, carried = state
            slot = lax.rem(i, 2)

            def iteration(exact_diag):
                cs = load_cumsums(slot)
                output_all(i - 1, carried)
                worst_next = cumsum_all(jnp.minimum(i + 1, n - 1), 1 - slot)
                return worst_next, intra_all(i, cs, exact_diag)
            return either_path(worst, iteration)
        _, carried_last = lax.fori_loop(1, n, body, (worst1, carried0))
        output_all(n - 1, carried_last)

    @pl.when(s == 0)
    def _():
        ctx_step()

    @pl.when(s > 0)
    def _():
        latent_step()


def _scan_call(inputs, in_specs, out_struct, out_specs, grid, *, n_heads, dk, dv, has_k,
               n_ctx_chunks, n_step_chunks, lat_index, name):
    n_units = 2 * n_heads
    scratch = [pltpu.VMEM((2, n_heads, dv, dk), F32),
               pltpu.VMEM((2, n_units, CHUNK, dk), F32),
               pltpu.VMEM((n_units, dv, dk), BF16)]
    if not has_k:
        scratch.append(pltpu.VMEM((2, n_units, CHUNK, dk), F32))
    return pl.pallas_call(
        functools.partial(_scan_kernel, n_heads=n_heads, dk=dk, dv=dv, has_k=has_k,
                          n_ctx_chunks=n_ctx_chunks, n_step_chunks=n_step_chunks, lat_index=lat_index),
        grid=grid,
        in_specs=in_specs,
        out_specs=out_specs,
        out_shape=out_struct,
        scratch_shapes=scratch,
        compiler_params=pltpu.CompilerParams(
            dimension_semantics=("arbitrary", "arbitrary", "arbitrary"),
            vmem_limit_bytes=VMEM_LIMIT_BYTES),
        name=name,
    )(*inputs)


A_SCAN_HEADS = 2
A_SCAN_CHUNKS = 64


def _scan_a(c_qi, c_lf, qi, lf):
    bsz, n_ctx, _ = c_qi.shape
    seq = qi.shape[1]
    g = A_SCAN_HEADS
    gw = g * A_DK
    n_hg = A_HEADS // g
    ts = A_SCAN_CHUNKS * CHUNK
    n_steps = seq // ts

    def fstep(s):
        return jnp.maximum(s - 1, 0)

    def bstep(s):
        return n_steps - 1 - jnp.maximum(s - 1, 0)

    cspec = lambda off: pl.BlockSpec((None, n_ctx, gw), lambda b, h, s: (b, 0, off + h))
    fspec = lambda off: pl.BlockSpec((None, ts, gw), lambda b, h, s: (b, fstep(s), off + h))
    bspec = lambda off: pl.BlockSpec((None, ts, gw), lambda b, h, s: (b, bstep(s), off + h))
    in_specs = [cspec(n_hg), cspec(0), cspec(n_hg),
                fspec(0), fspec(n_hg), fspec(0),
                bspec(0), bspec(n_hg), bspec(n_hg)]
    inputs = [c_qi, c_lf, c_lf, qi, qi, lf, qi, qi, lf]
    out_struct = [jax.ShapeDtypeStruct((bsz, seq, A_HEADS * A_DV), BF16)] * 2
    out_specs = [pl.BlockSpec((None, ts, gw), lambda b, h, s: (b, fstep(s), h)),
                 pl.BlockSpec((None, ts, gw), lambda b, h, s: (b, bstep(s), h))]

    def lat_index(chunk):
        row = chunk * CHUNK
        if not isinstance(row, int):
            row = pl.multiple_of(row, CHUNK)
        return (pl.ds(row, CHUNK),)

    return _scan_call(inputs, in_specs, out_struct, out_specs, (bsz, n_hg, n_steps + 1),
                      n_heads=g, dk=A_DK, dv=A_DV, has_k=False,
                      n_ctx_chunks=n_ctx // CHUNK, n_step_chunks=A_SCAN_CHUNKS,
                      lat_index=lat_index, name="scan_hgrn2")


B_SCAN_COLS = 16


def _scan_b(c_qkv, c_lf, qkv, lf):
    bsz, n_ctx, _ = c_qkv.shape
    n_rows = qkv.shape[2]
    kw = B_HEADS * B_DK
    nc = B_SCAN_COLS
    n_steps = GRID_W // nc
    chunks_per_col = n_rows // CHUNK

    def fblk(s):
        return jnp.maximum(s - 1, 0)

    def bblk(s):
        return n_steps - 1 - jnp.maximum(s - 1, 0)

    def cspec(width, off):
        return pl.BlockSpec((None, n_ctx, width), lambda b, h, s: (b, 0, off + h))

    def lspec(width, off, blk):
        return pl.BlockSpec((None, nc, n_rows, width), lambda b, h, s: (b, blk(s), 0, off + h))

    k_off, v_off = kw // B_DK, (2 * kw) // B_DV
    in_specs = [cspec(B_DK, k_off), cspec(B_DV, v_off), cspec(B_DK, 0), cspec(B_DK, B_HEADS),
                lspec(B_DK, 0, fblk), lspec(B_DK, k_off, fblk), lspec(B_DV, v_off, fblk), lspec(B_DK, 0, fblk),
                lspec(B_DK, 0, bblk), lspec(B_DK, k_off, bblk), lspec(B_DV, v_off, bblk),
                lspec(B_DK, B_HEADS, bblk)]
    inputs = [c_qkv, c_qkv, c_lf, c_lf, qkv, qkv, qkv, lf, qkv, qkv, qkv, lf]
    vw = B_HEADS * B_DV
    out_struct = [jax.ShapeDtypeStruct((bsz, GRID_W, n_rows, vw), BF16)] * 2
    out_specs = [pl.BlockSpec((None, nc, n_rows, B_DV), lambda b, h, s: (b, fblk(s), 0, h)),
                 pl.BlockSpec((None, nc, n_rows, B_DV), lambda b, h, s: (b, bblk(s), 0, h))]

    def lat_index(chunk):
        col = chunk // chunks_per_col
        row = (chunk - col * chunks_per_col) * CHUNK
        if not isinstance(row, int):
            row = pl.multiple_of(row, CHUNK)
        return (col, pl.ds(row, CHUNK))

    return _scan_call(inputs, in_specs, out_struct, out_specs, (bsz, B_HEADS, n_steps + 1),
                      n_heads=1, dk=B_DK, dv=B_DV, has_k=True,
                      n_ctx_chunks=n_ctx // CHUNK, n_step_chunks=nc * chunks_per_col,
                      lat_index=lat_index, name="scan_gla")


GLA_FIN_ROWS = 16


def _gla_finalize_kernel(of_ref, ob_ref, sg_ref, g_ref, o_ref):
    n_cols, n_rows, w = of_ref.shape
    o = of_ref[...].astype(F32) + ob_ref[...].astype(F32)
    ms = jnp.mean(o * o, axis=-1, keepdims=True)
    on = o * lax.rsqrt(ms + EPS) * g_ref[...]
    on = jnp.swapaxes(on, 0, 1).reshape(n_rows * n_cols, w)
    o_ref[...] = (on * sg_ref[...].astype(F32)).astype(o_ref.dtype)


def _gla_finalize(of, ob, gates, gain, *, sg_col_off):
    bsz, _, n_rows, vw = of.shape
    tm = GLA_FIN_ROWS * GRID_W
    tiles = n_rows // GLA_FIN_ROWS
    cm = pl.BlockSpec((None, GRID_W, GLA_FIN_ROWS, B_DV), lambda b, r, h: (b, 0, r, h))
    return pl.pallas_call(
        _gla_finalize_kernel,
        grid=(bsz, tiles, B_HEADS),
        in_specs=[cm, cm,
                  pl.BlockSpec((tm, B_DV), lambda b, r, h: (b * tiles + r, sg_col_off + h)),
                  pl.BlockSpec((1, B_DV), lambda b, r, h: (0, h))],
        out_specs=pl.BlockSpec((tm, B_DV), lambda b, r, h: (b * tiles + r, h)),
        out_shape=jax.ShapeDtypeStruct((bsz * n_rows * GRID_W, vw), BF16),
        compiler_params=pltpu.CompilerParams(
            dimension_semantics=("arbitrary", "arbitrary", "arbitrary"),
            vmem_limit_bytes=32 * 1024 * 1024),
        name="gla_finalize",
    )(of, ob, gates, gain)


def _merge_kernel(oaf, oab, obp, sga, sma, smb, ga, wpa, wpb, y_ref):
    o = oaf[...].astype(F32) + oab[...].astype(F32)
    parts = []
    for h in range(A_HEADS):
        oh = o[:, h * A_DV:(h + 1) * A_DV]
        ms = jnp.mean(oh * oh, axis=-1, keepdims=True)
        parts.append(oh * lax.rsqrt(ms + EPS))
    oa = (jnp.concatenate(parts, axis=1) * ga[...] * sga[...].astype(F32)).astype(BF16)
    ya = jnp.dot(oa, wpa[...], preferred_element_type=F32)
    yb = jnp.dot(obp[...], wpb[...], preferred_element_type=F32)
    y_ref[...] = (sma[...].astype(F32) * ya + smb[...].astype(F32) * yb).astype(BF16)


def _merge(oaf, oab, obp, gates, ga, wpa, wpb, *, tm):
    n_tok, d = oaf.shape
    tok = lambda off: pl.BlockSpec((tm, d), lambda i: (i, off))
    const = lambda shape: pl.BlockSpec(shape, lambda i: (0, 0), pipeline_mode=pl.Buffered(1))
    return pl.pallas_call(
        _merge_kernel,
        grid=(n_tok // tm,),
        in_specs=[tok(0), tok(0), tok(0), tok(0), tok(2), tok(3),
                  const((1, d)), const(wpa.shape), const(wpb.shape)],
        out_specs=pl.BlockSpec((tm, wpa.shape[1]), lambda i: (i, 0)),
        out_shape=jax.ShapeDtypeStruct((n_tok, wpa.shape[1]), BF16),
        compiler_params=pltpu.CompilerParams(
            dimension_semantics=("arbitrary",), vmem_limit_bytes=VMEM_LIMIT_BYTES),
        name="merge_proj",
    )(oaf, oab, obp, gates, gates, gates, ga, wpa, wpb)


def _final_kernel(y_ref, x_ref, mod_ref, w_ref, g_ref, o_ref):
    yo = jnp.dot(y_ref[...], w_ref[...], preferred_element_type=F32)
    z = x_ref[...] + mod_ref[0, 2:3, :] * yo
    ms = jnp.mean(z * z, axis=-1, keepdims=True)
    o_ref[...] = z * lax.rsqrt(ms + EPS) * g_ref[...]


def _final(y, x2d, mod3, w_out, fg, *, tm, tiles_per_batch):
    n_tok, d = x2d.shape
    const = lambda shape: pl.BlockSpec(shape, lambda i: (0, 0), pipeline_mode=pl.Buffered(1))
    return pl.pallas_call(
        _final_kernel,
        grid=(n_tok // tm,),
        in_specs=[pl.BlockSpec((tm, d), lambda i: (i, 0)),
                  pl.BlockSpec((tm, d), lambda i: (i, 0)),
                  pl.BlockSpec((1, 3, d), lambda i: (i // tiles_per_batch, 0, 0)),
                  const(w_out.shape), const((1, d))],
        out_specs=pl.BlockSpec((tm, d), lambda i: (i, 0)),
        out_shape=jax.ShapeDtypeStruct((n_tok, d), F32),
        compiler_params=pltpu.CompilerParams(
            dimension_semantics=("arbitrary",), vmem_limit_bytes=VMEM_LIMIT_BYTES),
        name="out_proj_final",
    )(y, x2d, mod3, w_out, fg)


def kernel(x, c, ctx, c_ctx, w_ada, b_ada, norm_g, w_in, hgrn_lb_logits, gla_w_gk, gla_b_gk,
           hgrn_onorm_g, gla_onorm_g, w_pa, w_pb, w_out, final_norm_g):
    bsz, seq, d = x.shape
    n_ctx = ctx.shape[1]
    depth = w_in.shape[0]
    assert depth == 1, "single-layer trunk"
    a_kw, a_vw = A_HEADS * A_DK, A_HEADS * A_DV
    b_kw, b_vw = B_HEADS * B_DK, B_HEADS * B_DV

    n_rows = -(-(bsz + 1) // SUBLANES) * SUBLANES
    cvec = jnp.zeros((n_rows, d), F32).at[:bsz].set(c).at[bsz].set(c_ctx)
    mod = _adaln(cvec, w_ada[0], b_ada[0].reshape(1, -1))
    mod3 = mod.reshape(n_rows, 3, d)

    lb = _lower_bounds(hgrn_lb_logits)[0:1]

    o_ag = 2 * a_kw + 2 * a_vw
    o_bq = o_ag + a_vw
    o_br = o_bq + 2 * b_kw + b_vw
    w_main, w_r = _prep_weights(
        jnp.swapaxes(w_in[0], 0, 1), [(0, o_ag), (o_bq, o_br - o_bq), (o_ag, o_bq - o_ag)],
        tail_start=o_br)
    gk_pad = jnp.zeros((LANES, 2 * b_kw), F32)
    gk_pad = gk_pad.at[0:B_RANK, 0:b_kw].set(gla_w_gk[0, 0]).at[B_RANK:2 * B_RANK, b_kw:].set(gla_w_gk[0, 1])
    gk_pad = gk_pad.astype(BF16)
    gkb = gla_b_gk[0].reshape(1, 2 * b_kw)
    ng = norm_g[0].reshape(1, d)

    x2d = x.reshape(bsz * seq, d)
    ctx2d = ctx.reshape(bsz * n_ctx, d)
    tm = 1024
    tiles_per_batch = seq // tm

    c_aqi, c_alf, c_bqkv, c_blf = _inproj(
        ctx2d, mod3, lambda i: bsz, ng, w_main, w_r, lb, gk_pad, gkb,
        tm=bsz * n_ctx, latent=False)
    aqi, alf, bqkv, blf, gates = _inproj(
        x2d, mod3, lambda i: i // tiles_per_batch, ng, w_main, w_r, lb, gk_pad, gkb,
        tm=tm, latent=True, n_batch=bsz)

    r3 = lambda t, n: t.reshape(bsz, n, t.shape[-1])
    oaf, oab = _scan_a(r3(c_aqi, n_ctx), r3(c_alf, n_ctx), r3(aqi, seq), r3(alf, seq))
    obf, obb = _scan_b(r3(c_bqkv, n_ctx), r3(c_blf, n_ctx), bqkv, blf)
    obp = _gla_finalize(obf, obb, gates, gla_onorm_g[0].reshape(1, -1), sg_col_off=a_vw // B_DV)

    r2 = lambda t: t.reshape(bsz * seq, t.shape[-1])
    y = _merge(r2(oaf), r2(oab), obp, gates, hgrn_onorm_g[0].reshape(1, -1),
               w_pa[0].astype(BF16), w_pb[0].astype(BF16), tm=512)
    out = _final(y, x2d, mod3, w_out[0].astype(BF16), final_norm_g.reshape(1, d),
                 tm=512, tiles_per_batch=seq // 512)
    return out.reshape(bsz, seq, d)
```

```python
import functools

import jax
import jax.numpy as jnp
from jax import lax
from jax.experimental import pallas as pl
from jax.experimental.pallas import tpu as pltpu

F32 = jnp.float32
BF16 = jnp.bfloat16

CHUNK = 64
SUB = 16
GRID_W = 64
EPS = 1e-6
A_HEADS, A_DK, A_DV = 16, 128, 128
B_HEADS, B_DK, B_DV = 4, 256, 512
B_RANK = 16
GATE_NORMALIZER = 16.0
LOG2E = 1.4426950408889634

VMEM_LIMIT_BYTES = 56 * 1024 * 1024
LANES = 128
SUBLANES = 8

NT_DIMS = (((1,), (1,)), ((), ()))
TN_DIMS = (((0,), (0,)), ((), ()))


def _sigmoid(z):
    return 1.0 / (1.0 + jnp.exp(-z))


def _adaln_kernel(c_ref, w_ref, b_ref, o_ref):
    c = c_ref[...]
    s = c * _sigmoid(c)
    o_ref[...] = jnp.dot(s, w_ref[...], preferred_element_type=F32,
                         precision=lax.Precision.HIGHEST) + b_ref[...]


def _adaln(cvec, w, b):
    rows, d = cvec.shape
    n = w.shape[1]
    tn = 768
    return pl.pallas_call(
        _adaln_kernel,
        grid=(n // tn,),
        in_specs=[pl.BlockSpec((rows, d), lambda j: (0, 0)),
                  pl.BlockSpec((d, tn), lambda j: (0, j)),
                  pl.BlockSpec((1, tn), lambda j: (0, j))],
        out_specs=pl.BlockSpec((rows, tn), lambda j: (0, j)),
        out_shape=jax.ShapeDtypeStruct((rows, n), F32),
        compiler_params=pltpu.CompilerParams(
            dimension_semantics=("arbitrary",), vmem_limit_bytes=32 * 1024 * 1024),
        name="adaln",
    )(cvec, w, b)


def _lower_bound_kernel(l_ref, o_ref):
    x = l_ref[...]
    n_rows = x.shape[0]
    m = jnp.max(x, axis=0, keepdims=True)
    e = jnp.exp(x - m)
    tot = jnp.sum(e, axis=0, keepdims=True)
    run = jnp.zeros_like(tot)
    for r in range(n_rows - 1):
        run = run + e[r:r + 1]
        o_ref[r:r + 1, :] = run / tot


def _lower_bounds(logits):
    n_rows, w = logits.shape
    return pl.pallas_call(
        _lower_bound_kernel,
        out_shape=jax.ShapeDtypeStruct((n_rows - 1, w), F32),
        name="hgrn_lower_bounds",
    )(logits)


def _prep_weights_kernel(wa_ref, wb_ref, o_ref, r_ref, *, j_tail, shift):
    j = pl.program_id(0)

    @pl.when(j < j_tail)
    def _():
        o_ref[...] = wa_ref[...].T.astype(BF16)

    @pl.when(j >= j_tail)
    def _():
        o_ref[...] = jnp.concatenate([wa_ref[shift:, :], wb_ref[...]], axis=0).T.astype(BF16)

    @pl.when(j == j_tail)
    def _():
        rank_rows = jnp.concatenate(
            [wa_ref[:shift, :], jnp.zeros((LANES - shift, wa_ref.shape[1]), F32)], axis=0)
        r_ref[...] = rank_rows.T.astype(BF16)


def _prep_weights(wt, group_starts, tail_start):
    n_in, d = wt.shape
    shift = 2 * B_RANK
    src_tiles = []
    for start, width in group_starts:
        assert start % TN == 0 and width % TN == 0
        src_tiles += list(range(start // TN, (start + width) // TN))
    assert tail_start % TN == 0
    j_tail = len(src_tiles)
    n_tail = (n_in - tail_start - shift) // TN
    n_out = j_tail + n_tail
    assert tail_start + shift + n_tail * TN == n_in and TN % shift == 0

    def src_a(j):
        idx = j - j_tail + tail_start // TN
        for dst in range(j_tail - 1, -1, -1):
            idx = jnp.where(j == dst, src_tiles[dst], idx)
        return idx

    def src_b(j):
        return (jnp.maximum(src_a(j), tail_start // TN) + 1) * (TN // shift)

    return pl.pallas_call(
        functools.partial(_prep_weights_kernel, j_tail=j_tail, shift=shift),
        grid=(n_out,),
        in_specs=[pl.BlockSpec((TN, d), lambda j: (src_a(j), 0)),
                  pl.BlockSpec((shift, d), lambda j: (src_b(j), 0))],
        out_specs=[pl.BlockSpec((d, TN), lambda j: (0, j)),
                   pl.BlockSpec((d, LANES), lambda j: (0, 0))],
        out_shape=[jax.ShapeDtypeStruct((d, n_out * TN), BF16),
                   jax.ShapeDtypeStruct((d, LANES), BF16)],
        compiler_params=pltpu.CompilerParams(
            dimension_semantics=("arbitrary",), vmem_limit_bytes=VMEM_LIMIT_BYTES),
        name="prep_weights",
    )(wt, wt)


TN = 1024
EPILOGUE_PARTS = 8
T_AQI = 4096 // TN
T_ALF = 4096 // TN
T_BQKV = 4096 // TN
T_GATES = 8192 // TN
J_ALF = T_AQI
J_BQKV = J_ALF + T_ALF
J_GATES = J_BQKV + T_BQKV
J_END = J_GATES + T_GATES
BLF_W = 2048 // T_BQKV


def _inproj_kernel(x_ref, mod_ref, ng_ref, w_ref, wr_ref, lb_ref, gkw_ref, gkb_ref, *rest, latent):
    if latent:
        aqi_ref, alf_ref, bqkv_ref, blf_ref, gates_ref, h_ref, r_ref = rest
    else:
        aqi_ref, alf_ref, bqkv_ref, blf_ref, h_ref, r_ref = rest
        gates_ref = None
    j = pl.program_id(1)

    def store_gla(out_ref, z):
        if not latent:
            out_ref[...] = z.astype(out_ref.dtype)
            return
        n_rows = z.shape[0] // GRID_W
        zc = jnp.swapaxes(z.reshape(n_rows, GRID_W, z.shape[1]), 0, 1)
        out_ref[...] = zc.astype(out_ref.dtype)

    @pl.when(j == 0)
    def _():
        part = x_ref.shape[0] // EPILOGUE_PARTS
        for m in range(EPILOGUE_PARTS):
            rows = slice(m * part, (m + 1) * part)
            x = x_ref[rows, :]
            ms = jnp.mean(x * x, axis=-1, keepdims=True)
            y = x * lax.rsqrt(ms + EPS) * ng_ref[...]
            h = (y * (1.0 + mod_ref[0, 1:2, :]) + mod_ref[0, 0:1, :]).astype(BF16)
            h_ref[rows, :] = h
            r_ref[rows, :] = jnp.dot(h, wr_ref[...], preferred_element_type=F32).astype(BF16)
            aqi_ref[rows, :] = jnp.dot(h, w_ref[...], preferred_element_type=F32).astype(BF16)

    def mm():
        return jnp.dot(h_ref[...], w_ref[...], preferred_element_type=F32)

    def mm_parts(out_ref, epilogue):
        part = h_ref.shape[0] // EPILOGUE_PARTS
        for m in range(EPILOGUE_PARTS):
            rows = slice(m * part, (m + 1) * part)
            z = jnp.dot(h_ref[rows, :], w_ref[...], preferred_element_type=F32)
            out_ref[rows, :] = epilogue(z).astype(out_ref.dtype)

    @pl.when((j > 0) & (j < J_ALF))
    def _():
        mm_parts(aqi_ref, lambda z: z)

    @pl.when((j >= J_ALF) & (j < J_BQKV))
    def _():
        lb = lb_ref[...]
        mm_parts(alf_ref, lambda z: jnp.log2(lb + (1.0 - lb) * _sigmoid(z)))

    @pl.when((j >= J_BQKV) & (j < J_GATES))
    def _():
        z = jnp.dot(r_ref[...], gkw_ref[...], preferred_element_type=F32) + gkb_ref[...]
        ls = jnp.minimum(z, 0.0) - jnp.log(1.0 + jnp.exp(-jnp.abs(z)))
        store_gla(blf_ref, ls * (LOG2E / GATE_NORMALIZER))
        scale = jnp.where(j < J_BQKV + (B_HEADS * B_DK) // TN, B_DK ** -0.5, 1.0)
        store_gla(bqkv_ref, mm() * scale)

    if latent:
        @pl.when(j >= J_GATES)
        def _():
            silu = j < J_GATES + T_GATES // 2
            mm_parts(gates_ref, lambda z: _sigmoid(z) * jnp.where(silu, z, 1.0))


def _inproj(x2d, mod3, mod_row_of_tile, norm_g, w_main, w_r, lb, gk_pad, gkb, *, tm, latent, n_batch=1):
    n_tok, d = x2d.shape
    n_j = J_END if latent else J_GATES
    tile_rows = tm // GRID_W
    seq_rows = n_tok // n_batch // GRID_W
    tiles_per_batch = n_tok // n_batch // tm

    def cl(j, lo, n):
        return jnp.clip(j - lo, 0, n - 1)

    def gla_spec(width):
        col = lambda j: cl(j, J_BQKV, T_BQKV)
        if not latent:
            return pl.BlockSpec((tm, width), lambda i, j: (i, col(j)))
        return pl.BlockSpec((None, GRID_W, tile_rows, width),
                            lambda i, j: (i // tiles_per_batch, 0, i % tiles_per_batch, col(j)))

    def gla_shape(width, dtype):
        if not latent:
            return jax.ShapeDtypeStruct((n_tok, T_BQKV * width), dtype)
        return jax.ShapeDtypeStruct((n_batch, GRID_W, seq_rows, T_BQKV * width), dtype)

    in_specs = [
        pl.BlockSpec((tm, d), lambda i, j: (i, 0), pipeline_mode=pl.Buffered(1)),
        pl.BlockSpec((1, 3, d), lambda i, j: (mod_row_of_tile(i), 0, 0)),
        pl.BlockSpec((1, d), lambda i, j: (0, 0)),
        pl.BlockSpec((d, TN), lambda i, j: (0, j)),
        pl.BlockSpec((d, LANES), lambda i, j: (0, 0)),
        pl.BlockSpec((1, TN), lambda i, j: (0, cl(j, J_ALF, T_ALF))),
        pl.BlockSpec((LANES, BLF_W), lambda i, j: (0, cl(j, J_BQKV, T_BQKV))),
        pl.BlockSpec((1, BLF_W), lambda i, j: (0, cl(j, J_BQKV, T_BQKV))),
    ]
    out_specs = [
        pl.BlockSpec((tm, TN), lambda i, j: (i, cl(j, 0, T_AQI))),
        pl.BlockSpec((tm, TN), lambda i, j: (i, cl(j, J_ALF, T_ALF))),
        gla_spec(TN),
        gla_spec(BLF_W),
    ]
    out_shape = [
        jax.ShapeDtypeStruct((n_tok, T_AQI * TN), BF16),
        jax.ShapeDtypeStruct((n_tok, T_ALF * TN), F32),
        gla_shape(TN, BF16),
        gla_shape(BLF_W, F32),
    ]
    if latent:
        out_specs.append(pl.BlockSpec((tm, TN), lambda i, j: (i, cl(j, J_GATES, T_GATES))))
        out_shape.append(jax.ShapeDtypeStruct((n_tok, T_GATES * TN), BF16))
    return pl.pallas_call(
        functools.partial(_inproj_kernel, latent=latent),
        grid=(n_tok // tm, n_j),
        in_specs=in_specs,
        out_specs=out_specs,
        out_shape=out_shape,
        scratch_shapes=[pltpu.VMEM((tm, d), BF16), pltpu.VMEM((tm, LANES), BF16)],
        compiler_params=pltpu.CompilerParams(
            dimension_semantics=("arbitrary", "arbitrary"), vmem_limit_bytes=VMEM_LIMIT_BYTES),
        name="inproj_latent" if latent else "inproj_ctx",
    )(x2d, mod3, norm_g, w_main, w_r, lb, gk_pad, gkb)


def _scan_consts():
    row = lax.broadcasted_iota(jnp.int32, (CHUNK, CHUNK), 0)
    col = lax.broadcasted_iota(jnp.int32, (CHUNK, CHUNK), 1)
    return dict(
        mask_f=col <= row,
        mask_b=col >= row,
        tri_f=jnp.where(col <= row, 1.0, 0.0).astype(BF16),
        tri_b=jnp.where(col >= row, 1.0, 0.0).astype(BF16),
        lane=lax.broadcasted_iota(jnp.int32, (SUBLANES, CHUNK), 1),
    )


DIAG_FACTOR_MAX_LOG2 = 96.0
FOLDED_SUB = 32


def _cumsum_stage(lf, has_k, fwd, cst):
    dk = lf.shape[1]
    tri = cst["tri_f"] if fwd else cst["tri_b"]
    hi = lf.astype(BF16)
    lo = (lf - hi.astype(F32)).astype(BF16)
    cc = jnp.dot(tri, jnp.concatenate([hi, lo], axis=1), preferred_element_type=F32)
    c = cc[:, :dk] + cc[:, dk:]
    a = None if has_k else c - jnp.log2(jnp.maximum(1.0 - jnp.exp2(lf), 0.0))
    mag = jnp.abs(lf)
    excess = None
    for r0 in range(0, CHUNK, FOLDED_SUB):
        e = jnp.sum(mag[r0:r0 + FOLDED_SUB], axis=0, keepdims=True)
        excess = e if excess is None else jnp.maximum(excess, e)
    return c, a, excess


def _intra_stage(q, k, v, c, a, st_ref, fwd, cst, keep, exact_diag=True):
    dk = c.shape[1]
    tot = c[CHUNK - 1:CHUNK] if fwd else c[0:1]

    if k is None:
        kf = None

        def kscaled(r0, r1, ref):
            return jnp.exp2(ref - a[r0:r1])
    else:
        a = c
        kf = k.astype(F32)

        def kscaled(r0, r1, ref):
            return kf[r0:r1] * jnp.exp2(ref - c[r0:r1])

    kte = kscaled(0, CHUNK, tot).astype(BF16)
    st = st_ref[...]
    if keep is not None:
        keep.prev_state(st.astype(BF16))
    st_ref[...] = st * jnp.exp2(tot) + lax.dot_general(v, kte, TN_DIMS, preferred_element_type=F32)
    if keep is None:
        return
    yield

    qf = q.astype(F32)
    keep.scaled_q((qf * jnp.exp2(c)).astype(BF16))
    half = SUB // SUBLANES
    sub, own = (SUB, 0) if exact_diag else (FOLDED_SUB, FOLDED_SUB)
    soffs = []
    for r0 in range(0, CHUNK, sub):
        if fwd:
            lo_row, hi_row = 0, r0 + own
            ref = c[r0:r0 + 1]
        else:
            lo_row, hi_row = r0 + sub - own, CHUNK
            ref = c[r0 + sub - 1:r0 + sub]
        if hi_row == lo_row:
            soffs.append(None)
            continue
        pieces = [kscaled(lo_row, hi_row, ref).astype(BF16)]
        if lo_row > 0:
            pieces.insert(0, jnp.zeros((lo_row, dk), BF16))
        if hi_row < CHUNK:
            pieces.append(jnp.zeros((CHUNK - hi_row, dk), BF16))
        kr = jnp.concatenate(pieces, axis=0) if len(pieces) > 1 else pieces[0]
        qr = (qf[r0:r0 + sub] * jnp.exp2(c[r0:r0 + sub] - ref)).astype(BF16)
        soffs.append(lax.dot_general(qr, kr, NT_DIMS, preferred_element_type=F32))
    yield

    if not exact_diag:
        p = jnp.concatenate(soffs, axis=0)
        keep.scores(jnp.where(cst["mask_f"] if fwd else cst["mask_b"], p, 0.0).astype(BF16))
        return

    rows = []
    for r, r0 in enumerate(range(0, CHUNK, SUB)):
        cb = c[r0:r0 + SUB]
        qb = qf[r0:r0 + SUB]
        acc = [jnp.zeros((SUBLANES, CHUNK), F32) for _ in range(half)]
        for jj in range(SUB):
            aj = a[r0 + jj:r0 + jj + 1]
            jg = jj // SUBLANES
            groups = range(jg, half) if fwd else range(0, jg + 1)
            for g in groups:
                e = cb[g * SUBLANES:(g + 1) * SUBLANES] - aj
                if g == jg:
                    e = jnp.minimum(e, 0.0)
                t = qb[g * SUBLANES:(g + 1) * SUBLANES] * jnp.exp2(e)
                if kf is not None:
                    t = t * kf[r0 + jj:r0 + jj + 1]
                sj = jnp.sum(t, axis=1, keepdims=True)
                acc[g] = jnp.where(cst["lane"] == r0 + jj, sj, acc[g])
        d = jnp.concatenate(acc, axis=0)
        rows.append(d if soffs[r] is None else d + soffs[r])
    p = jnp.concatenate(rows, axis=0)
    keep.scores(jnp.where(cst["mask_f"] if fwd else cst["mask_b"], p, 0.0).astype(BF16))


def _run_staged(units):
    live = list(units)
    while live:
        nxt = []
        for u in live:
            try:
                next(u)
                nxt.append(u)
            except StopIteration:
                pass
        live = nxt


def _scan_kernel(*refs, n_heads, dk, dv, has_k, n_ctx_chunks, n_step_chunks, lat_index):
    n_ctx_in = 4 if has_k else 3
    n_lat_in = 4 if has_k else 3
    ctx_refs = refs[:n_ctx_in]
    lat_refs = (refs[n_ctx_in:n_ctx_in + n_lat_in], refs[n_ctx_in + n_lat_in:n_ctx_in + 2 * n_lat_in])
    rest = refs[n_ctx_in + 2 * n_lat_in:]
    out_refs = rest[0:2]
    if has_k:
        st_ref, c_ref, stb_ref = rest[2:]
        a_ref = None
    else:
        st_ref, c_ref, stb_ref, a_ref = rest[2:]
    s = pl.program_id(2)
    cst = _scan_consts()
    units = [(h, d) for h in range(n_heads) for d in range(2)]

    def kcols(h):
        return slice(h * dk, (h + 1) * dk)

    def vcols(h):
        return slice(h * dv, (h + 1) * dv)

    def ctx_step():
        st_ref[...] = jnp.zeros(st_ref.shape, F32)
        if has_k:
            k_ref, v_ref, lff_ref, lfb_ref = ctx_refs
        else:
            v_ref, lff_ref, lfb_ref = ctx_refs
            k_ref = None

        def unit(h, d, rows):
            lf = (lff_ref, lfb_ref)[d][rows, kcols(h)]
            c, a, _ = _cumsum_stage(lf, has_k, d == 0, cst)
            yield
            k = None if k_ref is None else k_ref[rows, kcols(h)]
            yield from _intra_stage(None, k, v_ref[rows, vcols(h)], c, a, st_ref.at[d, h], d == 0, cst, None)

        def body(i, carry):
            rows = (pl.ds(pl.multiple_of(i * CHUNK, CHUNK), CHUNK),
                    pl.ds(pl.multiple_of((n_ctx_chunks - 1 - i) * CHUNK, CHUNK), CHUNK))
            _run_staged([unit(h, d, rows[d]) for h, d in units])
            return carry
        lax.fori_loop(0, n_ctx_chunks, body, 0)

    def latent_step():
        n = n_step_chunks

        def index(d, pos):
            return lat_index(pos if d == 0 else n - 1 - pos)

        def read(d, pos, h, what):
            refs_d = lat_refs[d]
            ref = refs_d[{"q": 0, "k": 1, "v": 2 if has_k else 1, "lf": 3 if has_k else 2}[what]]
            cols = vcols(h) if what == "v" else kcols(h)
            return ref[index(d, pos) + (cols,)]

        def cumsum_all(pos, slot):
            worst = None
            for u, (h, d) in enumerate(units):
                c, a, excess = _cumsum_stage(read(d, pos, h, "lf"), has_k, d == 0, cst)
                c_ref[slot, u] = c
                if a is not None:
                    a_ref[slot, u] = a
                worst = excess if worst is None else jnp.maximum(worst, excess)
            return jnp.max(worst)

        def output_all(pos, carried):
            for u, (h, d) in enumerate(units):
                p, qd = carried[u]
                o = jnp.dot(p, read(d, pos, h, "v"), preferred_element_type=F32)
                o = o + lax.dot_general(qd, stb_ref[u], NT_DIMS, preferred_element_type=F32)
                out_refs[d][index(d, pos) + (vcols(h),)] = o.astype(out_refs[d].dtype)

        class Keeper:
            def __init__(self, u):
                self.u = u
                self.p = self.qd = None

            def prev_state(self, stb):
                stb_ref[self.u] = stb

            def scaled_q(self, qd):
                self.qd = qd

            def scores(self, p):
                self.p = p

        def intra_all(pos, cs, exact_diag):
            keepers = [Keeper(u) for u in range(len(units))]
            gens = []
            for u, (h, d) in enumerate(units):
                k = read(d, pos, h, "k") if has_k else None
                c, a = cs[u]
                gens.append(_intra_stage(read(d, pos, h, "q"), k, read(d, pos, h, "v"), c, a,
                                         st_ref.at[d, h], d == 0, cst, keepers[u], exact_diag))
            _run_staged(gens)
            return tuple((kp.p, kp.qd) for kp in keepers)

        def load_cumsums(slot):
            return [(c_ref[slot, u], None if has_k else a_ref[slot, u]) for u in range(len(units))]

        def either_path(worst, build):
            return lax.cond(worst <= DIAG_FACTOR_MAX_LOG2,
                            functools.partial(build, False), functools.partial(build, True))

        worst0 = cumsum_all(0, 0)
        cs0 = load_cumsums(0)
        worst1 = cumsum_all(1, 1)
        carried0 = either_path(worst0, lambda exact_diag: intra_all(0, cs0, exact_diag))

        def body(i, state):
            worst, carried = state
            slot = lax.rem(i, 2)

            def iteration(exact_diag):
                cs = load_cumsums(slot)
                output_all(i - 1, carried)
                worst_next = cumsum_all(jnp.minimum(i + 1, n - 1), 1 - slot)
                return worst_next, intra_all(i, cs, exact_diag)
            return either_path(worst, iteration)
        _, carried_last = lax.fori_loop(1, n, body, (worst1, carried0))
        output_all(n - 1, carried_last)

    @pl.when(s == 0)
    def _():
        ctx_step()

    @pl.when(s > 0)
    def _():
        latent_step()


def _scan_call(inputs, in_specs, out_struct, out_specs, grid, *, n_heads, dk, dv, has_k,
               n_ctx_chunks, n_step_chunks, lat_index, name):
    n_units = 2 * n_heads
    scratch = [pltpu.VMEM((2, n_heads, dv, dk), F32),
               pltpu.VMEM((2, n_units, CHUNK, dk), F32),
               pltpu.VMEM((n_units, dv, dk), BF16)]
    if not has_k:
        scratch.append(pltpu.VMEM((2, n_units, CHUNK, dk), F32))
    return pl.pallas_call(
        functools.partial(_scan_kernel, n_heads=n_heads, dk=dk, dv=dv, has_k=has_k,
                          n_ctx_chunks=n_ctx_chunks, n_step_chunks=n_step_chunks, lat_index=lat_index),
        grid=grid,
        in_specs=in_specs,
        out_specs=out_specs,
        out_shape=out_struct,
        scratch_shapes=scratch,
        compiler_params=pltpu.CompilerParams(
            dimension_semantics=("arbitrary", "arbitrary", "arbitrary"),
            vmem_limit_bytes=VMEM_LIMIT_BYTES),
        name=name,
    )(*inputs)


A_SCAN_HEADS = 2
A_SCAN_CHUNKS = 32


def _scan_a(c_qi, c_lf, qi, lf):
    bsz, n_ctx, _ = c_qi.shape
    seq = qi.shape[1]
    g = A_SCAN_HEADS
    gw = g * A_DK
    n_hg = A_HEADS // g
    ts = A_SCAN_CHUNKS * CHUNK
    n_steps = seq // ts

    def fstep(s):
        return jnp.maximum(s - 1, 0)

    def bstep(s):
        return n_steps - 1 - jnp.maximum(s - 1, 0)

    cspec = lambda off: pl.BlockSpec((None, n_ctx, gw), lambda b, h, s: (b, 0, off + h))
    fspec = lambda off: pl.BlockSpec((None, ts, gw), lambda b, h, s: (b, fstep(s), off + h))
    bspec = lambda off: pl.BlockSpec((None, ts, gw), lambda b, h, s: (b, bstep(s), off + h))
    in_specs = [cspec(n_hg), cspec(0), cspec(n_hg),
                fspec(0), fspec(n_hg), fspec(0),
                bspec(0), bspec(n_hg), bspec(n_hg)]
    inputs = [c_qi, c_lf, c_lf, qi, qi, lf, qi, qi, lf]
    out_struct = [jax.ShapeDtypeStruct((bsz, seq, A_HEADS * A_DV), BF16)] * 2
    out_specs = [pl.BlockSpec((None, ts, gw), lambda b, h, s: (b, fstep(s), h)),
                 pl.BlockSpec((None, ts, gw), lambda b, h, s: (b, bstep(s), h))]

    def lat_index(chunk):
        row = chunk * CHUNK
        if not isinstance(row, int):
            row = pl.multiple_of(row, CHUNK)
        return (pl.ds(row, CHUNK),)

    return _scan_call(inputs, in_specs, out_struct, out_specs, (bsz, n_hg, n_steps + 1),
                      n_heads=g, dk=A_DK, dv=A_DV, has_k=False,
                      n_ctx_chunks=n_ctx // CHUNK, n_step_chunks=A_SCAN_CHUNKS,
                      lat_index=lat_index, name="scan_hgrn2")


B_SCAN_COLS = 16


def _scan_b(c_qkv, c_lf, qkv, lf):
    bsz, n_ctx, _ = c_qkv.shape
    n_rows = qkv.shape[2]
    kw = B_HEADS * B_DK
    nc = B_SCAN_COLS
    n_steps = GRID_W // nc
    chunks_per_col = n_rows // CHUNK

    def fblk(s):
        return jnp.maximum(s - 1, 0)

    def bblk(s):
        return n_steps - 1 - jnp.maximum(s - 1, 0)

    def cspec(width, off):
        return pl.BlockSpec((None, n_ctx, width), lambda b, h, s: (b, 0, off + h))

    def lspec(width, off, blk):
        return pl.BlockSpec((None, nc, n_rows, width), lambda b, h, s: (b, blk(s), 0, off + h))

    k_off, v_off = kw // B_DK, (2 * kw) // B_DV
    in_specs = [cspec(B_DK, k_off), cspec(B_DV, v_off), cspec(B_DK, 0), cspec(B_DK, B_HEADS),
                lspec(B_DK, 0, fblk), lspec(B_DK, k_off, fblk), lspec(B_DV, v_off, fblk), lspec(B_DK, 0, fblk),
                lspec(B_DK, 0, bblk), lspec(B_DK, k_off, bblk), lspec(B_DV, v_off, bblk),
                lspec(B_DK, B_HEADS, bblk)]
    inputs = [c_qkv, c_qkv, c_lf, c_lf, qkv, qkv, qkv, lf, qkv, qkv, qkv, lf]
    vw = B_HEADS * B_DV
    out_struct = [jax.ShapeDtypeStruct((bsz, GRID_W, n_rows, vw), BF16)] * 2
    out_specs = [pl.BlockSpec((None, nc, n_rows, B_DV), lambda b, h, s: (b, fblk(s), 0, h)),
                 pl.BlockSpec((None, nc, n_rows, B_DV), lambda b, h, s: (b, bblk(s), 0, h))]

    def lat_index(chunk):
        col = chunk // chunks_per_col
        row = (chunk - col * chunks_per_col) * CHUNK
        if not isinstance(row, int):
            row = pl.multiple_of(row, CHUNK)
        return (col, pl.ds(row, CHUNK))

    return _scan_call(inputs, in_specs, out_struct, out_specs, (bsz, B_HEADS, n_steps + 1),
                      n_heads=1, dk=B_DK, dv=B_DV, has_k=True,
                      n_ctx_chunks=n_ctx // CHUNK, n_step_chunks=nc * chunks_per_col,
                      lat_index=lat_index, name="scan_gla")


GLA_FIN_ROWS = 16


def _gla_finalize_kernel(of_ref, ob_ref, sg_ref, g_ref, o_ref):
    n_cols, n_rows, w = of_ref.shape
    o = of_ref[...].astype(F32) + ob_ref[...].astype(F32)
    ms = jnp.mean(o * o, axis=-1, keepdims=True)
    on = o * lax.rsqrt(ms + EPS) * g_ref[...]
    on = jnp.swapaxes(on, 0, 1).reshape(n_rows * n_cols, w)
    o_ref[...] = (on * sg_ref[...].astype(F32)).astype(o_ref.dtype)


def _gla_finalize(of, ob, gates, gain, *, sg_col_off):
    bsz, _, n_rows, vw = of.shape
    tm = GLA_FIN_ROWS * GRID_W
    tiles = n_rows // GLA_FIN_ROWS
    cm = pl.BlockSpec((None, GRID_W, GLA_FIN_ROWS, B_DV), lambda b, r, h: (b, 0, r, h))
    return pl.pallas_call(
        _gla_finalize_kernel,
        grid=(bsz, tiles, B_HEADS),
        in_specs=[cm, cm,
                  pl.BlockSpec((tm, B_DV), lambda b, r, h: (b * tiles + r, sg_col_off + h)),
                  pl.BlockSpec((1, B_DV), lambda b, r, h: (0, h))],
        out_specs=pl.BlockSpec((tm, B_DV), lambda b, r, h: (b * tiles + r, h)),
        out_shape=jax.ShapeDtypeStruct((bsz * n_rows * GRID_W, vw), BF16),
        compiler_params=pltpu.CompilerParams(
            dimension_semantics=("arbitrary", "arbitrary", "arbitrary"),
            vmem_limit_bytes=32 * 1024 * 1024),
        name="gla_finalize",
    )(of, ob, gates, gain)


def _merge_final_kernel(oaf, oab, obp, sga, sma, smb, x_ref, mod_ref, ga, wpa, wpb, wout, fg, o_ref):
    o = oaf[...].astype(F32) + oab[...].astype(F32)
    parts = []
    for h in range(A_HEADS):
        oh = o[:, h * A_DV:(h + 1) * A_DV]
        ms = jnp.mean(oh * oh, axis=-1, keepdims=True)
        parts.append(oh * lax.rsqrt(ms + EPS))
    oa = (jnp.concatenate(parts, axis=1) * ga[...] * sga[...].astype(F32)).astype(BF16)
    ya = jnp.dot(oa, wpa[...], preferred_element_type=F32)
    yb = jnp.dot(obp[...], wpb[...], preferred_element_type=F32)
    y = (sma[...].astype(F32) * ya + smb[...].astype(F32) * yb).astype(BF16)
    yo = jnp.dot(y, wout[...], preferred_element_type=F32)
    z = x_ref[...] + mod_ref[0, 2:3, :] * yo
    ms = jnp.mean(z * z, axis=-1, keepdims=True)
    o_ref[...] = z * lax.rsqrt(ms + EPS) * fg[...]


def _merge_final(oaf, oab, obp, gates, x2d, mod3, ga, wpa, wpb, w_out, fg, *, tm, tiles_per_batch):
    n_tok, d = oaf.shape
    tok = lambda off: pl.BlockSpec((tm, d), lambda i: (i, off))
    const = lambda shape: pl.BlockSpec(shape, lambda i: (0, 0), pipeline_mode=pl.Buffered(1))
    return pl.pallas_call(
        _merge_final_kernel,
        grid=(n_tok // tm,),
        in_specs=[tok(0), tok(0), tok(0), tok(0), tok(2), tok(3), tok(0),
                  pl.BlockSpec((1, 3, d), lambda i: (i // tiles_per_batch, 0, 0)),
                  const((1, d)), const(wpa.shape), const(wpb.shape), const(w_out.shape), const((1, d))],
        out_specs=pl.BlockSpec((tm, d), lambda i: (i, 0)),
        out_shape=jax.ShapeDtypeStruct((n_tok, d), F32),
        compiler_params=pltpu.CompilerParams(
            dimension_semantics=("arbitrary",), vmem_limit_bytes=VMEM_LIMIT_BYTES),
        name="merge_out_final",
    )(oaf, oab, obp, gates, gates, gates, x2d, mod3, ga, wpa, wpb, w_out, fg)


def kernel(x, c, ctx, c_ctx, w_ada, b_ada, norm_g, w_in, hgrn_lb_logits, gla_w_gk, gla_b_gk,
           hgrn_onorm_g, gla_onorm_g, w_pa, w_pb, w_out, final_norm_g):
    bsz, seq, d = x.shape
    n_ctx = ctx.shape[1]
    depth = w_in.shape[0]
    assert depth == 1, "single-layer trunk"
    a_kw, a_vw = A_HEADS * A_DK, A_HEADS * A_DV
    b_kw, b_vw = B_HEADS * B_DK, B_HEADS * B_DV

    n_rows = -(-(bsz + 1) // SUBLANES) * SUBLANES
    cvec = jnp.zeros((n_rows, d), F32).at[:bsz].set(c).at[bsz].set(c_ctx)
    mod = _adaln(cvec, w_ada[0], b_ada[0].reshape(1, -1))
    mod3 = mod.reshape(n_rows, 3, d)

    lb = _lower_bounds(hgrn_lb_logits)[0:1]

    o_ag = 2 * a_kw + 2 * a_vw
    o_bq = o_ag + a_vw
    o_br = o_bq + 2 * b_kw + b_vw
    w_main, w_r = _prep_weights(
        jnp.swapaxes(w_in[0], 0, 1), [(0, o_ag), (o_bq, o_br - o_bq), (o_ag, o_bq - o_ag)],
        tail_start=o_br)
    gk_pad = jnp.zeros((LANES, 2 * b_kw), F32)
    gk_pad = gk_pad.at[0:B_RANK, 0:b_kw].set(gla_w_gk[0, 0]).at[B_RANK:2 * B_RANK, b_kw:].set(gla_w_gk[0, 1])
    gk_pad = gk_pad.astype(BF16)
    gkb = gla_b_gk[0].reshape(1, 2 * b_kw)
    ng = norm_g[0].reshape(1, d)

    x2d = x.reshape(bsz * seq, d)
    ctx2d = ctx.reshape(bsz * n_ctx, d)
    tm = 1024
    tiles_per_batch = seq // tm

    c_aqi, c_alf, c_bqkv, c_blf = _inproj(
        ctx2d, mod3, lambda i: bsz, ng, w_main, w_r, lb, gk_pad, gkb,
        tm=bsz * n_ctx, latent=False)
    aqi, alf, bqkv, blf, gates = _inproj(
        x2d, mod3, lambda i: i // tiles_per_batch, ng, w_main, w_r, lb, gk_pad, gkb,
        tm=tm, latent=True, n_batch=bsz)

    r3 = lambda t, n: t.reshape(bsz, n, t.shape[-1])
    oaf, oab = _scan_a(r3(c_aqi, n_ctx), r3(c_alf, n_ctx), r3(aqi, seq), r3(alf, seq))
    obf, obb = _scan_b(r3(c_bqkv, n_ctx), r3(c_blf, n_ctx), bqkv, blf)
    obp = _gla_finalize(obf, obb, gates, gla_onorm_g[0].reshape(1, -1), sg_col_off=a_vw // B_DV)

    r2 = lambda t: t.reshape(bsz * seq, t.shape[-1])
    tm_out = 256
    out = _merge_final(r2(oaf), r2(oab), obp, gates, x2d, mod3, hgrn_onorm_g[0].reshape(1, -1),
                       w_pa[0].astype(BF16), w_pb[0].astype(BF16), w_out[0].astype(BF16),
                       final_norm_g.reshape(1, d), tm=tm_out, tiles_per_batch=seq // tm_out)
    return out.reshape(bsz, seq, d)
```
